```python
import math
import jax
import jax.numpy as jnp
from jax import lax
import numpy as np

D_MODEL = 2048
BATCH = 8
SEQ = 4096
DEPTH = 4

D_MIX = D_MODEL
S5_WIDTH = D_MIX // 4
S5_GROUP = 16
S5_GROUPS = S5_WIDTH // S5_GROUP
S5_STATE = 64
MLA_HEADS = 8
MLA_NOPE = 128
MLA_ROPE = 64
MLA_V = 128
MLA_Q_RANK = D_MODEL // 4
MLA_KV_RANK = D_MODEL // 8
MLA_WIDTH = MLA_HEADS * MLA_V
ROPE_THETA = 10000.0
Q_BLOCK = 128
MASK_VALUE = -1e30
HG_HEADS = 4
HG_DK = 128
HG_DV = (D_MIX - S5_WIDTH - MLA_WIDTH) // HG_HEADS
HG_WIDTH = HG_HEADS * HG_DV
HG_CHUNK = 64
D_FF = 5504
CONV_W = 3
EPS = 1e-6
IN_SIZES = (S5_WIDTH, MLA_Q_RANK, MLA_KV_RANK, MLA_ROPE,
            HG_HEADS * HG_DK, HG_HEADS * HG_DK, HG_WIDTH, HG_WIDTH)
IN_OFFSETS = tuple(int(v) for v in np.cumsum(IN_SIZES)[:-1])
D_IN = int(sum(IN_SIZES))

kernel_name = 'hybrid_s5_mla_hgrn2_block'


def rmsnorm(x, gain):
    xf = x.astype(jnp.float32)
    xf = xf * lax.rsqrt(jnp.mean(xf * xf, axis=-1, keepdims=True) + EPS)
    return (xf * gain.astype(jnp.float32)).astype(x.dtype)


def rope_tables(positions):
    inv_freq = 1.0 / (ROPE_THETA ** (jnp.arange(0, MLA_ROPE, 2, dtype=jnp.float32) / MLA_ROPE))
    ang = positions.astype(jnp.float32)[..., None] * inv_freq
    return jnp.cos(ang), jnp.sin(ang)


def apply_rope(x, cos, sin):
    half = x.shape[-1] // 2
    x1, x2 = x[..., :half], x[..., half:]
    cos = cos.astype(x.dtype)
    sin = sin.astype(x.dtype)
    return jnp.concatenate([x1 * cos - x2 * sin, x2 * cos + x1 * sin], axis=-1)


def s5_mixer(u, lam_re, lam_im, log_dt, b_re, b_im, c_re, c_im, d, w_glu):
    bsz, seq, _ = u.shape
    uf = u.astype(jnp.float32).reshape(bsz, seq, S5_GROUPS, S5_GROUP)
    lam = lax.complex(lam_re.astype(jnp.float32), lam_im.astype(jnp.float32))
    dt = jnp.exp(log_dt.astype(jnp.float32))[:, None]
    lam_bar = jnp.exp(lam * dt)
    b = lax.complex(b_re.astype(jnp.float32), b_im.astype(jnp.float32))
    b_bar = ((lam_bar - 1.0) / lam)[..., None] * b
    bu = jnp.einsum('gpc,bsgc->bsgp', b_bar, uf.astype(jnp.complex64))
    a = jnp.broadcast_to(lam_bar, bu.shape)

    def combine(left, right):
        a_l, b_l = left
        a_r, b_r = right
        return a_r * a_l, a_r * b_l + b_r

    _, h = lax.associative_scan(combine, (a, bu), axis=1)
    cm = lax.complex(c_re.astype(jnp.float32), c_im.astype(jnp.float32))
    y = (jnp.einsum('gcp,bsgp->bsgc', cm, h).real
         + d.astype(jnp.float32).reshape(S5_GROUPS, S5_GROUP) * uf)
    y = jax.nn.gelu(y.reshape(bsz, seq, S5_WIDTH))
    out = y * jax.nn.sigmoid(y @ w_glu.astype(jnp.float32))
    return out.astype(u.dtype)


def causal_attention_blocks(q_nope, q_rope, k_nope, k_rope, v):
    seq = q_nope.shape[1]
    scale = (MLA_NOPE + MLA_ROPE) ** -0.5
    outs = []
    for blk in range(seq // Q_BLOCK):
        q0, q1 = blk * Q_BLOCK, (blk + 1) * Q_BLOCK
        s = (jnp.einsum('bqhd,bkhd->bhqk', q_nope[:, q0:q1], k_nope[:, :q1])
             + jnp.einsum('bqhr,bkr->bhqk', q_rope[:, q0:q1], k_rope[:, :q1])).astype(jnp.float32) * scale
        causal = jnp.arange(q1)[None, :] <= jnp.arange(q0, q1)[:, None]
        p = jax.nn.softmax(jnp.where(causal, s, MASK_VALUE), axis=-1).astype(v.dtype)
        outs.append(jnp.einsum('bhqk,bkhd->bqhd', p, v[:, :q1]))
    return jnp.concatenate(outs, axis=1)


def mla_mixer(c_q, c_kv, k_rope_in, q_norm, w_uq, kv_norm, w_ukv, cos, sin):
    bsz, seq, _ = c_q.shape
    q = (rmsnorm(c_q, q_norm) @ w_uq).reshape(bsz, seq, MLA_HEADS, MLA_NOPE + MLA_ROPE)
    q_nope = q[..., :MLA_NOPE]
    q_rope = apply_rope(q[..., MLA_NOPE:], cos[:, :, None, :], sin[:, :, None, :])
    kv = (rmsnorm(c_kv, kv_norm) @ w_ukv).reshape(bsz, seq, MLA_HEADS, MLA_NOPE + MLA_V)
    k_nope, v = kv[..., :MLA_NOPE], kv[..., MLA_NOPE:]
    k_rope = apply_rope(k_rope_in, cos, sin)
    o = causal_attention_blocks(q_nope, q_rope, k_nope, k_rope, v)
    return o.reshape(bsz, seq, MLA_WIDTH)


def hgrn2_chunkwise(q, k, v, logf):
    bsz, seq, nh, dk = q.shape
    dv = v.shape[-1]
    n_chunks = seq // HG_CHUNK

    def to_chunks(t):
        return t.reshape(bsz, n_chunks, HG_CHUNK, nh, t.shape[-1]).transpose(1, 0, 3, 2, 4)

    mask = jnp.tril(jnp.ones((HG_CHUNK, HG_CHUNK), dtype=bool))[:, :, None]

    def step(state, inp):
        qc, kc, vc, gc = inp
        b = jnp.cumsum(gc, axis=2)
        o_inter = jnp.einsum('bhtk,bhkv->bhtv', qc * jnp.exp(b), state)
        diff = b[:, :, :, None, :] - b[:, :, None, :, :]
        decay = jnp.where(mask, jnp.exp(jnp.where(mask, diff, 0.0)), 0.0)
        attn = jnp.einsum('bhtk,bhsk,bhtsk->bhts', qc, kc, decay)
        o_intra = jnp.einsum('bhts,bhsv->bhtv', attn, vc)
        b_last = b[:, :, -1:, :]
        new_state = (jnp.exp(b_last[:, :, 0, :])[..., None] * state
                     + jnp.einsum('bhsk,bhsv->bhkv', kc * jnp.exp(b_last - b), vc))
        return new_state, o_inter + o_intra

    state0 = jnp.zeros((bsz, nh, dk, dv), jnp.float32)
    _, o = lax.scan(step, state0, (to_chunks(q), to_chunks(k), to_chunks(v), to_chunks(logf)))
    return o.transpose(1, 0, 3, 2, 4).reshape(bsz, seq, nh, dv)


def hgrn2_mixer(q_in, f_in, i_in, g_in, lb, out_norm):
    bsz, seq, _ = q_in.shape
    z = f_in.astype(jnp.float32)
    lb = lb.astype(jnp.float32)
    logf = jnp.log(lb + (1.0 - lb) * jax.nn.sigmoid(z))
    k = (1.0 - lb) * jax.nn.sigmoid(-z)
    q = jax.nn.silu(q_in.astype(jnp.float32))

    def heads(t):
        return t.reshape(bsz, seq, HG_HEADS, t.shape[-1] // HG_HEADS)

    o = hgrn2_chunkwise(heads(q), heads(k), heads(i_in.astype(jnp.float32)), heads(logf))
    o = rmsnorm(o, out_norm) * jax.nn.silu(heads(g_in.astype(jnp.float32)))
    return o.reshape(bsz, seq, HG_WIDTH).astype(q_in.dtype)


def causal_dwconv(u, w, b):
    seq = u.shape[1]
    taps = w.shape[0]
    up = jnp.pad(u, ((0, 0), (taps - 1, 0), (0, 0)))
    out = b
    for j in range(taps):
        out = out + up[:, j:j + seq] * w[j]
    return out


def conv_geglu_ffn(h, w_up, conv_w, conv_b, w_down):
    u = causal_dwconv(h @ w_up, conv_w, conv_b)
    gate, val = jnp.split(u, 2, axis=-1)
    return (jax.nn.gelu(gate, approximate=True) * val) @ w_down


def _fwd_setup_inputs(seed: int = 0) -> dict:
    key = jax.random.key(seed)
    k = jax.random.split(key, 32)
    L = DEPTH

    def nrm(i, shape, scale=1.0):
        return scale * jax.random.normal(k[i], shape, jnp.float32)

    def gain(i, shape):
        return 1.0 + nrm(i, shape, 0.1)

    x = nrm(0, (BATCH, SEQ, D_MODEL))
    c = nrm(1, (BATCH, D_MODEL))
    offsets = jax.random.randint(k[2], (BATCH, 1), 0, 1024, dtype=jnp.int32)
    positions = offsets + jnp.arange(SEQ, dtype=jnp.int32)[None, :]
    w_in = nrm(3, (L, D_MODEL, D_IN), D_MODEL ** -0.5)
    s5_lambda_re = -0.5 + nrm(4, (L, S5_GROUPS, S5_STATE), 0.01)
    s5_lambda_im = math.pi * jnp.arange(S5_STATE, dtype=jnp.float32) + nrm(5, (L, S5_GROUPS, S5_STATE), 0.01)
    s5_log_dt = jax.random.uniform(k[6], (L, S5_GROUPS), jnp.float32, math.log(1e-3), math.log(1e-1))
    s5_b_re = nrm(7, (L, S5_GROUPS, S5_STATE, S5_GROUP), (2 * S5_GROUP) ** -0.5)
    s5_b_im = nrm(8, (L, S5_GROUPS, S5_STATE, S5_GROUP), (2 * S5_GROUP) ** -0.5)
    s5_c_re = nrm(9, (L, S5_GROUPS, S5_GROUP, S5_STATE), (2 * S5_STATE) ** -0.5)
    s5_c_im = nrm(10, (L, S5_GROUPS, S5_GROUP, S5_STATE), (2 * S5_STATE) ** -0.5)
    s5_d = nrm(11, (L, S5_WIDTH))
    s5_w_glu = nrm(12, (L, S5_WIDTH, S5_WIDTH), S5_WIDTH ** -0.5)
    mla_q_norm = gain(13, (L, MLA_Q_RANK))
    mla_w_uq = nrm(14, (L, MLA_Q_RANK, MLA_HEADS * (MLA_NOPE + MLA_ROPE)), MLA_Q_RANK ** -0.5)
    mla_kv_norm = gain(15, (L, MLA_KV_RANK))
    mla_w_ukv = nrm(16, (L, MLA_KV_RANK, MLA_HEADS * (MLA_NOPE + MLA_V)), MLA_KV_RANK ** -0.5)
    hg_lb_logits = nrm(17, (L, HG_HEADS * HG_DK), 0.5)
    hg_out_norm = gain(18, (L, HG_DV))
    w_out = nrm(19, (L, D_MIX, D_MODEL), D_MIX ** -0.5)
    mix_pre_norm = gain(20, (L, D_MODEL))
    mix_post_norm = gain(21, (L, D_MODEL))
    ffn_pre_norm = gain(22, (L, D_MODEL))
    ffn_post_norm = gain(23, (L, D_MODEL))
    ffn_w_up = nrm(24, (L, D_MODEL, 2 * D_FF), D_MODEL ** -0.5)
    ffn_conv_w = nrm(25, (L, CONV_W, 2 * D_FF), CONV_W ** -0.5)
    ffn_conv_b = nrm(26, (L, 2 * D_FF), 0.02)
    ffn_w_down = nrm(27, (L, D_FF, D_MODEL), D_FF ** -0.5)
    w_ada = nrm(28, (L, D_MODEL, 6 * D_MODEL), 0.5 * D_MODEL ** -0.5)
    b_ada = nrm(29, (L, 6 * D_MODEL), 0.02)
    return {'x': x, 'c': c, 'positions': positions, 'w_in': w_in,
            's5_lambda_re': s5_lambda_re, 's5_lambda_im': s5_lambda_im, 's5_log_dt': s5_log_dt,
            's5_b_re': s5_b_re, 's5_b_im': s5_b_im, 's5_c_re': s5_c_re, 's5_c_im': s5_c_im,
            's5_d': s5_d, 's5_w_glu': s5_w_glu,
            'mla_q_norm': mla_q_norm, 'mla_w_uq': mla_w_uq, 'mla_kv_norm': mla_kv_norm, 'mla_w_ukv': mla_w_ukv,
            'hg_lb_logits': hg_lb_logits, 'hg_out_norm': hg_out_norm, 'w_out': w_out,
            'mix_pre_norm': mix_pre_norm, 'mix_post_norm': mix_post_norm,
            'ffn_pre_norm': ffn_pre_norm, 'ffn_post_norm': ffn_post_norm,
            'ffn_w_up': ffn_w_up, 'ffn_conv_w': ffn_conv_w, 'ffn_conv_b': ffn_conv_b, 'ffn_w_down': ffn_w_down,
            'w_ada': w_ada, 'b_ada': b_ada}


def _fwd_reference(x, c, positions, w_in, s5_lambda_re, s5_lambda_im, s5_log_dt, s5_b_re, s5_b_im,
              s5_c_re, s5_c_im, s5_d, s5_w_glu, mla_q_norm, mla_w_uq, mla_kv_norm, mla_w_ukv,
              hg_lb_logits, hg_out_norm, w_out, mix_pre_norm, mix_post_norm, ffn_pre_norm, ffn_post_norm,
              ffn_w_up, ffn_conv_w, ffn_conv_b, ffn_w_down, w_ada, b_ada):
    cos, sin = rope_tables(positions)
    probs = jax.nn.softmax(hg_lb_logits.astype(jnp.float32), axis=0)
    lower_bounds = jnp.cumsum(probs, axis=0) - probs[0:1]
    c_act = jax.nn.silu(c)
    for l in range(DEPTH):
        mod = c_act @ w_ada[l] + b_ada[l]
        sh1, sc1, g1, sh2, sc2, g2 = jnp.split(mod[:, None, :], 6, axis=-1)
        h = rmsnorm(x, mix_pre_norm[l]) * (1.0 + sc1) + sh1
        proj = h @ w_in[l]
        u_s5, c_q, c_kv, k_rope, hq, hf, hi, hg = jnp.split(proj, IN_OFFSETS, axis=-1)
        y_s5 = s5_mixer(u_s5, s5_lambda_re[l], s5_lambda_im[l], s5_log_dt[l], s5_b_re[l], s5_b_im[l],
                        s5_c_re[l], s5_c_im[l], s5_d[l], s5_w_glu[l])
        y_mla = mla_mixer(c_q, c_kv, k_rope, mla_q_norm[l], mla_w_uq[l], mla_kv_norm[l], mla_w_ukv[l], cos, sin)
        y_hg = hgrn2_mixer(hq, hf, hi, hg, lower_bounds[l], hg_out_norm[l])
        mixed = jnp.concatenate([y_s5, y_mla, y_hg], axis=-1) @ w_out[l]
        x = x + g1 * rmsnorm(mixed, mix_post_norm[l])
        h = rmsnorm(x, ffn_pre_norm[l]) * (1.0 + sc2) + sh2
        y = conv_geglu_ffn(h, ffn_w_up[l], ffn_conv_w[l], ffn_conv_b[l], ffn_w_down[l])
        x = x + g2 * rmsnorm(y, ffn_post_norm[l])
    return x


import jax as _jax
import jax.numpy as _jnp

TWIN_FORMAT = 'train_step'
FWD_PARAMS = ['x', 'c', 'positions', 'w_in', 's5_lambda_re', 's5_lambda_im', 's5_log_dt', 's5_b_re', 's5_b_im', 's5_c_re', 's5_c_im', 's5_d', 's5_w_glu', 'mla_q_norm', 'mla_w_uq', 'mla_kv_norm', 'mla_w_ukv', 'hg_lb_logits', 'hg_out_norm', 'w_out', 'mix_pre_norm', 'mix_post_norm', 'ffn_pre_norm', 'ffn_post_norm', 'ffn_w_up', 'ffn_conv_w', 'ffn_conv_b', 'ffn_w_down', 'w_ada', 'b_ada']
TWIN_WEIGHTS = ['w_in', 's5_lambda_re', 's5_lambda_im', 's5_log_dt', 's5_b_re', 's5_b_im', 's5_c_re', 's5_c_im', 's5_d', 's5_w_glu', 'mla_q_norm', 'mla_w_uq', 'mla_kv_norm', 'mla_w_ukv', 'hg_lb_logits', 'hg_out_norm', 'w_out', 'mix_pre_norm', 'mix_post_norm', 'ffn_pre_norm', 'ffn_post_norm', 'ffn_w_up', 'ffn_conv_w', 'ffn_conv_b', 'ffn_w_down', 'w_ada', 'b_ada']
TWIN_DIFF_INPUT = 'x'
TWIN_INPUTS = ['x', 'c', 'positions', 'w_in', 's5_lambda_re', 's5_lambda_im', 's5_log_dt', 's5_b_re', 's5_b_im', 's5_c_re', 's5_c_im', 's5_d', 's5_w_glu', 'mla_q_norm', 'mla_w_uq', 'mla_kv_norm', 'mla_w_ukv', 'hg_lb_logits', 'hg_out_norm', 'w_out', 'mix_pre_norm', 'mix_post_norm', 'ffn_pre_norm', 'ffn_post_norm', 'ffn_w_up', 'ffn_conv_w', 'ffn_conv_b', 'ffn_w_down', 'w_ada', 'b_ada', 'loss_target', 'm_w_in', 'm_s5_lambda_re', 'm_s5_lambda_im', 'm_s5_log_dt', 'm_s5_b_re', 'm_s5_b_im', 'm_s5_c_re', 'm_s5_c_im', 'm_s5_d', 'm_s5_w_glu', 'm_mla_q_norm', 'm_mla_w_uq', 'm_mla_kv_norm', 'm_mla_w_ukv', 'm_hg_lb_logits', 'm_hg_out_norm', 'm_w_out', 'm_mix_pre_norm', 'm_mix_post_norm', 'm_ffn_pre_norm', 'm_ffn_post_norm', 'm_ffn_w_up', 'm_ffn_conv_w', 'm_ffn_conv_b', 'm_ffn_w_down', 'm_w_ada', 'm_b_ada', 'v_w_in', 'v_s5_lambda_re', 'v_s5_lambda_im', 'v_s5_log_dt', 'v_s5_b_re', 'v_s5_b_im', 'v_s5_c_re', 'v_s5_c_im', 'v_s5_d', 'v_s5_w_glu', 'v_mla_q_norm', 'v_mla_w_uq', 'v_mla_kv_norm', 'v_mla_w_ukv', 'v_hg_lb_logits', 'v_hg_out_norm', 'v_w_out', 'v_mix_pre_norm', 'v_mix_post_norm', 'v_ffn_pre_norm', 'v_ffn_post_norm', 'v_ffn_w_up', 'v_ffn_conv_w', 'v_ffn_conv_b', 'v_ffn_w_down', 'v_w_ada', 'v_b_ada']
TWIN_OUTPUTS = ['loss', 'grad_x', 'grad_w_in', 'grad_s5_lambda_re', 'grad_s5_lambda_im', 'grad_s5_log_dt', 'grad_s5_b_re', 'grad_s5_b_im', 'grad_s5_c_re', 'grad_s5_c_im', 'grad_s5_d', 'grad_s5_w_glu', 'grad_mla_q_norm', 'grad_mla_w_uq', 'grad_mla_kv_norm', 'grad_mla_w_ukv', 'grad_hg_lb_logits', 'grad_hg_out_norm', 'grad_w_out', 'grad_mix_pre_norm', 'grad_mix_post_norm', 'grad_ffn_pre_norm', 'grad_ffn_post_norm', 'grad_ffn_w_up', 'grad_ffn_conv_w', 'grad_ffn_conv_b', 'grad_ffn_w_down', 'grad_w_ada', 'grad_b_ada', 'delta_w_in', 'delta_s5_lambda_re', 'delta_s5_lambda_im', 'delta_s5_log_dt', 'delta_s5_b_re', 'delta_s5_b_im', 'delta_s5_c_re', 'delta_s5_c_im', 'delta_s5_d', 'delta_s5_w_glu', 'delta_mla_q_norm', 'delta_mla_w_uq', 'delta_mla_kv_norm', 'delta_mla_w_ukv', 'delta_hg_lb_logits', 'delta_hg_out_norm', 'delta_w_out', 'delta_mix_pre_norm', 'delta_mix_post_norm', 'delta_ffn_pre_norm', 'delta_ffn_post_norm', 'delta_ffn_w_up', 'delta_ffn_conv_w', 'delta_ffn_conv_b', 'delta_ffn_w_down', 'delta_w_ada', 'delta_b_ada', 'new_m_w_in', 'new_m_s5_lambda_re', 'new_m_s5_lambda_im', 'new_m_s5_log_dt', 'new_m_s5_b_re', 'new_m_s5_b_im', 'new_m_s5_c_re', 'new_m_s5_c_im', 'new_m_s5_d', 'new_m_s5_w_glu', 'new_m_mla_q_norm', 'new_m_mla_w_uq', 'new_m_mla_kv_norm', 'new_m_mla_w_ukv', 'new_m_hg_lb_logits', 'new_m_hg_out_norm', 'new_m_w_out', 'new_m_mix_pre_norm', 'new_m_mix_post_norm', 'new_m_ffn_pre_norm', 'new_m_ffn_post_norm', 'new_m_ffn_w_up', 'new_m_ffn_conv_w', 'new_m_ffn_conv_b', 'new_m_ffn_w_down', 'new_m_w_ada', 'new_m_b_ada', 'new_v_w_in', 'new_v_s5_lambda_re', 'new_v_s5_lambda_im', 'new_v_s5_log_dt', 'new_v_s5_b_re', 'new_v_s5_b_im', 'new_v_s5_c_re', 'new_v_s5_c_im', 'new_v_s5_d', 'new_v_s5_w_glu', 'new_v_mla_q_norm', 'new_v_mla_w_uq', 'new_v_mla_kv_norm', 'new_v_mla_w_ukv', 'new_v_hg_lb_logits', 'new_v_hg_out_norm', 'new_v_w_out', 'new_v_mix_pre_norm', 'new_v_mix_post_norm', 'new_v_ffn_pre_norm', 'new_v_ffn_post_norm', 'new_v_ffn_w_up', 'new_v_ffn_conv_w', 'new_v_ffn_conv_b', 'new_v_ffn_w_down', 'new_v_w_ada', 'new_v_b_ada']
TWIN_LEAF_KINDS = {'loss': 'loss', 'grad_x': 'grad_x', 'grad_w_in': 'grad_w', 'grad_s5_lambda_re': 'grad_w', 'grad_s5_lambda_im': 'grad_w', 'grad_s5_log_dt': 'grad_w', 'grad_s5_b_re': 'grad_w', 'grad_s5_b_im': 'grad_w', 'grad_s5_c_re': 'grad_w', 'grad_s5_c_im': 'grad_w', 'grad_s5_d': 'grad_w', 'grad_s5_w_glu': 'grad_w', 'grad_mla_q_norm': 'grad_w', 'grad_mla_w_uq': 'grad_w', 'grad_mla_kv_norm': 'grad_w', 'grad_mla_w_ukv': 'grad_w', 'grad_hg_lb_logits': 'grad_w', 'grad_hg_out_norm': 'grad_w', 'grad_w_out': 'grad_w', 'grad_mix_pre_norm': 'grad_w', 'grad_mix_post_norm': 'grad_w', 'grad_ffn_pre_norm': 'grad_w', 'grad_ffn_post_norm': 'grad_w', 'grad_ffn_w_up': 'grad_w', 'grad_ffn_conv_w': 'grad_w', 'grad_ffn_conv_b': 'grad_w', 'grad_ffn_w_down': 'grad_w', 'grad_w_ada': 'grad_w', 'grad_b_ada': 'grad_w', 'delta_w_in': 'delta_w', 'delta_s5_lambda_re': 'delta_w', 'delta_s5_lambda_im': 'delta_w', 'delta_s5_log_dt': 'delta_w', 'delta_s5_b_re': 'delta_w', 'delta_s5_b_im': 'delta_w', 'delta_s5_c_re': 'delta_w', 'delta_s5_c_im': 'delta_w', 'delta_s5_d': 'delta_w', 'delta_s5_w_glu': 'delta_w', 'delta_mla_q_norm': 'delta_w', 'delta_mla_w_uq': 'delta_w', 'delta_mla_kv_norm': 'delta_w', 'delta_mla_w_ukv': 'delta_w', 'delta_hg_lb_logits': 'delta_w', 'delta_hg_out_norm': 'delta_w', 'delta_w_out': 'delta_w', 'delta_mix_pre_norm': 'delta_w', 'delta_mix_post_norm': 'delta_w', 'delta_ffn_pre_norm': 'delta_w', 'delta_ffn_post_norm': 'delta_w', 'delta_ffn_w_up': 'delta_w', 'delta_ffn_conv_w': 'delta_w', 'delta_ffn_conv_b': 'delta_w', 'delta_ffn_w_down': 'delta_w', 'delta_w_ada': 'delta_w', 'delta_b_ada': 'delta_w', 'new_m_w_in': 'new_m', 'new_m_s5_lambda_re': 'new_m', 'new_m_s5_lambda_im': 'new_m', 'new_m_s5_log_dt': 'new_m', 'new_m_s5_b_re': 'new_m', 'new_m_s5_b_im': 'new_m', 'new_m_s5_c_re': 'new_m', 'new_m_s5_c_im': 'new_m', 'new_m_s5_d': 'new_m', 'new_m_s5_w_glu': 'new_m', 'new_m_mla_q_norm': 'new_m', 'new_m_mla_w_uq': 'new_m', 'new_m_mla_kv_norm': 'new_m', 'new_m_mla_w_ukv': 'new_m', 'new_m_hg_lb_logits': 'new_m', 'new_m_hg_out_norm': 'new_m', 'new_m_w_out': 'new_m', 'new_m_mix_pre_norm': 'new_m', 'new_m_mix_post_norm': 'new_m', 'new_m_ffn_pre_norm': 'new_m', 'new_m_ffn_post_norm': 'new_m', 'new_m_ffn_w_up': 'new_m', 'new_m_ffn_conv_w': 'new_m', 'new_m_ffn_conv_b': 'new_m', 'new_m_ffn_w_down': 'new_m', 'new_m_w_ada': 'new_m', 'new_m_b_ada': 'new_m', 'new_v_w_in': 'new_v', 'new_v_s5_lambda_re': 'new_v', 'new_v_s5_lambda_im': 'new_v', 'new_v_s5_log_dt': 'new_v', 'new_v_s5_b_re': 'new_v', 'new_v_s5_b_im': 'new_v', 'new_v_s5_c_re': 'new_v', 'new_v_s5_c_im': 'new_v', 'new_v_s5_d': 'new_v', 'new_v_s5_w_glu': 'new_v', 'new_v_mla_q_norm': 'new_v', 'new_v_mla_w_uq': 'new_v', 'new_v_mla_kv_norm': 'new_v', 'new_v_mla_w_ukv': 'new_v', 'new_v_hg_lb_logits': 'new_v', 'new_v_hg_out_norm': 'new_v', 'new_v_w_out': 'new_v', 'new_v_mix_pre_norm': 'new_v', 'new_v_mix_post_norm': 'new_v', 'new_v_ffn_pre_norm': 'new_v', 'new_v_ffn_post_norm': 'new_v', 'new_v_ffn_w_up': 'new_v', 'new_v_ffn_conv_w': 'new_v', 'new_v_ffn_conv_b': 'new_v', 'new_v_ffn_w_down': 'new_v', 'new_v_w_ada': 'new_v', 'new_v_b_ada': 'new_v'}


def _forward(args):
    return _fwd_reference(*[args[k] for k in FWD_PARAMS])


def _output_shape():
    def fwd():
        inp = _fwd_setup_inputs(0)
        return _fwd_reference(*[inp[k] for k in FWD_PARAMS])
    out = _jax.eval_shape(fwd)
    return out.shape, out.dtype

N_MICROBATCH = 1
ADAM_LR = 0.001
ADAM_B1 = 0.9
ADAM_B2 = 0.999
ADAM_EPS = 1e-08
ADAM_WD = 0.01
ADAM_STEP = 10
PER_EXAMPLE_BATCH_AXIS = {'x': 0, 'c': 0, 'positions': 0, 'loss_target': 0}
SHARED_INPUTS = []
_WEIGHT_DTYPES = {'w_in': _jnp.float32, 's5_lambda_re': _jnp.float32, 's5_lambda_im': _jnp.float32, 's5_log_dt': _jnp.float32, 's5_b_re': _jnp.float32, 's5_b_im': _jnp.float32, 's5_c_re': _jnp.float32, 's5_c_im': _jnp.float32, 's5_d': _jnp.float32, 's5_w_glu': _jnp.float32, 'mla_q_norm': _jnp.float32, 'mla_w_uq': _jnp.float32, 'mla_kv_norm': _jnp.float32, 'mla_w_ukv': _jnp.float32, 'hg_lb_logits': _jnp.float32, 'hg_out_norm': _jnp.float32, 'w_out': _jnp.float32, 'mix_pre_norm': _jnp.float32, 'mix_post_norm': _jnp.float32, 'ffn_pre_norm': _jnp.float32, 'ffn_post_norm': _jnp.float32, 'ffn_w_up': _jnp.float32, 'ffn_conv_w': _jnp.float32, 'ffn_conv_b': _jnp.float32, 'ffn_w_down': _jnp.float32, 'w_ada': _jnp.float32, 'b_ada': _jnp.float32}
MOMENT_SCALE = {'w_in': 3.506292e-01, 's5_lambda_re': 1.320980e-02, 's5_lambda_im': 1.679275e-02, 's5_log_dt': 2.025834e+00, 's5_b_re': 1.061176e-02, 's5_b_im': 1.076227e-02, 's5_c_re': 2.187129e-02, 's5_c_im': 2.242110e-02, 's5_d': 3.811774e-01, 's5_w_glu': 6.107971e-02, 'mla_q_norm': 3.459790e-02, 'mla_w_uq': 1.998866e-02, 'mla_kv_norm': 1.245604e+00, 'mla_w_ukv': 4.279971e-01, 'hg_lb_logits': 6.608718e-03, 'hg_out_norm': 6.119752e-01, 'w_out': 4.954815e-01, 'mix_pre_norm': 2.139927e-01, 'mix_post_norm': 1.776629e+00, 'ffn_pre_norm': 1.056870e-01, 'ffn_post_norm': 1.632002e+00, 'ffn_w_up': 6.394879e-02, 'ffn_conv_w': 7.187851e-02, 'ffn_conv_b': 1.775267e-01, 'ffn_w_down': 1.322000e-01, 'w_ada': 8.312998e-01, 'b_ada': 1.641969e+00}


def _to_microbatches(a, axis):
    t = _jnp.moveaxis(a, axis, 0)
    t = t.reshape((N_MICROBATCH, t.shape[0] // N_MICROBATCH) + t.shape[1:])
    return _jnp.moveaxis(t, 1, axis + 1)


def setup_inputs(seed: int = 0) -> dict:
    inp = _fwd_setup_inputs(seed)
    key = _jax.random.fold_in(_jax.random.key(seed), 7919)
    shape, _ = _output_shape()
    out = dict(inp)
    out["loss_target"] = _jax.random.normal(_jax.random.fold_in(key, 0), shape, _jnp.float32)
    for i, name in enumerate(TWIN_WEIGHTS):
        w = inp[name].astype(_jnp.float32)
        if MOMENT_SCALE is None:
            s = _jnp.sqrt(_jnp.mean(_jnp.square(w)) + 1e-30)
        else:
            s = MOMENT_SCALE[name]
        km, kv = _jax.random.split(_jax.random.fold_in(key, i + 1))
        out[name] = w
        out["m_" + name] = s * _jax.random.normal(km, w.shape, _jnp.float32)
        out["v_" + name] = (s * s) * _jax.random.uniform(kv, w.shape, _jnp.float32, 0.5, 1.5)
    if N_MICROBATCH > 1:
        for name, axis in PER_EXAMPLE_BATCH_AXIS.items():
            out[name] = _to_microbatches(out[name], axis)
    return {'x': out['x'], 'c': out['c'], 'positions': out['positions'], 'w_in': out['w_in'], 's5_lambda_re': out['s5_lambda_re'], 's5_lambda_im': out['s5_lambda_im'], 's5_log_dt': out['s5_log_dt'], 's5_b_re': out['s5_b_re'], 's5_b_im': out['s5_b_im'], 's5_c_re': out['s5_c_re'], 's5_c_im': out['s5_c_im'], 's5_d': out['s5_d'], 's5_w_glu': out['s5_w_glu'], 'mla_q_norm': out['mla_q_norm'], 'mla_w_uq': out['mla_w_uq'], 'mla_kv_norm': out['mla_kv_norm'], 'mla_w_ukv': out['mla_w_ukv'], 'hg_lb_logits': out['hg_lb_logits'], 'hg_out_norm': out['hg_out_norm'], 'w_out': out['w_out'], 'mix_pre_norm': out['mix_pre_norm'], 'mix_post_norm': out['mix_post_norm'], 'ffn_pre_norm': out['ffn_pre_norm'], 'ffn_post_norm': out['ffn_post_norm'], 'ffn_w_up': out['ffn_w_up'], 'ffn_conv_w': out['ffn_conv_w'], 'ffn_conv_b': out['ffn_conv_b'], 'ffn_w_down': out['ffn_w_down'], 'w_ada': out['w_ada'], 'b_ada': out['b_ada'], 'loss_target': out['loss_target'], 'm_w_in': out['m_w_in'], 'm_s5_lambda_re': out['m_s5_lambda_re'], 'm_s5_lambda_im': out['m_s5_lambda_im'], 'm_s5_log_dt': out['m_s5_log_dt'], 'm_s5_b_re': out['m_s5_b_re'], 'm_s5_b_im': out['m_s5_b_im'], 'm_s5_c_re': out['m_s5_c_re'], 'm_s5_c_im': out['m_s5_c_im'], 'm_s5_d': out['m_s5_d'], 'm_s5_w_glu': out['m_s5_w_glu'], 'm_mla_q_norm': out['m_mla_q_norm'], 'm_mla_w_uq': out['m_mla_w_uq'], 'm_mla_kv_norm': out['m_mla_kv_norm'], 'm_mla_w_ukv': out['m_mla_w_ukv'], 'm_hg_lb_logits': out['m_hg_lb_logits'], 'm_hg_out_norm': out['m_hg_out_norm'], 'm_w_out': out['m_w_out'], 'm_mix_pre_norm': out['m_mix_pre_norm'], 'm_mix_post_norm': out['m_mix_post_norm'], 'm_ffn_pre_norm': out['m_ffn_pre_norm'], 'm_ffn_post_norm': out['m_ffn_post_norm'], 'm_ffn_w_up': out['m_ffn_w_up'], 'm_ffn_conv_w': out['m_ffn_conv_w'], 'm_ffn_conv_b': out['m_ffn_conv_b'], 'm_ffn_w_down': out['m_ffn_w_down'], 'm_w_ada': out['m_w_ada'], 'm_b_ada': out['m_b_ada'], 'v_w_in': out['v_w_in'], 'v_s5_lambda_re': out['v_s5_lambda_re'], 'v_s5_lambda_im': out['v_s5_lambda_im'], 'v_s5_log_dt': out['v_s5_log_dt'], 'v_s5_b_re': out['v_s5_b_re'], 'v_s5_b_im': out['v_s5_b_im'], 'v_s5_c_re': out['v_s5_c_re'], 'v_s5_c_im': out['v_s5_c_im'], 'v_s5_d': out['v_s5_d'], 'v_s5_w_glu': out['v_s5_w_glu'], 'v_mla_q_norm': out['v_mla_q_norm'], 'v_mla_w_uq': out['v_mla_w_uq'], 'v_mla_kv_norm': out['v_mla_kv_norm'], 'v_mla_w_ukv': out['v_mla_w_ukv'], 'v_hg_lb_logits': out['v_hg_lb_logits'], 'v_hg_out_norm': out['v_hg_out_norm'], 'v_w_out': out['v_w_out'], 'v_mix_pre_norm': out['v_mix_pre_norm'], 'v_mix_post_norm': out['v_mix_post_norm'], 'v_ffn_pre_norm': out['v_ffn_pre_norm'], 'v_ffn_post_norm': out['v_ffn_post_norm'], 'v_ffn_w_up': out['v_ffn_w_up'], 'v_ffn_conv_w': out['v_ffn_conv_w'], 'v_ffn_conv_b': out['v_ffn_conv_b'], 'v_ffn_w_down': out['v_ffn_w_down'], 'v_w_ada': out['v_w_ada'], 'v_b_ada': out['v_b_ada']}


def _loss(weights, diff, rest, loss_target):
    with _jax.named_scope("forward"):
        args = {**rest, TWIN_DIFF_INPUT: diff, **{k: w.astype(_WEIGHT_DTYPES[k]) for k, w in weights.items()}}
        y = _forward(args)
    with _jax.named_scope("loss_head"):
        err = _jnp.square(y.astype(_jnp.float32) - loss_target)
        return 0.5 * _jnp.sum(_jnp.mean(err, axis=-1)) if err.ndim else 0.5 * err


def _adamw(w, g, m, v):
    m = ADAM_B1 * m + (1.0 - ADAM_B1) * g
    v = ADAM_B2 * v + (1.0 - ADAM_B2) * _jnp.square(g)
    m_hat = m / (1.0 - ADAM_B1 ** ADAM_STEP)
    v_hat = v / (1.0 - ADAM_B2 ** ADAM_STEP)
    delta = -ADAM_LR * (m_hat / (_jnp.sqrt(v_hat) + ADAM_EPS) + ADAM_WD * w)
    return delta, m, v


def reference(x, c, positions, w_in, s5_lambda_re, s5_lambda_im, s5_log_dt, s5_b_re, s5_b_im, s5_c_re, s5_c_im, s5_d, s5_w_glu, mla_q_norm, mla_w_uq, mla_kv_norm, mla_w_ukv, hg_lb_logits, hg_out_norm, w_out, mix_pre_norm, mix_post_norm, ffn_pre_norm, ffn_post_norm, ffn_w_up, ffn_conv_w, ffn_conv_b, ffn_w_down, w_ada, b_ada, loss_target, m_w_in, m_s5_lambda_re, m_s5_lambda_im, m_s5_log_dt, m_s5_b_re, m_s5_b_im, m_s5_c_re, m_s5_c_im, m_s5_d, m_s5_w_glu, m_mla_q_norm, m_mla_w_uq, m_mla_kv_norm, m_mla_w_ukv, m_hg_lb_logits, m_hg_out_norm, m_w_out, m_mix_pre_norm, m_mix_post_norm, m_ffn_pre_norm, m_ffn_post_norm, m_ffn_w_up, m_ffn_conv_w, m_ffn_conv_b, m_ffn_w_down, m_w_ada, m_b_ada, v_w_in, v_s5_lambda_re, v_s5_lambda_im, v_s5_log_dt, v_s5_b_re, v_s5_b_im, v_s5_c_re, v_s5_c_im, v_s5_d, v_s5_w_glu, v_mla_q_norm, v_mla_w_uq, v_mla_kv_norm, v_mla_w_ukv, v_hg_lb_logits, v_hg_out_norm, v_w_out, v_mix_pre_norm, v_mix_post_norm, v_ffn_pre_norm, v_ffn_post_norm, v_ffn_w_up, v_ffn_conv_w, v_ffn_conv_b, v_ffn_w_down, v_w_ada, v_b_ada):
    given = dict(x=x, c=c, positions=positions, w_in=w_in, s5_lambda_re=s5_lambda_re, s5_lambda_im=s5_lambda_im, s5_log_dt=s5_log_dt, s5_b_re=s5_b_re, s5_b_im=s5_b_im, s5_c_re=s5_c_re, s5_c_im=s5_c_im, s5_d=s5_d, s5_w_glu=s5_w_glu, mla_q_norm=mla_q_norm, mla_w_uq=mla_w_uq, mla_kv_norm=mla_kv_norm, mla_w_ukv=mla_w_ukv, hg_lb_logits=hg_lb_logits, hg_out_norm=hg_out_norm, w_out=w_out, mix_pre_norm=mix_pre_norm, mix_post_norm=mix_post_norm, ffn_pre_norm=ffn_pre_norm, ffn_post_norm=ffn_post_norm, ffn_w_up=ffn_w_up, ffn_conv_w=ffn_conv_w, ffn_conv_b=ffn_conv_b, ffn_w_down=ffn_w_down, w_ada=w_ada, b_ada=b_ada, loss_target=loss_target, m_w_in=m_w_in, m_s5_lambda_re=m_s5_lambda_re, m_s5_lambda_im=m_s5_lambda_im, m_s5_log_dt=m_s5_log_dt, m_s5_b_re=m_s5_b_re, m_s5_b_im=m_s5_b_im, m_s5_c_re=m_s5_c_re, m_s5_c_im=m_s5_c_im, m_s5_d=m_s5_d, m_s5_w_glu=m_s5_w_glu, m_mla_q_norm=m_mla_q_norm, m_mla_w_uq=m_mla_w_uq, m_mla_kv_norm=m_mla_kv_norm, m_mla_w_ukv=m_mla_w_ukv, m_hg_lb_logits=m_hg_lb_logits, m_hg_out_norm=m_hg_out_norm, m_w_out=m_w_out, m_mix_pre_norm=m_mix_pre_norm, m_mix_post_norm=m_mix_post_norm, m_ffn_pre_norm=m_ffn_pre_norm, m_ffn_post_norm=m_ffn_post_norm, m_ffn_w_up=m_ffn_w_up, m_ffn_conv_w=m_ffn_conv_w, m_ffn_conv_b=m_ffn_conv_b, m_ffn_w_down=m_ffn_w_down, m_w_ada=m_w_ada, m_b_ada=m_b_ada, v_w_in=v_w_in, v_s5_lambda_re=v_s5_lambda_re, v_s5_lambda_im=v_s5_lambda_im, v_s5_log_dt=v_s5_log_dt, v_s5_b_re=v_s5_b_re, v_s5_b_im=v_s5_b_im, v_s5_c_re=v_s5_c_re, v_s5_c_im=v_s5_c_im, v_s5_d=v_s5_d, v_s5_w_glu=v_s5_w_glu, v_mla_q_norm=v_mla_q_norm, v_mla_w_uq=v_mla_w_uq, v_mla_kv_norm=v_mla_kv_norm, v_mla_w_ukv=v_mla_w_ukv, v_hg_lb_logits=v_hg_lb_logits, v_hg_out_norm=v_hg_out_norm, v_w_out=v_w_out, v_mix_pre_norm=v_mix_pre_norm, v_mix_post_norm=v_mix_post_norm, v_ffn_pre_norm=v_ffn_pre_norm, v_ffn_post_norm=v_ffn_post_norm, v_ffn_w_up=v_ffn_w_up, v_ffn_conv_w=v_ffn_conv_w, v_ffn_conv_b=v_ffn_conv_b, v_ffn_w_down=v_ffn_w_down, v_w_ada=v_w_ada, v_b_ada=v_b_ada)
    weights = {n: given[n] for n in TWIN_WEIGHTS}
    shared = {n: given[n] for n in SHARED_INPUTS}
    per_example = {n: given[n] for n in ['x', 'c', 'positions']}
    grad_fn = _jax.value_and_grad(_loss, argnums=(0, 1))

    def one_microbatch(ex, loss_target):
        ex = dict(ex)
        diff = ex.pop(TWIN_DIFF_INPUT)
        return grad_fn(weights, diff, {**shared, **ex}, loss_target)

    if N_MICROBATCH == 1:
        loss, (grad_w, grad_x) = one_microbatch(per_example, given["loss_target"])
    else:
        def body(carry, xs):
            loss_sum, grad_sum = carry
            l_k, (gw_k, gx_k) = one_microbatch(xs[0], xs[1])
            with _jax.named_scope("update"):
                return (loss_sum + l_k, _jax.tree.map(_jnp.add, grad_sum, gw_k)), gx_k

        init = (_jnp.zeros((), _jnp.float32), _jax.tree.map(_jnp.zeros_like, weights))
        (loss, grad_w), grad_x = _jax.lax.scan(body, init, (per_example, given["loss_target"]))
    with _jax.named_scope("update"):
        delta_w, new_m, new_v = {}, {}, {}
        for n in TWIN_WEIGHTS:
            delta_w[n], new_m[n], new_v[n] = _adamw(weights[n], grad_w[n], given["m_" + n], given["v_" + n])
    return (loss, grad_x, *[grad_w[n] for n in TWIN_WEIGHTS], *[delta_w[n] for n in TWIN_WEIGHTS],
            *[new_m[n] for n in TWIN_WEIGHTS], *[new_v[n] for n in TWIN_WEIGHTS])
```

```python
import functools
import math

import numpy as np
import jax
import jax.numpy as jnp
from jax import lax
from jax.experimental import pallas as pl
from jax.experimental.pallas import tpu as pltpu

F32 = jnp.float32
BF16 = jnp.bfloat16
MESH = pl.DeviceIdType.MESH

D_MODEL = 2048
S5_WIDTH, S5_GROUP, S5_GROUPS, S5_STATE = 512, 16, 32, 64
MLA_HEADS, MLA_NOPE, MLA_ROPE, MLA_V = 8, 128, 64, 128
MLA_Q_RANK, MLA_KV_RANK = 512, 256
MLA_WIDTH = MLA_HEADS * MLA_V
ROPE_THETA = 10000.0
HG_HEADS, HG_DK, HG_DV = 4, 128, 128
HG_WIDTH = HG_HEADS * HG_DV
EPS = 1e-6
ADAM_LR, ADAM_B1, ADAM_B2, ADAM_EPS, ADAM_WD, ADAM_STEP = 0.001, 0.9, 0.999, 1e-08, 0.01, 10
GELU_K0 = math.sqrt(2.0 / math.pi)
GELU_K1 = 0.044715

LANE = 128
SUBLANE = 8
VMEM_LIMIT = 56 * 1024 * 1024

P_S5, P_CQ, P_HQ, P_HF, P_HI, P_HG, P_CKV, P_KR = 0, 512, 1024, 1536, 2048, 2560, 3072, 3328
PROJ_W = 3584
N_STATE = S5_GROUPS * S5_STATE
SCAN_W = 512
HG_CHUNK = 64
N_CHIPS = 4
FLAT_COLS = 1024


def _sigmoid(x):
    return 1.0 / (1.0 + jnp.exp(-x))


def _silu(x):
    return x * _sigmoid(x)


def _dsilu(x):
    s = _sigmoid(x)
    return s * (1.0 + x * (1.0 - s))


def _gelu(x):
    return 0.5 * x * (1.0 + jnp.tanh(GELU_K0 * (x + GELU_K1 * x * x * x)))


def _dgelu(x):
    t = jnp.tanh(GELU_K0 * (x + GELU_K1 * x * x * x))
    return 0.5 * (1.0 + t) + 0.5 * x * (1.0 - t * t) * GELU_K0 * (1.0 + 3.0 * GELU_K1 * x * x)


def _colsum(v):
    return jnp.sum(v, axis=0, keepdims=True)


def _rowmean(v):
    return jnp.mean(v, axis=-1, keepdims=True)


def _dot(a, b, dims):
    return lax.dot_general(a.astype(BF16), b.astype(BF16), (dims, ((), ())), preferred_element_type=F32)


NN = ((1,), (0,))
NT = ((1,), (1,))
TN = ((0,), (0,))


def _pcall(body, **kw):
    return pl.pallas_call(body, **kw)


def _params(sem):
    return pltpu.CompilerParams(dimension_semantics=sem, vmem_limit_bytes=VMEM_LIMIT)


def _tile(dim, pref):
    if dim <= pref:
        return dim
    t = (pref // LANE) * LANE
    while t > LANE and dim % t:
        t -= LANE
    assert dim % t == 0, (dim, pref)
    return t


def _mm(a, b, *, mode, name, m, n, k, a_off=0, b_off=0, tm=512, tn=512, tk=512,
        out=((F32),), epi=None, extras=()):
    tm, tn, tk = _tile(m, tm), _tile(n, tn), _tile(k, tk)
    nk = k // tk
    dims = {"nn": NN, "nt": NT, "tn": TN}[mode]
    if mode == "tn":
        assert a_off % tm == 0 and b_off % tn == 0
        a_spec = pl.BlockSpec((tk, tm), lambda i, j, kk: (kk, i + a_off // tm))
        b_spec = pl.BlockSpec((tk, tn), lambda i, j, kk: (kk, j + b_off // tn))
    else:
        assert a_off % tk == 0 and b_off == 0
        a_spec = pl.BlockSpec((tm, tk), lambda i, j, kk: (i, kk + a_off // tk))
        if mode == "nn":
            b_spec = pl.BlockSpec((tk, tn), lambda i, j, kk: (kk, j))
        else:
            b_spec = pl.BlockSpec((tn, tk), lambda i, j, kk: (j, kk))
    in_specs, ex_arrays = [a_spec, b_spec], []
    for e in extras:
        if e[0] == "tile":
            off = e[2] // tn
            assert e[2] % tn == 0
            in_specs.append(pl.BlockSpec((tm, tn), lambda i, j, kk, off=off: (i, j + off)))
        else:
            in_specs.append(pl.BlockSpec((e[1].shape[0], tn), lambda i, j, kk: (0, j)))
        ex_arrays.append(e[1])
    n_ex = len(ex_arrays)
    n_out = len(out)

    def body(*refs):
        a_ref, b_ref = refs[0], refs[1]
        ex_refs = refs[2:2 + n_ex]
        o_refs = refs[2 + n_ex:2 + n_ex + n_out]
        acc_ref = refs[-1]
        kk = pl.program_id(2)

        @pl.when(kk == 0)
        def _():
            acc_ref[...] = jnp.zeros_like(acc_ref)

        acc_ref[...] += _dot(a_ref[...], b_ref[...], dims)

        @pl.when(kk == nk - 1)
        def _():
            acc = acc_ref[...]
            if epi is None:
                o_refs[0][...] = acc.astype(o_refs[0].dtype)
            else:
                vals = epi(acc, *[r[...] for r in ex_refs])
                for r, v in zip(o_refs, vals):
                    r[...] = v.astype(r.dtype)

    res = _pcall(
        body, name=name,
        out_shape=tuple(jax.ShapeDtypeStruct((m, n), d) for d in out),
        grid=(m // tm, n // tn, nk),
        in_specs=in_specs,
        out_specs=tuple(pl.BlockSpec((tm, tn), lambda i, j, kk: (i, j)) for _ in out),
        scratch_shapes=[pltpu.VMEM((tm, tn), F32)],
        compiler_params=_params(("parallel", "parallel", "arbitrary")),
    )(a, b, *ex_arrays)
    return res[0] if n_out == 1 else res


def _rows(fn, *, name, s, tm, ins, outs, ncb=1):
    tm = min(tm, s)
    assert s % tm == 0 and tm % SUBLANE == 0
    ni = s // tm
    r8 = tm // SUBLANE
    in_specs, arrays = [], []
    for e in ins:
        kind, arr = e[0], e[1]
        if kind in ("row", "prev8", "next8", "vecb"):
            off, w = e[2] // e[3], e[3]
            assert e[2] % e[3] == 0
        if kind == "row":
            in_specs.append(pl.BlockSpec((tm, w), lambda j, i, off=off: (i, off + j)))
        elif kind == "prev8":
            in_specs.append(pl.BlockSpec((SUBLANE, w), lambda j, i, off=off: (jnp.maximum(i * r8 - 1, 0), off + j)))
        elif kind == "next8":
            last = s // SUBLANE - 1
            in_specs.append(pl.BlockSpec((SUBLANE, w), lambda j, i, off=off: (jnp.minimum((i + 1) * r8, last), off + j)))
        elif kind == "vec":
            in_specs.append(pl.BlockSpec(arr.shape, lambda j, i, nd=arr.ndim: (0,) * nd))
        else:
            in_specs.append(pl.BlockSpec((arr.shape[0], w), lambda j, i, off=off: (0, off + j)))
        arrays.append(arr)
    out_shape, out_specs = [], []
    for e in outs:
        if e[0] == "row":
            out_shape.append(jax.ShapeDtypeStruct((s, ncb * e[1]), e[2]))
            out_specs.append(pl.BlockSpec((tm, e[1]), lambda j, i: (i, j)))
        else:
            out_shape.append(jax.ShapeDtypeStruct((e[1], ncb * e[2]), F32))
            out_specs.append(pl.BlockSpec((e[1], e[2]), lambda j, i: (0, j)))

    def body(*refs):
        fn(pl.program_id(1), *refs)

    res = _pcall(
        body, name=name, out_shape=tuple(out_shape), grid=(ncb, ni),
        in_specs=in_specs, out_specs=tuple(out_specs),
        compiler_params=_params(("parallel", "arbitrary")),
    )(*arrays)
    return res[0] if len(outs) == 1 else res


def _acc(ref, i, val):
    @pl.when(i == 0)
    def _():
        ref[...] = val

    @pl.when(i > 0)
    def _():
        ref[...] += val


def _normmod_fwd(x, gain, sc, sh, name):
    s, d = x.shape

    def fn(i, x_ref, g_ref, sc_ref, sh_ref, h_ref):
        xv = x_ref[...]
        r = lax.rsqrt(_rowmean(xv * xv) + EPS)
        h_ref[...] = (((xv * r) * g_ref[...]) * (1.0 + sc_ref[...]) + sh_ref[...]).astype(h_ref.dtype)

    return _rows(fn, name=name, s=s, tm=256, ins=[("row", x, 0, d), ("vec", gain), ("vec", sc), ("vec", sh)],
                 outs=[("row", d, BF16)])


def _normmod_bwd(dh, x, gain, sc, dx_add, name):
    s, d = x.shape

    def fn(i, dh_ref, x_ref, g_ref, sc_ref, add_ref, dx_ref, dg_ref, dsc_ref, dsh_ref):
        xv, dhv = x_ref[...], dh_ref[...]
        r = lax.rsqrt(_rowmean(xv * xv) + EPS)
        xn = xv * r
        gain_v, one_sc = g_ref[...], 1.0 + sc_ref[...]
        ghat = dhv * gain_v * one_sc
        dx_ref[...] = r * (ghat - xn * _rowmean(ghat * xn)) + add_ref[...]
        _acc(dg_ref, i, _colsum(dhv * xn * one_sc))
        _acc(dsc_ref, i, _colsum(dhv * xn * gain_v))
        _acc(dsh_ref, i, _colsum(dhv))

    return _rows(fn, name=name, s=s, tm=256,
                 ins=[("row", dh, 0, d), ("row", x, 0, d), ("vec", gain), ("vec", sc), ("row", dx_add, 0, d)],
                 outs=[("row", d, F32), ("acc", 1, d), ("acc", 1, d), ("acc", 1, d)])


def _postnorm_fwd(x, m, gain, gate, name):
    s, d = x.shape

    def fn(i, x_ref, m_ref, g_ref, gate_ref, o_ref):
        mv = m_ref[...]
        r = lax.rsqrt(_rowmean(mv * mv) + EPS)
        o_ref[...] = x_ref[...] + gate_ref[...] * ((mv * r) * g_ref[...])

    return _rows(fn, name=name, s=s, tm=256, ins=[("row", x, 0, d), ("row", m, 0, d), ("vec", gain), ("vec", gate)],
                 outs=[("row", d, F32)])


def _postnorm_bwd(dxo, m, gain, gate, name):
    s, d = m.shape

    def fn(i, dx_ref, m_ref, g_ref, gate_ref, dm_ref, dg_ref, dgate_ref):
        mv, dxv = m_ref[...], dx_ref[...]
        r = lax.rsqrt(_rowmean(mv * mv) + EPS)
        mn = mv * r
        gain_v, gate_v = g_ref[...], gate_ref[...]
        ghat = dxv * gate_v * gain_v
        dm_ref[...] = (r * (ghat - mn * _rowmean(ghat * mn))).astype(dm_ref.dtype)
        _acc(dg_ref, i, _colsum(dxv * gate_v * mn))
        _acc(dgate_ref, i, _colsum(dxv * mn * gain_v))

    return _rows(fn, name=name, s=s, tm=256, ins=[("row", dxo, 0, d), ("row", m, 0, d), ("vec", gain), ("vec", gate)],
                 outs=[("row", d, BF16), ("acc", 1, d), ("acc", 1, d)])


def _rms_fwd(src, off, w, gain, name):
    s = src.shape[0]

    def fn(i, x_ref, g_ref, o_ref):
        xv = x_ref[...]
        r = lax.rsqrt(_rowmean(xv * xv) + EPS)
        o_ref[...] = ((xv * r) * g_ref[...]).astype(o_ref.dtype)

    return _rows(fn, name=name, s=s, tm=512, ins=[("row", src, off, w), ("vec", gain)], outs=[("row", w, BF16)])


def _rms_bwd(dy, src, off, w, gain, name):
    s = src.shape[0]

    def fn(i, dy_ref, x_ref, g_ref, dx_ref, dg_ref):
        xv, dyv = x_ref[...], dy_ref[...]
        r = lax.rsqrt(_rowmean(xv * xv) + EPS)
        xn = xv * r
        ghat = dyv * g_ref[...]
        dx_ref[...] = r * (ghat - xn * _rowmean(ghat * xn))
        _acc(dg_ref, i, _colsum(dyv * xn))

    return _rows(fn, name=name, s=s, tm=512, ins=[("row", dy, 0, w), ("row", src, off, w), ("vec", gain)],
                 outs=[("row", w, F32), ("acc", 1, w)])


def _loss_grad(x, target):
    s, d = x.shape

    def fn(i, x_ref, t_ref, dx_ref, l_ref):
        diff = x_ref[...] - t_ref[...]
        dx_ref[...] = diff * (1.0 / d)
        part = _colsum(jnp.sum(diff * diff, axis=1, keepdims=True)) * (0.5 / d)
        _acc(l_ref, i, jnp.broadcast_to(part, (1, LANE)))

    return _rows(fn, name="loss_grad", s=s, tm=256, ins=[("row", x, 0, d), ("row", target, 0, d)],
                 outs=[("row", d, F32), ("acc", 1, LANE)])


FFN_WC = 512


def _shift_rows(xv, h_ref, i, row, k):
    out = pltpu.roll(xv, k, 0)
    for r in range(k):
        hrow = jnp.where(i > 0, h_ref[SUBLANE - k + r:SUBLANE - k + r + 1, :], 0.0)
        out = jnp.where(row == r, hrow, out)
    return out


def _conv_rows(x_ref, h_ref, w_ref, b_ref, i, row):
    xv = x_ref[...]
    s1, s2 = _shift_rows(xv, h_ref, i, row, 1), _shift_rows(xv, h_ref, i, row, 2)
    u = ((b_ref[...] + s2 * w_ref[0:1, :]) + s1 * w_ref[1:2, :]) + xv * w_ref[2:3, :]
    return u, s1, s2, xv


def _ffn_act_fwd(up, conv_w, conv_b, ffp):
    s = up.shape[0]
    wc, ncb = FFN_WC, ffp // FFN_WC

    def fn(i, g_ref, gh_ref, v_ref, vh_ref, wg_ref, wv_ref, bg_ref, bv_ref, a_ref):
        row = lax.broadcasted_iota(jnp.int32, g_ref.shape, 0)
        ug = _conv_rows(g_ref, gh_ref, wg_ref, bg_ref, i, row)[0]
        uv = _conv_rows(v_ref, vh_ref, wv_ref, bv_ref, i, row)[0]
        a_ref[...] = (_gelu(ug) * uv).astype(a_ref.dtype)

    return _rows(fn, name="ffn_act_fwd", s=s, tm=512, ncb=ncb,
                 ins=[("row", up, 0, wc), ("prev8", up, 0, wc), ("row", up, ffp, wc), ("prev8", up, ffp, wc),
                      ("vecb", conv_w, 0, wc), ("vecb", conv_w, ffp, wc), ("vecb", conv_b, 0, wc), ("vecb", conv_b, ffp, wc)],
                 outs=[("row", wc, BF16)])


def _ffn_act_bwd(da, up, conv_w, conv_b, ffp):
    s = up.shape[0]
    wc, ncb = FFN_WC, ffp // FFN_WC

    def fn(i, da_ref, g_ref, gh_ref, v_ref, vh_ref, wg_ref, wv_ref, bg_ref, bv_ref,
           dug_ref, duv_ref, dwg_ref, dwv_ref, dbg_ref, dbv_ref):
        row = lax.broadcasted_iota(jnp.int32, g_ref.shape, 0)
        ug, g1, g2, g0 = _conv_rows(g_ref, gh_ref, wg_ref, bg_ref, i, row)
        uv, v1, v2, v0 = _conv_rows(v_ref, vh_ref, wv_ref, bv_ref, i, row)
        dav = da_ref[...]
        dug = dav * uv * _dgelu(ug)
        duv = dav * _gelu(ug)
        dug_ref[...] = dug
        duv_ref[...] = duv
        for r, (gt, vt) in enumerate(((g2, v2), (g1, v1), (g0, v0))):
            _acc(dwg_ref.at[r:r + 1, :], i, _colsum(dug * gt))
            _acc(dwv_ref.at[r:r + 1, :], i, _colsum(duv * vt))
        _acc(dbg_ref, i, _colsum(dug))
        _acc(dbv_ref, i, _colsum(duv))

    return _rows(fn, name="ffn_act_bwd", s=s, tm=512, ncb=ncb,
                 ins=[("row", da, 0, wc), ("row", up, 0, wc), ("prev8", up, 0, wc), ("row", up, ffp, wc), ("prev8", up, ffp, wc),
                      ("vecb", conv_w, 0, wc), ("vecb", conv_w, ffp, wc), ("vecb", conv_b, 0, wc), ("vecb", conv_b, ffp, wc)],
                 outs=[("row", wc, F32), ("row", wc, F32), ("acc", 3, wc), ("acc", 3, wc), ("acc", 1, wc), ("acc", 1, wc)])


def _conv_bwd_input(du, conv_w, w_off, name):
    s, ffp = du.shape
    wc, ncb = FFN_WC, ffp // FFN_WC
    ni = s // min(512, s)

    def fn(i, du_ref, nx_ref, w_ref, o_ref):
        dv = du_ref[...]
        tm = dv.shape[0]
        row = lax.broadcasted_iota(jnp.int32, dv.shape, 0)
        n0 = jnp.where(i < ni - 1, nx_ref[0:1, :], 0.0)
        n1 = jnp.where(i < ni - 1, nx_ref[1:2, :], 0.0)
        u1 = jnp.where(row == tm - 1, n0, pltpu.roll(dv, tm - 1, 0))
        u2 = jnp.where(row == tm - 1, n1, jnp.where(row == tm - 2, n0, pltpu.roll(dv, tm - 2, 0)))
        o_ref[...] = (dv * w_ref[2:3, :] + u1 * w_ref[1:2, :] + u2 * w_ref[0:1, :]).astype(o_ref.dtype)

    return _rows(fn, name=name, s=s, tm=512, ncb=ncb,
                 ins=[("row", du, 0, wc), ("next8", du, 0, wc), ("vecb", conv_w, w_off, wc)],
                 outs=[("row", wc, BF16)])


def _cmul(ar, ai, br, bi):
    return ar * br - ai * bi, ar * bi + ai * br


def _s5_scan(x, a, *, reverse, h=None, name):
    s = x.shape[0]
    w = SCAN_W
    t_rows = min(256, s)
    nt = s // t_rows
    ncol = N_STATE // w
    nbits = t_rows.bit_length()
    r8 = t_rows // SUBLANE

    def tblk(t):
        return nt - 1 - t if reverse else t

    def body(*refs):
        if reverse:
            x_ref, a_ref, h_ref, hh_ref, o_ref, da_ref, carry, ptab = refs
        else:
            x_ref, a_ref, o_ref, carry, ptab = refs
        t = pl.program_id(1)
        row = lax.broadcasted_iota(jnp.int32, (t_rows, w), 0)
        idx = (t_rows - 1 - row) if reverse else row
        ar = a_ref[:, :w]
        ai = -a_ref[:, w:] if reverse else a_ref[:, w:]
        pows = [(ar, ai)]
        for _ in range(nbits - 1):
            pows.append(_cmul(*pows[-1], *pows[-1]))

        @pl.when(t == 0)
        def _():
            carry[...] = jnp.zeros_like(carry)
            pr, pi = jnp.ones((t_rows, w), F32), jnp.zeros((t_rows, w), F32)
            for kbit in range(nbits):
                bit = ((idx + 1) >> kbit) & 1
                fr = jnp.where(bit == 1, pows[kbit][0], 1.0)
                fi = jnp.where(bit == 1, pows[kbit][1], 0.0)
                pr, pi = _cmul(pr, pi, fr, fi)
            ptab[:, :w] = pr
            ptab[:, w:] = pi

        xr, xi = x_ref[:, :w], x_ref[:, w:]
        step = 1
        kbit = 0
        while step < t_rows:
            shift = (t_rows - step) if reverse else step
            yr, yi = pltpu.roll(xr, shift, 0), pltpu.roll(xi, shift, 0)
            zr, zi = _cmul(pows[kbit][0], pows[kbit][1], yr, yi)
            keep = idx >= step
            xr = xr + jnp.where(keep, zr, 0.0)
            xi = xi + jnp.where(keep, zi, 0.0)
            step *= 2
            kbit += 1
        cr, ci = carry[0:1, :w], carry[0:1, w:]
        zr, zi = _cmul(ptab[:, :w], ptab[:, w:], cr, ci)
        xr, xi = xr + zr, xi + zi
        o_ref[:, :w] = xr
        o_ref[:, w:] = xi
        last = 0 if reverse else t_rows - 1
        carry[0:1, :] = o_ref[last:last + 1, :]
        if reverse:
            halo_r = jnp.where(t < nt - 1, hh_ref[SUBLANE - 1:SUBLANE, :w], 0.0)
            halo_i = jnp.where(t < nt - 1, hh_ref[SUBLANE - 1:SUBLANE, w:], 0.0)
            hr = jnp.where(row == 0, halo_r, pltpu.roll(h_ref[:, :w], 1, 0))
            hi = jnp.where(row == 0, halo_i, pltpu.roll(h_ref[:, w:], 1, 0))
            _acc(da_ref.at[:, :w], t, _colsum(xr * hr + xi * hi))
            _acc(da_ref.at[:, w:], t, _colsum(xi * hr - xr * hi))

    blk = pl.BlockSpec((t_rows, 2 * w), lambda j, t: (tblk(t), j))
    a_spec = pl.BlockSpec((1, 2 * w), lambda j, t: (0, j))
    in_specs, arrays = [blk, a_spec], [x, a]
    out_shape = [jax.ShapeDtypeStruct((s, 2 * N_STATE), F32)]
    out_specs = [blk]
    if reverse:
        in_specs += [blk, pl.BlockSpec((SUBLANE, 2 * w), lambda j, t: (jnp.maximum(tblk(t) * r8 - 1, 0), j))]
        arrays += [h, h]
        out_shape.append(jax.ShapeDtypeStruct((1, 2 * N_STATE), F32))
        out_specs.append(a_spec)
    res = _pcall(
        body, name=name, out_shape=tuple(out_shape), grid=(ncol, nt), in_specs=in_specs, out_specs=tuple(out_specs),
        scratch_shapes=[pltpu.VMEM((SUBLANE, 2 * w), F32), pltpu.VMEM((t_rows, 2 * w), F32)],
        compiler_params=_params(("parallel", "arbitrary")),
    )(*arrays)
    return res if reverse else res[0]


def _s5_glu_bwd_a(dout, dout_off, y, z):
    s = y.shape[0]
    w = S5_WIDTH

    def fn(i, do_ref, y_ref, z_ref, dz_ref, p_ref):
        dov = do_ref[...]
        sg = _sigmoid(z_ref[...])
        dz_ref[...] = (dov * _gelu(y_ref[...]) * sg * (1.0 - sg)).astype(dz_ref.dtype)
        p_ref[...] = dov * sg

    return _rows(fn, name="s5_glu_bwd", s=s, tm=512, ins=[("row", dout, dout_off, w), ("row", y, 0, w), ("row", z, 0, w)],
                 outs=[("row", w, BF16), ("row", w, F32)])


def _s5_dd(dy, proj):
    s = dy.shape[0]
    w = S5_WIDTH

    def fn(i, dy_ref, u_ref, dd_ref):
        _acc(dd_ref, i, _colsum(dy_ref[...] * u_ref[...]))

    return _rows(fn, name="s5_dd", s=s, tm=512, ins=[("row", dy, 0, w), ("row", proj, P_S5, w)], outs=[("acc", 1, w)])


def _s5_fwd(proj, wl, s):
    bu = _mm(proj, wl["s5_bd"], mode="nn", name="s5_bu", m=s, n=2 * N_STATE, k=S5_WIDTH, a_off=P_S5)
    h = _s5_scan(bu, wl["s5_a"], reverse=False, name="s5_scan_fwd")
    def y_epi(acc, u, d):
        yv = acc + d * u
        return yv, _gelu(yv)

    y, yg = _mm(h, wl["s5_cd"], mode="nn", name="s5_y", m=s, n=S5_WIDTH, k=2 * N_STATE, out=(F32, BF16),
                extras=[("tile", proj, P_S5), ("row", wl["s5_d"])], epi=y_epi)
    z, out = _mm(yg, wl["s5_w_glu"], mode="nn", name="s5_glu", m=s, n=S5_WIDTH, k=S5_WIDTH, out=(F32, BF16),
                 extras=[("tile", y, 0)], epi=lambda acc, yv: (acc, _gelu(yv) * _sigmoid(acc)))
    return out, (h, y, z, yg)


def _s5_bwd(dcat, proj, wl, saved, s):
    h, y, z, yg = saved
    dz, p1 = _s5_glu_bwd_a(dcat, 0, y, z)
    dy = _mm(dz, wl["s5_w_glu"], mode="nt", name="s5_dyg", m=s, n=S5_WIDTH, k=S5_WIDTH,
             extras=[("tile", p1, 0), ("tile", y, 0)], epi=lambda acc, p, yv: ((p + acc) * _dgelu(yv),))
    gh = _mm(dy, wl["s5_cd"], mode="nt", name="s5_gh", m=s, n=2 * N_STATE, k=S5_WIDTH)
    adj, da = _s5_scan(gh, wl["s5_a"], reverse=True, h=h, name="s5_scan_bwd")
    du = _mm(adj, wl["s5_bd"], mode="nt", name="s5_du", m=s, n=S5_WIDTH, k=2 * N_STATE,
             extras=[("tile", dy, 0), ("row", wl["s5_d"])], epi=lambda acc, dyv, d: (acc + dyv * d,))
    grads = {
        "s5_a": da,
        "s5_bd": _mm(proj, adj, mode="tn", name="s5_dbd", m=S5_WIDTH, n=2 * N_STATE, k=s, a_off=P_S5),
        "s5_cd": _mm(h, dy, mode="tn", name="s5_dcd", m=2 * N_STATE, n=S5_WIDTH, k=s),
        "s5_d": _s5_dd(dy, proj),
        "s5_w_glu": _mm(yg, dz, mode="tn", name="s5_dwglu", m=S5_WIDTH, n=S5_WIDTH, k=s),
    }
    return du, grads


def _mla_prep(qraw, kvraw, proj, cs):
    s = qraw.shape[0]
    hw = MLA_HEADS * LANE

    def fn(i, q_ref, kv_ref, kr_ref, cs_ref, qn_ref, qr_ref, kvb_ref, krb_ref):
        csv = cs_ref[...]
        qn_ref[...] = q_ref[:, :hw].astype(BF16)
        for hd in range(MLA_HEADS):
            p = q_ref[:, hw + hd * LANE:hw + (hd + 1) * LANE] * csv
            qr_ref[:, hd * LANE:(hd + 1) * LANE] = (p + pltpu.roll(p, LANE // 2, 1)).astype(BF16)
        kvb_ref[...] = kv_ref[...].astype(BF16)
        p = kr_ref[...] * csv
        lane = lax.broadcasted_iota(jnp.int32, p.shape, 1)
        krb_ref[...] = jnp.where(lane < LANE // 2, p + pltpu.roll(p, LANE // 2, 1), 0.0).astype(BF16)

    return _rows(fn, name="mla_prep", s=s, tm=256,
                 ins=[("row", qraw, 0, 2 * hw), ("row", kvraw, 0, 2 * hw), ("row", proj, P_KR, LANE), ("row", cs, 0, LANE)],
                 outs=[("row", hw, BF16), ("row", hw, BF16), ("row", 2 * hw, BF16), ("row", LANE, BF16)])


def _mla_rope_bwd(dqn, dqr2, dkr2h, cs):
    s = dqn.shape[0]
    hw = MLA_HEADS * LANE

    def fn(i, dqn_ref, dqr_ref, dkr_ref, cs_ref, dq_ref, dk_ref):
        csv = cs_ref[...]
        dq_ref[:, :hw] = dqn_ref[...].astype(BF16)
        ksum = jnp.zeros(csv.shape, F32)
        for hd in range(MLA_HEADS):
            g = dqr_ref[:, hd * LANE:(hd + 1) * LANE]
            dq_ref[:, hw + hd * LANE:hw + (hd + 1) * LANE] = ((g + pltpu.roll(g, LANE // 2, 1)) * csv).astype(BF16)
            ksum = ksum + dkr_ref[:, hd * LANE:(hd + 1) * LANE]
        dk_ref[...] = ksum * csv

    return _rows(fn, name="mla_rope_bwd", s=s, tm=256,
                 ins=[("row", dqn, 0, hw), ("row", dqr2, 0, hw), ("row", dkr2h, 0, hw), ("row", cs, 0, LANE)],
                 outs=[("row", 2 * hw, BF16), ("row", LANE, F32)])


def _attn_scores(qn_ref, qr_ref, kn_ref, kr_ref, qi, ki, tq, tk):
    scale = (MLA_NOPE + MLA_ROPE) ** -0.5
    sc = (_dot(qn_ref[...], kn_ref[...], NT) + _dot(qr_ref[...], kr_ref[...], NT)) * scale
    rows = qi * tq + lax.broadcasted_iota(jnp.int32, (tq, tk), 0)
    cols = ki * tk + lax.broadcasted_iota(jnp.int32, (tq, tk), 1)
    return sc, cols <= rows


def _attn_specs(tq, tk, q_of, k_of):
    qs = pl.BlockSpec((tq, LANE), lambda h, a, b: (q_of(a, b), h))
    return [qs, qs,
            pl.BlockSpec((tk, LANE), lambda h, a, b: (k_of(a, b), 2 * h)),
            pl.BlockSpec((tk, LANE), lambda h, a, b: (k_of(a, b), 2 * h + 1)),
            pl.BlockSpec((tk, LANE), lambda h, a, b: (k_of(a, b), 0))]


def _flash_fwd(qn, qr2, kv, kr2):
    s = qn.shape[0]
    tq = tk = min(512, s)
    nq = s // tq

    def body(qn_ref, qr_ref, kn_ref, v_ref, kr_ref, o_ref, lse_ref, m_sc, l_sc, acc_sc):
        qi, ki = pl.program_id(1), pl.program_id(2)

        @pl.when(ki == 0)
        def _():
            m_sc[...] = jnp.full(m_sc.shape, -jnp.inf, F32)
            l_sc[...] = jnp.zeros_like(l_sc)
            acc_sc[...] = jnp.zeros_like(acc_sc)

        @pl.when(ki <= qi)
        def _():
            sc, causal = _attn_scores(qn_ref, qr_ref, kn_ref, kr_ref, qi, ki, tq, tk)
            sc = jnp.where(causal, sc, -1e30)
            m_new = jnp.maximum(m_sc[...], jnp.max(sc, axis=1, keepdims=True))
            alpha = jnp.exp(m_sc[...] - m_new)
            p = jnp.exp(sc - m_new)
            l_sc[...] = alpha * l_sc[...] + jnp.sum(p, axis=1, keepdims=True)
            acc_sc[...] = alpha * acc_sc[...] + _dot(p, v_ref[...], NN)
            m_sc[...] = m_new

        @pl.when(ki == qi)
        def _():
            o_ref[...] = acc_sc[...] / l_sc[...]
            lse_ref[0] = m_sc[...] + jnp.log(l_sc[...])

    return _pcall(
        body, name="mla_flash_fwd",
        out_shape=(jax.ShapeDtypeStruct((s, MLA_WIDTH), F32), jax.ShapeDtypeStruct((MLA_HEADS, s, 1), F32)),
        grid=(MLA_HEADS, nq, nq),
        in_specs=_attn_specs(tq, tk, lambda a, b: a, lambda a, b: jnp.minimum(a, b)),
        out_specs=(pl.BlockSpec((tq, LANE), lambda h, a, b: (a, h)), pl.BlockSpec((1, tq, 1), lambda h, a, b: (h, a, 0))),
        scratch_shapes=[pltpu.VMEM((tq, 1), F32), pltpu.VMEM((tq, 1), F32), pltpu.VMEM((tq, LANE), F32)],
        compiler_params=_params(("parallel", "parallel", "arbitrary")),
    )(qn, qr2, kv, kv, kr2)


def _flash_bwd_dq(qn, qr2, kv, kr2, do, do_off, o, lse):
    s = qn.shape[0]
    tq = tk = min(512, s)
    nq = s // tq
    scale = (MLA_NOPE + MLA_ROPE) ** -0.5
    ob = do_off // LANE

    def body(qn_ref, qr_ref, kn_ref, v_ref, kr_ref, do_ref, o_ref, lse_ref, dqn_ref, dqr_ref, dl_ref, dl_sc, an_sc, ar_sc):
        qi, ki = pl.program_id(1), pl.program_id(2)

        @pl.when(ki == 0)
        def _():
            dl_sc[...] = jnp.sum(do_ref[...] * o_ref[...], axis=1, keepdims=True)
            an_sc[...] = jnp.zeros_like(an_sc)
            ar_sc[...] = jnp.zeros_like(ar_sc)

        @pl.when(ki <= qi)
        def _():
            sc, causal = _attn_scores(qn_ref, qr_ref, kn_ref, kr_ref, qi, ki, tq, tk)
            p = jnp.where(causal, jnp.exp(sc - lse_ref[0]), 0.0)
            dp = _dot(do_ref[...], v_ref[...], NT)
            ds = (p * (dp - dl_sc[...]) * scale).astype(BF16)
            an_sc[...] += _dot(ds, kn_ref[...], NN)
            ar_sc[...] += _dot(ds, kr_ref[...], NN)

        @pl.when(ki == qi)
        def _():
            dqn_ref[...] = an_sc[...]
            dqr_ref[...] = ar_sc[...]
            dl_ref[0] = dl_sc[...]

    qblk = pl.BlockSpec((tq, LANE), lambda h, a, b: (a, h))
    vec = pl.BlockSpec((1, tq, 1), lambda h, a, b: (h, a, 0))
    return _pcall(
        body, name="mla_flash_dq",
        out_shape=(jax.ShapeDtypeStruct((s, MLA_WIDTH), F32), jax.ShapeDtypeStruct((s, MLA_WIDTH), F32),
                   jax.ShapeDtypeStruct((MLA_HEADS, s, 1), F32)),
        grid=(MLA_HEADS, nq, nq),
        in_specs=_attn_specs(tq, tk, lambda a, b: a, lambda a, b: jnp.minimum(a, b))
        + [pl.BlockSpec((tq, LANE), lambda h, a, b: (a, h + ob)), qblk, vec],
        out_specs=(qblk, qblk, vec),
        scratch_shapes=[pltpu.VMEM((tq, 1), F32), pltpu.VMEM((tq, LANE), F32), pltpu.VMEM((tq, LANE), F32)],
        compiler_params=_params(("parallel", "parallel", "arbitrary")),
    )(qn, qr2, kv, kv, kr2, do, o, lse)


def _flash_bwd_dkv(qn, qr2, kv, kr2, do, do_off, lse, delta):
    s = qn.shape[0]
    tq = tk = min(512, s)
    nq = s // tq
    scale = (MLA_NOPE + MLA_ROPE) ** -0.5
    ob = do_off // LANE

    def body(qn_ref, qr_ref, kn_ref, v_ref, kr_ref, do_ref, lse_ref, dl_ref, dkv_ref, dkr_ref, akn_sc, av_sc, akr_sc):
        ki, qi = pl.program_id(1), pl.program_id(2)

        @pl.when(qi == 0)
        def _():
            akn_sc[...] = jnp.zeros_like(akn_sc)
            av_sc[...] = jnp.zeros_like(av_sc)
            akr_sc[...] = jnp.zeros_like(akr_sc)

        @pl.when(qi >= ki)
        def _():
            sc, causal = _attn_scores(qn_ref, qr_ref, kn_ref, kr_ref, qi, ki, tq, tk)
            p = jnp.where(causal, jnp.exp(sc - lse_ref[0]), 0.0)
            dp = _dot(do_ref[...], v_ref[...], NT)
            ds = (p * (dp - dl_ref[0]) * scale).astype(BF16)
            av_sc[...] += _dot(p, do_ref[...], TN)
            akn_sc[...] += _dot(ds, qn_ref[...], TN)
            akr_sc[...] += _dot(ds, qr_ref[...], TN)

        @pl.when(qi == nq - 1)
        def _():
            dkv_ref[:, :LANE] = akn_sc[...]
            dkv_ref[:, LANE:] = av_sc[...]
            dkr_ref[...] = akr_sc[...]

    q_of = lambda a, b: jnp.maximum(a, b)
    k_of = lambda a, b: a
    vec = pl.BlockSpec((1, tq, 1), lambda h, a, b: (h, q_of(a, b), 0))
    return _pcall(
        body, name="mla_flash_dkv",
        out_shape=(jax.ShapeDtypeStruct((s, 2 * MLA_WIDTH), F32), jax.ShapeDtypeStruct((s, MLA_WIDTH), F32)),
        grid=(MLA_HEADS, nq, nq),
        in_specs=_attn_specs(tq, tk, q_of, k_of)
        + [pl.BlockSpec((tq, LANE), lambda h, a, b: (q_of(a, b), h + ob)), vec, vec],
        out_specs=(pl.BlockSpec((tk, 2 * LANE), lambda h, a, b: (a, h)), pl.BlockSpec((tk, LANE), lambda h, a, b: (a, h))),
        scratch_shapes=[pltpu.VMEM((tk, LANE), F32), pltpu.VMEM((tk, LANE), F32), pltpu.VMEM((tk, LANE), F32)],
        compiler_params=_params(("parallel", "parallel", "arbitrary")),
    )(qn, qr2, kv, kv, kr2, do, lse, delta)


def _mla_fwd(proj, wl, cs, s):
    cqn = _rms_fwd(proj, P_CQ, MLA_Q_RANK, wl["mla_q_norm"], "mla_q_rms")
    ckvn = _rms_fwd(proj, P_CKV, MLA_KV_RANK, wl["mla_kv_norm"], "mla_kv_rms")
    qraw = _mm(cqn, wl["mla_wq"], mode="nn", name="mla_q_proj", m=s, n=2 * MLA_WIDTH, k=MLA_Q_RANK)
    kvraw = _mm(ckvn, wl["mla_w_ukv"], mode="nn", name="mla_kv_proj", m=s, n=2 * MLA_WIDTH, k=MLA_KV_RANK)
    qn, qr2, kv, kr2 = _mla_prep(qraw, kvraw, proj, cs)
    o, lse = _flash_fwd(qn, qr2, kv, kr2)
    return o, (cqn, ckvn, qn, qr2, kv, kr2, o, lse)


def _mla_bwd(dcat, proj, wl, cs, saved, s):
    cqn, ckvn, qn, qr2, kv, kr2, o, lse = saved
    dqn, dqr2, delta = _flash_bwd_dq(qn, qr2, kv, kr2, dcat, S5_WIDTH, o, lse)
    dkv, dkr2h = _flash_bwd_dkv(qn, qr2, kv, kr2, dcat, S5_WIDTH, lse, delta)
    dqraw, dkr = _mla_rope_bwd(dqn, dqr2, dkr2h, cs)
    dcqn = _mm(dqraw, wl["mla_wq"], mode="nt", name="mla_dcqn", m=s, n=MLA_Q_RANK, k=2 * MLA_WIDTH)
    dckvn = _mm(dkv, wl["mla_w_ukv"], mode="nt", name="mla_dckvn", m=s, n=MLA_KV_RANK, k=2 * MLA_WIDTH)
    dcq, dqg = _rms_bwd(dcqn, proj, P_CQ, MLA_Q_RANK, wl["mla_q_norm"], "mla_q_rms_bwd")
    dckv, dkvg = _rms_bwd(dckvn, proj, P_CKV, MLA_KV_RANK, wl["mla_kv_norm"], "mla_kv_rms_bwd")
    grads = {
        "mla_wq": _mm(cqn, dqraw, mode="tn", name="mla_dwq", m=MLA_Q_RANK, n=2 * MLA_WIDTH, k=s),
        "mla_w_ukv": _mm(ckvn, dkv, mode="tn", name="mla_dwukv", m=MLA_KV_RANK, n=2 * MLA_WIDTH, k=s),
        "mla_q_norm": dqg,
        "mla_kv_norm": dkvg,
    }
    return dcq, dckv, dkr, grads


HG_ROWS = 256


def _cumsum_rows(x, row, reverse=False):
    n = x.shape[0]
    step = 1
    while step < n:
        if reverse:
            x = x + jnp.where(row < n - step, pltpu.roll(x, n - step, 0), 0.0)
        else:
            x = x + jnp.where(row >= step, pltpu.roll(x, step, 0), 0.0)
        step *= 2
    return x


def _hg_gates(hq, z, lb):
    sig = _sigmoid(z)
    sigm = _sigmoid(-z)
    f = lb + (1.0 - lb) * sig
    return _silu(hq), (1.0 - lb) * sigm, jnp.log(f), sig, sigm, f


HG_SUB = 16
HG_NSUB = HG_CHUNK // HG_SUB


def _hg_offdiag(q, k, b, row, b_buf):
    ops = []
    for j in range(HG_NSUB - 1):
        e = (j + 1) * HG_SUB
        be = b_buf[e - 1:e, :]
        fj = jnp.where(row >= e, jnp.exp(jnp.minimum(b - be, 0.0)), 0.0)
        gj = jnp.where((row >= e - HG_SUB) & (row < e), jnp.exp(jnp.minimum(be - b, 0.0)), 0.0)
        ops.append((q * fj, k * gj, fj, gj))
    return ops


def _hg_diag_weights(qb, bb, ks, bs, rr, srow):
    e = jnp.exp(jnp.minimum(bb - bs, 0.0))
    keep = rr >= srow
    w = jnp.where(keep, jnp.sum(qb * ks * e, axis=1, keepdims=True), 0.0)
    return e, keep, w


def _hg_specs(rows):
    def col(off):
        return pl.BlockSpec((rows, LANE), lambda h, n, off=off: (n, off // LANE + h))
    return col


def _hg_fwd(proj, lb, gamma):
    s = proj.shape[0]
    rows = min(HG_ROWS, s)
    c = HG_CHUNK
    npc = rows // c
    nb = s // rows

    def body(hq_ref, hf_ref, hi_ref, hg_ref, lb_ref, gm_ref, y_ref, o_ref, st_ref, st_sc, k_buf, b_buf):
        n = pl.program_id(1)

        @pl.when(n == 0)
        def _():
            st_sc[...] = jnp.zeros_like(st_sc)

        row = lax.broadcasted_iota(jnp.int32, (c, LANE), 0)
        rr = lax.broadcasted_iota(jnp.int32, (HG_SUB, 1), 0)
        lbv = lb_ref[...]
        for ch in range(npc):
            sl = slice(ch * c, (ch + 1) * c)
            q, k, logf, _, _, _ = _hg_gates(hq_ref[sl, :], hf_ref[sl, :], lbv)
            v = hi_ref[sl, :]
            b = _cumsum_rows(logf, row)
            bl = _colsum(logf)
            k_buf[...] = k
            b_buf[...] = b
            st = st_sc[...]
            st_ref[0, ch] = st
            a_off = sum(_dot(qf, kg, NT) for qf, kg, _, _ in _hg_offdiag(q, k, b, row, b_buf))
            o = _dot(q * jnp.exp(b), st, NT) + _dot(a_off, v, NN)
            st_sc[...] = st * jnp.exp(bl) + _dot(v, k * jnp.exp(bl - b), TN)
            diag = []
            for i in range(HG_NSUB):
                r0 = i * HG_SUB
                qb, bb = q[r0:r0 + HG_SUB], b[r0:r0 + HG_SUB]
                acc = jnp.zeros((HG_SUB, LANE), F32)
                for srow in range(HG_SUB):
                    t = r0 + srow
                    _, _, w = _hg_diag_weights(qb, bb, k_buf[t:t + 1, :], b_buf[t:t + 1, :], rr, srow)
                    acc = acc + w * hi_ref[ch * c + t:ch * c + t + 1, :]
                diag.append(acc)
            o = o + jnp.concatenate(diag, axis=0)
            o_ref[sl, :] = o
            r = lax.rsqrt(_rowmean(o * o) + EPS)
            y_ref[sl, :] = (((o * r) * gm_ref[...]) * _silu(hg_ref[sl, :])).astype(y_ref.dtype)

    col = _hg_specs(rows)
    out_blk = pl.BlockSpec((rows, LANE), lambda h, n: (n, h))
    return _pcall(
        body, name="hgrn_fwd",
        out_shape=(jax.ShapeDtypeStruct((s, HG_WIDTH), BF16), jax.ShapeDtypeStruct((s, HG_WIDTH), F32),
                   jax.ShapeDtypeStruct((HG_HEADS, s // c, HG_DV, HG_DK), F32)),
        grid=(HG_HEADS, nb),
        in_specs=[col(P_HQ), col(P_HF), col(P_HI), col(P_HG),
                  pl.BlockSpec((1, LANE), lambda h, n: (0, h)), pl.BlockSpec((1, LANE), lambda h, n: (0, 0))],
        out_specs=(out_blk, out_blk, pl.BlockSpec((1, npc, HG_DV, HG_DK), lambda h, n: (h, n, 0, 0))),
        scratch_shapes=[pltpu.VMEM((HG_DV, HG_DK), F32), pltpu.VMEM((c, LANE), F32), pltpu.VMEM((c, LANE), F32)],
        compiler_params=_params(("parallel", "arbitrary")),
    )(proj, proj, proj, proj, lb, gamma)


def _hg_bwd(dcat, dy_off, proj, lb, gamma, o_saved, states):
    s = proj.shape[0]
    rows = min(HG_ROWS, s)
    c = HG_CHUNK
    npc = rows // c
    nb = s // rows
    yb = dy_off // LANE

    def body(hq_ref, hf_ref, hi_ref, hg_ref, lb_ref, gm_ref, dy_ref, o_ref, st_ref,
             dq_ref, df_ref, di_ref, dg_ref, dlb_ref, dgm_ref, dst_sc, k_buf, b_buf, dk_buf, dv_buf):
        n = pl.program_id(1)

        @pl.when(n == 0)
        def _():
            dst_sc[...] = jnp.zeros_like(dst_sc)
            dlb_ref[...] = jnp.zeros_like(dlb_ref)
            dgm_ref[...] = jnp.zeros_like(dgm_ref)

        row = lax.broadcasted_iota(jnp.int32, (c, LANE), 0)
        rr = lax.broadcasted_iota(jnp.int32, (HG_SUB, 1), 0)
        lbv, gmv = lb_ref[...], gm_ref[...]
        for ch in reversed(range(npc)):
            sl = slice(ch * c, (ch + 1) * c)
            hq, z, v, g = hq_ref[sl, :], hf_ref[sl, :], hi_ref[sl, :], hg_ref[sl, :]
            q, k, logf, sig, sigm, f = _hg_gates(hq, z, lbv)
            b = _cumsum_rows(logf, row)
            bl = _colsum(logf)
            k_buf[...] = k
            b_buf[...] = b
            eb, ebl = jnp.exp(b), jnp.exp(bl)
            qe, kl = q * eb, k * jnp.exp(bl - b)
            st = st_ref[0, ch]
            dst = dst_sc[...]
            o, dyv = o_ref[sl, :], dy_ref[sl, :]
            r = lax.rsqrt(_rowmean(o * o) + EPS)
            on = o * r
            sg = _silu(g)
            dgm_ref[0] += _colsum(dyv * on * sg)
            dg_ref[sl, :] = dyv * on * gmv * _dsilu(g)
            go = dyv * gmv * sg
            do = r * (go - on * _rowmean(go * on))
            dq = _dot(do, st, NN) * eb
            dkl = _dot(v, dst, NN)
            dk = dkl * jnp.exp(bl - b)
            dv = _dot(kl, dst, NT)
            dbl = _colsum(dst * st) * ebl + _colsum(dkl * kl)
            dst_sc[...] = dst * ebl + _dot(do, qe, TN)
            ops = _hg_offdiag(q, k, b, row, b_buf)
            da = _dot(do, v, NT)
            a_off = sum(_dot(qf, kg, NT) for qf, kg, _, _ in ops)
            dv = dv + _dot(a_off, do, TN)
            for qf, kg, fj, gj in ops:
                dq = dq + _dot(da, kg, NN) * fj
                dk = dk + _dot(da, qf, TN) * gj
            dq_diag = []
            for i in range(HG_NSUB):
                r0 = i * HG_SUB
                qb, bb, dob = q[r0:r0 + HG_SUB], b[r0:r0 + HG_SUB], do[r0:r0 + HG_SUB]
                acc = jnp.zeros((HG_SUB, LANE), F32)
                for srow in range(HG_SUB):
                    t = r0 + srow
                    ks = k_buf[t:t + 1, :]
                    e, keep, w = _hg_diag_weights(qb, bb, ks, b_buf[t:t + 1, :], rr, srow)
                    dw = jnp.where(keep, jnp.sum(dob * hi_ref[ch * c + t:ch * c + t + 1, :], axis=1, keepdims=True), 0.0)
                    dv_buf[t:t + 1, :] = _colsum(w * dob)
                    dk_buf[t:t + 1, :] = _colsum(dw * qb * e)
                    acc = acc + dw * (ks * e)
                dq_diag.append(acc)
            dq = dq + jnp.concatenate(dq_diag, axis=0)
            dk = dk + dk_buf[...]
            di_ref[sl, :] = dv + dv_buf[...]
            db = q * dq - k * dk + jnp.where(row == c - 1, dbl, 0.0)
            dlogf = _cumsum_rows(db, row, reverse=True)
            dq_ref[sl, :] = dq * _dsilu(hq)
            s1 = sig * (1.0 - sig) * (1.0 - lbv)
            df_ref[sl, :] = dlogf * s1 / f - dk * s1
            dlb_ref[0] += _colsum(dlogf * sigm / f - dk * sigm)

    def col(off):
        return pl.BlockSpec((rows, LANE), lambda h, n, off=off: (nb - 1 - n, off // LANE + h))

    out_blk = pl.BlockSpec((rows, LANE), lambda h, n: (nb - 1 - n, h))
    acc_blk = pl.BlockSpec((1, 1, LANE), lambda h, n: (h, 0, 0))
    res = _pcall(
        body, name="hgrn_bwd",
        out_shape=tuple(jax.ShapeDtypeStruct((s, HG_WIDTH), F32) for _ in range(4))
        + (jax.ShapeDtypeStruct((HG_HEADS, 1, LANE), F32), jax.ShapeDtypeStruct((HG_HEADS, 1, LANE), F32)),
        grid=(HG_HEADS, nb),
        in_specs=[col(P_HQ), col(P_HF), col(P_HI), col(P_HG),
                  pl.BlockSpec((1, LANE), lambda h, n: (0, h)), pl.BlockSpec((1, LANE), lambda h, n: (0, 0)),
                  pl.BlockSpec((rows, LANE), lambda h, n: (nb - 1 - n, yb + h)), out_blk,
                  pl.BlockSpec((1, npc, HG_DV, HG_DK), lambda h, n: (h, nb - 1 - n, 0, 0))],
        out_specs=(out_blk, out_blk, out_blk, out_blk, acc_blk, acc_blk),
        scratch_shapes=[pltpu.VMEM((HG_DV, HG_DK), F32)] + [pltpu.VMEM((c, LANE), F32)] * 4,
        compiler_params=_params(("parallel", "arbitrary")),
    )(proj, proj, proj, proj, lb, gamma, dcat, o_saved, states)
    dq, df, di, dg, dlb, dgm = res
    return dq, df, di, dg, dlb.reshape(1, HG_WIDTH), dgm.reshape(HG_HEADS, LANE)


def _adamw_math(w, g, m, v):
    m = ADAM_B1 * m + (1.0 - ADAM_B1) * g
    v = ADAM_B2 * v + (1.0 - ADAM_B2) * (g * g)
    m_hat = m / (1.0 - ADAM_B1 ** ADAM_STEP)
    v_hat = v / (1.0 - ADAM_B2 ** ADAM_STEP)
    delta = -ADAM_LR * (m_hat / (jnp.sqrt(v_hat) + ADAM_EPS) + ADAM_WD * w)
    return delta, m, v


def _adamw(w, g, m, v, name):
    shape = w.shape
    width = shape[-1]
    rows = int(np.prod(shape[:-1]))
    tm = rows
    while tm * width * 4 > (1 << 20) and tm % 16 == 0:
        tm //= 2

    def fn(i, w_ref, g_ref, m_ref, v_ref, d_ref, mo_ref, vo_ref):
        d, mn, vn = _adamw_math(w_ref[...], g_ref[...], m_ref[...], v_ref[...])
        d_ref[...] = d
        mo_ref[...] = mn
        vo_ref[...] = vn

    v2 = lambda t: t.reshape(rows, width)
    res = _rows(fn, name=name, s=rows, tm=tm, ins=[("row", v2(t), 0, width) for t in (w, g, m, v)],
                outs=[("row", width, F32)] * 3)
    return tuple(r.reshape(shape) for r in res)


def _ada_grad_adamw(cact_all, dmod_cols, w, m, v):
    kdim, n = w.shape

    def epi(acc, wv, mv, vv):
        return (acc,) + _adamw_math(wv, acc, mv, vv)

    return _mm(cact_all, dmod_cols, mode="tn", name="ada_grad_adamw", m=kdim, n=n, k=cact_all.shape[0],
               tm=256, tn=1024, out=(F32, F32, F32, F32), epi=epi,
               extras=[("tile", w, 0), ("tile", m, 0), ("tile", v, 0)])


def _me():
    return lax.axis_index("x"), lax.axis_index("y"), lax.axis_index("c")


def _flip(k):
    x, y, c = _me()
    return (x ^ ((k >> 2) & 1), y ^ ((k >> 1) & 1), c ^ (k & 1))


def _lin(dev):
    return 4 * dev[0] + 2 * dev[1] + dev[2]


ANY = pl.BlockSpec(memory_space=pl.ANY)


def _all_gather8(x, name):
    def body(x_ref, out_ref, send_sems, recv_sems, local_sem):
        me = _lin(_me())
        mine = pltpu.make_async_copy(x_ref, out_ref.at[me], local_sem)
        mine.start()
        copies = []
        for k in range(1, 8):
            cp = pltpu.make_async_remote_copy(src_ref=x_ref, dst_ref=out_ref.at[me], send_sem=send_sems.at[k - 1],
                                              recv_sem=recv_sems.at[k - 1], device_id=_flip(k), device_id_type=MESH)
            cp.start()
            copies.append(cp)
        for k in range(1, 8):
            pltpu.make_async_remote_copy(src_ref=x_ref, dst_ref=out_ref.at[_lin(_flip(k))], send_sem=send_sems.at[k - 1],
                                         recv_sem=recv_sems.at[k - 1], device_id=_flip(k), device_id_type=MESH).wait_recv()
        for cp in copies:
            cp.wait_send()
        mine.wait()

    return _pcall(
        body, name=name, out_shape=jax.ShapeDtypeStruct((8,) + x.shape, x.dtype),
        in_specs=[ANY], out_specs=ANY,
        scratch_shapes=[pltpu.SemaphoreType.DMA((7,)), pltpu.SemaphoreType.DMA((7,)), pltpu.SemaphoreType.DMA],
    )(x)


CHIP_FLIPS = (2, 4, 6)


def _gather_weights(wflat):
    def body(w_ref, out_ref, send_sems, recv_sems, local_sem):
        x, y, c = _me()
        chip = 2 * x + y
        sib = _flip(1)
        mine = pltpu.make_async_copy(w_ref, out_ref.at[chip], local_sem)
        mine.start()

        def slot(dev, half):
            return out_ref.at[2 * dev[0] + dev[1], half]

        first = []
        for j, k in enumerate(CHIP_FLIPS):
            cp = pltpu.make_async_remote_copy(src_ref=w_ref.at[c], dst_ref=slot((x, y), c), send_sem=send_sems.at[j],
                                              recv_sem=recv_sems.at[j], device_id=_flip(k), device_id_type=MESH)
            cp.start()
            first.append(cp)
        passed = []
        for j, k in enumerate(CHIP_FLIPS):
            src = _flip(k)
            landed = slot(src, c)
            pltpu.make_async_remote_copy(src_ref=landed, dst_ref=landed, send_sem=send_sems.at[j], recv_sem=recv_sems.at[j],
                                         device_id=src, device_id_type=MESH).wait_recv()
            cp = pltpu.make_async_remote_copy(src_ref=landed, dst_ref=landed, send_sem=send_sems.at[3 + j],
                                              recv_sem=recv_sems.at[3 + j], device_id=sib, device_id_type=MESH)
            cp.start()
            passed.append(cp)
        for j, k in enumerate(CHIP_FLIPS):
            got = slot(_flip(k), 1 - c)
            pltpu.make_async_remote_copy(src_ref=got, dst_ref=got, send_sem=send_sems.at[3 + j], recv_sem=recv_sems.at[3 + j],
                                         device_id=sib, device_id_type=MESH).wait_recv()
        for cp in first + passed:
            cp.wait_send()
        mine.wait()

    return _pcall(
        body, name="gather_weights", out_shape=jax.ShapeDtypeStruct((N_CHIPS,) + wflat.shape, wflat.dtype),
        in_specs=[ANY], out_specs=ANY,
        scratch_shapes=[pltpu.SemaphoreType.DMA((6,)), pltpu.SemaphoreType.DMA((6,)), pltpu.SemaphoreType.DMA],
    )(wflat)


def _send_to_sibling(src, name, pick_other_half):
    n = N_CHIPS if pick_other_half else 1

    def body(s_ref, out_ref, send_sems, recv_sems):
        c = lax.axis_index("c")
        sib = _flip(1)
        copies = []
        for j in range(n):
            a = s_ref.at[j, 1 - c] if pick_other_half else s_ref
            b = out_ref.at[j] if pick_other_half else out_ref
            cp = pltpu.make_async_remote_copy(src_ref=a, dst_ref=b, send_sem=send_sems.at[j], recv_sem=recv_sems.at[j],
                                              device_id=sib, device_id_type=MESH)
            cp.start()
            copies.append(cp)
        for cp in copies:
            cp.wait()

    shape = (N_CHIPS,) + src.shape[2:] if pick_other_half else src.shape
    return _pcall(
        body, name=name, out_shape=jax.ShapeDtypeStruct(shape, src.dtype), in_specs=[ANY], out_specs=ANY,
        scratch_shapes=[pltpu.SemaphoreType.DMA((n,)), pltpu.SemaphoreType.DMA((n,))],
    )(src)


def _scatter_to_chips(part):
    def body(p_ref, out_ref, send_sems, recv_sems):
        copies = []
        for j, k in enumerate(CHIP_FLIPS):
            to = _flip(k)
            cp = pltpu.make_async_remote_copy(src_ref=p_ref.at[2 * to[0] + to[1]], dst_ref=out_ref.at[j],
                                              send_sem=send_sems.at[j], recv_sem=recv_sems.at[j], device_id=to, device_id_type=MESH)
            cp.start()
            copies.append(cp)
        for cp in copies:
            cp.wait()

    return _pcall(
        body, name="scatter_to_chips", out_shape=jax.ShapeDtypeStruct((3,) + part.shape[1:], part.dtype),
        in_specs=[ANY], out_specs=ANY,
        scratch_shapes=[pltpu.SemaphoreType.DMA((3,)), pltpu.SemaphoreType.DMA((3,))],
    )(part)


def _add_halves(g, r1):
    n, _, r, cdim = g.shape
    tm = 512 if r % 512 == 0 else SUBLANE

    def body(c_ref, g_ref, r_ref, o_ref):
        o_ref[...] = g_ref[...] + r_ref[...]

    return _pcall(
        body, name="add_halves", out_shape=jax.ShapeDtypeStruct((n, r, cdim), g.dtype),
        grid_spec=pltpu.PrefetchScalarGridSpec(
            num_scalar_prefetch=1, grid=(n, r // tm),
            in_specs=[pl.BlockSpec((None, None, tm, cdim), lambda j, i, c_ref: (j, c_ref[0], i, 0)),
                      pl.BlockSpec((None, tm, cdim), lambda j, i, c_ref: (j, i, 0))],
            out_specs=pl.BlockSpec((None, tm, cdim), lambda j, i, c_ref: (j, i, 0))),
        compiler_params=_params(("parallel", "parallel")),
    )(lax.axis_index("c").astype(jnp.int32).reshape(1), g, r1)


def _add_chips(part, got):
    _, r, cdim = part.shape
    tm = 512 if r % 512 == 0 else SUBLANE

    def body(chip_ref, p_ref, g_ref, o_ref):
        o_ref[...] = ((p_ref[...] + g_ref[0]) + g_ref[1]) + g_ref[2]

    chip = (2 * lax.axis_index("x") + lax.axis_index("y")).astype(jnp.int32).reshape(1)
    return _pcall(
        body, name="add_chips", out_shape=jax.ShapeDtypeStruct((r, cdim), part.dtype),
        grid_spec=pltpu.PrefetchScalarGridSpec(
            num_scalar_prefetch=1, grid=(r // tm,),
            in_specs=[pl.BlockSpec((None, tm, cdim), lambda i, chip_ref: (chip_ref[0], i, 0)),
                      pl.BlockSpec((3, tm, cdim), lambda i, chip_ref: (0, i, 0))],
            out_specs=pl.BlockSpec((tm, cdim), lambda i, chip_ref: (i, 0))),
        compiler_params=_params(("parallel",)),
    )(chip, part, got)


def _reduce_scatter(g):
    r1 = _send_to_sibling(g, "rs_sibling_halves", True)
    part = _add_halves(g, r1)
    got = _scatter_to_chips(part)
    mine = _add_chips(part, got)
    theirs = _send_to_sibling(mine, "rs_sibling_result", False)
    c = lax.axis_index("c")
    both = jnp.stack([mine, theirs])
    return jnp.where(c == 0, both, both[::-1])


def _sum8(x):
    _, r, n = x.shape
    tm = 128 if r % 128 == 0 else r

    def body(x_ref, o_ref):
        acc = x_ref[0]
        for d in range(1, 8):
            acc = acc + x_ref[d]
        o_ref[...] = acc

    return _pcall(body, name="sum8", out_shape=jax.ShapeDtypeStruct((r, n), x.dtype), grid=(r // tm,),
                  in_specs=[pl.BlockSpec((8, tm, n), lambda i: (0, i, 0))], out_specs=pl.BlockSpec((tm, n), lambda i: (i, 0)),
                  compiler_params=_params(("parallel",)))(x)


SHARDED = ("w_in", "s5_w_glu", "mla_w_uq", "mla_w_ukv", "w_out", "ffn_w_up", "ffn_conv_w", "ffn_w_down")
COL_SHARDED = ("w_in", "mla_w_uq", "mla_w_ukv", "ffn_w_up", "ffn_conv_w")
REPLICATED = ("s5_lambda_re", "s5_lambda_im", "s5_log_dt", "s5_b_re", "s5_b_im", "s5_c_re", "s5_c_im", "s5_d",
              "mla_q_norm", "mla_kv_norm", "hg_lb_logits", "hg_out_norm", "mix_pre_norm", "mix_post_norm",
              "ffn_pre_norm", "ffn_post_norm", "ffn_conv_b")
WEIGHTS = ("w_in", "s5_lambda_re", "s5_lambda_im", "s5_log_dt", "s5_b_re", "s5_b_im", "s5_c_re", "s5_c_im", "s5_d",
           "s5_w_glu", "mla_q_norm", "mla_w_uq", "mla_kv_norm", "mla_w_ukv", "hg_lb_logits", "hg_out_norm", "w_out",
           "mix_pre_norm", "mix_post_norm", "ffn_pre_norm", "ffn_post_norm", "ffn_w_up", "ffn_conv_w", "ffn_conv_b",
           "ffn_w_down", "w_ada", "b_ada")


def _flat_rows(shard_shapes):
    total = sum(int(np.prod(shard_shapes[k])) for k in SHARDED)
    unit = 2 * 16 * FLAT_COLS
    return (-(-total // unit) * unit) // (2 * FLAT_COLS)


def _flatten_shard(shards, rows):
    flat = jnp.concatenate([shards[k].reshape(-1) for k in SHARDED])
    return jnp.pad(flat, (0, 2 * rows * FLAT_COLS - flat.shape[0])).reshape(2, rows, FLAT_COLS)


def _unflatten_shard(flat, shard_shapes):
    flat = flat.reshape(-1)
    out, pos = {}, 0
    for k in SHARDED:
        n = int(np.prod(shard_shapes[k]))
        out[k] = flat[pos:pos + n].reshape(shard_shapes[k])
        pos += n
    return out


def _unflatten_full(gathered, shard_shapes):
    flat = gathered.reshape(N_CHIPS, -1)
    out, pos = {}, 0
    for k in SHARDED:
        r, cdim = shard_shapes[k]
        part = flat[:, pos:pos + r * cdim].reshape(N_CHIPS, r, cdim)
        pos += r * cdim
        out[k] = part.transpose(1, 0, 2).reshape(r, N_CHIPS * cdim) if k in COL_SHARDED else part.reshape(N_CHIPS * r, cdim)
    return out


def _flatten_full(full, shard_shapes, rows):
    parts = []
    for k in SHARDED:
        r, cdim = shard_shapes[k]
        g = full[k]
        g = g.reshape(r, N_CHIPS, cdim).transpose(1, 0, 2) if k in COL_SHARDED else g.reshape(N_CHIPS, r, cdim)
        parts.append(g.reshape(N_CHIPS, r * cdim))
    flat = jnp.concatenate(parts, axis=1)
    flat = jnp.pad(flat, ((0, 0), (0, 2 * rows * FLAT_COLS - flat.shape[1])))
    return flat.reshape(N_CHIPS, 2, rows, FLAT_COLS)


def _swap_half(t):
    half = t.shape[-1] // 2
    return jnp.concatenate([-t[..., half:], t[..., :half]], axis=-1)


def _prep_win(w):
    s5, cq, ckv, kr, hq, hf, hi, hg = jnp.split(w, (512, 1024, 1280, 1344, 1856, 2368, 2880), axis=1)
    pad = jnp.zeros((w.shape[0], PROJ_W - 3456), w.dtype)
    return jnp.concatenate([s5, cq, hq, hf, hi, hg, ckv, kr, _swap_half(kr), pad], axis=1)


def _prep_wq(w):
    w3 = w.reshape(w.shape[0], MLA_HEADS, MLA_NOPE + MLA_ROPE)
    nope, rope = w3[..., :MLA_NOPE], w3[..., MLA_NOPE:]
    pair = jnp.concatenate([rope, _swap_half(rope)], axis=-1)
    return jnp.concatenate([nope.reshape(w.shape[0], -1), pair.reshape(w.shape[0], -1)], axis=1)


def _pad_ff_cols(w, dff, ffp):
    z = jnp.zeros((w.shape[0], ffp - dff), w.dtype)
    return jnp.concatenate([w[:, :dff], z, w[:, dff:], z], axis=1)


def _pad_ff_rows(w, dff, ffp):
    return jnp.concatenate([w, jnp.zeros((ffp - dff, w.shape[1]), w.dtype)], axis=0)


def _interleave(re, im, axis):
    re, im = jnp.moveaxis(re, axis, -1), jnp.moveaxis(im, axis, -1)
    lead = re.shape[:-1]
    both = jnp.stack([re.reshape(lead + (N_STATE // SCAN_W, SCAN_W)), im.reshape(lead + (N_STATE // SCAN_W, SCAN_W))], axis=-2)
    return jnp.moveaxis(both.reshape(lead + (2 * N_STATE,)), -1, axis)


def _s5_prep(lre, lim, logdt, bre, bim, cre, cim):
    dt = jnp.exp(logdt)[:, None]
    er = jnp.exp(lre * dt)
    ar, ai = er * jnp.cos(lim * dt), er * jnp.sin(lim * dt)
    nr, den = ar - 1.0, lre * lre + lim * lim
    cr, ci = (nr * lre + ai * lim) / den, (ai * lre - nr * lim) / den
    bbr = cr[..., None] * bre - ci[..., None] * bim
    bbi = cr[..., None] * bim + ci[..., None] * bre
    eye = jnp.eye(S5_GROUPS, dtype=F32)
    bd = _interleave(jnp.einsum("gpc,gh->gchp", bbr, eye).reshape(S5_WIDTH, N_STATE),
                     jnp.einsum("gpc,gh->gchp", bbi, eye).reshape(S5_WIDTH, N_STATE), 1)
    cd = _interleave(jnp.einsum("gcp,gh->hpgc", cre, eye).reshape(N_STATE, S5_WIDTH),
                     jnp.einsum("gcp,gh->hpgc", -cim, eye).reshape(N_STATE, S5_WIDTH), 0)
    a = _interleave(ar.reshape(1, N_STATE), ai.reshape(1, N_STATE), 1)
    return a, bd, cd


def _lower_bounds(logits):
    probs = jax.nn.softmax(logits, axis=0)
    return jnp.cumsum(probs, axis=0) - probs[0:1]


def _rope_table(positions):
    inv_freq = 1.0 / (ROPE_THETA ** (jnp.arange(0, MLA_ROPE, 2, dtype=F32) / MLA_ROPE))
    ang = positions.astype(F32)[:, None] * inv_freq
    cos, sin = jnp.cos(ang), jnp.sin(ang)
    return jnp.concatenate([cos, cos, sin, sin], axis=1)


def _split_mod(mod):
    return [mod[:, i * D_MODEL:(i + 1) * D_MODEL] for i in range(6)]


def _layer_fwd(x, wl, mod, cs):
    s = x.shape[0]
    ffp = wl["wdown_p"].shape[0]
    sh1, sc1, g1, sh2, sc2, g2 = _split_mod(mod)
    h1 = _normmod_fwd(x, wl["mix_pre_norm"], sc1, sh1, "mix_pre")
    proj = _mm(h1, wl["win_p"], mode="nn", name="in_proj", m=s, n=PROJ_W, k=D_MODEL)
    out_s5, s5_saved = _s5_fwd(proj, wl, s)
    o_mla, mla_saved = _mla_fwd(proj, wl, cs, s)
    y_hg, o_hg, states = _hg_fwd(proj, wl["hg_lb"], wl["hg_out_norm"])
    cat = jnp.concatenate([out_s5, o_mla.astype(BF16), y_hg], axis=1)
    mixed = _mm(cat, wl["w_out"], mode="nn", name="out_proj", m=s, n=D_MODEL, k=D_MODEL)
    x2 = _postnorm_fwd(x, mixed, wl["mix_post_norm"], g1, "mix_post")
    h2 = _normmod_fwd(x2, wl["ffn_pre_norm"], sc2, sh2, "ffn_pre")
    up = _mm(h2, wl["wup_p"], mode="nn", name="ffn_up", m=s, n=2 * ffp, k=D_MODEL)
    act = _ffn_act_fwd(up, wl["conv_w_p"], wl["conv_b_p"], ffp)
    y = _mm(act, wl["wdown_p"], mode="nn", name="ffn_down", m=s, n=D_MODEL, k=ffp)
    x3 = _postnorm_fwd(x2, y, wl["ffn_post_norm"], g2, "ffn_post")
    return x3, (x, h1, proj, s5_saved, mla_saved, o_hg, states, cat, mixed, x2, h2, up, act, y)


def _layer_bwd(dx3, saved, wl, mod, cs):
    x, h1, proj, s5_saved, mla_saved, o_hg, states, cat, mixed, x2, h2, up, act, y = saved
    s = x.shape[0]
    ffp = wl["wdown_p"].shape[0]
    sh1, sc1, g1, sh2, sc2, g2 = _split_mod(mod)
    g = {}
    dy, g["ffn_post_norm"], dg2 = _postnorm_bwd(dx3, y, wl["ffn_post_norm"], g2, "ffn_post_bwd")
    da = _mm(dy, wl["wdown_p"], mode="nt", name="ffn_down_dx", m=s, n=ffp, k=D_MODEL)
    g["wdown_p"] = _mm(act, dy, mode="tn", name="ffn_down_dw", m=ffp, n=D_MODEL, k=s)
    dug, duv, dwg, dwv, dbg, dbv = _ffn_act_bwd(da, up, wl["conv_w_p"], wl["conv_b_p"], ffp)
    g["conv_w_p"] = jnp.concatenate([dwg, dwv], axis=1)
    g["conv_b_p"] = jnp.concatenate([dbg, dbv], axis=1)
    dup = jnp.concatenate([_conv_bwd_input(dug, wl["conv_w_p"], 0, "ffn_conv_bwd_gate"),
                           _conv_bwd_input(duv, wl["conv_w_p"], ffp, "ffn_conv_bwd_val")], axis=1)
    dh2 = _mm(dup, wl["wup_p"], mode="nt", name="ffn_up_dx", m=s, n=D_MODEL, k=2 * ffp)
    g["wup_p"] = _mm(h2, dup, mode="tn", name="ffn_up_dw", m=D_MODEL, n=2 * ffp, k=s)
    dx2, g["ffn_pre_norm"], dsc2, dsh2 = _normmod_bwd(dh2, x2, wl["ffn_pre_norm"], sc2, dx3, "ffn_pre_bwd")
    dmixed, g["mix_post_norm"], dg1 = _postnorm_bwd(dx2, mixed, wl["mix_post_norm"], g1, "mix_post_bwd")
    dcat = _mm(dmixed, wl["w_out"], mode="nt", name="out_proj_dx", m=s, n=D_MODEL, k=D_MODEL)
    g["w_out"] = _mm(cat, dmixed, mode="tn", name="out_proj_dw", m=D_MODEL, n=D_MODEL, k=s)
    du_s5, s5g = _s5_bwd(dcat, proj, wl, s5_saved, s)
    dcq, dckv, dkr, mlag = _mla_bwd(dcat, proj, wl, cs, mla_saved, s)
    dhq, dhf, dhi, dhg, g["hg_lb"], dgm = _hg_bwd(dcat, S5_WIDTH + MLA_WIDTH, proj, wl["hg_lb"], wl["hg_out_norm"], o_hg, states)
    g["hg_out_norm"] = jnp.sum(dgm, axis=0, keepdims=True)
    g.update(s5g)
    g.update(mlag)
    dproj = jnp.concatenate([du_s5, dcq, dhq, dhf, dhi, dhg, dckv, dkr, jnp.zeros((s, PROJ_W - 3456), F32)], axis=1).astype(BF16)
    dh1 = _mm(dproj, wl["win_p"], mode="nt", name="in_proj_dx", m=s, n=D_MODEL, k=PROJ_W)
    g["win_p"] = _mm(h1, dproj, mode="tn", name="in_proj_dw", m=D_MODEL, n=PROJ_W, k=s)
    dx, g["mix_pre_norm"], dsc1, dsh1 = _normmod_bwd(dh1, x, wl["mix_pre_norm"], sc1, dx2, "mix_pre_bwd")
    dmod = jnp.concatenate([dsh1, dsc1, dg1, dsh2, dsc2, dg2], axis=1)
    return dx, g, dmod


def _prepare_layer(full, rep, dff, ffp):
    def sharded_prep(w_in, s5_w_glu, mla_w_uq, mla_w_ukv, w_out, ffn_w_up, ffn_conv_w, ffn_w_down):
        return {"win_p": _prep_win(w_in), "s5_w_glu": s5_w_glu, "mla_wq": _prep_wq(mla_w_uq), "mla_w_ukv": mla_w_ukv,
                "w_out": w_out, "wup_p": _pad_ff_cols(ffn_w_up, dff, ffp), "conv_w_p": _pad_ff_cols(ffn_conv_w, dff, ffp),
                "wdown_p": _pad_ff_rows(ffn_w_down, dff, ffp)}

    def rep_prep(lre, lim, logdt, bre, bim, cre, cim, conv_b):
        a, bd, cd = _s5_prep(lre, lim, logdt, bre, bim, cre, cim)
        return {"s5_a": a, "s5_bd": bd, "s5_cd": cd, "conv_b_p": _pad_ff_cols(conv_b, dff, ffp)}

    sh_args = [full[k] for k in SHARDED]
    rep_names = ("s5_lambda_re", "s5_lambda_im", "s5_log_dt", "s5_b_re", "s5_b_im", "s5_c_re", "s5_c_im", "ffn_conv_b")
    rep_args = [rep[k] for k in rep_names]
    wl = sharded_prep(*sh_args)
    rep_out, rep_vjp = jax.vjp(rep_prep, *rep_args)
    wl.update(rep_out)
    sh_t = jax.linear_transpose(sharded_prep, *[jax.ShapeDtypeStruct(a.shape, F32) for a in sh_args])

    def back(g):
        out = dict(zip(SHARDED, sh_t({k: g[k] for k in ("win_p", "s5_w_glu", "mla_wq", "mla_w_ukv", "w_out", "wup_p", "conv_w_p", "wdown_p")})))
        out.update(zip(rep_names, rep_vjp({k: g[k] for k in ("s5_a", "s5_bd", "s5_cd", "conv_b_p")})))
        return out

    return wl, back


PER_LAYER_ROWS = ("s5_d", "mla_q_norm", "mla_kv_norm", "hg_out_norm", "mix_pre_norm", "mix_post_norm", "ffn_pre_norm", "ffn_post_norm")


def _flat_pad(parts, unit):
    flat = jnp.concatenate([p.reshape(-1) for p in parts])
    n = -(-flat.shape[0] // unit) * unit
    return jnp.pad(flat, (0, n - flat.shape[0])).reshape(-1, FLAT_COLS)


def _split_flat(flat, like):
    flat = flat.reshape(-1)
    out, pos = [], 0
    for t in like:
        out.append(flat[pos:pos + t.size].reshape(t.shape))
        pos += t.size
    return out


def kernel(x, c, positions, w_in, s5_lambda_re, s5_lambda_im, s5_log_dt, s5_b_re, s5_b_im, s5_c_re, s5_c_im, s5_d, s5_w_glu, mla_q_norm, mla_w_uq, mla_kv_norm, mla_w_ukv, hg_lb_logits, hg_out_norm, w_out, mix_pre_norm, mix_post_norm, ffn_pre_norm, ffn_post_norm, ffn_w_up, ffn_conv_w, ffn_conv_b, ffn_w_down, w_ada, b_ada, loss_target, m_w_in, m_s5_lambda_re, m_s5_lambda_im, m_s5_log_dt, m_s5_b_re, m_s5_b_im, m_s5_c_re, m_s5_c_im, m_s5_d, m_s5_w_glu, m_mla_q_norm, m_mla_w_uq, m_mla_kv_norm, m_mla_w_ukv, m_hg_lb_logits, m_hg_out_norm, m_w_out, m_mix_pre_norm, m_mix_post_norm, m_ffn_pre_norm, m_ffn_post_norm, m_ffn_w_up, m_ffn_conv_w, m_ffn_conv_b, m_ffn_w_down, m_w_ada, m_b_ada, v_w_in, v_s5_lambda_re, v_s5_lambda_im, v_s5_log_dt, v_s5_b_re, v_s5_b_im, v_s5_c_re, v_s5_c_im, v_s5_d, v_s5_w_glu, v_mla_q_norm, v_mla_w_uq, v_mla_kv_norm, v_mla_w_ukv, v_hg_lb_logits, v_hg_out_norm, v_w_out, v_mix_pre_norm, v_mix_post_norm, v_ffn_pre_norm, v_ffn_post_norm, v_ffn_w_up, v_ffn_conv_w, v_ffn_conv_b, v_ffn_w_down, v_w_ada, v_b_ada):
    p = dict(locals())
    n_layers = w_in.shape[0]
    dff = ffn_w_down.shape[1] * N_CHIPS
    ffp = -(-dff // FFN_WC) * FFN_WC
    xs, target = x[0], loss_target[0]
    me = 4 * lax.axis_index("x") + 2 * lax.axis_index("y") + lax.axis_index("c")
    chip = 2 * lax.axis_index("x") + lax.axis_index("y")
    cs = _rope_table(positions[0])

    cact = jax.nn.silu(_all_gather8(c, "gather_c")[:, 0, :])
    ada_cols = w_ada.shape[2]
    mod_part = jnp.stack([_mm(cact, w_ada[l], mode="nn", name="ada_mod", m=8, n=ada_cols, k=D_MODEL) for l in range(n_layers)])
    mod_all = _all_gather8(mod_part.reshape(1, -1), "gather_mod").reshape(N_CHIPS, 2, n_layers, 8, ada_cols)[:, 0]
    mod_mine = lax.dynamic_index_in_dim(mod_all, me, axis=2, keepdims=False)
    mods = mod_mine.transpose(1, 0, 2).reshape(n_layers, -1) + b_ada

    conv_w_all = _all_gather8(ffn_conv_w.reshape(1, -1), "gather_conv_w").reshape(N_CHIPS, 2, n_layers, 3, -1)[:, 0]
    conv_w_full = conv_w_all.transpose(1, 2, 0, 3).reshape(n_layers, 3, -1)

    lbs, lb_vjp = jax.vjp(_lower_bounds, hg_lb_logits)
    shard_shapes = {k: p[k].shape[1:] for k in SHARDED}
    rows = _flat_rows(shard_shapes)

    layers = []
    for l in range(n_layers):
        flat = _flatten_shard({k: p[k][l].astype(BF16) for k in SHARDED}, rows)
        full = _unflatten_full(_gather_weights(flat), shard_shapes)
        full["ffn_conv_w"] = conv_w_full[l]
        rep = {k: p[k][l] for k in ("s5_lambda_re", "s5_lambda_im", "s5_log_dt", "s5_b_re", "s5_b_im", "s5_c_re", "s5_c_im")}
        rep["ffn_conv_b"] = ffn_conv_b[l][None, :]
        wl, back = _prepare_layer(full, rep, dff, ffp)
        for k in PER_LAYER_ROWS:
            wl[k] = p[k][l][None, :]
        wl["hg_lb"] = lbs[l][None, :]
        layers.append((wl, back))

    h = xs
    saved = []
    for l in range(n_layers):
        h, sv = _layer_fwd(h, layers[l][0], mods[l][None, :], cs)
        saved.append(sv)
    dh, loss_part = _loss_grad(h, target)
    loss = lax.psum(loss_part[0, 0], ("x", "y", "c"))

    grads = {k: [None] * n_layers for k in WEIGHTS}
    dmods, dlbs = [None] * n_layers, [None] * n_layers
    for l in reversed(range(n_layers)):
        wl, back = layers[l]
        dh, g, dmods[l] = _layer_bwd(dh, saved[l], wl, mods[l][None, :], cs)
        ref_g = back(g)
        reduced = _unflatten_shard(_reduce_scatter(_flatten_full(ref_g, shard_shapes, rows)), shard_shapes)
        for k in SHARDED:
            grads[k][l] = reduced[k]
        for k in ("s5_lambda_re", "s5_lambda_im", "s5_log_dt", "s5_b_re", "s5_b_im", "s5_c_re", "s5_c_im"):
            grads[k][l] = ref_g[k]
        grads["ffn_conv_b"][l] = ref_g["ffn_conv_b"][0]
        for k in PER_LAYER_ROWS:
            grads[k][l] = g[k][0]
        dlbs[l] = g["hg_lb"][0]
    grads = {k: jnp.stack(v) for k, v in grads.items() if v[0] is not None}
    grads["hg_lb_logits"] = lb_vjp(jnp.stack(dlbs))[0]

    rep_flat = _flat_pad([grads[k] for k in REPLICATED], 128 * FLAT_COLS)
    rep_sum = _sum8(_all_gather8(rep_flat, "gather_small_grads"))
    for k, t in zip(REPLICATED, _split_flat(rep_sum, [grads[k] for k in REPLICATED])):
        grads[k] = t

    dmod_all = _all_gather8(jnp.stack(dmods).reshape(n_layers * 6 * D_MODEL // FLAT_COLS, FLAT_COLS), "gather_dmod")
    grads["b_ada"] = _sum8(dmod_all).reshape(n_layers, 6 * D_MODEL)
    dmod_cols = lax.dynamic_slice_in_dim(dmod_all.reshape(8, n_layers, N_CHIPS, ada_cols), chip, 1, axis=2)[:, :, 0]
    ada = [_ada_grad_adamw(cact, dmod_cols[:, l], w_ada[l], m_w_ada[l], v_w_ada[l]) for l in range(n_layers)]
    grads["w_ada"], delta_ada, new_m_ada, new_v_ada = (jnp.stack([a[i] for a in ada]) for i in range(4))

    delta, new_m, new_v = {"w_ada": delta_ada}, {"w_ada": new_m_ada}, {"w_ada": new_v_ada}
    for k in SHARDED:
        if k == "ffn_conv_w":
            continue
        delta[k], new_m[k], new_v[k] = _adamw(p[k], grads[k], p["m_" + k], p["v_" + k], "adamw_" + k)
    small = REPLICATED + ("b_ada", "ffn_conv_w")
    flats = [_flat_pad([src[k] for k in small], 128 * FLAT_COLS)
             for src in (p, grads, {k: p["m_" + k] for k in small}, {k: p["v_" + k] for k in small})]
    for dst, flat in zip((delta, new_m, new_v), _adamw(*flats, "adamw_small")):
        for k, t in zip(small, _split_flat(flat, [p[k] for k in small])):
            dst[k] = t

    return (loss, dh[None], *[grads[k] for k in WEIGHTS], *[delta[k] for k in WEIGHTS],
            *[new_m[k] for k in WEIGHTS], *[new_v[k] for k in WEIGHTS])
```

```python
import functools
import math

import numpy as np
import jax
import jax.numpy as jnp
from jax import lax
from jax.experimental import pallas as pl
from jax.experimental.pallas import tpu as pltpu

F32 = jnp.float32
BF16 = jnp.bfloat16
MESH = pl.DeviceIdType.MESH

D_MODEL = 2048
S5_WIDTH, S5_GROUP, S5_GROUPS, S5_STATE = 512, 16, 32, 64
MLA_HEADS, MLA_NOPE, MLA_ROPE, MLA_V = 8, 128, 64, 128
MLA_Q_RANK, MLA_KV_RANK = 512, 256
MLA_WIDTH = MLA_HEADS * MLA_V
ROPE_THETA = 10000.0
HG_HEADS, HG_DK, HG_DV = 4, 128, 128
HG_WIDTH = HG_HEADS * HG_DV
EPS = 1e-6
ADAM_LR, ADAM_B1, ADAM_B2, ADAM_EPS, ADAM_WD, ADAM_STEP = 0.001, 0.9, 0.999, 1e-08, 0.01, 10
GELU_K0 = math.sqrt(2.0 / math.pi)
GELU_K1 = 0.044715

LANE = 128
SUBLANE = 8
VMEM_LIMIT = 56 * 1024 * 1024

P_S5, P_CQ, P_HQ, P_HF, P_HI, P_HG, P_CKV, P_KR = 0, 512, 1024, 1536, 2048, 2560, 3072, 3328
PROJ_W = 3584
N_STATE = S5_GROUPS * S5_STATE
SCAN_W = 512
HG_CHUNK = 64
N_CHIPS = 4
FLAT_COLS = 1024
ADD_ROWS = 256


def _sigmoid(x):
    return 1.0 / (1.0 + jnp.exp(-x))


def _silu(x):
    return x * _sigmoid(x)


def _dsilu(x):
    s = _sigmoid(x)
    return s * (1.0 + x * (1.0 - s))


def _gelu(x):
    return 0.5 * x * (1.0 + jnp.tanh(GELU_K0 * (x + GELU_K1 * x * x * x)))


def _dgelu(x):
    t = jnp.tanh(GELU_K0 * (x + GELU_K1 * x * x * x))
    return 0.5 * (1.0 + t) + 0.5 * x * (1.0 - t * t) * GELU_K0 * (1.0 + 3.0 * GELU_K1 * x * x)


def _colsum(v):
    return jnp.sum(v, axis=0, keepdims=True)


def _rowmean(v):
    return jnp.mean(v, axis=-1, keepdims=True)


def _dot(a, b, dims):
    return lax.dot_general(a.astype(BF16), b.astype(BF16), (dims, ((), ())), preferred_element_type=F32)


NN = ((1,), (0,))
NT = ((1,), (1,))
TN = ((0,), (0,))


def _pcall(body, **kw):
    return pl.pallas_call(body, **kw)


def _params(sem):
    return pltpu.CompilerParams(dimension_semantics=sem, vmem_limit_bytes=VMEM_LIMIT)


def _tile(dim, pref):
    if dim <= pref:
        return dim
    t = (pref // LANE) * LANE
    while t > LANE and dim % t:
        t -= LANE
    assert dim % t == 0, (dim, pref)
    return t


def _mm(a, b, *, mode, name, m, n, k, a_off=0, b_off=0, tm=1024, tn=1024, tk=1024,
        out=((F32),), epi=None, extras=(), cm=None):
    if cm is not None:
        if mode == "nt":
            tk = _tile(cm, tk)
        else:
            tn = _tile(cm, tn)
    tm, tn, tk = _tile(m, tm), _tile(n, tn), _tile(k, tk)
    nk = k // tk
    dims = {"nn": NN, "nt": NT, "tn": TN}[mode]
    if mode == "tn":
        assert a_off % tm == 0 and b_off % tn == 0
        a_spec = pl.BlockSpec((tk, tm), lambda i, j, kk: (kk, i + a_off // tm))
        b_spec = pl.BlockSpec((tk, tn), lambda i, j, kk: (kk, j + b_off // tn))
    else:
        assert a_off % tk == 0 and b_off == 0
        a_spec = pl.BlockSpec((tm, tk), lambda i, j, kk: (i, kk + a_off // tk))
        if mode == "nn" and cm is not None:
            b_spec = pl.BlockSpec((None, tk, tn), lambda i, j, kk, per=cm // tn: (j // per, kk, j % per))
        elif mode == "nn":
            b_spec = pl.BlockSpec((tk, tn), lambda i, j, kk: (kk, j))
        elif cm is not None:
            b_spec = pl.BlockSpec((None, tn, tk), lambda i, j, kk, per=cm // tk: (kk // per, j, kk % per))
        else:
            b_spec = pl.BlockSpec((tn, tk), lambda i, j, kk: (j, kk))
    in_specs, ex_arrays = [a_spec, b_spec], []
    for e in extras:
        if e[0] == "tile":
            off = e[2] // tn
            assert e[2] % tn == 0
            in_specs.append(pl.BlockSpec((tm, tn), lambda i, j, kk, off=off: (i, j + off)))
        else:
            in_specs.append(pl.BlockSpec((e[1].shape[0], tn), lambda i, j, kk: (0, j)))
        ex_arrays.append(e[1])
    n_ex = len(ex_arrays)
    n_out = len(out)

    def body(*refs):
        a_ref, b_ref = refs[0], refs[1]
        ex_refs = refs[2:2 + n_ex]
        o_refs = refs[2 + n_ex:2 + n_ex + n_out]
        acc_ref = refs[-1]
        kk = pl.program_id(2)

        @pl.when(kk == 0)
        def _():
            acc_ref[...] = jnp.zeros_like(acc_ref)

        acc_ref[...] += _dot(a_ref[...], b_ref[...], dims)

        @pl.when(kk == nk - 1)
        def _():
            acc = acc_ref[...]
            if epi is None:
                o_refs[0][...] = acc.astype(o_refs[0].dtype)
            else:
                vals = epi(acc, *[r[...] for r in ex_refs])
                for r, v in zip(o_refs, vals):
                    r[...] = v.astype(r.dtype)

    if mode == "tn" and cm is not None:
        out_shape = tuple(jax.ShapeDtypeStruct((N_CHIPS, m, cm), d) for d in out)
        out_specs = tuple(pl.BlockSpec((None, tm, tn), lambda i, j, kk, per=cm // tn: (j // per, i, j % per)) for _ in out)
    else:
        out_shape = tuple(jax.ShapeDtypeStruct((m, n), d) for d in out)
        out_specs = tuple(pl.BlockSpec((tm, tn), lambda i, j, kk: (i, j)) for _ in out)
    res = _pcall(
        body, name=name,
        out_shape=out_shape,
        grid=(m // tm, n // tn, nk),
        in_specs=in_specs,
        out_specs=out_specs,
        scratch_shapes=[pltpu.VMEM((tm, tn), F32)],
        compiler_params=_params(("parallel", "parallel", "arbitrary")),
    )(a, b, *ex_arrays)
    return res[0] if n_out == 1 else res


def _rows(fn, *, name, s, tm, ins, outs, ncb=1):
    tm = min(tm, s)
    assert s % tm == 0 and tm % SUBLANE == 0
    ni = s // tm
    r8 = tm // SUBLANE
    in_specs, arrays = [], []
    for e in ins:
        kind, arr = e[0], e[1]
        if kind in ("row", "prev8", "next8", "vecb"):
            off, w = e[2] // e[3], e[3]
            assert e[2] % e[3] == 0
        if kind == "row":
            in_specs.append(pl.BlockSpec((tm, w), lambda j, i, off=off: (i, off + j)))
        elif kind == "prev8":
            in_specs.append(pl.BlockSpec((SUBLANE, w), lambda j, i, off=off: (jnp.maximum(i * r8 - 1, 0), off + j)))
        elif kind == "next8":
            last = s // SUBLANE - 1
            in_specs.append(pl.BlockSpec((SUBLANE, w), lambda j, i, off=off: (jnp.minimum((i + 1) * r8, last), off + j)))
        elif kind == "vec":
            in_specs.append(pl.BlockSpec(arr.shape, lambda j, i, nd=arr.ndim: (0,) * nd))
        else:
            in_specs.append(pl.BlockSpec((arr.shape[0], w), lambda j, i, off=off: (0, off + j)))
        arrays.append(arr)
    out_shape, out_specs = [], []
    for e in outs:
        if e[0] == "row":
            out_shape.append(jax.ShapeDtypeStruct((s, ncb * e[1]), e[2]))
            out_specs.append(pl.BlockSpec((tm, e[1]), lambda j, i: (i, j)))
        else:
            out_shape.append(jax.ShapeDtypeStruct((e[1], ncb * e[2]), F32))
            out_specs.append(pl.BlockSpec((e[1], e[2]), lambda j, i: (0, j)))

    def body(*refs):
        fn(pl.program_id(1), *refs)

    res = _pcall(
        body, name=name, out_shape=tuple(out_shape), grid=(ncb, ni),
        in_specs=in_specs, out_specs=tuple(out_specs),
        compiler_params=_params(("parallel", "arbitrary")),
    )(*arrays)
    return res[0] if len(outs) == 1 else res


def _acc(ref, i, val):
    @pl.when(i == 0)
    def _():
        ref[...] = val

    @pl.when(i > 0)
    def _():
        ref[...] += val


def _normmod_fwd(x, gain, sc, sh, name):
    s, d = x.shape

    def fn(i, x_ref, g_ref, sc_ref, sh_ref, h_ref):
        xv = x_ref[...]
        r = lax.rsqrt(_rowmean(xv * xv) + EPS)
        h_ref[...] = (((xv * r) * g_ref[...]) * (1.0 + sc_ref[...]) + sh_ref[...]).astype(h_ref.dtype)

    return _rows(fn, name=name, s=s, tm=256, ins=[("row", x, 0, d), ("vec", gain), ("vec", sc), ("vec", sh)],
                 outs=[("row", d, BF16)])


def _normmod_bwd(dh, x, gain, sc, dx_add, name):
    s, d = x.shape

    def fn(i, dh_ref, x_ref, g_ref, sc_ref, add_ref, dx_ref, dg_ref, dsc_ref, dsh_ref):
        xv, dhv = x_ref[...], dh_ref[...]
        r = lax.rsqrt(_rowmean(xv * xv) + EPS)
        xn = xv * r
        gain_v, one_sc = g_ref[...], 1.0 + sc_ref[...]
        ghat = dhv * gain_v * one_sc
        dx_ref[...] = r * (ghat - xn * _rowmean(ghat * xn)) + add_ref[...]
        _acc(dg_ref, i, _colsum(dhv * xn * one_sc))
        _acc(dsc_ref, i, _colsum(dhv * xn * gain_v))
        _acc(dsh_ref, i, _colsum(dhv))

    return _rows(fn, name=name, s=s, tm=256,
                 ins=[("row", dh, 0, d), ("row", x, 0, d), ("vec", gain), ("vec", sc), ("row", dx_add, 0, d)],
                 outs=[("row", d, F32), ("acc", 1, d), ("acc", 1, d), ("acc", 1, d)])


def _postnorm_fwd(x, m, gain, gate, name):
    s, d = x.shape

    def fn(i, x_ref, m_ref, g_ref, gate_ref, o_ref):
        mv = m_ref[...]
        r = lax.rsqrt(_rowmean(mv * mv) + EPS)
        o_ref[...] = x_ref[...] + gate_ref[...] * ((mv * r) * g_ref[...])

    return _rows(fn, name=name, s=s, tm=256, ins=[("row", x, 0, d), ("row", m, 0, d), ("vec", gain), ("vec", gate)],
                 outs=[("row", d, F32)])


def _postnorm_bwd(dxo, m, gain, gate, name):
    s, d = m.shape

    def fn(i, dx_ref, m_ref, g_ref, gate_ref, dm_ref, dg_ref, dgate_ref):
        mv, dxv = m_ref[...], dx_ref[...]
        r = lax.rsqrt(_rowmean(mv * mv) + EPS)
        mn = mv * r
        gain_v, gate_v = g_ref[...], gate_ref[...]
        ghat = dxv * gate_v * gain_v
        dm_ref[...] = (r * (ghat - mn * _rowmean(ghat * mn))).astype(dm_ref.dtype)
        _acc(dg_ref, i, _colsum(dxv * gate_v * mn))
        _acc(dgate_ref, i, _colsum(dxv * mn * gain_v))

    return _rows(fn, name=name, s=s, tm=256, ins=[("row", dxo, 0, d), ("row", m, 0, d), ("vec", gain), ("vec", gate)],
                 outs=[("row", d, BF16), ("acc", 1, d), ("acc", 1, d)])


def _rms_fwd(src, off, w, gain, name):
    s = src.shape[0]

    def fn(i, x_ref, g_ref, o_ref):
        xv = x_ref[...]
        r = lax.rsqrt(_rowmean(xv * xv) + EPS)
        o_ref[...] = ((xv * r) * g_ref[...]).astype(o_ref.dtype)

    return _rows(fn, name=name, s=s, tm=512, ins=[("row", src, off, w), ("vec", gain)], outs=[("row", w, BF16)])


def _rms_bwd(dy, src, off, w, gain, name):
    s = src.shape[0]

    def fn(i, dy_ref, x_ref, g_ref, dx_ref, dg_ref):
        xv, dyv = x_ref[...], dy_ref[...]
        r = lax.rsqrt(_rowmean(xv * xv) + EPS)
        xn = xv * r
        ghat = dyv * g_ref[...]
        dx_ref[...] = r * (ghat - xn * _rowmean(ghat * xn))
        _acc(dg_ref, i, _colsum(dyv * xn))

    return _rows(fn, name=name, s=s, tm=512, ins=[("row", dy, 0, w), ("row", src, off, w), ("vec", gain)],
                 outs=[("row", w, F32), ("acc", 1, w)])


def _loss_grad(x, target):
    s, d = x.shape

    def fn(i, x_ref, t_ref, dx_ref, l_ref):
        diff = x_ref[...] - t_ref[...]
        dx_ref[...] = diff * (1.0 / d)
        part = _colsum(jnp.sum(diff * diff, axis=1, keepdims=True)) * (0.5 / d)
        _acc(l_ref, i, jnp.broadcast_to(part, (1, LANE)))

    return _rows(fn, name="loss_grad", s=s, tm=256, ins=[("row", x, 0, d), ("row", target, 0, d)],
                 outs=[("row", d, F32), ("acc", 1, LANE)])


FFN_WC = 512


def _shift_rows(xv, h_ref, i, row, k):
    out = pltpu.roll(xv, k, 0)
    for r in range(k):
        hrow = jnp.where(i > 0, h_ref[SUBLANE - k + r:SUBLANE - k + r + 1, :], 0.0)
        out = jnp.where(row == r, hrow, out)
    return out


def _conv_rows(x_ref, h_ref, w_ref, b_ref, i, row):
    xv = x_ref[...]
    s1, s2 = _shift_rows(xv, h_ref, i, row, 1), _shift_rows(xv, h_ref, i, row, 2)
    u = ((b_ref[...] + s2 * w_ref[0:1, :]) + s1 * w_ref[1:2, :]) + xv * w_ref[2:3, :]
    return u, s1, s2, xv


def _ffn_act_fwd(up, conv_w, conv_b, ffp):
    s = up.shape[0]
    wc, ncb = FFN_WC, ffp // FFN_WC

    def fn(i, g_ref, gh_ref, v_ref, vh_ref, wg_ref, wv_ref, bg_ref, bv_ref, a_ref):
        row = lax.broadcasted_iota(jnp.int32, g_ref.shape, 0)
        ug = _conv_rows(g_ref, gh_ref, wg_ref, bg_ref, i, row)[0]
        uv = _conv_rows(v_ref, vh_ref, wv_ref, bv_ref, i, row)[0]
        a_ref[...] = (_gelu(ug) * uv).astype(a_ref.dtype)

    return _rows(fn, name="ffn_act_fwd", s=s, tm=512, ncb=ncb,
                 ins=[("row", up, 0, wc), ("prev8", up, 0, wc), ("row", up, ffp, wc), ("prev8", up, ffp, wc),
                      ("vecb", conv_w, 0, wc), ("vecb", conv_w, ffp, wc), ("vecb", conv_b, 0, wc), ("vecb", conv_b, ffp, wc)],
                 outs=[("row", wc, BF16)])


def _ffn_act_bwd(da, up, conv_w, conv_b, ffp):
    s = up.shape[0]
    wc, ncb = FFN_WC, ffp // FFN_WC

    def fn(i, da_ref, g_ref, gh_ref, v_ref, vh_ref, wg_ref, wv_ref, bg_ref, bv_ref,
           dug_ref, duv_ref, dwg_ref, dwv_ref, dbg_ref, dbv_ref):
        row = lax.broadcasted_iota(jnp.int32, g_ref.shape, 0)
        ug, g1, g2, g0 = _conv_rows(g_ref, gh_ref, wg_ref, bg_ref, i, row)
        uv, v1, v2, v0 = _conv_rows(v_ref, vh_ref, wv_ref, bv_ref, i, row)
        dav = da_ref[...]
        dug = dav * uv * _dgelu(ug)
        duv = dav * _gelu(ug)
        dug_ref[...] = dug
        duv_ref[...] = duv
        for r, (gt, vt) in enumerate(((g2, v2), (g1, v1), (g0, v0))):
            _acc(dwg_ref.at[r:r + 1, :], i, _colsum(dug * gt))
            _acc(dwv_ref.at[r:r + 1, :], i, _colsum(duv * vt))
        _acc(dbg_ref, i, _colsum(dug))
        _acc(dbv_ref, i, _colsum(duv))

    return _rows(fn, name="ffn_act_bwd", s=s, tm=512, ncb=ncb,
                 ins=[("row", da, 0, wc), ("row", up, 0, wc), ("prev8", up, 0, wc), ("row", up, ffp, wc), ("prev8", up, ffp, wc),
                      ("vecb", conv_w, 0, wc), ("vecb", conv_w, ffp, wc), ("vecb", conv_b, 0, wc), ("vecb", conv_b, ffp, wc)],
                 outs=[("row", wc, F32), ("row", wc, F32), ("acc", 3, wc), ("acc", 3, wc), ("acc", 1, wc), ("acc", 1, wc)])


def _conv_bwd_input(du, conv_w, w_off, name):
    s, ffp = du.shape
    wc, ncb = FFN_WC, ffp // FFN_WC
    ni = s // min(512, s)

    def fn(i, du_ref, nx_ref, w_ref, o_ref):
        dv = du_ref[...]
        tm = dv.shape[0]
        row = lax.broadcasted_iota(jnp.int32, dv.shape, 0)
        n0 = jnp.where(i < ni - 1, nx_ref[0:1, :], 0.0)
        n1 = jnp.where(i < ni - 1, nx_ref[1:2, :], 0.0)
        u1 = jnp.where(row == tm - 1, n0, pltpu.roll(dv, tm - 1, 0))
        u2 = jnp.where(row == tm - 1, n1, jnp.where(row == tm - 2, n0, pltpu.roll(dv, tm - 2, 0)))
        o_ref[...] = (dv * w_ref[2:3, :] + u1 * w_ref[1:2, :] + u2 * w_ref[0:1, :]).astype(o_ref.dtype)

    return _rows(fn, name=name, s=s, tm=512, ncb=ncb,
                 ins=[("row", du, 0, wc), ("next8", du, 0, wc), ("vecb", conv_w, w_off, wc)],
                 outs=[("row", wc, BF16)])


def _cmul(ar, ai, br, bi):
    return ar * br - ai * bi, ar * bi + ai * br


def _s5_scan(x, a, *, reverse, h=None, name):
    s = x.shape[0]
    w = SCAN_W
    t_rows = min(256, s)
    nt = s // t_rows
    ncol = N_STATE // w
    nbits = t_rows.bit_length()
    r8 = t_rows // SUBLANE

    def tblk(t):
        return nt - 1 - t if reverse else t

    def body(*refs):
        if reverse:
            x_ref, a_ref, h_ref, hh_ref, o_ref, da_ref, carry, ptab = refs
        else:
            x_ref, a_ref, o_ref, carry, ptab = refs
        t = pl.program_id(1)
        row = lax.broadcasted_iota(jnp.int32, (t_rows, w), 0)
        idx = (t_rows - 1 - row) if reverse else row
        ar = a_ref[:, :w]
        ai = -a_ref[:, w:] if reverse else a_ref[:, w:]
        pows = [(ar, ai)]
        for _ in range(nbits - 1):
            pows.append(_cmul(*pows[-1], *pows[-1]))

        @pl.when(t == 0)
        def _():
            carry[...] = jnp.zeros_like(carry)
            pr, pi = jnp.ones((t_rows, w), F32), jnp.zeros((t_rows, w), F32)
            for kbit in range(nbits):
                bit = ((idx + 1) >> kbit) & 1
                fr = jnp.where(bit == 1, pows[kbit][0], 1.0)
                fi = jnp.where(bit == 1, pows[kbit][1], 0.0)
                pr, pi = _cmul(pr, pi, fr, fi)
            ptab[:, :w] = pr
            ptab[:, w:] = pi

        xr, xi = x_ref[:, :w], x_ref[:, w:]
        step = 1
        kbit = 0
        while step < t_rows:
            shift = (t_rows - step) if reverse else step
            yr, yi = pltpu.roll(xr, shift, 0), pltpu.roll(xi, shift, 0)
            zr, zi = _cmul(pows[kbit][0], pows[kbit][1], yr, yi)
            keep = idx >= step
            xr = xr + jnp.where(keep, zr, 0.0)
            xi = xi + jnp.where(keep, zi, 0.0)
            step *= 2
            kbit += 1
        cr, ci = carry[0:1, :w], carry[0:1, w:]
        zr, zi = _cmul(ptab[:, :w], ptab[:, w:], cr, ci)
        xr, xi = xr + zr, xi + zi
        o_ref[:, :w] = xr
        o_ref[:, w:] = xi
        last = 0 if reverse else t_rows - 1
        carry[0:1, :] = o_ref[last:last + 1, :]
        if reverse:
            halo_r = jnp.where(t < nt - 1, hh_ref[SUBLANE - 1:SUBLANE, :w], 0.0)
            halo_i = jnp.where(t < nt - 1, hh_ref[SUBLANE - 1:SUBLANE, w:], 0.0)
            hr = jnp.where(row == 0, halo_r, pltpu.roll(h_ref[:, :w], 1, 0))
            hi = jnp.where(row == 0, halo_i, pltpu.roll(h_ref[:, w:], 1, 0))
            _acc(da_ref.at[:, :w], t, _colsum(xr * hr + xi * hi))
            _acc(da_ref.at[:, w:], t, _colsum(xi * hr - xr * hi))

    blk = pl.BlockSpec((t_rows, 2 * w), lambda j, t: (tblk(t), j))
    a_spec = pl.BlockSpec((1, 2 * w), lambda j, t: (0, j))
    in_specs, arrays = [blk, a_spec], [x, a]
    out_shape = [jax.ShapeDtypeStruct((s, 2 * N_STATE), F32)]
    out_specs = [blk]
    if reverse:
        in_specs += [blk, pl.BlockSpec((SUBLANE, 2 * w), lambda j, t: (jnp.maximum(tblk(t) * r8 - 1, 0), j))]
        arrays += [h, h]
        out_shape.append(jax.ShapeDtypeStruct((1, 2 * N_STATE), F32))
        out_specs.append(a_spec)
    res = _pcall(
        body, name=name, out_shape=tuple(out_shape), grid=(ncol, nt), in_specs=in_specs, out_specs=tuple(out_specs),
        scratch_shapes=[pltpu.VMEM((SUBLANE, 2 * w), F32), pltpu.VMEM((t_rows, 2 * w), F32)],
        compiler_params=_params(("parallel", "arbitrary")),
    )(*arrays)
    return res if reverse else res[0]


def _s5_glu_bwd_a(dout, dout_off, y, z):
    s = y.shape[0]
    w = S5_WIDTH

    def fn(i, do_ref, y_ref, z_ref, dz_ref, p_ref):
        dov = do_ref[...]
        sg = _sigmoid(z_ref[...])
        dz_ref[...] = (dov * _gelu(y_ref[...]) * sg * (1.0 - sg)).astype(dz_ref.dtype)
        p_ref[...] = dov * sg

    return _rows(fn, name="s5_glu_bwd", s=s, tm=512, ins=[("row", dout, dout_off, w), ("row", y, 0, w), ("row", z, 0, w)],
                 outs=[("row", w, BF16), ("row", w, F32)])


def _s5_dd(dy, proj):
    s = dy.shape[0]
    w = S5_WIDTH

    def fn(i, dy_ref, u_ref, dd_ref):
        _acc(dd_ref, i, _colsum(dy_ref[...] * u_ref[...]))

    return _rows(fn, name="s5_dd", s=s, tm=512, ins=[("row", dy, 0, w), ("row", proj, P_S5, w)], outs=[("acc", 1, w)])


def _s5_fwd(proj, wl, s):
    bu = _mm(proj, wl["s5_bd"], mode="nn", name="s5_bu", m=s, n=2 * N_STATE, k=S5_WIDTH, a_off=P_S5)
    h = _s5_scan(bu, wl["s5_a"], reverse=False, name="s5_scan_fwd")
    def y_epi(acc, u, d):
        yv = acc + d * u
        return yv, _gelu(yv)

    y, yg = _mm(h, wl["s5_cd"], mode="nn", name="s5_y", m=s, n=S5_WIDTH, k=2 * N_STATE, out=(F32, BF16),
                extras=[("tile", proj, P_S5), ("row", wl["s5_d"])], epi=y_epi)
    z, out = _mm(yg, wl["s5_w_glu"], mode="nn", name="s5_glu", m=s, n=S5_WIDTH, k=S5_WIDTH, out=(F32, BF16),
                 extras=[("tile", y, 0)], epi=lambda acc, yv: (acc, _gelu(yv) * _sigmoid(acc)))
    return out, (h, y, z, yg)


def _s5_bwd(dcat, proj, wl, saved, s):
    h, y, z, yg = saved
    dz, p1 = _s5_glu_bwd_a(dcat, 0, y, z)
    dy = _mm(dz, wl["s5_w_glu"], mode="nt", name="s5_dyg", m=s, n=S5_WIDTH, k=S5_WIDTH,
             extras=[("tile", p1, 0), ("tile", y, 0)], epi=lambda acc, p, yv: ((p + acc) * _dgelu(yv),))
    gh = _mm(dy, wl["s5_cd"], mode="nt", name="s5_gh", m=s, n=2 * N_STATE, k=S5_WIDTH)
    adj, da = _s5_scan(gh, wl["s5_a"], reverse=True, h=h, name="s5_scan_bwd")
    du = _mm(adj, wl["s5_bd"], mode="nt", name="s5_du", m=s, n=S5_WIDTH, k=2 * N_STATE,
             extras=[("tile", dy, 0), ("row", wl["s5_d"])], epi=lambda acc, dyv, d: (acc + dyv * d,))
    grads = {
        "s5_a": da,
        "s5_bd": _mm(proj, adj, mode="tn", name="s5_dbd", m=S5_WIDTH, n=2 * N_STATE, k=s, a_off=P_S5),
        "s5_cd": _mm(h, dy, mode="tn", name="s5_dcd", m=2 * N_STATE, n=S5_WIDTH, k=s),
        "s5_d": _s5_dd(dy, proj),
        "s5_w_glu": _mm(yg, dz, mode="tn", name="s5_dwglu", m=S5_WIDTH, n=S5_WIDTH, k=s),
    }
    return du, grads


def _mla_prep(qraw, kvraw, proj, cs):
    s = qraw.shape[0]
    hw = MLA_HEADS * LANE

    def fn(i, q_ref, kv_ref, kr_ref, cs_ref, qn_ref, qr_ref, kvb_ref, krb_ref):
        csv = cs_ref[...]
        qn_ref[...] = q_ref[:, :hw].astype(BF16)
        for hd in range(MLA_HEADS):
            p = q_ref[:, hw + hd * LANE:hw + (hd + 1) * LANE] * csv
            qr_ref[:, hd * LANE:(hd + 1) * LANE] = (p + pltpu.roll(p, LANE // 2, 1)).astype(BF16)
        kvb_ref[...] = kv_ref[...].astype(BF16)
        p = kr_ref[...] * csv
        lane = lax.broadcasted_iota(jnp.int32, p.shape, 1)
        krb_ref[...] = jnp.where(lane < LANE // 2, p + pltpu.roll(p, LANE // 2, 1), 0.0).astype(BF16)

    return _rows(fn, name="mla_prep", s=s, tm=256,
                 ins=[("row", qraw, 0, 2 * hw), ("row", kvraw, 0, 2 * hw), ("row", proj, P_KR, LANE), ("row", cs, 0, LANE)],
                 outs=[("row", hw, BF16), ("row", hw, BF16), ("row", 2 * hw, BF16), ("row", LANE, BF16)])


def _mla_rope_bwd(dqn, dqr2, dkr2h, cs):
    s = dqn.shape[0]
    hw = MLA_HEADS * LANE

    def fn(i, dqn_ref, dqr_ref, dkr_ref, cs_ref, dq_ref, dk_ref):
        csv = cs_ref[...]
        dq_ref[:, :hw] = dqn_ref[...].astype(BF16)
        ksum = jnp.zeros(csv.shape, F32)
        for hd in range(MLA_HEADS):
            g = dqr_ref[:, hd * LANE:(hd + 1) * LANE]
            dq_ref[:, hw + hd * LANE:hw + (hd + 1) * LANE] = ((g + pltpu.roll(g, LANE // 2, 1)) * csv).astype(BF16)
            ksum = ksum + dkr_ref[:, hd * LANE:(hd + 1) * LANE]
        dk_ref[...] = ksum * csv

    return _rows(fn, name="mla_rope_bwd", s=s, tm=256,
                 ins=[("row", dqn, 0, hw), ("row", dqr2, 0, hw), ("row", dkr2h, 0, hw), ("row", cs, 0, LANE)],
                 outs=[("row", 2 * hw, BF16), ("row", LANE, F32)])


def _attn_scores(qn_ref, qr_ref, kn_ref, kr_ref, qi, ki, tq, tk):
    scale = (MLA_NOPE + MLA_ROPE) ** -0.5
    sc = (_dot(qn_ref[...], kn_ref[...], NT) + _dot(qr_ref[...], kr_ref[...], NT)) * scale
    assert tq == tk
    return sc, lax.broadcasted_iota(jnp.int32, (tq, tk), 1) <= lax.broadcasted_iota(jnp.int32, (tq, tk), 0)


def _attn_specs(tq, tk, q_of, k_of):
    qs = pl.BlockSpec((tq, LANE), lambda h, a, b: (q_of(a, b), h))
    return [qs, qs,
            pl.BlockSpec((tk, LANE), lambda h, a, b: (k_of(a, b), 2 * h)),
            pl.BlockSpec((tk, LANE), lambda h, a, b: (k_of(a, b), 2 * h + 1)),
            pl.BlockSpec((tk, LANE), lambda h, a, b: (k_of(a, b), 0))]


def _flash_fwd(qn, qr2, kv, kr2):
    s = qn.shape[0]
    tq = tk = min(512, s)
    nq = s // tq

    def body(qn_ref, qr_ref, kn_ref, v_ref, kr_ref, o_ref, lse_ref, m_sc, l_sc, acc_sc):
        qi, ki = pl.program_id(1), pl.program_id(2)

        @pl.when(ki == 0)
        def _():
            m_sc[...] = jnp.full(m_sc.shape, -jnp.inf, F32)
            l_sc[...] = jnp.zeros_like(l_sc)
            acc_sc[...] = jnp.zeros_like(acc_sc)

        def step(diagonal):
            sc, causal = _attn_scores(qn_ref, qr_ref, kn_ref, kr_ref, qi, ki, tq, tk)
            if diagonal:
                sc = jnp.where(causal, sc, -1e30)
            m_new = jnp.maximum(m_sc[...], jnp.max(sc, axis=1, keepdims=True))
            alpha = jnp.exp(m_sc[...] - m_new)
            p = jnp.exp(sc - m_new)
            l_sc[...] = alpha * l_sc[...] + jnp.sum(p, axis=1, keepdims=True)
            acc_sc[...] = alpha * acc_sc[...] + _dot(p, v_ref[...], NN)
            m_sc[...] = m_new

        @pl.when(ki < qi)
        def _():
            step(False)

        @pl.when(ki == qi)
        def _():
            step(True)
            o_ref[...] = acc_sc[...] / l_sc[...]
            lse_ref[0] = m_sc[...] + jnp.log(l_sc[...])

    return _pcall(
        body, name="mla_flash_fwd",
        out_shape=(jax.ShapeDtypeStruct((s, MLA_WIDTH), F32), jax.ShapeDtypeStruct((MLA_HEADS, s, 1), F32)),
        grid=(MLA_HEADS, nq, nq),
        in_specs=_attn_specs(tq, tk, lambda a, b: a, lambda a, b: jnp.minimum(a, b)),
        out_specs=(pl.BlockSpec((tq, LANE), lambda h, a, b: (a, h)), pl.BlockSpec((1, tq, 1), lambda h, a, b: (h, a, 0))),
        scratch_shapes=[pltpu.VMEM((tq, 1), F32), pltpu.VMEM((tq, 1), F32), pltpu.VMEM((tq, LANE), F32)],
        compiler_params=_params(("parallel", "parallel", "arbitrary")),
    )(qn, qr2, kv, kv, kr2)


def _flash_bwd_dq(qn, qr2, kv, kr2, do, do_off, o, lse):
    s = qn.shape[0]
    tq = tk = min(512, s)
    nq = s // tq
    scale = (MLA_NOPE + MLA_ROPE) ** -0.5
    ob = do_off // LANE

    def body(qn_ref, qr_ref, kn_ref, v_ref, kr_ref, do_ref, o_ref, lse_ref, dqn_ref, dqr_ref, dl_ref, dl_sc, an_sc, ar_sc):
        qi, ki = pl.program_id(1), pl.program_id(2)

        @pl.when(ki == 0)
        def _():
            dl_sc[...] = jnp.sum(do_ref[...] * o_ref[...], axis=1, keepdims=True)
            an_sc[...] = jnp.zeros_like(an_sc)
            ar_sc[...] = jnp.zeros_like(ar_sc)

        def step(diagonal):
            sc, causal = _attn_scores(qn_ref, qr_ref, kn_ref, kr_ref, qi, ki, tq, tk)
            p = jnp.exp(sc - lse_ref[0])
            if diagonal:
                p = jnp.where(causal, p, 0.0)
            dp = _dot(do_ref[...], v_ref[...], NT)
            ds = (p * (dp - dl_sc[...]) * scale).astype(BF16)
            an_sc[...] += _dot(ds, kn_ref[...], NN)
            ar_sc[...] += _dot(ds, kr_ref[...], NN)

        @pl.when(ki < qi)
        def _():
            step(False)

        @pl.when(ki == qi)
        def _():
            step(True)
            dqn_ref[...] = an_sc[...]
            dqr_ref[...] = ar_sc[...]
            dl_ref[0] = dl_sc[...]

    qblk = pl.BlockSpec((tq, LANE), lambda h, a, b: (a, h))
    vec = pl.BlockSpec((1, tq, 1), lambda h, a, b: (h, a, 0))
    return _pcall(
        body, name="mla_flash_dq",
        out_shape=(jax.ShapeDtypeStruct((s, MLA_WIDTH), F32), jax.ShapeDtypeStruct((s, MLA_WIDTH), F32),
                   jax.ShapeDtypeStruct((MLA_HEADS, s, 1), F32)),
        grid=(MLA_HEADS, nq, nq),
        in_specs=_attn_specs(tq, tk, lambda a, b: a, lambda a, b: jnp.minimum(a, b))
        + [pl.BlockSpec((tq, LANE), lambda h, a, b: (a, h + ob)), qblk, vec],
        out_specs=(qblk, qblk, vec),
        scratch_shapes=[pltpu.VMEM((tq, 1), F32), pltpu.VMEM((tq, LANE), F32), pltpu.VMEM((tq, LANE), F32)],
        compiler_params=_params(("parallel", "parallel", "arbitrary")),
    )(qn, qr2, kv, kv, kr2, do, o, lse)


def _flash_bwd_dkv(qn, qr2, kv, kr2, do, do_off, lse, delta):
    s = qn.shape[0]
    tq = tk = min(512, s)
    nq = s // tq
    scale = (MLA_NOPE + MLA_ROPE) ** -0.5
    ob = do_off // LANE

    def body(qn_ref, qr_ref, kn_ref, v_ref, kr_ref, do_ref, lse_ref, dl_ref, dkv_ref, dkr_ref, akn_sc, av_sc, akr_sc):
        ki, qi = pl.program_id(1), pl.program_id(2)

        @pl.when(qi == 0)
        def _():
            akn_sc[...] = jnp.zeros_like(akn_sc)
            av_sc[...] = jnp.zeros_like(av_sc)
            akr_sc[...] = jnp.zeros_like(akr_sc)

        def step(diagonal):
            sc, causal = _attn_scores(qn_ref, qr_ref, kn_ref, kr_ref, qi, ki, tq, tk)
            p = jnp.exp(sc - lse_ref[0])
            if diagonal:
                p = jnp.where(causal, p, 0.0)
            dp = _dot(do_ref[...], v_ref[...], NT)
            ds = (p * (dp - dl_ref[0]) * scale).astype(BF16)
            av_sc[...] += _dot(p, do_ref[...], TN)
            akn_sc[...] += _dot(ds, qn_ref[...], TN)
            akr_sc[...] += _dot(ds, qr_ref[...], TN)

        @pl.when(qi > ki)
        def _():
            step(False)

        @pl.when(qi == ki)
        def _():
            step(True)

        @pl.when(qi == nq - 1)
        def _():
            dkv_ref[:, :LANE] = akn_sc[...]
            dkv_ref[:, LANE:] = av_sc[...]
            dkr_ref[...] = akr_sc[...]

    q_of = lambda a, b: jnp.maximum(a, b)
    k_of = lambda a, b: a
    vec = pl.BlockSpec((1, tq, 1), lambda h, a, b: (h, q_of(a, b), 0))
    return _pcall(
        body, name="mla_flash_dkv",
        out_shape=(jax.ShapeDtypeStruct((s, 2 * MLA_WIDTH), F32), jax.ShapeDtypeStruct((s, MLA_WIDTH), F32)),
        grid=(MLA_HEADS, nq, nq),
        in_specs=_attn_specs(tq, tk, q_of, k_of)
        + [pl.BlockSpec((tq, LANE), lambda h, a, b: (q_of(a, b), h + ob)), vec, vec],
        out_specs=(pl.BlockSpec((tk, 2 * LANE), lambda h, a, b: (a, h)), pl.BlockSpec((tk, LANE), lambda h, a, b: (a, h))),
        scratch_shapes=[pltpu.VMEM((tk, LANE), F32), pltpu.VMEM((tk, LANE), F32), pltpu.VMEM((tk, LANE), F32)],
        compiler_params=_params(("parallel", "parallel", "arbitrary")),
    )(qn, qr2, kv, kv, kr2, do, lse, delta)


def _mla_fwd(proj, wl, cs, s):
    cqn = _rms_fwd(proj, P_CQ, MLA_Q_RANK, wl["mla_q_norm"], "mla_q_rms")
    ckvn = _rms_fwd(proj, P_CKV, MLA_KV_RANK, wl["mla_kv_norm"], "mla_kv_rms")
    qraw = _mm(cqn, wl["mla_wq"], mode="nn", name="mla_q_proj", m=s, n=2 * MLA_WIDTH, k=MLA_Q_RANK)
    kvraw = _mm(ckvn, wl["mla_w_ukv"], mode="nn", name="mla_kv_proj", m=s, n=2 * MLA_WIDTH, k=MLA_KV_RANK)
    qn, qr2, kv, kr2 = _mla_prep(qraw, kvraw, proj, cs)
    o, lse = _flash_fwd(qn, qr2, kv, kr2)
    return o, (cqn, ckvn, qn, qr2, kv, kr2, o, lse)


def _mla_bwd(dcat, proj, wl, cs, saved, s):
    cqn, ckvn, qn, qr2, kv, kr2, o, lse = saved
    dqn, dqr2, delta = _flash_bwd_dq(qn, qr2, kv, kr2, dcat, S5_WIDTH, o, lse)
    dkv, dkr2h = _flash_bwd_dkv(qn, qr2, kv, kr2, dcat, S5_WIDTH, lse, delta)
    dqraw, dkr = _mla_rope_bwd(dqn, dqr2, dkr2h, cs)
    dcqn = _mm(dqraw, wl["mla_wq"], mode="nt", name="mla_dcqn", m=s, n=MLA_Q_RANK, k=2 * MLA_WIDTH)
    dckvn = _mm(dkv, wl["mla_w_ukv"], mode="nt", name="mla_dckvn", m=s, n=MLA_KV_RANK, k=2 * MLA_WIDTH)
    dcq, dqg = _rms_bwd(dcqn, proj, P_CQ, MLA_Q_RANK, wl["mla_q_norm"], "mla_q_rms_bwd")
    dckv, dkvg = _rms_bwd(dckvn, proj, P_CKV, MLA_KV_RANK, wl["mla_kv_norm"], "mla_kv_rms_bwd")
    grads = {
        "mla_wq": _mm(cqn, dqraw, mode="tn", name="mla_dwq", m=MLA_Q_RANK, n=2 * MLA_WIDTH, k=s),
        "mla_w_ukv": _mm(ckvn, dkv, mode="tn", name="mla_dwukv", m=MLA_KV_RANK, n=2 * MLA_WIDTH, k=s),
        "mla_q_norm": dqg,
        "mla_kv_norm": dkvg,
    }
    return dcq, dckv, dkr, grads


HG_ROWS = 256


def _cumsum_rows(x, row, reverse=False):
    n = x.shape[0]
    step = 1
    while step < n:
        if reverse:
            x = x + jnp.where(row < n - step, pltpu.roll(x, n - step, 0), 0.0)
        else:
            x = x + jnp.where(row >= step, pltpu.roll(x, step, 0), 0.0)
        step *= 2
    return x


def _hg_gates(hq, z, lb):
    sig = _sigmoid(z)
    sigm = _sigmoid(-z)
    f = lb + (1.0 - lb) * sig
    return _silu(hq), (1.0 - lb) * sigm, jnp.log(f), sig, sigm, f


HG_SUB = 16
HG_NSUB = HG_CHUNK // HG_SUB


def _hg_offdiag(q, k, b, row, b_buf):
    ops = []
    for j in range(HG_NSUB - 1):
        e = (j + 1) * HG_SUB
        be = b_buf[e - 1:e, :]
        fj = jnp.where(row >= e, jnp.exp(jnp.minimum(b - be, 0.0)), 0.0)
        gj = jnp.where((row >= e - HG_SUB) & (row < e), jnp.exp(jnp.minimum(be - b, 0.0)), 0.0)
        ops.append((q * fj, k * gj, fj, gj))
    return ops


def _hg_diag_weights(qb, bb, ks, bs, rr, srow):
    e = jnp.exp(jnp.minimum(bb - bs, 0.0))
    keep = rr >= srow
    w = jnp.where(keep, jnp.sum(qb * ks * e, axis=1, keepdims=True), 0.0)
    return e, keep, w


def _hg_specs(rows):
    def col(off):
        return pl.BlockSpec((rows, LANE), lambda h, n, off=off: (n, off // LANE + h))
    return col


def _hg_fwd(proj, lb, gamma):
    s = proj.shape[0]
    rows = min(HG_ROWS, s)
    c = HG_CHUNK
    npc = rows // c
    nb = s // rows

    def body(hq_ref, hf_ref, hi_ref, hg_ref, lb_ref, gm_ref, y_ref, o_ref, st_ref, st_sc, k_buf, b_buf):
        n = pl.program_id(1)

        @pl.when(n == 0)
        def _():
            st_sc[...] = jnp.zeros_like(st_sc)

        row = lax.broadcasted_iota(jnp.int32, (c, LANE), 0)
        rr = lax.broadcasted_iota(jnp.int32, (HG_SUB, 1), 0)
        lbv = lb_ref[...]
        for ch in range(npc):
            sl = slice(ch * c, (ch + 1) * c)
            q, k, logf, _, _, _ = _hg_gates(hq_ref[sl, :], hf_ref[sl, :], lbv)
            v = hi_ref[sl, :]
            b = _cumsum_rows(logf, row)
            bl = _colsum(logf)
            k_buf[...] = k
            b_buf[...] = b
            st = st_sc[...]
            st_ref[0, ch] = st
            a_off = sum(_dot(qf, kg, NT) for qf, kg, _, _ in _hg_offdiag(q, k, b, row, b_buf))
            o = _dot(q * jnp.exp(b), st, NT) + _dot(a_off, v, NN)
            st_sc[...] = st * jnp.exp(bl) + _dot(v, k * jnp.exp(bl - b), TN)
            diag = []
            for i in range(HG_NSUB):
                r0 = i * HG_SUB
                qb, bb = q[r0:r0 + HG_SUB], b[r0:r0 + HG_SUB]
                acc = jnp.zeros((HG_SUB, LANE), F32)
                for srow in range(HG_SUB):
                    t = r0 + srow
                    _, _, w = _hg_diag_weights(qb, bb, k_buf[t:t + 1, :], b_buf[t:t + 1, :], rr, srow)
                    acc = acc + w * hi_ref[ch * c + t:ch * c + t + 1, :]
                diag.append(acc)
            o = o + jnp.concatenate(diag, axis=0)
            o_ref[sl, :] = o
            r = lax.rsqrt(_rowmean(o * o) + EPS)
            y_ref[sl, :] = (((o * r) * gm_ref[...]) * _silu(hg_ref[sl, :])).astype(y_ref.dtype)

    col = _hg_specs(rows)
    out_blk = pl.BlockSpec((rows, LANE), lambda h, n: (n, h))
    return _pcall(
        body, name="hgrn_fwd",
        out_shape=(jax.ShapeDtypeStruct((s, HG_WIDTH), BF16), jax.ShapeDtypeStruct((s, HG_WIDTH), F32),
                   jax.ShapeDtypeStruct((HG_HEADS, s // c, HG_DV, HG_DK), F32)),
        grid=(HG_HEADS, nb),
        in_specs=[col(P_HQ), col(P_HF), col(P_HI), col(P_HG),
                  pl.BlockSpec((1, LANE), lambda h, n: (0, h)), pl.BlockSpec((1, LANE), lambda h, n: (0, 0))],
        out_specs=(out_blk, out_blk, pl.BlockSpec((1, npc, HG_DV, HG_DK), lambda h, n: (h, n, 0, 0))),
        scratch_shapes=[pltpu.VMEM((HG_DV, HG_DK), F32), pltpu.VMEM((c, LANE), F32), pltpu.VMEM((c, LANE), F32)],
        compiler_params=_params(("parallel", "arbitrary")),
    )(proj, proj, proj, proj, lb, gamma)


def _hg_bwd(dcat, dy_off, proj, lb, gamma, o_saved, states):
    s = proj.shape[0]
    rows = min(HG_ROWS, s)
    c = HG_CHUNK
    npc = rows // c
    nb = s // rows
    yb = dy_off // LANE

    def body(hq_ref, hf_ref, hi_ref, hg_ref, lb_ref, gm_ref, dy_ref, o_ref, st_ref,
             dq_ref, df_ref, di_ref, dg_ref, dlb_ref, dgm_ref, dst_sc, k_buf, b_buf, dk_buf, dv_buf):
        n = pl.program_id(1)

        @pl.when(n == 0)
        def _():
            dst_sc[...] = jnp.zeros_like(dst_sc)
            dlb_ref[...] = jnp.zeros_like(dlb_ref)
            dgm_ref[...] = jnp.zeros_like(dgm_ref)

        row = lax.broadcasted_iota(jnp.int32, (c, LANE), 0)
        rr = lax.broadcasted_iota(jnp.int32, (HG_SUB, 1), 0)
        lbv, gmv = lb_ref[...], gm_ref[...]
        for ch in reversed(range(npc)):
            sl = slice(ch * c, (ch + 1) * c)
            hq, z, v, g = hq_ref[sl, :], hf_ref[sl, :], hi_ref[sl, :], hg_ref[sl, :]
            q, k, logf, sig, sigm, f = _hg_gates(hq, z, lbv)
            b = _cumsum_rows(logf, row)
            bl = _colsum(logf)
            k_buf[...] = k
            b_buf[...] = b
            eb, ebl = jnp.exp(b), jnp.exp(bl)
            qe, kl = q * eb, k * jnp.exp(bl - b)
            st = st_ref[0, ch]
            dst = dst_sc[...]
            o, dyv = o_ref[sl, :], dy_ref[sl, :]
            r = lax.rsqrt(_rowmean(o * o) + EPS)
            on = o * r
            sg = _silu(g)
            dgm_ref[0] += _colsum(dyv * on * sg)
            dg_ref[sl, :] = dyv * on * gmv * _dsilu(g)
            go = dyv * gmv * sg
            do = r * (go - on * _rowmean(go * on))
            dq = _dot(do, st, NN) * eb
            dkl = _dot(v, dst, NN)
            dk = dkl * jnp.exp(bl - b)
            dv = _dot(kl, dst, NT)
            dbl = _colsum(dst * st) * ebl + _colsum(dkl * kl)
            dst_sc[...] = dst * ebl + _dot(do, qe, TN)
            ops = _hg_offdiag(q, k, b, row, b_buf)
            da = _dot(do, v, NT)
            a_off = sum(_dot(qf, kg, NT) for qf, kg, _, _ in ops)
            dv = dv + _dot(a_off, do, TN)
            for qf, kg, fj, gj in ops:
                dq = dq + _dot(da, kg, NN) * fj
                dk = dk + _dot(da, qf, TN) * gj
            dq_diag = []
            for i in range(HG_NSUB):
                r0 = i * HG_SUB
                qb, bb, dob = q[r0:r0 + HG_SUB], b[r0:r0 + HG_SUB], do[r0:r0 + HG_SUB]
                acc = jnp.zeros((HG_SUB, LANE), F32)
                for srow in range(HG_SUB):
                    t = r0 + srow
                    ks = k_buf[t:t + 1, :]
                    e, keep, w = _hg_diag_weights(qb, bb, ks, b_buf[t:t + 1, :], rr, srow)
                    dw = jnp.where(keep, jnp.sum(dob * hi_ref[ch * c + t:ch * c + t + 1, :], axis=1, keepdims=True), 0.0)
                    dv_buf[t:t + 1, :] = _colsum(w * dob)
                    dk_buf[t:t + 1, :] = _colsum(dw * qb * e)
                    acc = acc + dw * (ks * e)
                dq_diag.append(acc)
            dq = dq + jnp.concatenate(dq_diag, axis=0)
            dk = dk + dk_buf[...]
            di_ref[sl, :] = dv + dv_buf[...]
            db = q * dq - k * dk + jnp.where(row == c - 1, dbl, 0.0)
            dlogf = _cumsum_rows(db, row, reverse=True)
            dq_ref[sl, :] = dq * _dsilu(hq)
            s1 = sig * (1.0 - sig) * (1.0 - lbv)
            df_ref[sl, :] = dlogf * s1 / f - dk * s1
            dlb_ref[0] += _colsum(dlogf * sigm / f - dk * sigm)

    def col(off):
        return pl.BlockSpec((rows, LANE), lambda h, n, off=off: (nb - 1 - n, off // LANE + h))

    out_blk = pl.BlockSpec((rows, LANE), lambda h, n: (nb - 1 - n, h))
    acc_blk = pl.BlockSpec((1, 1, LANE), lambda h, n: (h, 0, 0))
    res = _pcall(
        body, name="hgrn_bwd",
        out_shape=tuple(jax.ShapeDtypeStruct((s, HG_WIDTH), F32) for _ in range(4))
        + (jax.ShapeDtypeStruct((HG_HEADS, 1, LANE), F32), jax.ShapeDtypeStruct((HG_HEADS, 1, LANE), F32)),
        grid=(HG_HEADS, nb),
        in_specs=[col(P_HQ), col(P_HF), col(P_HI), col(P_HG),
                  pl.BlockSpec((1, LANE), lambda h, n: (0, h)), pl.BlockSpec((1, LANE), lambda h, n: (0, 0)),
                  pl.BlockSpec((rows, LANE), lambda h, n: (nb - 1 - n, yb + h)), out_blk,
                  pl.BlockSpec((1, npc, HG_DV, HG_DK), lambda h, n: (h, nb - 1 - n, 0, 0))],
        out_specs=(out_blk, out_blk, out_blk, out_blk, acc_blk, acc_blk),
        scratch_shapes=[pltpu.VMEM((HG_DV, HG_DK), F32)] + [pltpu.VMEM((c, LANE), F32)] * 4,
        compiler_params=_params(("parallel", "arbitrary")),
    )(proj, proj, proj, proj, lb, gamma, dcat, o_saved, states)
    dq, df, di, dg, dlb, dgm = res
    return dq, df, di, dg, dlb.reshape(1, HG_WIDTH), dgm.reshape(HG_HEADS, LANE)


def _adamw_math(w, g, m, v):
    m = ADAM_B1 * m + (1.0 - ADAM_B1) * g
    v = ADAM_B2 * v + (1.0 - ADAM_B2) * (g * g)
    m_hat = m / (1.0 - ADAM_B1 ** ADAM_STEP)
    v_hat = v / (1.0 - ADAM_B2 ** ADAM_STEP)
    delta = -ADAM_LR * (m_hat / (jnp.sqrt(v_hat) + ADAM_EPS) + ADAM_WD * w)
    return delta, m, v


def _adamw(w, g, m, v, name):
    shape = w.shape
    width = shape[-1]
    rows = int(np.prod(shape[:-1]))
    tm = rows
    while tm * width * 4 > (1 << 20) and tm % 16 == 0:
        tm //= 2

    def fn(i, w_ref, g_ref, m_ref, v_ref, d_ref, mo_ref, vo_ref):
        d, mn, vn = _adamw_math(w_ref[...], g_ref[...], m_ref[...], v_ref[...])
        d_ref[...] = d
        mo_ref[...] = mn
        vo_ref[...] = vn

    v2 = lambda t: t.reshape(rows, width)
    res = _rows(fn, name=name, s=rows, tm=tm, ins=[("row", v2(t), 0, width) for t in (w, g, m, v)],
                outs=[("row", width, F32)] * 3)
    return tuple(r.reshape(shape) for r in res)


def _ada_grad_adamw(cact_all, dmod_cols, w, m, v):
    kdim, n = w.shape

    def epi(acc, wv, mv, vv):
        return (acc,) + _adamw_math(wv, acc, mv, vv)

    return _mm(cact_all, dmod_cols, mode="tn", name="ada_grad_adamw", m=kdim, n=n, k=cact_all.shape[0],
               tm=256, tn=1024, out=(F32, F32, F32, F32), epi=epi,
               extras=[("tile", w, 0), ("tile", m, 0), ("tile", v, 0)])


def _me():
    return lax.axis_index("x"), lax.axis_index("y"), lax.axis_index("c")


def _flip(k):
    x, y, c = _me()
    return (x ^ ((k >> 2) & 1), y ^ ((k >> 1) & 1), c ^ (k & 1))


def _lin(dev):
    return 4 * dev[0] + 2 * dev[1] + dev[2]


ANY = pl.BlockSpec(memory_space=pl.ANY)


def _all_gather8(x, name):
    def body(x_ref, out_ref, send_sems, recv_sems, local_sem):
        me = _lin(_me())
        mine = pltpu.make_async_copy(x_ref, out_ref.at[me], local_sem)
        mine.start()
        copies = []
        for k in range(1, 8):
            cp = pltpu.make_async_remote_copy(src_ref=x_ref, dst_ref=out_ref.at[me], send_sem=send_sems.at[k - 1],
                                              recv_sem=recv_sems.at[k - 1], device_id=_flip(k), device_id_type=MESH)
            cp.start()
            copies.append(cp)
        for k in range(1, 8):
            pltpu.make_async_remote_copy(src_ref=x_ref, dst_ref=out_ref.at[_lin(_flip(k))], send_sem=send_sems.at[k - 1],
                                         recv_sem=recv_sems.at[k - 1], device_id=_flip(k), device_id_type=MESH).wait_recv()
        for cp in copies:
            cp.wait_send()
        mine.wait()

    return _pcall(
        body, name=name, out_shape=jax.ShapeDtypeStruct((8,) + x.shape, x.dtype),
        in_specs=[ANY], out_specs=ANY,
        scratch_shapes=[pltpu.SemaphoreType.DMA((7,)), pltpu.SemaphoreType.DMA((7,)), pltpu.SemaphoreType.DMA],
    )(x)


CHIP_FLIPS = (2, 4, 6)


def _row_tile(r, cdim):
    best = SUBLANE
    for t in range(SUBLANE, r + 1, SUBLANE):
        if r % t == 0 and t * cdim * 4 <= (2 << 20):
            best = t
    assert r % best == 0
    return best


def _gather_weights(ws):
    n = len(ws)

    def body(*refs):
        w_refs, out_refs = refs[:n], refs[n:2 * n]
        send_sems, recv_sems, local_sems = refs[2 * n:]
        x, y, c = _me()
        chip = 2 * x + y
        sib = _flip(1)
        mine = [pltpu.make_async_copy(w_refs[a], out_refs[a].at[chip], local_sems.at[a]) for a in range(n)]
        for cp in mine:
            cp.start()

        def slot(a, dev, half):
            return out_refs[a].at[2 * dev[0] + dev[1], half]

        first = []
        for a in range(n):
            for j, k in enumerate(CHIP_FLIPS):
                cp = pltpu.make_async_remote_copy(src_ref=w_refs[a].at[c], dst_ref=slot(a, (x, y), c), send_sem=send_sems.at[6 * a + j],
                                                  recv_sem=recv_sems.at[6 * a + j], device_id=_flip(k), device_id_type=MESH)
                cp.start()
                first.append(cp)
        passed = []
        for a in range(n):
            for j, k in enumerate(CHIP_FLIPS):
                src = _flip(k)
                landed = slot(a, src, c)
                pltpu.make_async_remote_copy(src_ref=landed, dst_ref=landed, send_sem=send_sems.at[6 * a + j],
                                             recv_sem=recv_sems.at[6 * a + j], device_id=src, device_id_type=MESH).wait_recv()
                cp = pltpu.make_async_remote_copy(src_ref=landed, dst_ref=landed, send_sem=send_sems.at[6 * a + 3 + j],
                                                  recv_sem=recv_sems.at[6 * a + 3 + j], device_id=sib, device_id_type=MESH)
                cp.start()
                passed.append(cp)
        for a in range(n):
            for j, k in enumerate(CHIP_FLIPS):
                got = slot(a, _flip(k), 1 - c)
                pltpu.make_async_remote_copy(src_ref=got, dst_ref=got, send_sem=send_sems.at[6 * a + 3 + j],
                                             recv_sem=recv_sems.at[6 * a + 3 + j], device_id=sib, device_id_type=MESH).wait_recv()
        for cp in first + passed:
            cp.wait_send()
        for cp in mine:
            cp.wait()

    return _pcall(
        body, name="gather_weights", out_shape=tuple(jax.ShapeDtypeStruct((N_CHIPS,) + w.shape, w.dtype) for w in ws),
        in_specs=[ANY] * n, out_specs=(ANY,) * n,
        scratch_shapes=[pltpu.SemaphoreType.DMA((6 * n,)), pltpu.SemaphoreType.DMA((6 * n,)), pltpu.SemaphoreType.DMA((n,))],
    )(*ws)


def _sibling_halves(gs):
    n = len(gs)

    def body(*refs):
        s_refs, out_refs = refs[:n], refs[n:2 * n]
        send_sems, recv_sems = refs[2 * n:]
        c = lax.axis_index("c")
        sib = _flip(1)
        copies = []
        for a in range(n):
            for j in range(N_CHIPS):
                cp = pltpu.make_async_remote_copy(src_ref=s_refs[a].at[j, 1 - c], dst_ref=out_refs[a].at[j], send_sem=send_sems.at[4 * a + j],
                                                  recv_sem=recv_sems.at[4 * a + j], device_id=sib, device_id_type=MESH)
                cp.start()
                copies.append(cp)
        for cp in copies:
            cp.wait()

    return _pcall(
        body, name="rs_sibling_halves", out_shape=tuple(jax.ShapeDtypeStruct((N_CHIPS,) + g.shape[2:], g.dtype) for g in gs),
        in_specs=[ANY] * n, out_specs=(ANY,) * n,
        scratch_shapes=[pltpu.SemaphoreType.DMA((4 * n,)), pltpu.SemaphoreType.DMA((4 * n,))],
    )(*gs)


def _scatter_to_chips(parts):
    n = len(parts)

    def body(*refs):
        p_refs, out_refs = refs[:n], refs[n:2 * n]
        send_sems, recv_sems = refs[2 * n:]
        copies = []
        for a in range(n):
            for j, k in enumerate(CHIP_FLIPS):
                to = _flip(k)
                cp = pltpu.make_async_remote_copy(src_ref=p_refs[a].at[2 * to[0] + to[1]], dst_ref=out_refs[a].at[j],
                                                  send_sem=send_sems.at[3 * a + j], recv_sem=recv_sems.at[3 * a + j],
                                                  device_id=to, device_id_type=MESH)
                cp.start()
                copies.append(cp)
        for cp in copies:
            cp.wait()

    return _pcall(
        body, name="scatter_to_chips", out_shape=tuple(jax.ShapeDtypeStruct((3,) + p.shape[1:], p.dtype) for p in parts),
        in_specs=[ANY] * n, out_specs=(ANY,) * n,
        scratch_shapes=[pltpu.SemaphoreType.DMA((3 * n,)), pltpu.SemaphoreType.DMA((3 * n,))],
    )(*parts)


def _sibling_result(halves):
    n = len(halves)

    def body(*refs):
        h_refs, out_refs = refs[:n], refs[n:2 * n]
        send_sems, recv_sems, local_sems = refs[2 * n:]
        c = lax.axis_index("c")
        sib = _flip(1)
        copies, local = [], []
        for a in range(n):
            lc = pltpu.make_async_copy(h_refs[a], out_refs[a].at[c], local_sems.at[a])
            lc.start()
            local.append(lc)
            cp = pltpu.make_async_remote_copy(src_ref=h_refs[a], dst_ref=out_refs[a].at[c], send_sem=send_sems.at[a],
                                              recv_sem=recv_sems.at[a], device_id=sib, device_id_type=MESH)
            cp.start()
            copies.append(cp)
        for a in range(n):
            pltpu.make_async_remote_copy(src_ref=h_refs[a], dst_ref=out_refs[a].at[1 - c], send_sem=send_sems.at[a],
                                         recv_sem=recv_sems.at[a], device_id=sib, device_id_type=MESH).wait_recv()
        for cp in copies:
            cp.wait_send()
        for lc in local:
            lc.wait()

    return _pcall(
        body, name="rs_sibling_result", out_shape=tuple(jax.ShapeDtypeStruct((2,) + h.shape, h.dtype) for h in halves),
        in_specs=[ANY] * n, out_specs=(ANY,) * n,
        scratch_shapes=[pltpu.SemaphoreType.DMA((n,)), pltpu.SemaphoreType.DMA((n,)), pltpu.SemaphoreType.DMA((n,))],
    )(*halves)


def _add_halves(g, r1, name):
    n, _, r, cdim = g.shape
    tm = _row_tile(r, cdim)

    def body(c_ref, g_ref, r_ref, o_ref):
        o_ref[...] = g_ref[...] + r_ref[...]

    return _pcall(
        body, name=name, out_shape=jax.ShapeDtypeStruct((n, r, cdim), g.dtype),
        grid_spec=pltpu.PrefetchScalarGridSpec(
            num_scalar_prefetch=1, grid=(n, r // tm),
            in_specs=[pl.BlockSpec((None, None, tm, cdim), lambda j, i, c_ref: (j, c_ref[0], i, 0)),
                      pl.BlockSpec((None, tm, cdim), lambda j, i, c_ref: (j, i, 0))],
            out_specs=pl.BlockSpec((None, tm, cdim), lambda j, i, c_ref: (j, i, 0))),
        compiler_params=_params(("parallel", "parallel")),
    )(lax.axis_index("c").astype(jnp.int32).reshape(1), g, r1)


def _add_chips(part, got, name):
    _, r, cdim = part.shape
    tm = _row_tile(r, cdim)

    def body(chip_ref, p_ref, g_ref, o_ref):
        o_ref[...] = ((p_ref[...] + g_ref[0]) + g_ref[1]) + g_ref[2]

    chip = (2 * lax.axis_index("x") + lax.axis_index("y")).astype(jnp.int32).reshape(1)
    return _pcall(
        body, name=name, out_shape=jax.ShapeDtypeStruct((r, cdim), part.dtype),
        grid_spec=pltpu.PrefetchScalarGridSpec(
            num_scalar_prefetch=1, grid=(r // tm,),
            in_specs=[pl.BlockSpec((None, tm, cdim), lambda i, chip_ref: (chip_ref[0], i, 0)),
                      pl.BlockSpec((3, tm, cdim), lambda i, chip_ref: (0, i, 0))],
            out_specs=pl.BlockSpec((tm, cdim), lambda i, chip_ref: (i, 0))),
        compiler_params=_params(("parallel",)),
    )(chip, part, got)


def _reduce_scatter(gs, names):
    r1 = _sibling_halves(gs)
    parts = [_add_halves(g, r, "add_halves_" + nm) for g, r, nm in zip(gs, r1, names)]
    got = _scatter_to_chips(parts)
    mine = [_add_chips(p, q, "add_chips_" + nm) for p, q, nm in zip(parts, got, names)]
    return _sibling_result(mine)


def _sum8(x):
    _, r, n = x.shape
    tm = 128 if r % 128 == 0 else r

    def body(x_ref, o_ref):
        acc = x_ref[0]
        for d in range(1, 8):
            acc = acc + x_ref[d]
        o_ref[...] = acc

    return _pcall(body, name="sum8", out_shape=jax.ShapeDtypeStruct((r, n), x.dtype), grid=(r // tm,),
                  in_specs=[pl.BlockSpec((8, tm, n), lambda i: (0, i, 0))], out_specs=pl.BlockSpec((tm, n), lambda i: (i, 0)),
                  compiler_params=_params(("parallel",)))(x)


SHARDED = ("w_in", "s5_w_glu", "mla_w_uq", "mla_w_ukv", "w_out", "ffn_w_up", "ffn_w_down")
COL_SHARDED = ("w_in", "mla_w_uq", "mla_w_ukv", "ffn_w_up")
REPLICATED = ("s5_lambda_re", "s5_lambda_im", "s5_log_dt", "s5_b_re", "s5_b_im", "s5_c_re", "s5_c_im", "s5_d",
              "mla_q_norm", "mla_kv_norm", "hg_lb_logits", "hg_out_norm", "mix_pre_norm", "mix_post_norm",
              "ffn_pre_norm", "ffn_post_norm", "ffn_conv_b")
WEIGHTS = ("w_in", "s5_lambda_re", "s5_lambda_im", "s5_log_dt", "s5_b_re", "s5_b_im", "s5_c_re", "s5_c_im", "s5_d",
           "s5_w_glu", "mla_q_norm", "mla_w_uq", "mla_kv_norm", "mla_w_ukv", "hg_lb_logits", "hg_out_norm", "w_out",
           "mix_pre_norm", "mix_post_norm", "ffn_pre_norm", "ffn_post_norm", "ffn_w_up", "ffn_conv_w", "ffn_conv_b",
           "ffn_w_down", "w_ada", "b_ada")


FF_PAD = 256


def _halves(t):
    return t.reshape(t.shape[:-2] + (2, t.shape[-2] // 2, t.shape[-1]))


def _unhalve(t):
    return t.reshape(t.shape[:-3] + (2 * t.shape[-2], t.shape[-1]))


def _cols_from_chips(t):
    return jnp.concatenate([t[j] for j in range(N_CHIPS)], axis=1)


def _swap_half(t):
    half = t.shape[-1] // 2
    return jnp.concatenate([-t[..., half:], t[..., :half]], axis=-1)


def _prep_win(w):
    s5, cq, ckv, kr, hq, hf, hi, hg = jnp.split(w, (512, 1024, 1280, 1344, 1856, 2368, 2880), axis=1)
    pad = jnp.zeros((w.shape[0], PROJ_W - 3456), w.dtype)
    return jnp.concatenate([s5, cq, hq, hf, hi, hg, ckv, kr, _swap_half(kr), pad], axis=1)


def _prep_wq(w):
    w3 = w.reshape(w.shape[0], MLA_HEADS, MLA_NOPE + MLA_ROPE)
    nope, rope = w3[..., :MLA_NOPE], w3[..., MLA_NOPE:]
    pair = jnp.concatenate([rope, _swap_half(rope)], axis=-1)
    return jnp.concatenate([nope.reshape(w.shape[0], -1), pair.reshape(w.shape[0], -1)], axis=1)


def _pad_ff_cols(w, cpad):
    r = w.shape[0]
    w3 = w.reshape(r, N_CHIPS, -1)
    return jnp.pad(w3, ((0, 0), (0, 0), (0, cpad - w3.shape[2]))).reshape(r, N_CHIPS * cpad)


def _pad_ff_rows(w, cpad):
    w3 = w.reshape(2, 2 * w.shape[1], w.shape[2])
    return jnp.pad(w3, ((0, 0), (0, cpad - w3.shape[1]), (0, 0))).reshape(2 * cpad, w.shape[2])


def _interleave(re, im, axis):
    re, im = jnp.moveaxis(re, axis, -1), jnp.moveaxis(im, axis, -1)
    lead = re.shape[:-1]
    both = jnp.stack([re.reshape(lead + (N_STATE // SCAN_W, SCAN_W)), im.reshape(lead + (N_STATE // SCAN_W, SCAN_W))], axis=-2)
    return jnp.moveaxis(both.reshape(lead + (2 * N_STATE,)), -1, axis)


def _s5_prep(lre, lim, logdt, bre, bim, cre, cim):
    dt = jnp.exp(logdt)[:, None]
    er = jnp.exp(lre * dt)
    ar, ai = er * jnp.cos(lim * dt), er * jnp.sin(lim * dt)
    nr, den = ar - 1.0, lre * lre + lim * lim
    cr, ci = (nr * lre + ai * lim) / den, (ai * lre - nr * lim) / den
    bbr = cr[..., None] * bre - ci[..., None] * bim
    bbi = cr[..., None] * bim + ci[..., None] * bre
    eye = jnp.eye(S5_GROUPS, dtype=F32)[:, None, :, None]

    def block_diag(t):
        return (t[:, :, None, :] * eye).reshape(S5_GROUPS * t.shape[1], S5_GROUPS * t.shape[2])

    tr = lambda t: jnp.transpose(t, (0, 2, 1))
    bd = _interleave(block_diag(tr(bbr)), block_diag(tr(bbi)), 1)
    cd = _interleave(block_diag(tr(cre)), block_diag(tr(-cim)), 0)
    a = _interleave(ar.reshape(1, N_STATE), ai.reshape(1, N_STATE), 1)
    return a, bd, cd


def _lower_bounds(logits):
    probs = jax.nn.softmax(logits, axis=0)
    return jnp.cumsum(probs, axis=0) - probs[0:1]


def _rope_table(positions):
    inv_freq = 1.0 / (ROPE_THETA ** (jnp.arange(0, MLA_ROPE, 2, dtype=F32) / MLA_ROPE))
    ang = positions.astype(F32)[:, None] * inv_freq
    cos, sin = jnp.cos(ang), jnp.sin(ang)
    return jnp.concatenate([cos, cos, sin, sin], axis=1)


def _split_mod(mod):
    return [mod[:, i * D_MODEL:(i + 1) * D_MODEL] for i in range(6)]


def _layer_fwd(x, wl, mod, cs):
    s = x.shape[0]
    ffp = wl["wdown_p"].shape[0]
    sh1, sc1, g1, sh2, sc2, g2 = _split_mod(mod)
    h1 = _normmod_fwd(x, wl["mix_pre_norm"], sc1, sh1, "mix_pre")
    proj = _mm(h1, wl["win_p"], mode="nn", name="in_proj", m=s, n=PROJ_W, k=D_MODEL)
    out_s5, s5_saved = _s5_fwd(proj, wl, s)
    o_mla, mla_saved = _mla_fwd(proj, wl, cs, s)
    y_hg, o_hg, states = _hg_fwd(proj, wl["hg_lb"], wl["hg_out_norm"])
    cat = jnp.concatenate([out_s5, o_mla.astype(BF16), y_hg], axis=1)
    mixed = _mm(cat, wl["w_out"], mode="nn", name="out_proj", m=s, n=D_MODEL, k=D_MODEL)
    x2 = _postnorm_fwd(x, mixed, wl["mix_post_norm"], g1, "mix_post")
    h2 = _normmod_fwd(x2, wl["ffn_pre_norm"], sc2, sh2, "ffn_pre")
    up = _mm(h2, wl["wup_cm"], mode="nn", name="ffn_up", m=s, n=2 * ffp, k=D_MODEL, cm=ffp // 2, tn=ffp // 4)
    act = _ffn_act_fwd(up, wl["conv_w_p"], wl["conv_b_p"], ffp)
    y = _mm(act, wl["wdown_p"], mode="nn", name="ffn_down", m=s, n=D_MODEL, k=ffp)
    x3 = _postnorm_fwd(x2, y, wl["ffn_post_norm"], g2, "ffn_post")
    return x3, (x, h1, proj, s5_saved, mla_saved, o_hg, states, cat, mixed, x2, h2, up, act, y)


def _layer_bwd(dx3, saved, wl, mod, cs):
    x, h1, proj, s5_saved, mla_saved, o_hg, states, cat, mixed, x2, h2, up, act, y = saved
    s = x.shape[0]
    ffp = wl["wdown_p"].shape[0]
    sh1, sc1, g1, sh2, sc2, g2 = _split_mod(mod)
    g = {}
    dy, g["ffn_post_norm"], dg2 = _postnorm_bwd(dx3, y, wl["ffn_post_norm"], g2, "ffn_post_bwd")
    da = _mm(dy, wl["wdown_p"], mode="nt", name="ffn_down_dx", m=s, n=ffp, k=D_MODEL)
    g["wdown_p"] = _mm(act, dy, mode="tn", name="ffn_down_dw", m=ffp, n=D_MODEL, k=s)
    dug, duv, dwg, dwv, dbg, dbv = _ffn_act_bwd(da, up, wl["conv_w_p"], wl["conv_b_p"], ffp)
    g["conv_w_p"] = jnp.concatenate([dwg, dwv], axis=1)
    g["conv_b_p"] = jnp.concatenate([dbg, dbv], axis=1)
    dup = jnp.concatenate([_conv_bwd_input(dug, wl["conv_w_p"], 0, "ffn_conv_bwd_gate"),
                           _conv_bwd_input(duv, wl["conv_w_p"], ffp, "ffn_conv_bwd_val")], axis=1)
    dh2 = _mm(dup, wl["wup_cm"], mode="nt", name="ffn_up_dx", m=s, n=D_MODEL, k=2 * ffp, cm=ffp // 2, tk=ffp // 4)
    g["wup_cm"] = _mm(h2, dup, mode="tn", name="ffn_up_dw", m=D_MODEL, n=2 * ffp, k=s, cm=ffp // 2, tn=ffp // 4)
    dx2, g["ffn_pre_norm"], dsc2, dsh2 = _normmod_bwd(dh2, x2, wl["ffn_pre_norm"], sc2, dx3, "ffn_pre_bwd")
    dmixed, g["mix_post_norm"], dg1 = _postnorm_bwd(dx2, mixed, wl["mix_post_norm"], g1, "mix_post_bwd")
    dcat = _mm(dmixed, wl["w_out"], mode="nt", name="out_proj_dx", m=s, n=D_MODEL, k=D_MODEL)
    g["w_out"] = _mm(cat, dmixed, mode="tn", name="out_proj_dw", m=D_MODEL, n=D_MODEL, k=s)
    du_s5, s5g = _s5_bwd(dcat, proj, wl, s5_saved, s)
    dcq, dckv, dkr, mlag = _mla_bwd(dcat, proj, wl, cs, mla_saved, s)
    dhq, dhf, dhi, dhg, g["hg_lb"], dgm = _hg_bwd(dcat, S5_WIDTH + MLA_WIDTH, proj, wl["hg_lb"], wl["hg_out_norm"], o_hg, states)
    g["hg_out_norm"] = jnp.sum(dgm, axis=0, keepdims=True)
    g.update(s5g)
    g.update(mlag)
    dproj = jnp.concatenate([du_s5, dcq, dhq, dhf, dhi, dhg, dckv, dkr, jnp.zeros((s, PROJ_W - 3456), F32)], axis=1).astype(BF16)
    dh1 = _mm(dproj, wl["win_p"], mode="nt", name="in_proj_dx", m=s, n=D_MODEL, k=PROJ_W)
    g["win_p"] = _mm(h1, dproj, mode="tn", name="in_proj_dw", m=D_MODEL, n=PROJ_W, k=s)
    dx, g["mix_pre_norm"], dsc1, dsh1 = _normmod_bwd(dh1, x, wl["mix_pre_norm"], sc1, dx2, "mix_pre_bwd")
    dmod = jnp.concatenate([dsh1, dsc1, dg1, dsh2, dsc2, dg2], axis=1)
    return dx, g, dmod


def _prepare_layer(gathered, conv_w, rep, cpad):
    def sharded_prep(w_in, s5_w_glu, mla_w_uq, mla_w_ukv, w_out, ffn_w_up, ffn_w_down, ffn_conv_w):
        merge = lambda t: t.reshape(N_CHIPS * t.shape[1], t.shape[2])
        return {"win_p": _prep_win(_cols_from_chips(w_in)), "s5_w_glu": merge(s5_w_glu), "mla_wq": _prep_wq(_cols_from_chips(mla_w_uq)),
                "mla_w_ukv": _cols_from_chips(mla_w_ukv), "w_out": merge(w_out), "wup_cm": ffn_w_up,
                "wdown_p": _pad_ff_rows(ffn_w_down, cpad), "conv_w_p": _pad_ff_cols(ffn_conv_w, cpad)}

    def rep_prep(lre, lim, logdt, bre, bim, cre, cim, conv_b):
        a, bd, cd = _s5_prep(lre, lim, logdt, bre, bim, cre, cim)
        return {"s5_a": a, "s5_bd": bd, "s5_cd": cd, "conv_b_p": _pad_ff_cols(conv_b, cpad)}

    sh_args = [gathered[k] for k in SHARDED] + [conv_w]
    rep_names = ("s5_lambda_re", "s5_lambda_im", "s5_log_dt", "s5_b_re", "s5_b_im", "s5_c_re", "s5_c_im", "ffn_conv_b")
    rep_args = [rep[k] for k in rep_names]
    wl = sharded_prep(*sh_args)
    rep_out, rep_vjp = jax.vjp(rep_prep, *rep_args)
    wl.update(rep_out)
    sh_t = jax.linear_transpose(sharded_prep, *[jax.ShapeDtypeStruct(a.shape, F32) for a in sh_args])

    def back(g):
        out = dict(zip(SHARDED + ("ffn_conv_w",), sh_t({k: g[k] for k in ("win_p", "s5_w_glu", "mla_wq", "mla_w_ukv", "w_out", "wup_cm", "wdown_p", "conv_w_p")})))
        out.update(zip(rep_names, rep_vjp({k: g[k] for k in ("s5_a", "s5_bd", "s5_cd", "conv_b_p")})))
        return out

    return wl, back


PER_LAYER_ROWS = ("s5_d", "mla_q_norm", "mla_kv_norm", "hg_out_norm", "mix_pre_norm", "mix_post_norm", "ffn_pre_norm", "ffn_post_norm")


def _flat_pad(parts, unit):
    flat = jnp.concatenate([p.reshape(-1) for p in parts])
    n = -(-flat.shape[0] // unit) * unit
    return jnp.pad(flat, (0, n - flat.shape[0])).reshape(-1, FLAT_COLS)


def _split_flat(flat, like):
    flat = flat.reshape(-1)
    out, pos = [], 0
    for t in like:
        out.append(flat[pos:pos + t.size].reshape(t.shape))
        pos += t.size
    return out


def kernel(x, c, positions, w_in, s5_lambda_re, s5_lambda_im, s5_log_dt, s5_b_re, s5_b_im, s5_c_re, s5_c_im, s5_d, s5_w_glu, mla_q_norm, mla_w_uq, mla_kv_norm, mla_w_ukv, hg_lb_logits, hg_out_norm, w_out, mix_pre_norm, mix_post_norm, ffn_pre_norm, ffn_post_norm, ffn_w_up, ffn_conv_w, ffn_conv_b, ffn_w_down, w_ada, b_ada, loss_target, m_w_in, m_s5_lambda_re, m_s5_lambda_im, m_s5_log_dt, m_s5_b_re, m_s5_b_im, m_s5_c_re, m_s5_c_im, m_s5_d, m_s5_w_glu, m_mla_q_norm, m_mla_w_uq, m_mla_kv_norm, m_mla_w_ukv, m_hg_lb_logits, m_hg_out_norm, m_w_out, m_mix_pre_norm, m_mix_post_norm, m_ffn_pre_norm, m_ffn_post_norm, m_ffn_w_up, m_ffn_conv_w, m_ffn_conv_b, m_ffn_w_down, m_w_ada, m_b_ada, v_w_in, v_s5_lambda_re, v_s5_lambda_im, v_s5_log_dt, v_s5_b_re, v_s5_b_im, v_s5_c_re, v_s5_c_im, v_s5_d, v_s5_w_glu, v_mla_q_norm, v_mla_w_uq, v_mla_kv_norm, v_mla_w_ukv, v_hg_lb_logits, v_hg_out_norm, v_w_out, v_mix_pre_norm, v_mix_post_norm, v_ffn_pre_norm, v_ffn_post_norm, v_ffn_w_up, v_ffn_conv_w, v_ffn_conv_b, v_ffn_w_down, v_w_ada, v_b_ada):
    p = dict(locals())
    n_layers = w_in.shape[0]
    c_up = ffn_w_up.shape[2]
    cpad = -(-c_up // FF_PAD) * FF_PAD
    xs, target = x[0], loss_target[0]
    me = 4 * lax.axis_index("x") + 2 * lax.axis_index("y") + lax.axis_index("c")
    chip = 2 * lax.axis_index("x") + lax.axis_index("y")
    cs = _rope_table(positions[0])

    cact = jax.nn.silu(_all_gather8(c, "gather_c")[:, 0, :])
    ada_cols = w_ada.shape[2]
    mod_part = jnp.stack([_mm(cact, w_ada[l], mode="nn", name="ada_mod", m=8, n=ada_cols, k=D_MODEL) for l in range(n_layers)])
    mod_all = _all_gather8(mod_part.reshape(1, -1), "gather_mod").reshape(N_CHIPS, 2, n_layers, 8, ada_cols)[:, 0]
    mod_mine = lax.dynamic_index_in_dim(mod_all, me, axis=2, keepdims=False)
    mods = mod_mine.transpose(1, 0, 2).reshape(n_layers, -1) + b_ada

    conv_w_all = _all_gather8(ffn_conv_w.reshape(1, -1), "gather_conv_w").reshape(N_CHIPS, 2, n_layers, 3, -1)[:, 0]
    conv_w_full = conv_w_all.transpose(1, 2, 0, 3).reshape(n_layers, 3, -1)

    lbs, lb_vjp = jax.vjp(_lower_bounds, hg_lb_logits)

    layers = []
    for l in range(n_layers):
        shards = {k: p[k][l] for k in SHARDED}
        shards["ffn_w_up"] = jnp.pad(shards["ffn_w_up"], ((0, 0), (0, cpad - c_up)))
        got = _gather_weights([_halves(shards[k].astype(BF16)) for k in SHARDED])
        gathered = {k: _unhalve(t) for k, t in zip(SHARDED, got)}
        rep = {k: p[k][l] for k in ("s5_lambda_re", "s5_lambda_im", "s5_log_dt", "s5_b_re", "s5_b_im", "s5_c_re", "s5_c_im")}
        rep["ffn_conv_b"] = ffn_conv_b[l][None, :]
        wl, back = _prepare_layer(gathered, conv_w_full[l], rep, cpad)
        for k in PER_LAYER_ROWS:
            wl[k] = p[k][l][None, :]
        wl["hg_lb"] = lbs[l][None, :]
        layers.append((wl, back))

    h = xs
    saved = []
    for l in range(n_layers):
        h, sv = _layer_fwd(h, layers[l][0], mods[l][None, :], cs)
        saved.append(sv)
    dh, loss_part = _loss_grad(h, target)
    loss = lax.psum(loss_part[0, 0], ("x", "y", "c"))

    grads = {k: [None] * n_layers for k in WEIGHTS}
    dmods, dlbs = [None] * n_layers, [None] * n_layers
    for l in reversed(range(n_layers)):
        wl, back = layers[l]
        dh, g, dmods[l] = _layer_bwd(dh, saved[l], wl, mods[l][None, :], cs)
        ref_g = back(g)
        reduced = _reduce_scatter([_halves(ref_g[k]) for k in SHARDED], SHARDED)
        for k, t in zip(SHARDED, reduced):
            grads[k][l] = _unhalve(t)
        grads["ffn_w_up"][l] = grads["ffn_w_up"][l][:, :c_up]
        grads["ffn_conv_w"][l] = ref_g["ffn_conv_w"]
        for k in ("s5_lambda_re", "s5_lambda_im", "s5_log_dt", "s5_b_re", "s5_b_im", "s5_c_re", "s5_c_im"):
            grads[k][l] = ref_g[k]
        grads["ffn_conv_b"][l] = ref_g["ffn_conv_b"][0]
        for k in PER_LAYER_ROWS:
            grads[k][l] = g[k][0]
        dlbs[l] = g["hg_lb"][0]
    grads = {k: jnp.stack(v) for k, v in grads.items() if v[0] is not None}
    grads["hg_lb_logits"] = lb_vjp(jnp.stack(dlbs))[0]

    summed = REPLICATED + ("ffn_conv_w",)
    rep_flat = _flat_pad([grads[k] for k in summed], 128 * FLAT_COLS)
    rep_sum = _sum8(_all_gather8(rep_flat, "gather_small_grads"))
    for k, t in zip(summed, _split_flat(rep_sum, [grads[k] for k in summed])):
        grads[k] = t
    conv_cols = grads["ffn_conv_w"].reshape(n_layers, 3, N_CHIPS, c_up)
    grads["ffn_conv_w"] = lax.dynamic_index_in_dim(conv_cols, chip, axis=2, keepdims=False)

    dmod_all = _all_gather8(jnp.stack(dmods).reshape(n_layers * 6 * D_MODEL // FLAT_COLS, FLAT_COLS), "gather_dmod")
    grads["b_ada"] = _sum8(dmod_all).reshape(n_layers, 6 * D_MODEL)
    dmod_cols = lax.dynamic_slice_in_dim(dmod_all.reshape(8, n_layers, N_CHIPS, ada_cols), chip, 1, axis=2)[:, :, 0]
    ada = [_ada_grad_adamw(cact, dmod_cols[:, l], w_ada[l], m_w_ada[l], v_w_ada[l]) for l in range(n_layers)]
    grads["w_ada"], delta_ada, new_m_ada, new_v_ada = (jnp.stack([a[i] for a in ada]) for i in range(4))

    delta, new_m, new_v = {"w_ada": delta_ada}, {"w_ada": new_m_ada}, {"w_ada": new_v_ada}
    for k in SHARDED:
        delta[k], new_m[k], new_v[k] = _adamw(p[k], grads[k], p["m_" + k], p["v_" + k], "adamw_" + k)
    small = REPLICATED + ("b_ada", "ffn_conv_w")
    flats = [_flat_pad([src[k] for k in small], 128 * FLAT_COLS)
             for src in (p, grads, {k: p["m_" + k] for k in small}, {k: p["v_" + k] for k in small})]
    for dst, flat in zip((delta, new_m, new_v), _adamw(*flats, "adamw_small")):
        for k, t in zip(small, _split_flat(flat, [p[k] for k in small])):
            dst[k] = t

    return (loss, dh[None], *[grads[k] for k in WEIGHTS], *[delta[k] for k in WEIGHTS],
            *[new_m[k] for k in WEIGHTS], *[new_v[k] for k in WEIGHTS])
```

```python
import functools
import math

import numpy as np
import jax
import jax.numpy as jnp
from jax import lax
from jax.experimental import pallas as pl
from jax.experimental.pallas import tpu as pltpu

F32 = jnp.float32
BF16 = jnp.bfloat16
MESH = pl.DeviceIdType.MESH

D_MODEL = 2048
S5_WIDTH, S5_GROUP, S5_GROUPS, S5_STATE = 512, 16, 32, 64
MLA_HEADS, MLA_NOPE, MLA_ROPE, MLA_V = 8, 128, 64, 128
MLA_Q_RANK, MLA_KV_RANK = 512, 256
MLA_WIDTH = MLA_HEADS * MLA_V
ROPE_THETA = 10000.0
HG_HEADS, HG_DK, HG_DV = 4, 128, 128
HG_WIDTH = HG_HEADS * HG_DV
EPS = 1e-6
ADAM_LR, ADAM_B1, ADAM_B2, ADAM_EPS, ADAM_WD, ADAM_STEP = 0.001, 0.9, 0.999, 1e-08, 0.01, 10
GELU_K0 = math.sqrt(2.0 / math.pi)
GELU_K1 = 0.044715

LANE = 128
SUBLANE = 8
VMEM_LIMIT = 56 * 1024 * 1024

P_S5, P_CQ, P_HQ, P_HF, P_HI, P_HG, P_CKV, P_KR = 0, 512, 1024, 1536, 2048, 2560, 3072, 3328
PROJ_W = 3584
N_STATE = S5_GROUPS * S5_STATE
SCAN_W = 512
HG_CHUNK = 64
N_CHIPS = 4
FLAT_COLS = 1024
ADD_ROWS = 256


def _sigmoid(x):
    return 1.0 / (1.0 + jnp.exp(-x))


def _silu(x):
    return x * _sigmoid(x)


def _dsilu(x):
    s = _sigmoid(x)
    return s * (1.0 + x * (1.0 - s))


def _gelu(x):
    return 0.5 * x * (1.0 + jnp.tanh(GELU_K0 * (x + GELU_K1 * x * x * x)))


def _dgelu(x):
    t = jnp.tanh(GELU_K0 * (x + GELU_K1 * x * x * x))
    return 0.5 * (1.0 + t) + 0.5 * x * (1.0 - t * t) * GELU_K0 * (1.0 + 3.0 * GELU_K1 * x * x)


def _colsum(v):
    return jnp.sum(v, axis=0, keepdims=True)


def _rowmean(v):
    return jnp.mean(v, axis=-1, keepdims=True)


def _dot(a, b, dims):
    return lax.dot_general(a.astype(BF16), b.astype(BF16), (dims, ((), ())), preferred_element_type=F32)


NN = ((1,), (0,))
NT = ((1,), (1,))
TN = ((0,), (0,))


def _pcall(body, **kw):
    return pl.pallas_call(body, **kw)


def _params(sem):
    return pltpu.CompilerParams(dimension_semantics=sem, vmem_limit_bytes=VMEM_LIMIT)


def _tile(dim, pref):
    if dim <= pref:
        return dim
    t = (pref // LANE) * LANE
    while t > LANE and dim % t:
        t -= LANE
    assert dim % t == 0, (dim, pref)
    return t


def _mm(a, b, *, mode, name, m, n, k, a_off=0, b_off=0, tm=1024, tn=1024, tk=1024,
        out=((F32),), epi=None, extras=(), cm=None):
    if cm is not None:
        if mode == "nt":
            tk = _tile(cm, tk)
        else:
            tn = _tile(cm, tn)
    tm, tn, tk = _tile(m, tm), _tile(n, tn), _tile(k, tk)
    nk = k // tk
    dims = {"nn": NN, "nt": NT, "tn": TN}[mode]
    if mode == "tn":
        assert a_off % tm == 0 and b_off % tn == 0
        a_spec = pl.BlockSpec((tk, tm), lambda i, j, kk: (kk, i + a_off // tm))
        b_spec = pl.BlockSpec((tk, tn), lambda i, j, kk: (kk, j + b_off // tn))
    else:
        assert a_off % tk == 0 and b_off == 0
        a_spec = pl.BlockSpec((tm, tk), lambda i, j, kk: (i, kk + a_off // tk))
        if mode == "nn" and cm is not None:
            b_spec = pl.BlockSpec((None, tk, tn), lambda i, j, kk, per=cm // tn: (j // per, kk, j % per))
        elif mode == "nn":
            b_spec = pl.BlockSpec((tk, tn), lambda i, j, kk: (kk, j))
        elif cm is not None:
            b_spec = pl.BlockSpec((None, tn, tk), lambda i, j, kk, per=cm // tk: (kk // per, j, kk % per))
        else:
            b_spec = pl.BlockSpec((tn, tk), lambda i, j, kk: (j, kk))
    in_specs, ex_arrays = [a_spec, b_spec], []
    for e in extras:
        if e[0] == "tile":
            off = e[2] // tn
            assert e[2] % tn == 0
            in_specs.append(pl.BlockSpec((tm, tn), lambda i, j, kk, off=off: (i, j + off)))
        else:
            in_specs.append(pl.BlockSpec((e[1].shape[0], tn), lambda i, j, kk: (0, j)))
        ex_arrays.append(e[1])
    n_ex = len(ex_arrays)
    n_out = len(out)

    def body(*refs):
        a_ref, b_ref = refs[0], refs[1]
        ex_refs = refs[2:2 + n_ex]
        o_refs = refs[2 + n_ex:2 + n_ex + n_out]
        acc_ref = refs[-1]
        kk = pl.program_id(2)

        @pl.when(kk == 0)
        def _():
            acc_ref[...] = jnp.zeros_like(acc_ref)

        acc_ref[...] += _dot(a_ref[...], b_ref[...], dims)

        @pl.when(kk == nk - 1)
        def _():
            acc = acc_ref[...]
            if epi is None:
                o_refs[0][...] = acc.astype(o_refs[0].dtype)
            else:
                vals = epi(acc, *[r[...] for r in ex_refs])
                for r, v in zip(o_refs, vals):
                    r[...] = v.astype(r.dtype)

    if mode == "tn" and cm is not None:
        out_shape = tuple(jax.ShapeDtypeStruct((N_CHIPS, m, cm), d) for d in out)
        out_specs = tuple(pl.BlockSpec((None, tm, tn), lambda i, j, kk, per=cm // tn: (j // per, i, j % per)) for _ in out)
    else:
        out_shape = tuple(jax.ShapeDtypeStruct((m, n), d) for d in out)
        out_specs = tuple(pl.BlockSpec((tm, tn), lambda i, j, kk: (i, j)) for _ in out)
    res = _pcall(
        body, name=name,
        out_shape=out_shape,
        grid=(m // tm, n // tn, nk),
        in_specs=in_specs,
        out_specs=out_specs,
        scratch_shapes=[pltpu.VMEM((tm, tn), F32)],
        compiler_params=_params(("parallel", "parallel", "arbitrary")),
    )(a, b, *ex_arrays)
    return res[0] if n_out == 1 else res


def _rows(fn, *, name, s, tm, ins, outs, ncb=1):
    tm = min(tm, s)
    assert s % tm == 0 and tm % SUBLANE == 0
    ni = s // tm
    r8 = tm // SUBLANE
    in_specs, arrays = [], []
    for e in ins:
        kind, arr = e[0], e[1]
        if kind in ("row", "prev8", "next8", "vecb"):
            off, w = e[2] // e[3], e[3]
            assert e[2] % e[3] == 0
        if kind == "row":
            in_specs.append(pl.BlockSpec((tm, w), lambda j, i, off=off: (i, off + j)))
        elif kind == "prev8":
            in_specs.append(pl.BlockSpec((SUBLANE, w), lambda j, i, off=off: (jnp.maximum(i * r8 - 1, 0), off + j)))
        elif kind == "next8":
            last = s // SUBLANE - 1
            in_specs.append(pl.BlockSpec((SUBLANE, w), lambda j, i, off=off: (jnp.minimum((i + 1) * r8, last), off + j)))
        elif kind == "vec":
            in_specs.append(pl.BlockSpec(arr.shape, lambda j, i, nd=arr.ndim: (0,) * nd))
        else:
            in_specs.append(pl.BlockSpec((arr.shape[0], w), lambda j, i, off=off: (0, off + j)))
        arrays.append(arr)
    out_shape, out_specs = [], []
    for e in outs:
        if e[0] == "row":
            out_shape.append(jax.ShapeDtypeStruct((s, ncb * e[1]), e[2]))
            out_specs.append(pl.BlockSpec((tm, e[1]), lambda j, i: (i, j)))
        else:
            out_shape.append(jax.ShapeDtypeStruct((e[1], ncb * e[2]), F32))
            out_specs.append(pl.BlockSpec((e[1], e[2]), lambda j, i: (0, j)))

    def body(*refs):
        fn(pl.program_id(1), *refs)

    res = _pcall(
        body, name=name, out_shape=tuple(out_shape), grid=(ncb, ni),
        in_specs=in_specs, out_specs=tuple(out_specs),
        compiler_params=_params(("parallel", "arbitrary")),
    )(*arrays)
    return res[0] if len(outs) == 1 else res


def _acc(ref, i, val):
    @pl.when(i == 0)
    def _():
        ref[...] = val

    @pl.when(i > 0)
    def _():
        ref[...] += val


def _normmod_fwd(x, gain, sc, sh, name):
    s, d = x.shape

    def fn(i, x_ref, g_ref, sc_ref, sh_ref, h_ref):
        xv = x_ref[...]
        r = lax.rsqrt(_rowmean(xv * xv) + EPS)
        h_ref[...] = (((xv * r) * g_ref[...]) * (1.0 + sc_ref[...]) + sh_ref[...]).astype(h_ref.dtype)

    return _rows(fn, name=name, s=s, tm=256, ins=[("row", x, 0, d), ("vec", gain), ("vec", sc), ("vec", sh)],
                 outs=[("row", d, BF16)])


def _normmod_bwd(dh, x, gain, sc, dx_add, name):
    s, d = x.shape

    def fn(i, dh_ref, x_ref, g_ref, sc_ref, add_ref, dx_ref, dg_ref, dsc_ref, dsh_ref):
        xv, dhv = x_ref[...], dh_ref[...]
        r = lax.rsqrt(_rowmean(xv * xv) + EPS)
        xn = xv * r
        gain_v, one_sc = g_ref[...], 1.0 + sc_ref[...]
        ghat = dhv * gain_v * one_sc
        dx_ref[...] = r * (ghat - xn * _rowmean(ghat * xn)) + add_ref[...]
        _acc(dg_ref, i, _colsum(dhv * xn * one_sc))
        _acc(dsc_ref, i, _colsum(dhv * xn * gain_v))
        _acc(dsh_ref, i, _colsum(dhv))

    return _rows(fn, name=name, s=s, tm=256,
                 ins=[("row", dh, 0, d), ("row", x, 0, d), ("vec", gain), ("vec", sc), ("row", dx_add, 0, d)],
                 outs=[("row", d, F32), ("acc", 1, d), ("acc", 1, d), ("acc", 1, d)])


def _postnorm_fwd(x, m, gain, gate, name):
    s, d = x.shape

    def fn(i, x_ref, m_ref, g_ref, gate_ref, o_ref):
        mv = m_ref[...]
        r = lax.rsqrt(_rowmean(mv * mv) + EPS)
        o_ref[...] = x_ref[...] + gate_ref[...] * ((mv * r) * g_ref[...])

    return _rows(fn, name=name, s=s, tm=256, ins=[("row", x, 0, d), ("row", m, 0, d), ("vec", gain), ("vec", gate)],
                 outs=[("row", d, F32)])


def _postnorm_bwd(dxo, m, gain, gate, name):
    s, d = m.shape

    def fn(i, dx_ref, m_ref, g_ref, gate_ref, dm_ref, dg_ref, dgate_ref):
        mv, dxv = m_ref[...], dx_ref[...]
        r = lax.rsqrt(_rowmean(mv * mv) + EPS)
        mn = mv * r
        gain_v, gate_v = g_ref[...], gate_ref[...]
        ghat = dxv * gate_v * gain_v
        dm_ref[...] = (r * (ghat - mn * _rowmean(ghat * mn))).astype(dm_ref.dtype)
        _acc(dg_ref, i, _colsum(dxv * gate_v * mn))
        _acc(dgate_ref, i, _colsum(dxv * mn * gain_v))

    return _rows(fn, name=name, s=s, tm=256, ins=[("row", dxo, 0, d), ("row", m, 0, d), ("vec", gain), ("vec", gate)],
                 outs=[("row", d, BF16), ("acc", 1, d), ("acc", 1, d)])


def _rms_fwd(src, off, w, gain, name):
    s = src.shape[0]

    def fn(i, x_ref, g_ref, o_ref):
        xv = x_ref[...]
        r = lax.rsqrt(_rowmean(xv * xv) + EPS)
        o_ref[...] = ((xv * r) * g_ref[...]).astype(o_ref.dtype)

    return _rows(fn, name=name, s=s, tm=512, ins=[("row", src, off, w), ("vec", gain)], outs=[("row", w, BF16)])


def _rms_bwd(dy, src, off, w, gain, name):
    s = src.shape[0]

    def fn(i, dy_ref, x_ref, g_ref, dx_ref, dg_ref):
        xv, dyv = x_ref[...], dy_ref[...]
        r = lax.rsqrt(_rowmean(xv * xv) + EPS)
        xn = xv * r
        ghat = dyv * g_ref[...]
        dx_ref[...] = r * (ghat - xn * _rowmean(ghat * xn))
        _acc(dg_ref, i, _colsum(dyv * xn))

    return _rows(fn, name=name, s=s, tm=512, ins=[("row", dy, 0, w), ("row", src, off, w), ("vec", gain)],
                 outs=[("row", w, F32), ("acc", 1, w)])


def _loss_grad(x, target):
    s, d = x.shape

    def fn(i, x_ref, t_ref, dx_ref, l_ref):
        diff = x_ref[...] - t_ref[...]
        dx_ref[...] = diff * (1.0 / d)
        part = _colsum(jnp.sum(diff * diff, axis=1, keepdims=True)) * (0.5 / d)
        _acc(l_ref, i, jnp.broadcast_to(part, (1, LANE)))

    return _rows(fn, name="loss_grad", s=s, tm=256, ins=[("row", x, 0, d), ("row", target, 0, d)],
                 outs=[("row", d, F32), ("acc", 1, LANE)])


FFN_WC = 512


def _shift_rows(xv, h_ref, i, row, k):
    out = pltpu.roll(xv, k, 0)
    for r in range(k):
        hrow = jnp.where(i > 0, h_ref[SUBLANE - k + r:SUBLANE - k + r + 1, :], 0.0)
        out = jnp.where(row == r, hrow, out)
    return out


def _conv_rows(x_ref, h_ref, w_ref, b_ref, i, row):
    xv = x_ref[...]
    s1, s2 = _shift_rows(xv, h_ref, i, row, 1), _shift_rows(xv, h_ref, i, row, 2)
    u = ((b_ref[...] + s2 * w_ref[0:1, :]) + s1 * w_ref[1:2, :]) + xv * w_ref[2:3, :]
    return u, s1, s2, xv


def _ffn_act_fwd(up, conv_w, conv_b, ffp):
    s = up.shape[0]
    wc, ncb = FFN_WC, ffp // FFN_WC

    def fn(i, g_ref, gh_ref, v_ref, vh_ref, wg_ref, wv_ref, bg_ref, bv_ref, a_ref):
        row = lax.broadcasted_iota(jnp.int32, g_ref.shape, 0)
        ug = _conv_rows(g_ref, gh_ref, wg_ref, bg_ref, i, row)[0]
        uv = _conv_rows(v_ref, vh_ref, wv_ref, bv_ref, i, row)[0]
        a_ref[...] = (_gelu(ug) * uv).astype(a_ref.dtype)

    return _rows(fn, name="ffn_act_fwd", s=s, tm=512, ncb=ncb,
                 ins=[("row", up, 0, wc), ("prev8", up, 0, wc), ("row", up, ffp, wc), ("prev8", up, ffp, wc),
                      ("vecb", conv_w, 0, wc), ("vecb", conv_w, ffp, wc), ("vecb", conv_b, 0, wc), ("vecb", conv_b, ffp, wc)],
                 outs=[("row", wc, BF16)])


def _ffn_act_bwd(da, up, conv_w, conv_b, ffp):
    s = up.shape[0]
    wc, ncb = FFN_WC, ffp // FFN_WC

    def fn(i, da_ref, g_ref, gh_ref, v_ref, vh_ref, wg_ref, wv_ref, bg_ref, bv_ref,
           dug_ref, duv_ref, dwg_ref, dwv_ref, dbg_ref, dbv_ref):
        row = lax.broadcasted_iota(jnp.int32, g_ref.shape, 0)
        ug, g1, g2, g0 = _conv_rows(g_ref, gh_ref, wg_ref, bg_ref, i, row)
        uv, v1, v2, v0 = _conv_rows(v_ref, vh_ref, wv_ref, bv_ref, i, row)
        dav = da_ref[...]
        dug = dav * uv * _dgelu(ug)
        duv = dav * _gelu(ug)
        dug_ref[...] = dug
        duv_ref[...] = duv
        for r, (gt, vt) in enumerate(((g2, v2), (g1, v1), (g0, v0))):
            _acc(dwg_ref.at[r:r + 1, :], i, _colsum(dug * gt))
            _acc(dwv_ref.at[r:r + 1, :], i, _colsum(duv * vt))
        _acc(dbg_ref, i, _colsum(dug))
        _acc(dbv_ref, i, _colsum(duv))

    return _rows(fn, name="ffn_act_bwd", s=s, tm=512, ncb=ncb,
                 ins=[("row", da, 0, wc), ("row", up, 0, wc), ("prev8", up, 0, wc), ("row", up, ffp, wc), ("prev8", up, ffp, wc),
                      ("vecb", conv_w, 0, wc), ("vecb", conv_w, ffp, wc), ("vecb", conv_b, 0, wc), ("vecb", conv_b, ffp, wc)],
                 outs=[("row", wc, F32), ("row", wc, F32), ("acc", 3, wc), ("acc", 3, wc), ("acc", 1, wc), ("acc", 1, wc)])


def _conv_bwd_input(du, conv_w, w_off, name):
    s, ffp = du.shape
    wc, ncb = FFN_WC, ffp // FFN_WC
    ni = s // min(512, s)

    def fn(i, du_ref, nx_ref, w_ref, o_ref):
        dv = du_ref[...]
        tm = dv.shape[0]
        row = lax.broadcasted_iota(jnp.int32, dv.shape, 0)
        n0 = jnp.where(i < ni - 1, nx_ref[0:1, :], 0.0)
        n1 = jnp.where(i < ni - 1, nx_ref[1:2, :], 0.0)
        u1 = jnp.where(row == tm - 1, n0, pltpu.roll(dv, tm - 1, 0))
        u2 = jnp.where(row == tm - 1, n1, jnp.where(row == tm - 2, n0, pltpu.roll(dv, tm - 2, 0)))
        o_ref[...] = (dv * w_ref[2:3, :] + u1 * w_ref[1:2, :] + u2 * w_ref[0:1, :]).astype(o_ref.dtype)

    return _rows(fn, name=name, s=s, tm=512, ncb=ncb,
                 ins=[("row", du, 0, wc), ("next8", du, 0, wc), ("vecb", conv_w, w_off, wc)],
                 outs=[("row", wc, BF16)])


def _cmul(ar, ai, br, bi):
    return ar * br - ai * bi, ar * bi + ai * br


def _s5_scan(x, a, *, reverse, h=None, name):
    s = x.shape[0]
    w = SCAN_W
    t_rows = min(256, s)
    nt = s // t_rows
    ncol = N_STATE // w
    nbits = t_rows.bit_length()
    r8 = t_rows // SUBLANE

    def tblk(t):
        return nt - 1 - t if reverse else t

    def body(*refs):
        if reverse:
            x_ref, a_ref, h_ref, hh_ref, o_ref, da_ref, carry, ptab = refs
        else:
            x_ref, a_ref, o_ref, carry, ptab = refs
        t = pl.program_id(1)
        row = lax.broadcasted_iota(jnp.int32, (t_rows, w), 0)
        idx = (t_rows - 1 - row) if reverse else row
        ar = a_ref[:, :w]
        ai = -a_ref[:, w:] if reverse else a_ref[:, w:]
        pows = [(ar, ai)]
        for _ in range(nbits - 1):
            pows.append(_cmul(*pows[-1], *pows[-1]))

        @pl.when(t == 0)
        def _():
            carry[...] = jnp.zeros_like(carry)
            pr, pi = jnp.ones((t_rows, w), F32), jnp.zeros((t_rows, w), F32)
            for kbit in range(nbits):
                bit = ((idx + 1) >> kbit) & 1
                fr = jnp.where(bit == 1, pows[kbit][0], 1.0)
                fi = jnp.where(bit == 1, pows[kbit][1], 0.0)
                pr, pi = _cmul(pr, pi, fr, fi)
            ptab[:, :w] = pr
            ptab[:, w:] = pi

        xr, xi = x_ref[:, :w], x_ref[:, w:]
        step = 1
        kbit = 0
        while step < t_rows:
            shift = (t_rows - step) if reverse else step
            yr, yi = pltpu.roll(xr, shift, 0), pltpu.roll(xi, shift, 0)
            zr, zi = _cmul(pows[kbit][0], pows[kbit][1], yr, yi)
            keep = idx >= step
            xr = xr + jnp.where(keep, zr, 0.0)
            xi = xi + jnp.where(keep, zi, 0.0)
            step *= 2
            kbit += 1
        cr, ci = carry[0:1, :w], carry[0:1, w:]
        zr, zi = _cmul(ptab[:, :w], ptab[:, w:], cr, ci)
        xr, xi = xr + zr, xi + zi
        o_ref[:, :w] = xr
        o_ref[:, w:] = xi
        last = 0 if reverse else t_rows - 1
        carry[0:1, :] = o_ref[last:last + 1, :]
        if reverse:
            halo_r = jnp.where(t < nt - 1, hh_ref[SUBLANE - 1:SUBLANE, :w], 0.0)
            halo_i = jnp.where(t < nt - 1, hh_ref[SUBLANE - 1:SUBLANE, w:], 0.0)
            hr = jnp.where(row == 0, halo_r, pltpu.roll(h_ref[:, :w], 1, 0))
            hi = jnp.where(row == 0, halo_i, pltpu.roll(h_ref[:, w:], 1, 0))
            _acc(da_ref.at[:, :w], t, _colsum(xr * hr + xi * hi))
            _acc(da_ref.at[:, w:], t, _colsum(xi * hr - xr * hi))

    blk = pl.BlockSpec((t_rows, 2 * w), lambda j, t: (tblk(t), j))
    a_spec = pl.BlockSpec((1, 2 * w), lambda j, t: (0, j))
    in_specs, arrays = [blk, a_spec], [x, a]
    out_shape = [jax.ShapeDtypeStruct((s, 2 * N_STATE), F32)]
    out_specs = [blk]
    if reverse:
        in_specs += [blk, pl.BlockSpec((SUBLANE, 2 * w), lambda j, t: (jnp.maximum(tblk(t) * r8 - 1, 0), j))]
        arrays += [h, h]
        out_shape.append(jax.ShapeDtypeStruct((1, 2 * N_STATE), F32))
        out_specs.append(a_spec)
    res = _pcall(
        body, name=name, out_shape=tuple(out_shape), grid=(ncol, nt), in_specs=in_specs, out_specs=tuple(out_specs),
        scratch_shapes=[pltpu.VMEM((SUBLANE, 2 * w), F32), pltpu.VMEM((t_rows, 2 * w), F32)],
        compiler_params=_params(("parallel", "arbitrary")),
    )(*arrays)
    return res if reverse else res[0]


def _s5_glu_bwd_a(dout, dout_off, y, z):
    s = y.shape[0]
    w = S5_WIDTH

    def fn(i, do_ref, y_ref, z_ref, dz_ref, p_ref):
        dov = do_ref[...]
        sg = _sigmoid(z_ref[...])
        dz_ref[...] = (dov * _gelu(y_ref[...]) * sg * (1.0 - sg)).astype(dz_ref.dtype)
        p_ref[...] = dov * sg

    return _rows(fn, name="s5_glu_bwd", s=s, tm=512, ins=[("row", dout, dout_off, w), ("row", y, 0, w), ("row", z, 0, w)],
                 outs=[("row", w, BF16), ("row", w, F32)])


def _s5_dd(dy, proj):
    s = dy.shape[0]
    w = S5_WIDTH

    def fn(i, dy_ref, u_ref, dd_ref):
        _acc(dd_ref, i, _colsum(dy_ref[...] * u_ref[...]))

    return _rows(fn, name="s5_dd", s=s, tm=512, ins=[("row", dy, 0, w), ("row", proj, P_S5, w)], outs=[("acc", 1, w)])


def _s5_fwd(proj, wl, s):
    bu = _mm(proj, wl["s5_bd"], mode="nn", name="s5_bu", m=s, n=2 * N_STATE, k=S5_WIDTH, a_off=P_S5)
    h = _s5_scan(bu, wl["s5_a"], reverse=False, name="s5_scan_fwd")
    def y_epi(acc, u, d):
        yv = acc + d * u
        return yv, _gelu(yv)

    y, yg = _mm(h, wl["s5_cd"], mode="nn", name="s5_y", m=s, n=S5_WIDTH, k=2 * N_STATE, out=(F32, BF16),
                extras=[("tile", proj, P_S5), ("row", wl["s5_d"])], epi=y_epi)
    z, out = _mm(yg, wl["s5_w_glu"], mode="nn", name="s5_glu", m=s, n=S5_WIDTH, k=S5_WIDTH, out=(F32, BF16),
                 extras=[("tile", y, 0)], epi=lambda acc, yv: (acc, _gelu(yv) * _sigmoid(acc)))
    return out, (h, y, z, yg)


def _s5_bwd(dcat, proj, wl, saved, s):
    h, y, z, yg = saved
    dz, p1 = _s5_glu_bwd_a(dcat, 0, y, z)
    dy = _mm(dz, wl["s5_w_glu"], mode="nt", name="s5_dyg", m=s, n=S5_WIDTH, k=S5_WIDTH,
             extras=[("tile", p1, 0), ("tile", y, 0)], epi=lambda acc, p, yv: ((p + acc) * _dgelu(yv),))
    gh = _mm(dy, wl["s5_cd"], mode="nt", name="s5_gh", m=s, n=2 * N_STATE, k=S5_WIDTH)
    adj, da = _s5_scan(gh, wl["s5_a"], reverse=True, h=h, name="s5_scan_bwd")
    du = _mm(adj, wl["s5_bd"], mode="nt", name="s5_du", m=s, n=S5_WIDTH, k=2 * N_STATE,
             extras=[("tile", dy, 0), ("row", wl["s5_d"])], epi=lambda acc, dyv, d: (acc + dyv * d,))
    grads = {
        "s5_a": da,
        "s5_bd": _mm(proj, adj, mode="tn", name="s5_dbd", m=S5_WIDTH, n=2 * N_STATE, k=s, a_off=P_S5),
        "s5_cd": _mm(h, dy, mode="tn", name="s5_dcd", m=2 * N_STATE, n=S5_WIDTH, k=s),
        "s5_d": _s5_dd(dy, proj),
        "s5_w_glu": _mm(yg, dz, mode="tn", name="s5_dwglu", m=S5_WIDTH, n=S5_WIDTH, k=s),
    }
    return du, grads


def _mla_prep(qraw, kvraw, proj, cs):
    s = qraw.shape[0]
    hw = MLA_HEADS * LANE

    def fn(i, q_ref, kv_ref, kr_ref, cs_ref, qn_ref, qr_ref, kvb_ref, krb_ref):
        csv = cs_ref[...]
        qn_ref[...] = q_ref[:, :hw].astype(BF16)
        for hd in range(MLA_HEADS):
            p = q_ref[:, hw + hd * LANE:hw + (hd + 1) * LANE] * csv
            qr_ref[:, hd * LANE:(hd + 1) * LANE] = (p + pltpu.roll(p, LANE // 2, 1)).astype(BF16)
        kvb_ref[...] = kv_ref[...].astype(BF16)
        p = kr_ref[...] * csv
        lane = lax.broadcasted_iota(jnp.int32, p.shape, 1)
        krb_ref[...] = jnp.where(lane < LANE // 2, p + pltpu.roll(p, LANE // 2, 1), 0.0).astype(BF16)

    return _rows(fn, name="mla_prep", s=s, tm=256,
                 ins=[("row", qraw, 0, 2 * hw), ("row", kvraw, 0, 2 * hw), ("row", proj, P_KR, LANE), ("row", cs, 0, LANE)],
                 outs=[("row", hw, BF16), ("row", hw, BF16), ("row", 2 * hw, BF16), ("row", LANE, BF16)])


def _mla_rope_bwd(dqn, dqr2, dkr2h, cs):
    s = dqn.shape[0]
    hw = MLA_HEADS * LANE

    def fn(i, dqn_ref, dqr_ref, dkr_ref, cs_ref, dq_ref, dk_ref):
        csv = cs_ref[...]
        dq_ref[:, :hw] = dqn_ref[...].astype(BF16)
        ksum = jnp.zeros(csv.shape, F32)
        for hd in range(MLA_HEADS):
            g = dqr_ref[:, hd * LANE:(hd + 1) * LANE]
            dq_ref[:, hw + hd * LANE:hw + (hd + 1) * LANE] = ((g + pltpu.roll(g, LANE // 2, 1)) * csv).astype(BF16)
            ksum = ksum + dkr_ref[:, hd * LANE:(hd + 1) * LANE]
        dk_ref[...] = ksum * csv

    return _rows(fn, name="mla_rope_bwd", s=s, tm=256,
                 ins=[("row", dqn, 0, hw), ("row", dqr2, 0, hw), ("row", dkr2h, 0, hw), ("row", cs, 0, LANE)],
                 outs=[("row", 2 * hw, BF16), ("row", LANE, F32)])


def _lanes(a_ref, b_ref):
    return jnp.concatenate([a_ref[...], b_ref[...]], axis=1)


def _attn_scores(qn_ref, qr_ref, kn_ref, kr_ref, qi, ki, tq, tk):
    scale = (MLA_NOPE + MLA_ROPE) ** -0.5
    sc = _dot(_lanes(qn_ref, qr_ref), _lanes(kn_ref, kr_ref), NT) * scale
    assert tq == tk
    return sc, lax.broadcasted_iota(jnp.int32, (tq, tk), 1) <= lax.broadcasted_iota(jnp.int32, (tq, tk), 0)


def _attn_specs(tq, tk, q_of, k_of):
    qs = pl.BlockSpec((tq, LANE), lambda h, a, b: (q_of(a, b), h))
    return [qs, qs,
            pl.BlockSpec((tk, LANE), lambda h, a, b: (k_of(a, b), 2 * h)),
            pl.BlockSpec((tk, LANE), lambda h, a, b: (k_of(a, b), 2 * h + 1)),
            pl.BlockSpec((tk, LANE), lambda h, a, b: (k_of(a, b), 0))]


def _flash_fwd(qn, qr2, kv, kr2):
    s = qn.shape[0]
    tq = tk = min(512, s)
    nq = s // tq

    def body(qn_ref, qr_ref, kn_ref, v_ref, kr_ref, o_ref, lse_ref, m_sc, l_sc, acc_sc):
        qi, ki = pl.program_id(1), pl.program_id(2)

        @pl.when(ki == 0)
        def _():
            m_sc[...] = jnp.full(m_sc.shape, -jnp.inf, F32)
            l_sc[...] = jnp.zeros_like(l_sc)
            acc_sc[...] = jnp.zeros_like(acc_sc)

        def step(diagonal):
            sc, causal = _attn_scores(qn_ref, qr_ref, kn_ref, kr_ref, qi, ki, tq, tk)
            if diagonal:
                sc = jnp.where(causal, sc, -1e30)
            m_new = jnp.maximum(m_sc[...], jnp.max(sc, axis=1, keepdims=True))
            alpha = jnp.exp(m_sc[...] - m_new)
            p = jnp.exp(sc - m_new)
            l_sc[...] = alpha * l_sc[...] + jnp.sum(p, axis=1, keepdims=True)
            acc_sc[...] = alpha * acc_sc[...] + _dot(p, v_ref[...], NN)
            m_sc[...] = m_new

        @pl.when(ki < qi)
        def _():
            step(False)

        @pl.when(ki == qi)
        def _():
            step(True)
            o_ref[...] = acc_sc[...] / l_sc[...]
            lse_ref[0] = m_sc[...] + jnp.log(l_sc[...])

    return _pcall(
        body, name="mla_flash_fwd",
        out_shape=(jax.ShapeDtypeStruct((s, MLA_WIDTH), F32), jax.ShapeDtypeStruct((MLA_HEADS, s, 1), F32)),
        grid=(MLA_HEADS, nq, nq),
        in_specs=_attn_specs(tq, tk, lambda a, b: a, lambda a, b: jnp.minimum(a, b)),
        out_specs=(pl.BlockSpec((tq, LANE), lambda h, a, b: (a, h)), pl.BlockSpec((1, tq, 1), lambda h, a, b: (h, a, 0))),
        scratch_shapes=[pltpu.VMEM((tq, 1), F32), pltpu.VMEM((tq, 1), F32), pltpu.VMEM((tq, LANE), F32)],
        compiler_params=_params(("parallel", "parallel", "arbitrary")),
    )(qn, qr2, kv, kv, kr2)


def _flash_bwd_dq(qn, qr2, kv, kr2, do, do_off, o, lse):
    s = qn.shape[0]
    tq = tk = min(512, s)
    nq = s // tq
    scale = (MLA_NOPE + MLA_ROPE) ** -0.5
    ob = do_off // LANE

    def body(qn_ref, qr_ref, kn_ref, v_ref, kr_ref, do_ref, o_ref, lse_ref, dqn_ref, dqr_ref, dl_ref, dl_sc, aq_sc):
        qi, ki = pl.program_id(1), pl.program_id(2)

        @pl.when(ki == 0)
        def _():
            dl_sc[...] = jnp.sum(do_ref[...] * o_ref[...], axis=1, keepdims=True)
            aq_sc[...] = jnp.zeros_like(aq_sc)

        def step(diagonal):
            sc, causal = _attn_scores(qn_ref, qr_ref, kn_ref, kr_ref, qi, ki, tq, tk)
            p = jnp.exp(sc - lse_ref[0])
            if diagonal:
                p = jnp.where(causal, p, 0.0)
            dp = _dot(do_ref[...], v_ref[...], NT)
            ds = (p * (dp - dl_sc[...]) * scale).astype(BF16)
            aq_sc[...] += _dot(ds, _lanes(kn_ref, kr_ref), NN)

        @pl.when(ki < qi)
        def _():
            step(False)

        @pl.when(ki == qi)
        def _():
            step(True)
            dqn_ref[...] = aq_sc[:, :LANE]
            dqr_ref[...] = aq_sc[:, LANE:]
            dl_ref[0] = dl_sc[...]

    qblk = pl.BlockSpec((tq, LANE), lambda h, a, b: (a, h))
    vec = pl.BlockSpec((1, tq, 1), lambda h, a, b: (h, a, 0))
    return _pcall(
        body, name="mla_flash_dq",
        out_shape=(jax.ShapeDtypeStruct((s, MLA_WIDTH), F32), jax.ShapeDtypeStruct((s, MLA_WIDTH), F32),
                   jax.ShapeDtypeStruct((MLA_HEADS, s, 1), F32)),
        grid=(MLA_HEADS, nq, nq),
        in_specs=_attn_specs(tq, tk, lambda a, b: a, lambda a, b: jnp.minimum(a, b))
        + [pl.BlockSpec((tq, LANE), lambda h, a, b: (a, h + ob)), qblk, vec],
        out_specs=(qblk, qblk, vec),
        scratch_shapes=[pltpu.VMEM((tq, 1), F32), pltpu.VMEM((tq, 2 * LANE), F32)],
        compiler_params=_params(("parallel", "parallel", "arbitrary")),
    )(qn, qr2, kv, kv, kr2, do, o, lse)


def _flash_bwd_dkv(qn, qr2, kv, kr2, do, do_off, lse, delta):
    s = qn.shape[0]
    tq = tk = min(512, s)
    nq = s // tq
    scale = (MLA_NOPE + MLA_ROPE) ** -0.5
    ob = do_off // LANE

    def body(qn_ref, qr_ref, kn_ref, v_ref, kr_ref, do_ref, lse_ref, dl_ref, dkv_ref, dkr_ref, ak_sc, av_sc):
        ki, qi = pl.program_id(1), pl.program_id(2)

        @pl.when(qi == 0)
        def _():
            ak_sc[...] = jnp.zeros_like(ak_sc)
            av_sc[...] = jnp.zeros_like(av_sc)

        def step(diagonal):
            sc, causal = _attn_scores(qn_ref, qr_ref, kn_ref, kr_ref, qi, ki, tq, tk)
            p = jnp.exp(sc - lse_ref[0])
            if diagonal:
                p = jnp.where(causal, p, 0.0)
            dp = _dot(do_ref[...], v_ref[...], NT)
            ds = (p * (dp - dl_ref[0]) * scale).astype(BF16)
            av_sc[...] += _dot(p, do_ref[...], TN)
            ak_sc[...] += _dot(ds, _lanes(qn_ref, qr_ref), TN)

        @pl.when(qi > ki)
        def _():
            step(False)

        @pl.when(qi == ki)
        def _():
            step(True)

        @pl.when(qi == nq - 1)
        def _():
            dkv_ref[:, :LANE] = ak_sc[:, :LANE]
            dkv_ref[:, LANE:] = av_sc[...]
            dkr_ref[...] = ak_sc[:, LANE:]

    q_of = lambda a, b: jnp.maximum(a, b)
    k_of = lambda a, b: a
    vec = pl.BlockSpec((1, tq, 1), lambda h, a, b: (h, q_of(a, b), 0))
    return _pcall(
        body, name="mla_flash_dkv",
        out_shape=(jax.ShapeDtypeStruct((s, 2 * MLA_WIDTH), F32), jax.ShapeDtypeStruct((s, MLA_WIDTH), F32)),
        grid=(MLA_HEADS, nq, nq),
        in_specs=_attn_specs(tq, tk, q_of, k_of)
        + [pl.BlockSpec((tq, LANE), lambda h, a, b: (q_of(a, b), h + ob)), vec, vec],
        out_specs=(pl.BlockSpec((tk, 2 * LANE), lambda h, a, b: (a, h)), pl.BlockSpec((tk, LANE), lambda h, a, b: (a, h))),
        scratch_shapes=[pltpu.VMEM((tk, 2 * LANE), F32), pltpu.VMEM((tk, LANE), F32)],
        compiler_params=_params(("parallel", "parallel", "arbitrary")),
    )(qn, qr2, kv, kv, kr2, do, lse, delta)


def _mla_fwd(proj, wl, cs, s):
    cqn = _rms_fwd(proj, P_CQ, MLA_Q_RANK, wl["mla_q_norm"], "mla_q_rms")
    ckvn = _rms_fwd(proj, P_CKV, MLA_KV_RANK, wl["mla_kv_norm"], "mla_kv_rms")
    qraw = _mm(cqn, wl["mla_wq"], mode="nn", name="mla_q_proj", m=s, n=2 * MLA_WIDTH, k=MLA_Q_RANK)
    kvraw = _mm(ckvn, wl["mla_w_ukv"], mode="nn", name="mla_kv_proj", m=s, n=2 * MLA_WIDTH, k=MLA_KV_RANK)
    qn, qr2, kv, kr2 = _mla_prep(qraw, kvraw, proj, cs)
    o, lse = _flash_fwd(qn, qr2, kv, kr2)
    return o, (cqn, ckvn, qn, qr2, kv, kr2, o, lse)


def _mla_bwd(dcat, proj, wl, cs, saved, s):
    cqn, ckvn, qn, qr2, kv, kr2, o, lse = saved
    dqn, dqr2, delta = _flash_bwd_dq(qn, qr2, kv, kr2, dcat, S5_WIDTH, o, lse)
    dkv, dkr2h = _flash_bwd_dkv(qn, qr2, kv, kr2, dcat, S5_WIDTH, lse, delta)
    dqraw, dkr = _mla_rope_bwd(dqn, dqr2, dkr2h, cs)
    dcqn = _mm(dqraw, wl["mla_wq"], mode="nt", name="mla_dcqn", m=s, n=MLA_Q_RANK, k=2 * MLA_WIDTH)
    dckvn = _mm(dkv, wl["mla_w_ukv"], mode="nt", name="mla_dckvn", m=s, n=MLA_KV_RANK, k=2 * MLA_WIDTH)
    dcq, dqg = _rms_bwd(dcqn, proj, P_CQ, MLA_Q_RANK, wl["mla_q_norm"], "mla_q_rms_bwd")
    dckv, dkvg = _rms_bwd(dckvn, proj, P_CKV, MLA_KV_RANK, wl["mla_kv_norm"], "mla_kv_rms_bwd")
    grads = {
        "mla_wq": _mm(cqn, dqraw, mode="tn", name="mla_dwq", m=MLA_Q_RANK, n=2 * MLA_WIDTH, k=s),
        "mla_w_ukv": _mm(ckvn, dkv, mode="tn", name="mla_dwukv", m=MLA_KV_RANK, n=2 * MLA_WIDTH, k=s),
        "mla_q_norm": dqg,
        "mla_kv_norm": dkvg,
    }
    return dcq, dckv, dkr, grads


HG_ROWS = 256


def _cumsum_rows(x, row, reverse=False):
    n = x.shape[0]
    step = 1
    while step < n:
        if reverse:
            x = x + jnp.where(row < n - step, pltpu.roll(x, n - step, 0), 0.0)
        else:
            x = x + jnp.where(row >= step, pltpu.roll(x, step, 0), 0.0)
        step *= 2
    return x


def _hg_gates(hq, z, lb):
    sig = _sigmoid(z)
    sigm = _sigmoid(-z)
    f = lb + (1.0 - lb) * sig
    return _silu(hq), (1.0 - lb) * sigm, jnp.log(f), sig, sigm, f


HG_SUB = 16
HG_NSUB = HG_CHUNK // HG_SUB


def _hg_offdiag(q, k, b, row, b_buf):
    ops = []
    for j in range(HG_NSUB - 1):
        e = (j + 1) * HG_SUB
        be = b_buf[e - 1:e, :]
        fj = jnp.where(row >= e, jnp.exp(jnp.minimum(b - be, 0.0)), 0.0)
        gj = jnp.where((row >= e - HG_SUB) & (row < e), jnp.exp(jnp.minimum(be - b, 0.0)), 0.0)
        ops.append((q * fj, k * gj, fj, gj))
    return ops


def _hg_diag_weights(qb, bb, ks, bs, rr, srow):
    e = jnp.exp(jnp.minimum(bb - bs, 0.0))
    keep = rr >= srow
    w = jnp.where(keep, jnp.sum(qb * ks * e, axis=1, keepdims=True), 0.0)
    return e, keep, w


def _hg_specs(rows):
    def col(off):
        return pl.BlockSpec((rows, LANE), lambda h, n, off=off: (n, off // LANE + h))
    return col


def _hg_fwd(proj, lb, gamma):
    s = proj.shape[0]
    rows = min(HG_ROWS, s)
    c = HG_CHUNK
    npc = rows // c
    nb = s // rows

    def body(hq_ref, hf_ref, hi_ref, hg_ref, lb_ref, gm_ref, y_ref, o_ref, st_ref, st_sc, k_buf, b_buf):
        n = pl.program_id(1)

        @pl.when(n == 0)
        def _():
            st_sc[...] = jnp.zeros_like(st_sc)

        row = lax.broadcasted_iota(jnp.int32, (c, LANE), 0)
        rr = lax.broadcasted_iota(jnp.int32, (HG_SUB, 1), 0)
        lbv = lb_ref[...]
        for ch in range(npc):
            sl = slice(ch * c, (ch + 1) * c)
            q, k, logf, _, _, _ = _hg_gates(hq_ref[sl, :], hf_ref[sl, :], lbv)
            v = hi_ref[sl, :]
            b = _cumsum_rows(logf, row)
            bl = _colsum(logf)
            k_buf[...] = k
            b_buf[...] = b
            st = st_sc[...]
            st_ref[0, ch] = st
            a_off = sum(_dot(qf, kg, NT) for qf, kg, _, _ in _hg_offdiag(q, k, b, row, b_buf))
            o = _dot(q * jnp.exp(b), st, NT) + _dot(a_off, v, NN)
            st_sc[...] = st * jnp.exp(bl) + _dot(v, k * jnp.exp(bl - b), TN)
            diag = []
            for i in range(HG_NSUB):
                r0 = i * HG_SUB
                qb, bb = q[r0:r0 + HG_SUB], b[r0:r0 + HG_SUB]
                acc = jnp.zeros((HG_SUB, LANE), F32)
                for srow in range(HG_SUB):
                    t = r0 + srow
                    _, _, w = _hg_diag_weights(qb, bb, k_buf[t:t + 1, :], b_buf[t:t + 1, :], rr, srow)
                    acc = acc + w * hi_ref[ch * c + t:ch * c + t + 1, :]
                diag.append(acc)
            o = o + jnp.concatenate(diag, axis=0)
            o_ref[sl, :] = o
            r = lax.rsqrt(_rowmean(o * o) + EPS)
            y_ref[sl, :] = (((o * r) * gm_ref[...]) * _silu(hg_ref[sl, :])).astype(y_ref.dtype)

    col = _hg_specs(rows)
    out_blk = pl.BlockSpec((rows, LANE), lambda h, n: (n, h))
    return _pcall(
        body, name="hgrn_fwd",
        out_shape=(jax.ShapeDtypeStruct((s, HG_WIDTH), BF16), jax.ShapeDtypeStruct((s, HG_WIDTH), F32),
                   jax.ShapeDtypeStruct((HG_HEADS, s // c, HG_DV, HG_DK), F32)),
        grid=(HG_HEADS, nb),
        in_specs=[col(P_HQ), col(P_HF), col(P_HI), col(P_HG),
                  pl.BlockSpec((1, LANE), lambda h, n: (0, h)), pl.BlockSpec((1, LANE), lambda h, n: (0, 0))],
        out_specs=(out_blk, out_blk, pl.BlockSpec((1, npc, HG_DV, HG_DK), lambda h, n: (h, n, 0, 0))),
        scratch_shapes=[pltpu.VMEM((HG_DV, HG_DK), F32), pltpu.VMEM((c, LANE), F32), pltpu.VMEM((c, LANE), F32)],
        compiler_params=_params(("parallel", "arbitrary")),
    )(proj, proj, proj, proj, lb, gamma)


def _hg_bwd(dcat, dy_off, proj, lb, gamma, o_saved, states):
    s = proj.shape[0]
    rows = min(HG_ROWS, s)
    c = HG_CHUNK
    npc = rows // c
    nb = s // rows
    yb = dy_off // LANE

    def body(hq_ref, hf_ref, hi_ref, hg_ref, lb_ref, gm_ref, dy_ref, o_ref, st_ref,
             dq_ref, df_ref, di_ref, dg_ref, dlb_ref, dgm_ref, dst_sc, k_buf, b_buf, dk_buf, dv_buf):
        n = pl.program_id(1)

        @pl.when(n == 0)
        def _():
            dst_sc[...] = jnp.zeros_like(dst_sc)
            dlb_ref[...] = jnp.zeros_like(dlb_ref)
            dgm_ref[...] = jnp.zeros_like(dgm_ref)

        row = lax.broadcasted_iota(jnp.int32, (c, LANE), 0)
        rr = lax.broadcasted_iota(jnp.int32, (HG_SUB, 1), 0)
        lbv, gmv = lb_ref[...], gm_ref[...]
        for ch in reversed(range(npc)):
            sl = slice(ch * c, (ch + 1) * c)
            hq, z, v, g = hq_ref[sl, :], hf_ref[sl, :], hi_ref[sl, :], hg_ref[sl, :]
            q, k, logf, sig, sigm, f = _hg_gates(hq, z, lbv)
            b = _cumsum_rows(logf, row)
            bl = _colsum(logf)
            k_buf[...] = k
            b_buf[...] = b
            eb, ebl = jnp.exp(b), jnp.exp(bl)
            qe, kl = q * eb, k * jnp.exp(bl - b)
            st = st_ref[0, ch]
            dst = dst_sc[...]
            o, dyv = o_ref[sl, :], dy_ref[sl, :]
            r = lax.rsqrt(_rowmean(o * o) + EPS)
            on = o * r
            sg = _silu(g)
            dgm_ref[0] += _colsum(dyv * on * sg)
            dg_ref[sl, :] = dyv * on * gmv * _dsilu(g)
            go = dyv * gmv * sg
            do = r * (go - on * _rowmean(go * on))
            dq = _dot(do, st, NN) * eb
            dkl = _dot(v, dst, NN)
            dk = dkl * jnp.exp(bl - b)
            dv = _dot(kl, dst, NT)
            dbl = _colsum(dst * st) * ebl + _colsum(dkl * kl)
            dst_sc[...] = dst * ebl + _dot(do, qe, TN)
            ops = _hg_offdiag(q, k, b, row, b_buf)
            da = _dot(do, v, NT)
            a_off = sum(_dot(qf, kg, NT) for qf, kg, _, _ in ops)
            dv = dv + _dot(a_off, do, TN)
            for qf, kg, fj, gj in ops:
                dq = dq + _dot(da, kg, NN) * fj
                dk = dk + _dot(da, qf, TN) * gj
            dq_diag = []
            for i in range(HG_NSUB):
                r0 = i * HG_SUB
                qb, bb, dob = q[r0:r0 + HG_SUB], b[r0:r0 + HG_SUB], do[r0:r0 + HG_SUB]
                acc = jnp.zeros((HG_SUB, LANE), F32)
                for srow in range(HG_SUB):
                    t = r0 + srow
                    ks = k_buf[t:t + 1, :]
                    e, keep, w = _hg_diag_weights(qb, bb, ks, b_buf[t:t + 1, :], rr, srow)
                    dw = jnp.where(keep, jnp.sum(dob * hi_ref[ch * c + t:ch * c + t + 1, :], axis=1, keepdims=True), 0.0)
                    dv_buf[t:t + 1, :] = _colsum(w * dob)
                    dk_buf[t:t + 1, :] = _colsum(dw * qb * e)
                    acc = acc + dw * (ks * e)
                dq_diag.append(acc)
            dq = dq + jnp.concatenate(dq_diag, axis=0)
            dk = dk + dk_buf[...]
            di_ref[sl, :] = dv + dv_buf[...]
            db = q * dq - k * dk + jnp.where(row == c - 1, dbl, 0.0)
            dlogf = _cumsum_rows(db, row, reverse=True)
            dq_ref[sl, :] = dq * _dsilu(hq)
            s1 = sig * (1.0 - sig) * (1.0 - lbv)
            df_ref[sl, :] = dlogf * s1 / f - dk * s1
            dlb_ref[0] += _colsum(dlogf * sigm / f - dk * sigm)

    def col(off):
        return pl.BlockSpec((rows, LANE), lambda h, n, off=off: (nb - 1 - n, off // LANE + h))

    out_blk = pl.BlockSpec((rows, LANE), lambda h, n: (nb - 1 - n, h))
    acc_blk = pl.BlockSpec((1, 1, LANE), lambda h, n: (h, 0, 0))
    res = _pcall(
        body, name="hgrn_bwd",
        out_shape=tuple(jax.ShapeDtypeStruct((s, HG_WIDTH), F32) for _ in range(4))
        + (jax.ShapeDtypeStruct((HG_HEADS, 1, LANE), F32), jax.ShapeDtypeStruct((HG_HEADS, 1, LANE), F32)),
        grid=(HG_HEADS, nb),
        in_specs=[col(P_HQ), col(P_HF), col(P_HI), col(P_HG),
                  pl.BlockSpec((1, LANE), lambda h, n: (0, h)), pl.BlockSpec((1, LANE), lambda h, n: (0, 0)),
                  pl.BlockSpec((rows, LANE), lambda h, n: (nb - 1 - n, yb + h)), out_blk,
                  pl.BlockSpec((1, npc, HG_DV, HG_DK), lambda h, n: (h, nb - 1 - n, 0, 0))],
        out_specs=(out_blk, out_blk, out_blk, out_blk, acc_blk, acc_blk),
        scratch_shapes=[pltpu.VMEM((HG_DV, HG_DK), F32)] + [pltpu.VMEM((c, LANE), F32)] * 4,
        compiler_params=_params(("parallel", "arbitrary")),
    )(proj, proj, proj, proj, lb, gamma, dcat, o_saved, states)
    dq, df, di, dg, dlb, dgm = res
    return dq, df, di, dg, dlb.reshape(1, HG_WIDTH), dgm.reshape(HG_HEADS, LANE)


def _adamw_math(w, g, m, v):
    m = ADAM_B1 * m + (1.0 - ADAM_B1) * g
    v = ADAM_B2 * v + (1.0 - ADAM_B2) * (g * g)
    m_hat = m / (1.0 - ADAM_B1 ** ADAM_STEP)
    v_hat = v / (1.0 - ADAM_B2 ** ADAM_STEP)
    delta = -ADAM_LR * (m_hat / (jnp.sqrt(v_hat) + ADAM_EPS) + ADAM_WD * w)
    return delta, m, v


def _adamw(w, g, m, v, name):
    shape = w.shape
    width = shape[-1]
    rows = int(np.prod(shape[:-1]))
    tm = rows
    while tm * width * 4 > (1 << 20) and tm % 16 == 0:
        tm //= 2

    def fn(i, w_ref, g_ref, m_ref, v_ref, d_ref, mo_ref, vo_ref):
        d, mn, vn = _adamw_math(w_ref[...], g_ref[...], m_ref[...], v_ref[...])
        d_ref[...] = d
        mo_ref[...] = mn
        vo_ref[...] = vn

    v2 = lambda t: t.reshape(rows, width)
    res = _rows(fn, name=name, s=rows, tm=tm, ins=[("row", v2(t), 0, width) for t in (w, g, m, v)],
                outs=[("row", width, F32)] * 3)
    return tuple(r.reshape(shape) for r in res)


def _ada_grad_adamw(cact_all, dmod_cols, w, m, v):
    n_layers, kdim, n = w.shape
    tm, tn = _tile(kdim, 256), _tile(n, 1024)

    def body(c_ref, d_ref, w_ref, m_ref, v_ref, g_ref, dl_ref, mo_ref, vo_ref):
        g = _dot(c_ref[...], d_ref[...], TN)
        d, mn, vn = _adamw_math(w_ref[...], g, m_ref[...], v_ref[...])
        g_ref[...] = g
        dl_ref[...] = d
        mo_ref[...] = mn
        vo_ref[...] = vn

    blk = pl.BlockSpec((None, tm, tn), lambda l, i, j: (l, i, j))
    return _pcall(
        body, name="ada_grad_adamw", out_shape=tuple(jax.ShapeDtypeStruct(w.shape, F32) for _ in range(4)),
        grid=(n_layers, kdim // tm, n // tn),
        in_specs=[pl.BlockSpec((cact_all.shape[0], tm), lambda l, i, j: (0, i)),
                  pl.BlockSpec((None, dmod_cols.shape[1], tn), lambda l, i, j: (l, 0, j)), blk, blk, blk],
        out_specs=(blk, blk, blk, blk),
        compiler_params=_params(("parallel", "parallel", "parallel")),
    )(cact_all, dmod_cols, w, m, v)


def _me():
    return lax.axis_index("x"), lax.axis_index("y"), lax.axis_index("c")


def _flip(k):
    x, y, c = _me()
    return (x ^ ((k >> 2) & 1), y ^ ((k >> 1) & 1), c ^ (k & 1))


def _lin(dev):
    return 4 * dev[0] + 2 * dev[1] + dev[2]


ANY = pl.BlockSpec(memory_space=pl.ANY)


def _all_gather8(x, name):
    def body(x_ref, out_ref, send_sems, recv_sems, local_sem):
        me = _lin(_me())
        mine = pltpu.make_async_copy(x_ref, out_ref.at[me], local_sem)
        mine.start()
        copies = []
        for k in range(1, 8):
            cp = pltpu.make_async_remote_copy(src_ref=x_ref, dst_ref=out_ref.at[me], send_sem=send_sems.at[k - 1],
                                              recv_sem=recv_sems.at[k - 1], device_id=_flip(k), device_id_type=MESH)
            cp.start()
            copies.append(cp)
        for k in range(1, 8):
            pltpu.make_async_remote_copy(src_ref=x_ref, dst_ref=out_ref.at[_lin(_flip(k))], send_sem=send_sems.at[k - 1],
                                         recv_sem=recv_sems.at[k - 1], device_id=_flip(k), device_id_type=MESH).wait_recv()
        for cp in copies:
            cp.wait_send()
        mine.wait()

    return _pcall(
        body, name=name, out_shape=jax.ShapeDtypeStruct((8,) + x.shape, x.dtype),
        in_specs=[ANY], out_specs=ANY,
        scratch_shapes=[pltpu.SemaphoreType.DMA((7,)), pltpu.SemaphoreType.DMA((7,)), pltpu.SemaphoreType.DMA],
    )(x)


CHIP_FLIPS = (2, 4, 6)


def _row_tile(r, cdim):
    best = SUBLANE
    for t in range(SUBLANE, r + 1, SUBLANE):
        if r % t == 0 and t * cdim * 4 <= (3 << 20):
            best = t
    assert r % best == 0
    return best


def _gather_weights(ws):
    n = len(ws)

    def body(*refs):
        w_refs, out_refs = refs[:n], refs[n:2 * n]
        send_sems, recv_sems = refs[2 * n:]
        x, y, c = _me()
        sib = _flip(1)

        def slot(a, dev, half):
            return out_refs[a].at[2 * dev[0] + dev[1], half]

        first = []
        for a in range(n):
            for j, k in enumerate(CHIP_FLIPS):
                cp = pltpu.make_async_remote_copy(src_ref=w_refs[a].at[c], dst_ref=slot(a, (x, y), c), send_sem=send_sems.at[6 * a + j],
                                                  recv_sem=recv_sems.at[6 * a + j], device_id=_flip(k), device_id_type=MESH)
                cp.start()
                first.append(cp)
        passed = []
        for a in range(n):
            for j, k in enumerate(CHIP_FLIPS):
                src = _flip(k)
                landed = slot(a, src, c)
                pltpu.make_async_remote_copy(src_ref=landed, dst_ref=landed, send_sem=send_sems.at[6 * a + j],
                                             recv_sem=recv_sems.at[6 * a + j], device_id=src, device_id_type=MESH).wait_recv()
                cp = pltpu.make_async_remote_copy(src_ref=landed, dst_ref=landed, send_sem=send_sems.at[6 * a + 3 + j],
                                                  recv_sem=recv_sems.at[6 * a + 3 + j], device_id=sib, device_id_type=MESH)
                cp.start()
                passed.append(cp)
        for a in range(n):
            for j, k in enumerate(CHIP_FLIPS):
                got = slot(a, _flip(k), 1 - c)
                pltpu.make_async_remote_copy(src_ref=got, dst_ref=got, send_sem=send_sems.at[6 * a + 3 + j],
                                             recv_sem=recv_sems.at[6 * a + 3 + j], device_id=sib, device_id_type=MESH).wait_recv()
        for cp in first + passed:
            cp.wait_send()

    got = _pcall(
        body, name="gather_weights", out_shape=tuple(jax.ShapeDtypeStruct((N_CHIPS,) + w.shape, w.dtype) for w in ws),
        in_specs=[ANY] * n, out_specs=(ANY,) * n,
        scratch_shapes=[pltpu.SemaphoreType.DMA((6 * n,)), pltpu.SemaphoreType.DMA((6 * n,))],
    )(*ws)
    chip = 2 * lax.axis_index("x") + lax.axis_index("y")
    return [lax.dynamic_update_index_in_dim(g, w, chip, axis=0) for g, w in zip(got, ws)]


def _sibling_halves(gs):
    n = len(gs)

    def body(*refs):
        s_refs, out_refs = refs[:n], refs[n:2 * n]
        send_sems, recv_sems = refs[2 * n:]
        c = lax.axis_index("c")
        sib = _flip(1)
        copies = []
        for a in range(n):
            for j in range(N_CHIPS):
                cp = pltpu.make_async_remote_copy(src_ref=s_refs[a].at[j, 1 - c], dst_ref=out_refs[a].at[j], send_sem=send_sems.at[4 * a + j],
                                                  recv_sem=recv_sems.at[4 * a + j], device_id=sib, device_id_type=MESH)
                cp.start()
                copies.append(cp)
        for cp in copies:
            cp.wait()

    return _pcall(
        body, name="rs_sibling_halves", out_shape=tuple(jax.ShapeDtypeStruct((N_CHIPS,) + g.shape[2:], g.dtype) for g in gs),
        in_specs=[ANY] * n, out_specs=(ANY,) * n,
        scratch_shapes=[pltpu.SemaphoreType.DMA((4 * n,)), pltpu.SemaphoreType.DMA((4 * n,))],
    )(*gs)


def _scatter_to_chips(parts):
    n = len(parts)

    def body(*refs):
        p_refs, out_refs = refs[:n], refs[n:2 * n]
        send_sems, recv_sems = refs[2 * n:]
        copies = []
        for a in range(n):
            for j, k in enumerate(CHIP_FLIPS):
                to = _flip(k)
                cp = pltpu.make_async_remote_copy(src_ref=p_refs[a].at[2 * to[0] + to[1]], dst_ref=out_refs[a].at[j],
                                                  send_sem=send_sems.at[3 * a + j], recv_sem=recv_sems.at[3 * a + j],
                                                  device_id=to, device_id_type=MESH)
                cp.start()
                copies.append(cp)
        for cp in copies:
            cp.wait()

    return _pcall(
        body, name="scatter_to_chips", out_shape=tuple(jax.ShapeDtypeStruct((3,) + p.shape[1:], p.dtype) for p in parts),
        in_specs=[ANY] * n, out_specs=(ANY,) * n,
        scratch_shapes=[pltpu.SemaphoreType.DMA((3 * n,)), pltpu.SemaphoreType.DMA((3 * n,))],
    )(*parts)


def _sibling_result(halves):
    n = len(halves)

    def body(*refs):
        h_refs, out_refs = refs[:n], refs[n:2 * n]
        send_sems, recv_sems = refs[2 * n:]
        sib = _flip(1)
        copies = []
        for a in range(n):
            cp = pltpu.make_async_remote_copy(src_ref=h_refs[a], dst_ref=out_refs[a], send_sem=send_sems.at[a],
                                              recv_sem=recv_sems.at[a], device_id=sib, device_id_type=MESH)
            cp.start()
            copies.append(cp)
        for cp in copies:
            cp.wait()

    theirs = _pcall(
        body, name="rs_sibling_result", out_shape=tuple(jax.ShapeDtypeStruct(h.shape, h.dtype) for h in halves),
        in_specs=[ANY] * n, out_specs=(ANY,) * n,
        scratch_shapes=[pltpu.SemaphoreType.DMA((n,)), pltpu.SemaphoreType.DMA((n,))],
    )(*halves)
    c = lax.axis_index("c")
    return [jnp.where(c == 0, jnp.stack([m, t]), jnp.stack([t, m])) for m, t in zip(halves, theirs)]


def _add_halves(g, r1, name):
    n, _, r, cdim = g.shape
    tm = _row_tile(r, cdim)

    def body(c_ref, g_ref, r_ref, o_ref):
        o_ref[...] = g_ref[...] + r_ref[...]

    return _pcall(
        body, name=name, out_shape=jax.ShapeDtypeStruct((n, r, cdim), g.dtype),
        grid_spec=pltpu.PrefetchScalarGridSpec(
            num_scalar_prefetch=1, grid=(n, r // tm),
            in_specs=[pl.BlockSpec((None, None, tm, cdim), lambda j, i, c_ref: (j, c_ref[0], i, 0)),
                      pl.BlockSpec((None, tm, cdim), lambda j, i, c_ref: (j, i, 0))],
            out_specs=pl.BlockSpec((None, tm, cdim), lambda j, i, c_ref: (j, i, 0))),
        compiler_params=_params(("parallel", "parallel")),
    )(lax.axis_index("c").astype(jnp.int32).reshape(1), g, r1)


def _add_chips(part, got, name):
    _, r, cdim = part.shape
    tm = _row_tile(r, cdim)

    def body(chip_ref, p_ref, g_ref, o_ref):
        o_ref[...] = ((p_ref[...] + g_ref[0]) + g_ref[1]) + g_ref[2]

    chip = (2 * lax.axis_index("x") + lax.axis_index("y")).astype(jnp.int32).reshape(1)
    return _pcall(
        body, name=name, out_shape=jax.ShapeDtypeStruct((r, cdim), part.dtype),
        grid_spec=pltpu.PrefetchScalarGridSpec(
            num_scalar_prefetch=1, grid=(r // tm,),
            in_specs=[pl.BlockSpec((None, tm, cdim), lambda i, chip_ref: (chip_ref[0], i, 0)),
                      pl.BlockSpec((3, tm, cdim), lambda i, chip_ref: (0, i, 0))],
            out_specs=pl.BlockSpec((tm, cdim), lambda i, chip_ref: (i, 0))),
        compiler_params=_params(("parallel",)),
    )(chip, part, got)


def _reduce_scatter(gs, names):
    r1 = _sibling_halves(gs)
    parts = [_add_halves(g, r, "add_halves_" + nm) for g, r, nm in zip(gs, r1, names)]
    got = _scatter_to_chips(parts)
    mine = [_add_chips(p, q, "add_chips_" + nm) for p, q, nm in zip(parts, got, names)]
    return _sibling_result(mine)


def _sum8(x):
    _, r, n = x.shape
    tm = 128 if r % 128 == 0 else r

    def body(x_ref, o_ref):
        acc = x_ref[0]
        for d in range(1, 8):
            acc = acc + x_ref[d]
        o_ref[...] = acc

    return _pcall(body, name="sum8", out_shape=jax.ShapeDtypeStruct((r, n), x.dtype), grid=(r // tm,),
                  in_specs=[pl.BlockSpec((8, tm, n), lambda i: (0, i, 0))], out_specs=pl.BlockSpec((tm, n), lambda i: (i, 0)),
                  compiler_params=_params(("parallel",)))(x)


SHARDED = ("w_in", "s5_w_glu", "mla_w_uq", "mla_w_ukv", "w_out", "ffn_w_up", "ffn_w_down")
COL_SHARDED = ("w_in", "mla_w_uq", "mla_w_ukv", "ffn_w_up")
REPLICATED = ("s5_lambda_re", "s5_lambda_im", "s5_log_dt", "s5_b_re", "s5_b_im", "s5_c_re", "s5_c_im", "s5_d",
              "mla_q_norm", "mla_kv_norm", "hg_lb_logits", "hg_out_norm", "mix_pre_norm", "mix_post_norm",
              "ffn_pre_norm", "ffn_post_norm", "ffn_conv_b")
WEIGHTS = ("w_in", "s5_lambda_re", "s5_lambda_im", "s5_log_dt", "s5_b_re", "s5_b_im", "s5_c_re", "s5_c_im", "s5_d",
           "s5_w_glu", "mla_q_norm", "mla_w_uq", "mla_kv_norm", "mla_w_ukv", "hg_lb_logits", "hg_out_norm", "w_out",
           "mix_pre_norm", "mix_post_norm", "ffn_pre_norm", "ffn_post_norm", "ffn_w_up", "ffn_conv_w", "ffn_conv_b",
           "ffn_w_down", "w_ada", "b_ada")


FF_PAD = 256


def _halves(t):
    return t.reshape(t.shape[:-2] + (2, t.shape[-2] // 2, t.shape[-1]))


def _unhalve(t):
    return t.reshape(t.shape[:-3] + (2 * t.shape[-2], t.shape[-1]))


def _cols_from_chips(t):
    return jnp.concatenate([t[j] for j in range(N_CHIPS)], axis=1)


def _swap_half(t):
    half = t.shape[-1] // 2
    return jnp.concatenate([-t[..., half:], t[..., :half]], axis=-1)


def _prep_win(w):
    s5, cq, ckv, kr, hq, hf, hi, hg = jnp.split(w, (512, 1024, 1280, 1344, 1856, 2368, 2880), axis=1)
    pad = jnp.zeros((w.shape[0], PROJ_W - 3456), w.dtype)
    return jnp.concatenate([s5, cq, hq, hf, hi, hg, ckv, kr, _swap_half(kr), pad], axis=1)


def _prep_wq(w):
    w3 = w.reshape(w.shape[0], MLA_HEADS, MLA_NOPE + MLA_ROPE)
    nope, rope = w3[..., :MLA_NOPE], w3[..., MLA_NOPE:]
    pair = jnp.concatenate([rope, _swap_half(rope)], axis=-1)
    return jnp.concatenate([nope.reshape(w.shape[0], -1), pair.reshape(w.shape[0], -1)], axis=1)


def _pad_ff_cols(w, cpad):
    r = w.shape[0]
    w3 = w.reshape(r, N_CHIPS, -1)
    return jnp.pad(w3, ((0, 0), (0, 0), (0, cpad - w3.shape[2]))).reshape(r, N_CHIPS * cpad)


def _pad_ff_rows(w, cpad):
    w3 = w.reshape(2, 2 * w.shape[1], w.shape[2])
    return jnp.pad(w3, ((0, 0), (0, cpad - w3.shape[1]), (0, 0))).reshape(2 * cpad, w.shape[2])


def _interleave(re, im, axis):
    re, im = jnp.moveaxis(re, axis, -1), jnp.moveaxis(im, axis, -1)
    lead = re.shape[:-1]
    both = jnp.stack([re.reshape(lead + (N_STATE // SCAN_W, SCAN_W)), im.reshape(lead + (N_STATE // SCAN_W, SCAN_W))], axis=-2)
    return jnp.moveaxis(both.reshape(lead + (2 * N_STATE,)), -1, axis)


def _s5_prep(lre, lim, logdt, bre, bim, cre, cim):
    dt = jnp.exp(logdt)[:, None]
    er = jnp.exp(lre * dt)
    ar, ai = er * jnp.cos(lim * dt), er * jnp.sin(lim * dt)
    nr, den = ar - 1.0, lre * lre + lim * lim
    cr, ci = (nr * lre + ai * lim) / den, (ai * lre - nr * lim) / den
    bbr = cr[..., None] * bre - ci[..., None] * bim
    bbi = cr[..., None] * bim + ci[..., None] * bre
    eye = jnp.eye(S5_GROUPS, dtype=F32)[:, None, :, None]

    def block_diag(t):
        return (t[:, :, None, :] * eye).reshape(S5_GROUPS * t.shape[1], S5_GROUPS * t.shape[2])

    tr = lambda t: jnp.transpose(t, (0, 2, 1))
    bd = _interleave(block_diag(tr(bbr)), block_diag(tr(bbi)), 1)
    cd = _interleave(block_diag(tr(cre)), block_diag(tr(-cim)), 0)
    a = _interleave(ar.reshape(1, N_STATE), ai.reshape(1, N_STATE), 1)
    return a, bd, cd


def _lower_bounds(logits):
    probs = jax.nn.softmax(logits, axis=0)
    return jnp.cumsum(probs, axis=0) - probs[0:1]


def _rope_table(positions):
    inv_freq = 1.0 / (ROPE_THETA ** (jnp.arange(0, MLA_ROPE, 2, dtype=F32) / MLA_ROPE))
    ang = positions.astype(F32)[:, None] * inv_freq
    cos, sin = jnp.cos(ang), jnp.sin(ang)
    return jnp.concatenate([cos, cos, sin, sin], axis=1)


def _split_mod(mod):
    return [mod[:, i * D_MODEL:(i + 1) * D_MODEL] for i in range(6)]


def _layer_fwd(x, wl, mod, cs):
    s = x.shape[0]
    ffp = wl["wdown_p"].shape[0]
    sh1, sc1, g1, sh2, sc2, g2 = _split_mod(mod)
    h1 = _normmod_fwd(x, wl["mix_pre_norm"], sc1, sh1, "mix_pre")
    proj = _mm(h1, wl["win_p"], mode="nn", name="in_proj", m=s, n=PROJ_W, k=D_MODEL)
    out_s5, s5_saved = _s5_fwd(proj, wl, s)
    o_mla, mla_saved = _mla_fwd(proj, wl, cs, s)
    y_hg, o_hg, states = _hg_fwd(proj, wl["hg_lb"], wl["hg_out_norm"])
    cat = jnp.concatenate([out_s5, o_mla.astype(BF16), y_hg], axis=1)
    mixed = _mm(cat, wl["w_out"], mode="nn", name="out_proj", m=s, n=D_MODEL, k=D_MODEL)
    x2 = _postnorm_fwd(x, mixed, wl["mix_post_norm"], g1, "mix_post")
    h2 = _normmod_fwd(x2, wl["ffn_pre_norm"], sc2, sh2, "ffn_pre")
    up = _mm(h2, wl["wup_cm"], mode="nn", name="ffn_up", m=s, n=2 * ffp, k=D_MODEL, cm=ffp // 2, tn=ffp // 4)
    act = _ffn_act_fwd(up, wl["conv_w_p"], wl["conv_b_p"], ffp)
    y = _mm(act, wl["wdown_p"], mode="nn", name="ffn_down", m=s, n=D_MODEL, k=ffp)
    x3 = _postnorm_fwd(x2, y, wl["ffn_post_norm"], g2, "ffn_post")
    return x3, (x, h1, proj, s5_saved, mla_saved, o_hg, states, cat, mixed, x2, h2, up, act, y)


def _layer_bwd(dx3, saved, wl, mod, cs):
    x, h1, proj, s5_saved, mla_saved, o_hg, states, cat, mixed, x2, h2, up, act, y = saved
    s = x.shape[0]
    ffp = wl["wdown_p"].shape[0]
    sh1, sc1, g1, sh2, sc2, g2 = _split_mod(mod)
    g = {}
    dy, g["ffn_post_norm"], dg2 = _postnorm_bwd(dx3, y, wl["ffn_post_norm"], g2, "ffn_post_bwd")
    da = _mm(dy, wl["wdown_p"], mode="nt", name="ffn_down_dx", m=s, n=ffp, k=D_MODEL)
    g["wdown_p"] = _mm(act, dy, mode="tn", name="ffn_down_dw", m=ffp, n=D_MODEL, k=s)
    dug, duv, dwg, dwv, dbg, dbv = _ffn_act_bwd(da, up, wl["conv_w_p"], wl["conv_b_p"], ffp)
    g["conv_w_p"] = jnp.concatenate([dwg, dwv], axis=1)
    g["conv_b_p"] = jnp.concatenate([dbg, dbv], axis=1)
    dup = jnp.concatenate([_conv_bwd_input(dug, wl["conv_w_p"], 0, "ffn_conv_bwd_gate"),
                           _conv_bwd_input(duv, wl["conv_w_p"], ffp, "ffn_conv_bwd_val")], axis=1)
    dh2 = _mm(dup, wl["wup_cm"], mode="nt", name="ffn_up_dx", m=s, n=D_MODEL, k=2 * ffp, cm=ffp // 2, tk=ffp // 4)
    g["wup_cm"] = _mm(h2, dup, mode="tn", name="ffn_up_dw", m=D_MODEL, n=2 * ffp, k=s, cm=ffp // 2, tn=ffp // 4)
    dx2, g["ffn_pre_norm"], dsc2, dsh2 = _normmod_bwd(dh2, x2, wl["ffn_pre_norm"], sc2, dx3, "ffn_pre_bwd")
    dmixed, g["mix_post_norm"], dg1 = _postnorm_bwd(dx2, mixed, wl["mix_post_norm"], g1, "mix_post_bwd")
    dcat = _mm(dmixed, wl["w_out"], mode="nt", name="out_proj_dx", m=s, n=D_MODEL, k=D_MODEL)
    g["w_out"] = _mm(cat, dmixed, mode="tn", name="out_proj_dw", m=D_MODEL, n=D_MODEL, k=s)
    du_s5, s5g = _s5_bwd(dcat, proj, wl, s5_saved, s)
    dcq, dckv, dkr, mlag = _mla_bwd(dcat, proj, wl, cs, mla_saved, s)
    dhq, dhf, dhi, dhg, g["hg_lb"], dgm = _hg_bwd(dcat, S5_WIDTH + MLA_WIDTH, proj, wl["hg_lb"], wl["hg_out_norm"], o_hg, states)
    g["hg_out_norm"] = jnp.sum(dgm, axis=0, keepdims=True)
    g.update(s5g)
    g.update(mlag)
    dproj = jnp.concatenate([du_s5, dcq, dhq, dhf, dhi, dhg, dckv, dkr, jnp.zeros((s, PROJ_W - 3456), F32)], axis=1).astype(BF16)
    dh1 = _mm(dproj, wl["win_p"], mode="nt", name="in_proj_dx", m=s, n=D_MODEL, k=PROJ_W)
    g["win_p"] = _mm(h1, dproj, mode="tn", name="in_proj_dw", m=D_MODEL, n=PROJ_W, k=s)
    dx, g["mix_pre_norm"], dsc1, dsh1 = _normmod_bwd(dh1, x, wl["mix_pre_norm"], sc1, dx2, "mix_pre_bwd")
    dmod = jnp.concatenate([dsh1, dsc1, dg1, dsh2, dsc2, dg2], axis=1)
    return dx, g, dmod


def _prepare_layer(gathered, conv_w, rep, cpad):
    def sharded_prep(w_in, s5_w_glu, mla_w_uq, mla_w_ukv, w_out, ffn_w_up, ffn_w_down, ffn_conv_w):
        merge = lambda t: t.reshape(N_CHIPS * t.shape[1], t.shape[2])
        return {"win_p": _prep_win(_cols_from_chips(w_in)), "s5_w_glu": merge(s5_w_glu), "mla_wq": _prep_wq(_cols_from_chips(mla_w_uq)),
                "mla_w_ukv": _cols_from_chips(mla_w_ukv), "w_out": merge(w_out), "wup_cm": ffn_w_up,
                "wdown_p": _pad_ff_rows(ffn_w_down, cpad), "conv_w_p": _pad_ff_cols(ffn_conv_w, cpad)}

    def rep_prep(lre, lim, logdt, bre, bim, cre, cim, conv_b):
        a, bd, cd = _s5_prep(lre, lim, logdt, bre, bim, cre, cim)
        return {"s5_a": a, "s5_bd": bd, "s5_cd": cd, "conv_b_p": _pad_ff_cols(conv_b, cpad)}

    sh_args = [gathered[k] for k in SHARDED] + [conv_w]
    rep_names = ("s5_lambda_re", "s5_lambda_im", "s5_log_dt", "s5_b_re", "s5_b_im", "s5_c_re", "s5_c_im", "ffn_conv_b")
    rep_args = [rep[k] for k in rep_names]
    wl = sharded_prep(*sh_args)
    rep_out, rep_vjp = jax.vjp(rep_prep, *rep_args)
    wl.update(rep_out)
    sh_t = jax.linear_transpose(sharded_prep, *[jax.ShapeDtypeStruct(a.shape, F32) for a in sh_args])

    def back(g):
        out = dict(zip(SHARDED + ("ffn_conv_w",), sh_t({k: g[k] for k in ("win_p", "s5_w_glu", "mla_wq", "mla_w_ukv", "w_out", "wup_cm", "wdown_p", "conv_w_p")})))
        out.update(zip(rep_names, rep_vjp({k: g[k] for k in ("s5_a", "s5_bd", "s5_cd", "conv_b_p")})))
        return out

    return wl, back


PER_LAYER_ROWS = ("s5_d", "mla_q_norm", "mla_kv_norm", "hg_out_norm", "mix_pre_norm", "mix_post_norm", "ffn_pre_norm", "ffn_post_norm")


def _flat_pad(parts, unit):
    flat = jnp.concatenate([p.reshape(-1) for p in parts])
    n = -(-flat.shape[0] // unit) * unit
    return jnp.pad(flat, (0, n - flat.shape[0])).reshape(-1, FLAT_COLS)


def _split_flat(flat, like):
    flat = flat.reshape(-1)
    out, pos = [], 0
    for t in like:
        out.append(flat[pos:pos + t.size].reshape(t.shape))
        pos += t.size
    return out


def kernel(x, c, positions, w_in, s5_lambda_re, s5_lambda_im, s5_log_dt, s5_b_re, s5_b_im, s5_c_re, s5_c_im, s5_d, s5_w_glu, mla_q_norm, mla_w_uq, mla_kv_norm, mla_w_ukv, hg_lb_logits, hg_out_norm, w_out, mix_pre_norm, mix_post_norm, ffn_pre_norm, ffn_post_norm, ffn_w_up, ffn_conv_w, ffn_conv_b, ffn_w_down, w_ada, b_ada, loss_target, m_w_in, m_s5_lambda_re, m_s5_lambda_im, m_s5_log_dt, m_s5_b_re, m_s5_b_im, m_s5_c_re, m_s5_c_im, m_s5_d, m_s5_w_glu, m_mla_q_norm, m_mla_w_uq, m_mla_kv_norm, m_mla_w_ukv, m_hg_lb_logits, m_hg_out_norm, m_w_out, m_mix_pre_norm, m_mix_post_norm, m_ffn_pre_norm, m_ffn_post_norm, m_ffn_w_up, m_ffn_conv_w, m_ffn_conv_b, m_ffn_w_down, m_w_ada, m_b_ada, v_w_in, v_s5_lambda_re, v_s5_lambda_im, v_s5_log_dt, v_s5_b_re, v_s5_b_im, v_s5_c_re, v_s5_c_im, v_s5_d, v_s5_w_glu, v_mla_q_norm, v_mla_w_uq, v_mla_kv_norm, v_mla_w_ukv, v_hg_lb_logits, v_hg_out_norm, v_w_out, v_mix_pre_norm, v_mix_post_norm, v_ffn_pre_norm, v_ffn_post_norm, v_ffn_w_up, v_ffn_conv_w, v_ffn_conv_b, v_ffn_w_down, v_w_ada, v_b_ada):
    p = dict(locals())
    n_layers = w_in.shape[0]
    c_up = ffn_w_up.shape[2]
    cpad = -(-c_up // FF_PAD) * FF_PAD
    xs, target = x[0], loss_target[0]
    me = 4 * lax.axis_index("x") + 2 * lax.axis_index("y") + lax.axis_index("c")
    chip = 2 * lax.axis_index("x") + lax.axis_index("y")
    cs = _rope_table(positions[0])

    cact = jax.nn.silu(_all_gather8(c, "gather_c")[:, 0, :])
    ada_cols = w_ada.shape[2]
    mod_part = jnp.stack([_mm(cact, w_ada[l], mode="nn", name="ada_mod", m=8, n=ada_cols, k=D_MODEL) for l in range(n_layers)])
    mod_all = _all_gather8(mod_part.reshape(1, -1), "gather_mod").reshape(N_CHIPS, 2, n_layers, 8, ada_cols)[:, 0]
    mod_mine = lax.dynamic_index_in_dim(mod_all, me, axis=2, keepdims=False)
    mods = mod_mine.transpose(1, 0, 2).reshape(n_layers, -1) + b_ada

    conv_w_all = _all_gather8(ffn_conv_w.reshape(1, -1), "gather_conv_w").reshape(N_CHIPS, 2, n_layers, 3, -1)[:, 0]
    conv_w_full = conv_w_all.transpose(1, 2, 0, 3).reshape(n_layers, 3, -1)

    lbs, lb_vjp = jax.vjp(_lower_bounds, hg_lb_logits)

    layers = []
    for l in range(n_layers):
        shards = {k: p[k][l] for k in SHARDED}
        shards["ffn_w_up"] = jnp.pad(shards["ffn_w_up"], ((0, 0), (0, cpad - c_up)))
        got = _gather_weights([_halves(shards[k].astype(BF16)) for k in SHARDED])
        gathered = {k: _unhalve(t) for k, t in zip(SHARDED, got)}
        rep = {k: p[k][l] for k in ("s5_lambda_re", "s5_lambda_im", "s5_log_dt", "s5_b_re", "s5_b_im", "s5_c_re", "s5_c_im")}
        rep["ffn_conv_b"] = ffn_conv_b[l][None, :]
        wl, back = _prepare_layer(gathered, conv_w_full[l], rep, cpad)
        for k in PER_LAYER_ROWS:
            wl[k] = p[k][l][None, :]
        wl["hg_lb"] = lbs[l][None, :]
        layers.append((wl, back))

    h = xs
    saved = []
    for l in range(n_layers):
        h, sv = _layer_fwd(h, layers[l][0], mods[l][None, :], cs)
        saved.append(sv)
    dh, loss_part = _loss_grad(h, target)
    loss = lax.psum(loss_part[0, 0], ("x", "y", "c"))

    grads = {k: [None] * n_layers for k in WEIGHTS}
    dmods, dlbs = [None] * n_layers, [None] * n_layers
    for l in reversed(range(n_layers)):
        wl, back = layers[l]
        dh, g, dmods[l] = _layer_bwd(dh, saved[l], wl, mods[l][None, :], cs)
        ref_g = back(g)
        reduced = _reduce_scatter([_halves(ref_g[k]) for k in SHARDED], SHARDED)
        for k, t in zip(SHARDED, reduced):
            grads[k][l] = _unhalve(t)
        grads["ffn_w_up"][l] = grads["ffn_w_up"][l][:, :c_up]
        grads["ffn_conv_w"][l] = ref_g["ffn_conv_w"]
        for k in ("s5_lambda_re", "s5_lambda_im", "s5_log_dt", "s5_b_re", "s5_b_im", "s5_c_re", "s5_c_im"):
            grads[k][l] = ref_g[k]
        grads["ffn_conv_b"][l] = ref_g["ffn_conv_b"][0]
        for k in PER_LAYER_ROWS:
            grads[k][l] = g[k][0]
        dlbs[l] = g["hg_lb"][0]
    grads = {k: jnp.stack(v) for k, v in grads.items() if v[0] is not None}
    grads["hg_lb_logits"] = lb_vjp(jnp.stack(dlbs))[0]

    summed = REPLICATED + ("ffn_conv_w",)
    rep_flat = _flat_pad([grads[k] for k in summed], 128 * FLAT_COLS)
    rep_sum = _sum8(_all_gather8(rep_flat, "gather_small_grads"))
    for k, t in zip(summed, _split_flat(rep_sum, [grads[k] for k in summed])):
        grads[k] = t
    conv_cols = grads["ffn_conv_w"].reshape(n_layers, 3, N_CHIPS, c_up)
    grads["ffn_conv_w"] = lax.dynamic_index_in_dim(conv_cols, chip, axis=2, keepdims=False)

    dmod_all = _all_gather8(jnp.stack(dmods).reshape(n_layers * 6 * D_MODEL // FLAT_COLS, FLAT_COLS), "gather_dmod")
    grads["b_ada"] = _sum8(dmod_all).reshape(n_layers, 6 * D_MODEL)
    dmod_cols = lax.dynamic_slice_in_dim(dmod_all.reshape(8, n_layers, N_CHIPS, ada_cols), chip, 1, axis=2)[:, :, 0]
    grads["w_ada"], delta_ada, new_m_ada, new_v_ada = _ada_grad_adamw(cact, dmod_cols.transpose(1, 0, 2), w_ada, m_w_ada, v_w_ada)

    delta, new_m, new_v = {"w_ada": delta_ada}, {"w_ada": new_m_ada}, {"w_ada": new_v_ada}
    for k in SHARDED:
        delta[k], new_m[k], new_v[k] = _adamw(p[k], grads[k], p["m_" + k], p["v_" + k], "adamw_" + k)
    small = REPLICATED + ("b_ada", "ffn_conv_w")
    flats = [_flat_pad([src[k] for k in small], 128 * FLAT_COLS)
             for src in (p, grads, {k: p["m_" + k] for k in small}, {k: p["v_" + k] for k in small})]
    for dst, flat in zip((delta, new_m, new_v), _adamw(*flats, "adamw_small")):
        for k, t in zip(small, _split_flat(flat, [p[k] for k in small])):
            dst[k] = t

    return (loss, dh[None], *[grads[k] for k in WEIGHTS], *[delta[k] for k in WEIGHTS],
            *[new_m[k] for k in WEIGHTS], *[new_v[k] for k in WEIGHTS])
```

```python
import functools
import math

import numpy as np
import jax
import jax.numpy as jnp
from jax import lax
from jax.experimental import pallas as pl
from jax.experimental.pallas import tpu as pltpu
from jax.experimental.pallas import tpu_sc as plsc

F32 = jnp.float32
BF16 = jnp.bfloat16
MESH = pl.DeviceIdType.MESH

D_MODEL = 2048
S5_WIDTH, S5_GROUP, S5_GROUPS, S5_STATE = 512, 16, 32, 64
MLA_HEADS, MLA_NOPE, MLA_ROPE, MLA_V = 8, 128, 64, 128
MLA_Q_RANK, MLA_KV_RANK = 512, 256
MLA_WIDTH = MLA_HEADS * MLA_V
ROPE_THETA = 10000.0
HG_HEADS, HG_DK, HG_DV = 4, 128, 128
HG_WIDTH = HG_HEADS * HG_DV
EPS = 1e-6
ADAM_LR, ADAM_B1, ADAM_B2, ADAM_EPS, ADAM_WD, ADAM_STEP = 0.001, 0.9, 0.999, 1e-08, 0.01, 10
GELU_K0 = math.sqrt(2.0 / math.pi)
GELU_K1 = 0.044715

LANE = 128
SUBLANE = 8
VMEM_LIMIT = 56 * 1024 * 1024

P_S5, P_CQ, P_HQ, P_HF, P_HI, P_HG, P_CKV, P_KR = 0, 512, 1024, 1536, 2048, 2560, 3072, 3328
PROJ_W = 3584
N_STATE = S5_GROUPS * S5_STATE
SCAN_W = 512
HG_CHUNK = 64
N_CHIPS = 4
FLAT_COLS = 1024
ADD_ROWS = 256


def _sigmoid(x):
    return 1.0 / (1.0 + jnp.exp(-x))


def _silu(x):
    return x * _sigmoid(x)


def _dsilu(x):
    s = _sigmoid(x)
    return s * (1.0 + x * (1.0 - s))


def _gelu(x):
    return 0.5 * x * (1.0 + jnp.tanh(GELU_K0 * (x + GELU_K1 * x * x * x)))


def _dgelu(x):
    t = jnp.tanh(GELU_K0 * (x + GELU_K1 * x * x * x))
    return 0.5 * (1.0 + t) + 0.5 * x * (1.0 - t * t) * GELU_K0 * (1.0 + 3.0 * GELU_K1 * x * x)


def _colsum(v):
    return jnp.sum(v, axis=0, keepdims=True)


def _rowmean(v):
    return jnp.mean(v, axis=-1, keepdims=True)


def _dot(a, b, dims):
    return lax.dot_general(a.astype(BF16), b.astype(BF16), (dims, ((), ())), preferred_element_type=F32)


NN = ((1,), (0,))
NT = ((1,), (1,))
TN = ((0,), (0,))


def _pcall(body, **kw):
    return pl.pallas_call(body, **kw)


def _params(sem):
    return pltpu.CompilerParams(dimension_semantics=sem, vmem_limit_bytes=VMEM_LIMIT)


def _tile(dim, pref):
    if dim <= pref:
        return dim
    t = (pref // LANE) * LANE
    while t > LANE and dim % t:
        t -= LANE
    assert dim % t == 0, (dim, pref)
    return t


def _mm(a, b, *, mode, name, m, n, k, a_off=0, b_off=0, tm=1024, tn=1024, tk=1024,
        out=((F32),), epi=None, extras=(), cm=None):
    if cm is not None:
        if mode == "nt":
            tk = _tile(cm, tk)
        else:
            tn = _tile(cm, tn)
    tm, tn, tk = _tile(m, tm), _tile(n, tn), _tile(k, tk)
    nk = k // tk
    dims = {"nn": NN, "nt": NT, "tn": TN}[mode]
    if mode == "tn":
        assert a_off % tm == 0 and b_off % tn == 0
        a_spec = pl.BlockSpec((tk, tm), lambda i, j, kk: (kk, i + a_off // tm))
        b_spec = pl.BlockSpec((tk, tn), lambda i, j, kk: (kk, j + b_off // tn))
    else:
        assert a_off % tk == 0 and b_off == 0
        a_spec = pl.BlockSpec((tm, tk), lambda i, j, kk: (i, kk + a_off // tk))
        if mode == "nn" and cm is not None:
            b_spec = pl.BlockSpec((None, tk, tn), lambda i, j, kk, per=cm // tn: (j // per, kk, j % per))
        elif mode == "nn":
            b_spec = pl.BlockSpec((tk, tn), lambda i, j, kk: (kk, j))
        elif cm is not None:
            b_spec = pl.BlockSpec((None, tn, tk), lambda i, j, kk, per=cm // tk: (kk // per, j, kk % per))
        else:
            b_spec = pl.BlockSpec((tn, tk), lambda i, j, kk: (j, kk))
    in_specs, ex_arrays = [a_spec, b_spec], []
    for e in extras:
        if e[0] == "tile":
            off = e[2] // tn
            assert e[2] % tn == 0
            in_specs.append(pl.BlockSpec((tm, tn), lambda i, j, kk, off=off: (i, j + off)))
        else:
            in_specs.append(pl.BlockSpec((e[1].shape[0], tn), lambda i, j, kk: (0, j)))
        ex_arrays.append(e[1])
    n_ex = len(ex_arrays)
    n_out = len(out)

    def body(*refs):
        a_ref, b_ref = refs[0], refs[1]
        ex_refs = refs[2:2 + n_ex]
        o_refs = refs[2 + n_ex:2 + n_ex + n_out]
        acc_ref = refs[-1]
        kk = pl.program_id(2)

        @pl.when(kk == 0)
        def _():
            acc_ref[...] = jnp.zeros_like(acc_ref)

        acc_ref[...] += _dot(a_ref[...], b_ref[...], dims)

        @pl.when(kk == nk - 1)
        def _():
            acc = acc_ref[...]
            if epi is None:
                o_refs[0][...] = acc.astype(o_refs[0].dtype)
            else:
                vals = epi(acc, *[r[...] for r in ex_refs])
                for r, v in zip(o_refs, vals):
                    r[...] = v.astype(r.dtype)

    if mode == "tn" and cm is not None:
        out_shape = tuple(jax.ShapeDtypeStruct((N_CHIPS, m, cm), d) for d in out)
        out_specs = tuple(pl.BlockSpec((None, tm, tn), lambda i, j, kk, per=cm // tn: (j // per, i, j % per)) for _ in out)
    else:
        out_shape = tuple(jax.ShapeDtypeStruct((m, n), d) for d in out)
        out_specs = tuple(pl.BlockSpec((tm, tn), lambda i, j, kk: (i, j)) for _ in out)
    res = _pcall(
        body, name=name,
        out_shape=out_shape,
        grid=(m // tm, n // tn, nk),
        in_specs=in_specs,
        out_specs=out_specs,
        scratch_shapes=[pltpu.VMEM((tm, tn), F32)],
        compiler_params=_params(("parallel", "parallel", "arbitrary")),
    )(a, b, *ex_arrays)
    return res[0] if n_out == 1 else res


def _rows(fn, *, name, s, tm, ins, outs, ncb=1):
    tm = min(tm, s)
    assert s % tm == 0 and tm % SUBLANE == 0
    ni = s // tm
    r8 = tm // SUBLANE
    in_specs, arrays = [], []
    for e in ins:
        kind, arr = e[0], e[1]
        if kind in ("row", "prev8", "next8", "vecb"):
            off, w = e[2] // e[3], e[3]
            assert e[2] % e[3] == 0
        if kind == "row":
            in_specs.append(pl.BlockSpec((tm, w), lambda j, i, off=off: (i, off + j)))
        elif kind == "prev8":
            in_specs.append(pl.BlockSpec((SUBLANE, w), lambda j, i, off=off: (jnp.maximum(i * r8 - 1, 0), off + j)))
        elif kind == "next8":
            last = s // SUBLANE - 1
            in_specs.append(pl.BlockSpec((SUBLANE, w), lambda j, i, off=off: (jnp.minimum((i + 1) * r8, last), off + j)))
        elif kind == "vec":
            in_specs.append(pl.BlockSpec(arr.shape, lambda j, i, nd=arr.ndim: (0,) * nd))
        else:
            in_specs.append(pl.BlockSpec((arr.shape[0], w), lambda j, i, off=off: (0, off + j)))
        arrays.append(arr)
    out_shape, out_specs = [], []
    for e in outs:
        if e[0] == "row":
            out_shape.append(jax.ShapeDtypeStruct((s, ncb * e[1]), e[2]))
            out_specs.append(pl.BlockSpec((tm, e[1]), lambda j, i: (i, j)))
        else:
            out_shape.append(jax.ShapeDtypeStruct((e[1], ncb * e[2]), F32))
            out_specs.append(pl.BlockSpec((e[1], e[2]), lambda j, i: (0, j)))

    def body(*refs):
        fn(pl.program_id(1), *refs)

    res = _pcall(
        body, name=name, out_shape=tuple(out_shape), grid=(ncb, ni),
        in_specs=in_specs, out_specs=tuple(out_specs),
        compiler_params=_params(("parallel", "arbitrary")),
    )(*arrays)
    return res[0] if len(outs) == 1 else res


def _acc(ref, i, val):
    @pl.when(i == 0)
    def _():
        ref[...] = val

    @pl.when(i > 0)
    def _():
        ref[...] += val


def _normmod_fwd(x, gain, sc, sh, name):
    s, d = x.shape

    def fn(i, x_ref, g_ref, sc_ref, sh_ref, h_ref):
        xv = x_ref[...]
        r = lax.rsqrt(_rowmean(xv * xv) + EPS)
        h_ref[...] = (((xv * r) * g_ref[...]) * (1.0 + sc_ref[...]) + sh_ref[...]).astype(h_ref.dtype)

    return _rows(fn, name=name, s=s, tm=256, ins=[("row", x, 0, d), ("vec", gain), ("vec", sc), ("vec", sh)],
                 outs=[("row", d, BF16)])


def _normmod_bwd(dh, x, gain, sc, dx_add, name):
    s, d = x.shape

    def fn(i, dh_ref, x_ref, g_ref, sc_ref, add_ref, dx_ref, dg_ref, dsc_ref, dsh_ref):
        xv, dhv = x_ref[...], dh_ref[...]
        r = lax.rsqrt(_rowmean(xv * xv) + EPS)
        xn = xv * r
        gain_v, one_sc = g_ref[...], 1.0 + sc_ref[...]
        ghat = dhv * gain_v * one_sc
        dx_ref[...] = r * (ghat - xn * _rowmean(ghat * xn)) + add_ref[...]
        _acc(dg_ref, i, _colsum(dhv * xn * one_sc))
        _acc(dsc_ref, i, _colsum(dhv * xn * gain_v))
        _acc(dsh_ref, i, _colsum(dhv))

    return _rows(fn, name=name, s=s, tm=256,
                 ins=[("row", dh, 0, d), ("row", x, 0, d), ("vec", gain), ("vec", sc), ("row", dx_add, 0, d)],
                 outs=[("row", d, F32), ("acc", 1, d), ("acc", 1, d), ("acc", 1, d)])


def _postnorm_fwd(x, m, gain, gate, name):
    s, d = x.shape

    def fn(i, x_ref, m_ref, g_ref, gate_ref, o_ref):
        mv = m_ref[...]
        r = lax.rsqrt(_rowmean(mv * mv) + EPS)
        o_ref[...] = x_ref[...] + gate_ref[...] * ((mv * r) * g_ref[...])

    return _rows(fn, name=name, s=s, tm=256, ins=[("row", x, 0, d), ("row", m, 0, d), ("vec", gain), ("vec", gate)],
                 outs=[("row", d, F32)])


def _postnorm_bwd(dxo, m, gain, gate, name):
    s, d = m.shape

    def fn(i, dx_ref, m_ref, g_ref, gate_ref, dm_ref, dg_ref, dgate_ref):
        mv, dxv = m_ref[...], dx_ref[...]
        r = lax.rsqrt(_rowmean(mv * mv) + EPS)
        mn = mv * r
        gain_v, gate_v = g_ref[...], gate_ref[...]
        ghat = dxv * gate_v * gain_v
        dm_ref[...] = (r * (ghat - mn * _rowmean(ghat * mn))).astype(dm_ref.dtype)
        _acc(dg_ref, i, _colsum(dxv * gate_v * mn))
        _acc(dgate_ref, i, _colsum(dxv * mn * gain_v))

    return _rows(fn, name=name, s=s, tm=256, ins=[("row", dxo, 0, d), ("row", m, 0, d), ("vec", gain), ("vec", gate)],
                 outs=[("row", d, BF16), ("acc", 1, d), ("acc", 1, d)])


def _rms_fwd(src, off, w, gain, name):
    s = src.shape[0]

    def fn(i, x_ref, g_ref, o_ref):
        xv = x_ref[...]
        r = lax.rsqrt(_rowmean(xv * xv) + EPS)
        o_ref[...] = ((xv * r) * g_ref[...]).astype(o_ref.dtype)

    return _rows(fn, name=name, s=s, tm=512, ins=[("row", src, off, w), ("vec", gain)], outs=[("row", w, BF16)])


def _rms_bwd(dy, src, off, w, gain, name):
    s = src.shape[0]

    def fn(i, dy_ref, x_ref, g_ref, dx_ref, dg_ref):
        xv, dyv = x_ref[...], dy_ref[...]
        r = lax.rsqrt(_rowmean(xv * xv) + EPS)
        xn = xv * r
        ghat = dyv * g_ref[...]
        dx_ref[...] = r * (ghat - xn * _rowmean(ghat * xn))
        _acc(dg_ref, i, _colsum(dyv * xn))

    return _rows(fn, name=name, s=s, tm=512, ins=[("row", dy, 0, w), ("row", src, off, w), ("vec", gain)],
                 outs=[("row", w, F32), ("acc", 1, w)])


def _loss_grad(x, target):
    s, d = x.shape

    def fn(i, x_ref, t_ref, dx_ref, l_ref):
        diff = x_ref[...] - t_ref[...]
        dx_ref[...] = diff * (1.0 / d)
        part = _colsum(jnp.sum(diff * diff, axis=1, keepdims=True)) * (0.5 / d)
        _acc(l_ref, i, jnp.broadcast_to(part, (1, LANE)))

    return _rows(fn, name="loss_grad", s=s, tm=256, ins=[("row", x, 0, d), ("row", target, 0, d)],
                 outs=[("row", d, F32), ("acc", 1, LANE)])


FFN_WC = 512


def _shift_rows(xv, h_ref, i, row, k):
    out = pltpu.roll(xv, k, 0)
    for r in range(k):
        hrow = jnp.where(i > 0, h_ref[SUBLANE - k + r:SUBLANE - k + r + 1, :], 0.0)
        out = jnp.where(row == r, hrow, out)
    return out


def _conv_rows(x_ref, h_ref, w_ref, b_ref, i, row):
    xv = x_ref[...]
    s1, s2 = _shift_rows(xv, h_ref, i, row, 1), _shift_rows(xv, h_ref, i, row, 2)
    u = ((b_ref[...] + s2 * w_ref[0:1, :]) + s1 * w_ref[1:2, :]) + xv * w_ref[2:3, :]
    return u, s1, s2, xv


def _ffn_act_fwd(up, conv_w, conv_b, ffp):
    s = up.shape[0]
    wc, ncb = FFN_WC, ffp // FFN_WC

    def fn(i, g_ref, gh_ref, v_ref, vh_ref, wg_ref, wv_ref, bg_ref, bv_ref, a_ref):
        row = lax.broadcasted_iota(jnp.int32, g_ref.shape, 0)
        ug = _conv_rows(g_ref, gh_ref, wg_ref, bg_ref, i, row)[0]
        uv = _conv_rows(v_ref, vh_ref, wv_ref, bv_ref, i, row)[0]
        a_ref[...] = (_gelu(ug) * uv).astype(a_ref.dtype)

    return _rows(fn, name="ffn_act_fwd", s=s, tm=512, ncb=ncb,
                 ins=[("row", up, 0, wc), ("prev8", up, 0, wc), ("row", up, ffp, wc), ("prev8", up, ffp, wc),
                      ("vecb", conv_w, 0, wc), ("vecb", conv_w, ffp, wc), ("vecb", conv_b, 0, wc), ("vecb", conv_b, ffp, wc)],
                 outs=[("row", wc, BF16)])


def _ffn_act_bwd(da, up, conv_w, conv_b, ffp):
    s = up.shape[0]
    wc, ncb = FFN_WC, ffp // FFN_WC

    def fn(i, da_ref, g_ref, gh_ref, v_ref, vh_ref, wg_ref, wv_ref, bg_ref, bv_ref,
           dug_ref, duv_ref, dwg_ref, dwv_ref, dbg_ref, dbv_ref):
        row = lax.broadcasted_iota(jnp.int32, g_ref.shape, 0)
        ug, g1, g2, g0 = _conv_rows(g_ref, gh_ref, wg_ref, bg_ref, i, row)
        uv, v1, v2, v0 = _conv_rows(v_ref, vh_ref, wv_ref, bv_ref, i, row)
        dav = da_ref[...]
        dug = dav * uv * _dgelu(ug)
        duv = dav * _gelu(ug)
        dug_ref[...] = dug
        duv_ref[...] = duv
        for r, (gt, vt) in enumerate(((g2, v2), (g1, v1), (g0, v0))):
            _acc(dwg_ref.at[r:r + 1, :], i, _colsum(dug * gt))
            _acc(dwv_ref.at[r:r + 1, :], i, _colsum(duv * vt))
        _acc(dbg_ref, i, _colsum(dug))
        _acc(dbv_ref, i, _colsum(duv))

    return _rows(fn, name="ffn_act_bwd", s=s, tm=512, ncb=ncb,
                 ins=[("row", da, 0, wc), ("row", up, 0, wc), ("prev8", up, 0, wc), ("row", up, ffp, wc), ("prev8", up, ffp, wc),
                      ("vecb", conv_w, 0, wc), ("vecb", conv_w, ffp, wc), ("vecb", conv_b, 0, wc), ("vecb", conv_b, ffp, wc)],
                 outs=[("row", wc, F32), ("row", wc, F32), ("acc", 3, wc), ("acc", 3, wc), ("acc", 1, wc), ("acc", 1, wc)])


def _conv_bwd_input(du, conv_w, w_off, name):
    s, ffp = du.shape
    wc, ncb = FFN_WC, ffp // FFN_WC
    ni = s // min(512, s)

    def fn(i, du_ref, nx_ref, w_ref, o_ref):
        dv = du_ref[...]
        tm = dv.shape[0]
        row = lax.broadcasted_iota(jnp.int32, dv.shape, 0)
        n0 = jnp.where(i < ni - 1, nx_ref[0:1, :], 0.0)
        n1 = jnp.where(i < ni - 1, nx_ref[1:2, :], 0.0)
        u1 = jnp.where(row == tm - 1, n0, pltpu.roll(dv, tm - 1, 0))
        u2 = jnp.where(row == tm - 1, n1, jnp.where(row == tm - 2, n0, pltpu.roll(dv, tm - 2, 0)))
        o_ref[...] = (dv * w_ref[2:3, :] + u1 * w_ref[1:2, :] + u2 * w_ref[0:1, :]).astype(o_ref.dtype)

    return _rows(fn, name=name, s=s, tm=512, ncb=ncb,
                 ins=[("row", du, 0, wc), ("next8", du, 0, wc), ("vecb", conv_w, w_off, wc)],
                 outs=[("row", wc, BF16)])


def _cmul(ar, ai, br, bi):
    return ar * br - ai * bi, ar * bi + ai * br


def _s5_scan(x, a, *, reverse, h=None, name):
    s = x.shape[0]
    w = SCAN_W
    t_rows = min(256, s)
    nt = s // t_rows
    ncol = N_STATE // w
    nbits = t_rows.bit_length()
    r8 = t_rows // SUBLANE

    def tblk(t):
        return nt - 1 - t if reverse else t

    def body(*refs):
        if reverse:
            x_ref, a_ref, h_ref, hh_ref, o_ref, da_ref, carry, ptab = refs
        else:
            x_ref, a_ref, o_ref, carry, ptab = refs
        t = pl.program_id(1)
        row = lax.broadcasted_iota(jnp.int32, (t_rows, w), 0)
        idx = (t_rows - 1 - row) if reverse else row
        ar = a_ref[:, :w]
        ai = -a_ref[:, w:] if reverse else a_ref[:, w:]
        pows = [(ar, ai)]
        for _ in range(nbits - 1):
            pows.append(_cmul(*pows[-1], *pows[-1]))

        @pl.when(t == 0)
        def _():
            carry[...] = jnp.zeros_like(carry)
            pr, pi = jnp.ones((t_rows, w), F32), jnp.zeros((t_rows, w), F32)
            for kbit in range(nbits):
                bit = ((idx + 1) >> kbit) & 1
                fr = jnp.where(bit == 1, pows[kbit][0], 1.0)
                fi = jnp.where(bit == 1, pows[kbit][1], 0.0)
                pr, pi = _cmul(pr, pi, fr, fi)
            ptab[:, :w] = pr
            ptab[:, w:] = pi

        xr, xi = x_ref[:, :w], x_ref[:, w:]
        step = 1
        kbit = 0
        while step < t_rows:
            shift = (t_rows - step) if reverse else step
            yr, yi = pltpu.roll(xr, shift, 0), pltpu.roll(xi, shift, 0)
            zr, zi = _cmul(pows[kbit][0], pows[kbit][1], yr, yi)
            keep = idx >= step
            xr = xr + jnp.where(keep, zr, 0.0)
            xi = xi + jnp.where(keep, zi, 0.0)
            step *= 2
            kbit += 1
        cr, ci = carry[0:1, :w], carry[0:1, w:]
        zr, zi = _cmul(ptab[:, :w], ptab[:, w:], cr, ci)
        xr, xi = xr + zr, xi + zi
        o_ref[:, :w] = xr
        o_ref[:, w:] = xi
        last = 0 if reverse else t_rows - 1
        carry[0:1, :] = o_ref[last:last + 1, :]
        if reverse:
            halo_r = jnp.where(t < nt - 1, hh_ref[SUBLANE - 1:SUBLANE, :w], 0.0)
            halo_i = jnp.where(t < nt - 1, hh_ref[SUBLANE - 1:SUBLANE, w:], 0.0)
            hr = jnp.where(row == 0, halo_r, pltpu.roll(h_ref[:, :w], 1, 0))
            hi = jnp.where(row == 0, halo_i, pltpu.roll(h_ref[:, w:], 1, 0))
            _acc(da_ref.at[:, :w], t, _colsum(xr * hr + xi * hi))
            _acc(da_ref.at[:, w:], t, _colsum(xi * hr - xr * hi))

    blk = pl.BlockSpec((t_rows, 2 * w), lambda j, t: (tblk(t), j))
    a_spec = pl.BlockSpec((1, 2 * w), lambda j, t: (0, j))
    in_specs, arrays = [blk, a_spec], [x, a]
    out_shape = [jax.ShapeDtypeStruct((s, 2 * N_STATE), F32)]
    out_specs = [blk]
    if reverse:
        in_specs += [blk, pl.BlockSpec((SUBLANE, 2 * w), lambda j, t: (jnp.maximum(tblk(t) * r8 - 1, 0), j))]
        arrays += [h, h]
        out_shape.append(jax.ShapeDtypeStruct((1, 2 * N_STATE), F32))
        out_specs.append(a_spec)
    res = _pcall(
        body, name=name, out_shape=tuple(out_shape), grid=(ncol, nt), in_specs=in_specs, out_specs=tuple(out_specs),
        scratch_shapes=[pltpu.VMEM((SUBLANE, 2 * w), F32), pltpu.VMEM((t_rows, 2 * w), F32)],
        compiler_params=_params(("parallel", "arbitrary")),
    )(*arrays)
    return res if reverse else res[0]


def _s5_glu_bwd_a(dout, dout_off, y, z):
    s = y.shape[0]
    w = S5_WIDTH

    def fn(i, do_ref, y_ref, z_ref, dz_ref, p_ref):
        dov = do_ref[...]
        sg = _sigmoid(z_ref[...])
        dz_ref[...] = (dov * _gelu(y_ref[...]) * sg * (1.0 - sg)).astype(dz_ref.dtype)
        p_ref[...] = dov * sg

    return _rows(fn, name="s5_glu_bwd", s=s, tm=512, ins=[("row", dout, dout_off, w), ("row", y, 0, w), ("row", z, 0, w)],
                 outs=[("row", w, BF16), ("row", w, F32)])


def _s5_dd(dy, proj):
    s = dy.shape[0]
    w = S5_WIDTH

    def fn(i, dy_ref, u_ref, dd_ref):
        _acc(dd_ref, i, _colsum(dy_ref[...] * u_ref[...]))

    return _rows(fn, name="s5_dd", s=s, tm=512, ins=[("row", dy, 0, w), ("row", proj, P_S5, w)], outs=[("acc", 1, w)])


def _s5_fwd(proj, wl, s):
    bu = _mm(proj, wl["s5_bd"], mode="nn", name="s5_bu", m=s, n=2 * N_STATE, k=S5_WIDTH, a_off=P_S5)
    h = _s5_scan(bu, wl["s5_a"], reverse=False, name="s5_scan_fwd")
    def y_epi(acc, u, d):
        yv = acc + d * u
        return yv, _gelu(yv)

    y, yg = _mm(h, wl["s5_cd"], mode="nn", name="s5_y", m=s, n=S5_WIDTH, k=2 * N_STATE, out=(F32, BF16),
                extras=[("tile", proj, P_S5), ("row", wl["s5_d"])], epi=y_epi)
    z, out = _mm(yg, wl["s5_w_glu"], mode="nn", name="s5_glu", m=s, n=S5_WIDTH, k=S5_WIDTH, out=(F32, BF16),
                 extras=[("tile", y, 0)], epi=lambda acc, yv: (acc, _gelu(yv) * _sigmoid(acc)))
    return out, (h, y, z, yg)


def _s5_bwd(dcat, proj, wl, saved, s):
    h, y, z, yg = saved
    dz, p1 = _s5_glu_bwd_a(dcat, 0, y, z)
    dy = _mm(dz, wl["s5_w_glu"], mode="nt", name="s5_dyg", m=s, n=S5_WIDTH, k=S5_WIDTH,
             extras=[("tile", p1, 0), ("tile", y, 0)], epi=lambda acc, p, yv: ((p + acc) * _dgelu(yv),))
    gh = _mm(dy, wl["s5_cd"], mode="nt", name="s5_gh", m=s, n=2 * N_STATE, k=S5_WIDTH)
    adj, da = _s5_scan(gh, wl["s5_a"], reverse=True, h=h, name="s5_scan_bwd")
    du = _mm(adj, wl["s5_bd"], mode="nt", name="s5_du", m=s, n=S5_WIDTH, k=2 * N_STATE,
             extras=[("tile", dy, 0), ("row", wl["s5_d"])], epi=lambda acc, dyv, d: (acc + dyv * d,))
    grads = {
        "s5_a": da,
        "s5_bd": _mm(proj, adj, mode="tn", name="s5_dbd", m=S5_WIDTH, n=2 * N_STATE, k=s, a_off=P_S5),
        "s5_cd": _mm(h, dy, mode="tn", name="s5_dcd", m=2 * N_STATE, n=S5_WIDTH, k=s),
        "s5_d": _s5_dd(dy, proj),
        "s5_w_glu": _mm(yg, dz, mode="tn", name="s5_dwglu", m=S5_WIDTH, n=S5_WIDTH, k=s),
    }
    return du, grads


def _mla_prep(qraw, kvraw, proj, cs):
    s = qraw.shape[0]
    hw = MLA_HEADS * LANE

    def fn(i, q_ref, kv_ref, kr_ref, cs_ref, qn_ref, qr_ref, kvb_ref, krb_ref):
        csv = cs_ref[...]
        qn_ref[...] = q_ref[:, :hw].astype(BF16)
        for hd in range(MLA_HEADS):
            p = q_ref[:, hw + hd * LANE:hw + (hd + 1) * LANE] * csv
            qr_ref[:, hd * LANE:(hd + 1) * LANE] = (p + pltpu.roll(p, LANE // 2, 1)).astype(BF16)
        kvb_ref[...] = kv_ref[...].astype(BF16)
        p = kr_ref[...] * csv
        lane = lax.broadcasted_iota(jnp.int32, p.shape, 1)
        krb_ref[...] = jnp.where(lane < LANE // 2, p + pltpu.roll(p, LANE // 2, 1), 0.0).astype(BF16)

    return _rows(fn, name="mla_prep", s=s, tm=256,
                 ins=[("row", qraw, 0, 2 * hw), ("row", kvraw, 0, 2 * hw), ("row", proj, P_KR, LANE), ("row", cs, 0, LANE)],
                 outs=[("row", hw, BF16), ("row", hw, BF16), ("row", 2 * hw, BF16), ("row", LANE, BF16)])


def _mla_rope_bwd(dqn, dqr2, dkr2h, cs):
    s = dqn.shape[0]
    hw = MLA_HEADS * LANE

    def fn(i, dqn_ref, dqr_ref, dkr_ref, cs_ref, dq_ref, dk_ref):
        csv = cs_ref[...]
        dq_ref[:, :hw] = dqn_ref[...].astype(BF16)
        ksum = jnp.zeros(csv.shape, F32)
        for hd in range(MLA_HEADS):
            g = dqr_ref[:, hd * LANE:(hd + 1) * LANE]
            dq_ref[:, hw + hd * LANE:hw + (hd + 1) * LANE] = ((g + pltpu.roll(g, LANE // 2, 1)) * csv).astype(BF16)
            ksum = ksum + dkr_ref[:, hd * LANE:(hd + 1) * LANE]
        dk_ref[...] = ksum * csv

    return _rows(fn, name="mla_rope_bwd", s=s, tm=256,
                 ins=[("row", dqn, 0, hw), ("row", dqr2, 0, hw), ("row", dkr2h, 0, hw), ("row", cs, 0, LANE)],
                 outs=[("row", 2 * hw, BF16), ("row", LANE, F32)])


def _lanes(a_ref, b_ref):
    return jnp.concatenate([a_ref[...], b_ref[...]], axis=1)


def _attn_scores(qn_ref, qr_ref, kn_ref, kr_ref, qi, ki, tq, tk):
    scale = (MLA_NOPE + MLA_ROPE) ** -0.5
    sc = _dot(_lanes(qn_ref, qr_ref), _lanes(kn_ref, kr_ref), NT) * scale
    assert tq == tk
    return sc, lax.broadcasted_iota(jnp.int32, (tq, tk), 1) <= lax.broadcasted_iota(jnp.int32, (tq, tk), 0)


def _attn_specs(tq, tk, q_of, k_of):
    qs = pl.BlockSpec((tq, LANE), lambda h, a, b: (q_of(a, b), h))
    return [qs, qs,
            pl.BlockSpec((tk, LANE), lambda h, a, b: (k_of(a, b), 2 * h)),
            pl.BlockSpec((tk, LANE), lambda h, a, b: (k_of(a, b), 2 * h + 1)),
            pl.BlockSpec((tk, LANE), lambda h, a, b: (k_of(a, b), 0))]


def _flash_fwd(qn, qr2, kv, kr2):
    s = qn.shape[0]
    tq = tk = min(512, s)
    nq = s // tq

    def body(qn_ref, qr_ref, kn_ref, v_ref, kr_ref, o_ref, lse_ref, m_sc, l_sc, acc_sc):
        qi, ki = pl.program_id(1), pl.program_id(2)

        @pl.when(ki == 0)
        def _():
            m_sc[...] = jnp.full(m_sc.shape, -jnp.inf, F32)
            l_sc[...] = jnp.zeros_like(l_sc)
            acc_sc[...] = jnp.zeros_like(acc_sc)

        def step(diagonal):
            sc, causal = _attn_scores(qn_ref, qr_ref, kn_ref, kr_ref, qi, ki, tq, tk)
            if diagonal:
                sc = jnp.where(causal, sc, -1e30)
            m_new = jnp.maximum(m_sc[...], jnp.max(sc, axis=1, keepdims=True))
            alpha = jnp.exp(m_sc[...] - m_new)
            p = jnp.exp(sc - m_new)
            l_sc[...] = alpha * l_sc[...] + jnp.sum(p, axis=1, keepdims=True)
            acc_sc[...] = alpha * acc_sc[...] + _dot(p, v_ref[...], NN)
            m_sc[...] = m_new

        @pl.when(ki < qi)
        def _():
            step(False)

        @pl.when(ki == qi)
        def _():
            step(True)
            o_ref[...] = acc_sc[...] / l_sc[...]
            lse_ref[0] = m_sc[...] + jnp.log(l_sc[...])

    return _pcall(
        body, name="mla_flash_fwd",
        out_shape=(jax.ShapeDtypeStruct((s, MLA_WIDTH), F32), jax.ShapeDtypeStruct((MLA_HEADS, s, 1), F32)),
        grid=(MLA_HEADS, nq, nq),
        in_specs=_attn_specs(tq, tk, lambda a, b: a, lambda a, b: jnp.minimum(a, b)),
        out_specs=(pl.BlockSpec((tq, LANE), lambda h, a, b: (a, h)), pl.BlockSpec((1, tq, 1), lambda h, a, b: (h, a, 0))),
        scratch_shapes=[pltpu.VMEM((tq, 1), F32), pltpu.VMEM((tq, 1), F32), pltpu.VMEM((tq, LANE), F32)],
        compiler_params=_params(("parallel", "parallel", "arbitrary")),
    )(qn, qr2, kv, kv, kr2)


def _flash_bwd_dq(qn, qr2, kv, kr2, do, do_off, o, lse):
    s = qn.shape[0]
    tq = tk = min(512, s)
    nq = s // tq
    scale = (MLA_NOPE + MLA_ROPE) ** -0.5
    ob = do_off // LANE

    def body(qn_ref, qr_ref, kn_ref, v_ref, kr_ref, do_ref, o_ref, lse_ref, dqn_ref, dqr_ref, dl_ref, dl_sc, aq_sc):
        qi, ki = pl.program_id(1), pl.program_id(2)

        @pl.when(ki == 0)
        def _():
            dl_sc[...] = jnp.sum(do_ref[...] * o_ref[...], axis=1, keepdims=True)
            aq_sc[...] = jnp.zeros_like(aq_sc)

        def step(diagonal):
            sc, causal = _attn_scores(qn_ref, qr_ref, kn_ref, kr_ref, qi, ki, tq, tk)
            p = jnp.exp(sc - lse_ref[0])
            if diagonal:
                p = jnp.where(causal, p, 0.0)
            dp = _dot(do_ref[...], v_ref[...], NT)
            ds = (p * (dp - dl_sc[...]) * scale).astype(BF16)
            aq_sc[...] += _dot(ds, _lanes(kn_ref, kr_ref), NN)

        @pl.when(ki < qi)
        def _():
            step(False)

        @pl.when(ki == qi)
        def _():
            step(True)
            dqn_ref[...] = aq_sc[:, :LANE]
            dqr_ref[...] = aq_sc[:, LANE:]
            dl_ref[0] = dl_sc[...]

    qblk = pl.BlockSpec((tq, LANE), lambda h, a, b: (a, h))
    vec = pl.BlockSpec((1, tq, 1), lambda h, a, b: (h, a, 0))
    return _pcall(
        body, name="mla_flash_dq",
        out_shape=(jax.ShapeDtypeStruct((s, MLA_WIDTH), F32), jax.ShapeDtypeStruct((s, MLA_WIDTH), F32),
                   jax.ShapeDtypeStruct((MLA_HEADS, s, 1), F32)),
        grid=(MLA_HEADS, nq, nq),
        in_specs=_attn_specs(tq, tk, lambda a, b: a, lambda a, b: jnp.minimum(a, b))
        + [pl.BlockSpec((tq, LANE), lambda h, a, b: (a, h + ob)), qblk, vec],
        out_specs=(qblk, qblk, vec),
        scratch_shapes=[pltpu.VMEM((tq, 1), F32), pltpu.VMEM((tq, 2 * LANE), F32)],
        compiler_params=_params(("parallel", "parallel", "arbitrary")),
    )(qn, qr2, kv, kv, kr2, do, o, lse)


def _flash_bwd_dkv(qn, qr2, kv, kr2, do, do_off, lse, delta):
    s = qn.shape[0]
    tq = tk = min(512, s)
    nq = s // tq
    scale = (MLA_NOPE + MLA_ROPE) ** -0.5
    ob = do_off // LANE

    def body(qn_ref, qr_ref, kn_ref, v_ref, kr_ref, do_ref, lse_ref, dl_ref, dkv_ref, dkr_ref, ak_sc, av_sc):
        ki, qi = pl.program_id(1), pl.program_id(2)

        @pl.when(qi == 0)
        def _():
            ak_sc[...] = jnp.zeros_like(ak_sc)
            av_sc[...] = jnp.zeros_like(av_sc)

        def step(diagonal):
            sc, causal = _attn_scores(qn_ref, qr_ref, kn_ref, kr_ref, qi, ki, tq, tk)
            p = jnp.exp(sc - lse_ref[0])
            if diagonal:
                p = jnp.where(causal, p, 0.0)
            dp = _dot(do_ref[...], v_ref[...], NT)
            ds = (p * (dp - dl_ref[0]) * scale).astype(BF16)
            av_sc[...] += _dot(p, do_ref[...], TN)
            ak_sc[...] += _dot(ds, _lanes(qn_ref, qr_ref), TN)

        @pl.when(qi > ki)
        def _():
            step(False)

        @pl.when(qi == ki)
        def _():
            step(True)

        @pl.when(qi == nq - 1)
        def _():
            dkv_ref[:, :LANE] = ak_sc[:, :LANE]
            dkv_ref[:, LANE:] = av_sc[...]
            dkr_ref[...] = ak_sc[:, LANE:]

    q_of = lambda a, b: jnp.maximum(a, b)
    k_of = lambda a, b: a
    vec = pl.BlockSpec((1, tq, 1), lambda h, a, b: (h, q_of(a, b), 0))
    return _pcall(
        body, name="mla_flash_dkv",
        out_shape=(jax.ShapeDtypeStruct((s, 2 * MLA_WIDTH), F32), jax.ShapeDtypeStruct((s, MLA_WIDTH), F32)),
        grid=(MLA_HEADS, nq, nq),
        in_specs=_attn_specs(tq, tk, q_of, k_of)
        + [pl.BlockSpec((tq, LANE), lambda h, a, b: (q_of(a, b), h + ob)), vec, vec],
        out_specs=(pl.BlockSpec((tk, 2 * LANE), lambda h, a, b: (a, h)), pl.BlockSpec((tk, LANE), lambda h, a, b: (a, h))),
        scratch_shapes=[pltpu.VMEM((tk, 2 * LANE), F32), pltpu.VMEM((tk, LANE), F32)],
        compiler_params=_params(("parallel", "parallel", "arbitrary")),
    )(qn, qr2, kv, kv, kr2, do, lse, delta)


def _mla_fwd(proj, wl, cs, s):
    cqn = _rms_fwd(proj, P_CQ, MLA_Q_RANK, wl["mla_q_norm"], "mla_q_rms")
    ckvn = _rms_fwd(proj, P_CKV, MLA_KV_RANK, wl["mla_kv_norm"], "mla_kv_rms")
    qraw = _mm(cqn, wl["mla_wq"], mode="nn", name="mla_q_proj", m=s, n=2 * MLA_WIDTH, k=MLA_Q_RANK)
    kvraw = _mm(ckvn, wl["mla_w_ukv"], mode="nn", name="mla_kv_proj", m=s, n=2 * MLA_WIDTH, k=MLA_KV_RANK)
    qn, qr2, kv, kr2 = _mla_prep(qraw, kvraw, proj, cs)
    o, lse = _flash_fwd(qn, qr2, kv, kr2)
    return o, (cqn, ckvn, qn, qr2, kv, kr2, o, lse)


def _mla_bwd(dcat, proj, wl, cs, saved, s):
    cqn, ckvn, qn, qr2, kv, kr2, o, lse = saved
    dqn, dqr2, delta = _flash_bwd_dq(qn, qr2, kv, kr2, dcat, S5_WIDTH, o, lse)
    dkv, dkr2h = _flash_bwd_dkv(qn, qr2, kv, kr2, dcat, S5_WIDTH, lse, delta)
    dqraw, dkr = _mla_rope_bwd(dqn, dqr2, dkr2h, cs)
    dcqn = _mm(dqraw, wl["mla_wq"], mode="nt", name="mla_dcqn", m=s, n=MLA_Q_RANK, k=2 * MLA_WIDTH)
    dckvn = _mm(dkv, wl["mla_w_ukv"], mode="nt", name="mla_dckvn", m=s, n=MLA_KV_RANK, k=2 * MLA_WIDTH)
    dcq, dqg = _rms_bwd(dcqn, proj, P_CQ, MLA_Q_RANK, wl["mla_q_norm"], "mla_q_rms_bwd")
    dckv, dkvg = _rms_bwd(dckvn, proj, P_CKV, MLA_KV_RANK, wl["mla_kv_norm"], "mla_kv_rms_bwd")
    grads = {
        "mla_wq": _mm(cqn, dqraw, mode="tn", name="mla_dwq", m=MLA_Q_RANK, n=2 * MLA_WIDTH, k=s),
        "mla_w_ukv": _mm(ckvn, dkv, mode="tn", name="mla_dwukv", m=MLA_KV_RANK, n=2 * MLA_WIDTH, k=s),
        "mla_q_norm": dqg,
        "mla_kv_norm": dkvg,
    }
    return dcq, dckv, dkr, grads


HG_ROWS = 256


def _cumsum_rows(x, row, reverse=False):
    n = x.shape[0]
    step = 1
    while step < n:
        if reverse:
            x = x + jnp.where(row < n - step, pltpu.roll(x, n - step, 0), 0.0)
        else:
            x = x + jnp.where(row >= step, pltpu.roll(x, step, 0), 0.0)
        step *= 2
    return x


def _hg_gates(hq, z, lb):
    sig = _sigmoid(z)
    sigm = _sigmoid(-z)
    f = lb + (1.0 - lb) * sig
    return _silu(hq), (1.0 - lb) * sigm, jnp.log(f), sig, sigm, f


HG_SUB = 16
HG_NSUB = HG_CHUNK // HG_SUB


def _hg_offdiag(q, k, b, row, b_buf):
    ops = []
    for j in range(HG_NSUB - 1):
        e = (j + 1) * HG_SUB
        be = b_buf[e - 1:e, :]
        fj = jnp.where(row >= e, jnp.exp(jnp.minimum(b - be, 0.0)), 0.0)
        gj = jnp.where((row >= e - HG_SUB) & (row < e), jnp.exp(jnp.minimum(be - b, 0.0)), 0.0)
        ops.append((q * fj, k * gj, fj, gj))
    return ops


def _hg_diag_weights(qb, bb, ks, bs, rr, srow):
    e = jnp.exp(jnp.minimum(bb - bs, 0.0))
    keep = rr >= srow
    w = jnp.where(keep, jnp.sum(qb * ks * e, axis=1, keepdims=True), 0.0)
    return e, keep, w


def _hg_specs(rows):
    def col(off):
        return pl.BlockSpec((rows, LANE), lambda h, n, off=off: (n, off // LANE + h))
    return col


def _hg_fwd(proj, lb, gamma):
    s = proj.shape[0]
    rows = min(HG_ROWS, s)
    c = HG_CHUNK
    npc = rows // c
    nb = s // rows

    def body(hq_ref, hf_ref, hi_ref, hg_ref, lb_ref, gm_ref, y_ref, o_ref, st_ref, st_sc, k_buf, b_buf):
        n = pl.program_id(1)

        @pl.when(n == 0)
        def _():
            st_sc[...] = jnp.zeros_like(st_sc)

        row = lax.broadcasted_iota(jnp.int32, (c, LANE), 0)
        rr = lax.broadcasted_iota(jnp.int32, (HG_SUB, 1), 0)
        lbv = lb_ref[...]
        for ch in range(npc):
            sl = slice(ch * c, (ch + 1) * c)
            q, k, logf, _, _, _ = _hg_gates(hq_ref[sl, :], hf_ref[sl, :], lbv)
            v = hi_ref[sl, :]
            b = _cumsum_rows(logf, row)
            bl = _colsum(logf)
            k_buf[...] = k
            b_buf[...] = b
            st = st_sc[...]
            st_ref[0, ch] = st
            a_off = sum(_dot(qf, kg, NT) for qf, kg, _, _ in _hg_offdiag(q, k, b, row, b_buf))
            o = _dot(q * jnp.exp(b), st, NT) + _dot(a_off, v, NN)
            st_sc[...] = st * jnp.exp(bl) + _dot(v, k * jnp.exp(bl - b), TN)
            diag = []
            for i in range(HG_NSUB):
                r0 = i * HG_SUB
                qb, bb = q[r0:r0 + HG_SUB], b[r0:r0 + HG_SUB]
                acc = jnp.zeros((HG_SUB, LANE), F32)
                for srow in range(HG_SUB):
                    t = r0 + srow
                    _, _, w = _hg_diag_weights(qb, bb, k_buf[t:t + 1, :], b_buf[t:t + 1, :], rr, srow)
                    acc = acc + w * hi_ref[ch * c + t:ch * c + t + 1, :]
                diag.append(acc)
            o = o + jnp.concatenate(diag, axis=0)
            o_ref[sl, :] = o
            r = lax.rsqrt(_rowmean(o * o) + EPS)
            y_ref[sl, :] = (((o * r) * gm_ref[...]) * _silu(hg_ref[sl, :])).astype(y_ref.dtype)

    col = _hg_specs(rows)
    out_blk = pl.BlockSpec((rows, LANE), lambda h, n: (n, h))
    return _pcall(
        body, name="hgrn_fwd",
        out_shape=(jax.ShapeDtypeStruct((s, HG_WIDTH), BF16), jax.ShapeDtypeStruct((s, HG_WIDTH), F32),
                   jax.ShapeDtypeStruct((HG_HEADS, s // c, HG_DV, HG_DK), F32)),
        grid=(HG_HEADS, nb),
        in_specs=[col(P_HQ), col(P_HF), col(P_HI), col(P_HG),
                  pl.BlockSpec((1, LANE), lambda h, n: (0, h)), pl.BlockSpec((1, LANE), lambda h, n: (0, 0))],
        out_specs=(out_blk, out_blk, pl.BlockSpec((1, npc, HG_DV, HG_DK), lambda h, n: (h, n, 0, 0))),
        scratch_shapes=[pltpu.VMEM((HG_DV, HG_DK), F32), pltpu.VMEM((c, LANE), F32), pltpu.VMEM((c, LANE), F32)],
        compiler_params=_params(("parallel", "arbitrary")),
    )(proj, proj, proj, proj, lb, gamma)


def _hg_bwd(dcat, dy_off, proj, lb, gamma, o_saved, states):
    s = proj.shape[0]
    rows = min(HG_ROWS, s)
    c = HG_CHUNK
    npc = rows // c
    nb = s // rows
    yb = dy_off // LANE

    def body(hq_ref, hf_ref, hi_ref, hg_ref, lb_ref, gm_ref, dy_ref, o_ref, st_ref,
             dq_ref, df_ref, di_ref, dg_ref, dlb_ref, dgm_ref, dst_sc, k_buf, b_buf, dk_buf, dv_buf):
        n = pl.program_id(1)

        @pl.when(n == 0)
        def _():
            dst_sc[...] = jnp.zeros_like(dst_sc)
            dlb_ref[...] = jnp.zeros_like(dlb_ref)
            dgm_ref[...] = jnp.zeros_like(dgm_ref)

        row = lax.broadcasted_iota(jnp.int32, (c, LANE), 0)
        rr = lax.broadcasted_iota(jnp.int32, (HG_SUB, 1), 0)
        lbv, gmv = lb_ref[...], gm_ref[...]
        for ch in reversed(range(npc)):
            sl = slice(ch * c, (ch + 1) * c)
            hq, z, v, g = hq_ref[sl, :], hf_ref[sl, :], hi_ref[sl, :], hg_ref[sl, :]
            q, k, logf, sig, sigm, f = _hg_gates(hq, z, lbv)
            b = _cumsum_rows(logf, row)
            bl = _colsum(logf)
            k_buf[...] = k
            b_buf[...] = b
            eb, ebl = jnp.exp(b), jnp.exp(bl)
            qe, kl = q * eb, k * jnp.exp(bl - b)
            st = st_ref[0, ch]
            dst = dst_sc[...]
            o, dyv = o_ref[sl, :], dy_ref[sl, :]
            r = lax.rsqrt(_rowmean(o * o) + EPS)
            on = o * r
            sg = _silu(g)
            dgm_ref[0] += _colsum(dyv * on * sg)
            dg_ref[sl, :] = dyv * on * gmv * _dsilu(g)
            go = dyv * gmv * sg
            do = r * (go - on * _rowmean(go * on))
            dq = _dot(do, st, NN) * eb
            dkl = _dot(v, dst, NN)
            dk = dkl * jnp.exp(bl - b)
            dv = _dot(kl, dst, NT)
            dbl = _colsum(dst * st) * ebl + _colsum(dkl * kl)
            dst_sc[...] = dst * ebl + _dot(do, qe, TN)
            ops = _hg_offdiag(q, k, b, row, b_buf)
            da = _dot(do, v, NT)
            a_off = sum(_dot(qf, kg, NT) for qf, kg, _, _ in ops)
            dv = dv + _dot(a_off, do, TN)
            for qf, kg, fj, gj in ops:
                dq = dq + _dot(da, kg, NN) * fj
                dk = dk + _dot(da, qf, TN) * gj
            dq_diag = []
            for i in range(HG_NSUB):
                r0 = i * HG_SUB
                qb, bb, dob = q[r0:r0 + HG_SUB], b[r0:r0 + HG_SUB], do[r0:r0 + HG_SUB]
                acc = jnp.zeros((HG_SUB, LANE), F32)
                for srow in range(HG_SUB):
                    t = r0 + srow
                    ks = k_buf[t:t + 1, :]
                    e, keep, w = _hg_diag_weights(qb, bb, ks, b_buf[t:t + 1, :], rr, srow)
                    dw = jnp.where(keep, jnp.sum(dob * hi_ref[ch * c + t:ch * c + t + 1, :], axis=1, keepdims=True), 0.0)
                    dv_buf[t:t + 1, :] = _colsum(w * dob)
                    dk_buf[t:t + 1, :] = _colsum(dw * qb * e)
                    acc = acc + dw * (ks * e)
                dq_diag.append(acc)
            dq = dq + jnp.concatenate(dq_diag, axis=0)
            dk = dk + dk_buf[...]
            di_ref[sl, :] = dv + dv_buf[...]
            db = q * dq - k * dk + jnp.where(row == c - 1, dbl, 0.0)
            dlogf = _cumsum_rows(db, row, reverse=True)
            dq_ref[sl, :] = dq * _dsilu(hq)
            s1 = sig * (1.0 - sig) * (1.0 - lbv)
            df_ref[sl, :] = dlogf * s1 / f - dk * s1
            dlb_ref[0] += _colsum(dlogf * sigm / f - dk * sigm)

    def col(off):
        return pl.BlockSpec((rows, LANE), lambda h, n, off=off: (nb - 1 - n, off // LANE + h))

    out_blk = pl.BlockSpec((rows, LANE), lambda h, n: (nb - 1 - n, h))
    acc_blk = pl.BlockSpec((1, 1, LANE), lambda h, n: (h, 0, 0))
    res = _pcall(
        body, name="hgrn_bwd",
        out_shape=tuple(jax.ShapeDtypeStruct((s, HG_WIDTH), F32) for _ in range(4))
        + (jax.ShapeDtypeStruct((HG_HEADS, 1, LANE), F32), jax.ShapeDtypeStruct((HG_HEADS, 1, LANE), F32)),
        grid=(HG_HEADS, nb),
        in_specs=[col(P_HQ), col(P_HF), col(P_HI), col(P_HG),
                  pl.BlockSpec((1, LANE), lambda h, n: (0, h)), pl.BlockSpec((1, LANE), lambda h, n: (0, 0)),
                  pl.BlockSpec((rows, LANE), lambda h, n: (nb - 1 - n, yb + h)), out_blk,
                  pl.BlockSpec((1, npc, HG_DV, HG_DK), lambda h, n: (h, nb - 1 - n, 0, 0))],
        out_specs=(out_blk, out_blk, out_blk, out_blk, acc_blk, acc_blk),
        scratch_shapes=[pltpu.VMEM((HG_DV, HG_DK), F32)] + [pltpu.VMEM((c, LANE), F32)] * 4,
        compiler_params=_params(("parallel", "arbitrary")),
    )(proj, proj, proj, proj, lb, gamma, dcat, o_saved, states)
    dq, df, di, dg, dlb, dgm = res
    return dq, df, di, dg, dlb.reshape(1, HG_WIDTH), dgm.reshape(HG_HEADS, LANE)


def _adamw_math(w, g, m, v):
    m = ADAM_B1 * m + (1.0 - ADAM_B1) * g
    v = ADAM_B2 * v + (1.0 - ADAM_B2) * (g * g)
    m_hat = m / (1.0 - ADAM_B1 ** ADAM_STEP)
    v_hat = v / (1.0 - ADAM_B2 ** ADAM_STEP)
    delta = -ADAM_LR * (m_hat / (jnp.sqrt(v_hat) + ADAM_EPS) + ADAM_WD * w)
    return delta, m, v


def _adamw(w, g, m, v, name):
    shape = w.shape
    width = shape[-1]
    rows = int(np.prod(shape[:-1]))
    tm = rows
    while tm * width * 4 > (1 << 20) and tm % 16 == 0:
        tm //= 2

    def fn(i, w_ref, g_ref, m_ref, v_ref, d_ref, mo_ref, vo_ref):
        d, mn, vn = _adamw_math(w_ref[...], g_ref[...], m_ref[...], v_ref[...])
        d_ref[...] = d
        mo_ref[...] = mn
        vo_ref[...] = vn

    v2 = lambda t: t.reshape(rows, width)
    res = _rows(fn, name=name, s=rows, tm=tm, ins=[("row", v2(t), 0, width) for t in (w, g, m, v)],
                outs=[("row", width, F32)] * 3)
    return tuple(r.reshape(shape) for r in res)


def _ada_grad_adamw(cact_all, dmod_cols, w, m, v):
    n_layers, kdim, n = w.shape
    tm, tn = _tile(kdim, 256), _tile(n, 1024)

    def body(c_ref, d_ref, w_ref, m_ref, v_ref, g_ref, dl_ref, mo_ref, vo_ref):
        g = _dot(c_ref[...], d_ref[...], TN)
        d, mn, vn = _adamw_math(w_ref[...], g, m_ref[...], v_ref[...])
        g_ref[...] = g
        dl_ref[...] = d
        mo_ref[...] = mn
        vo_ref[...] = vn

    blk = pl.BlockSpec((None, tm, tn), lambda l, i, j: (l, i, j))
    return _pcall(
        body, name="ada_grad_adamw", out_shape=tuple(jax.ShapeDtypeStruct(w.shape, F32) for _ in range(4)),
        grid=(n_layers, kdim // tm, n // tn),
        in_specs=[pl.BlockSpec((cact_all.shape[0], tm), lambda l, i, j: (0, i)),
                  pl.BlockSpec((None, dmod_cols.shape[1], tn), lambda l, i, j: (l, 0, j)), blk, blk, blk],
        out_specs=(blk, blk, blk, blk),
        compiler_params=_params(("parallel", "parallel", "parallel")),
    )(cact_all, dmod_cols, w, m, v)


def _me():
    return lax.axis_index("x"), lax.axis_index("y"), lax.axis_index("c")


def _flip(k):
    x, y, c = _me()
    return (x ^ ((k >> 2) & 1), y ^ ((k >> 1) & 1), c ^ (k & 1))


def _lin(dev):
    return 4 * dev[0] + 2 * dev[1] + dev[2]


ANY = pl.BlockSpec(memory_space=pl.ANY)


def _all_gather8(x, name):
    def body(x_ref, out_ref, send_sems, recv_sems, local_sem):
        me = _lin(_me())
        mine = pltpu.make_async_copy(x_ref, out_ref.at[me], local_sem)
        mine.start()
        copies = []
        for k in range(1, 8):
            cp = pltpu.make_async_remote_copy(src_ref=x_ref, dst_ref=out_ref.at[me], send_sem=send_sems.at[k - 1],
                                              recv_sem=recv_sems.at[k - 1], device_id=_flip(k), device_id_type=MESH)
            cp.start()
            copies.append(cp)
        for k in range(1, 8):
            pltpu.make_async_remote_copy(src_ref=x_ref, dst_ref=out_ref.at[_lin(_flip(k))], send_sem=send_sems.at[k - 1],
                                         recv_sem=recv_sems.at[k - 1], device_id=_flip(k), device_id_type=MESH).wait_recv()
        for cp in copies:
            cp.wait_send()
        mine.wait()

    return _pcall(
        body, name=name, out_shape=jax.ShapeDtypeStruct((8,) + x.shape, x.dtype),
        in_specs=[ANY], out_specs=ANY,
        scratch_shapes=[pltpu.SemaphoreType.DMA((7,)), pltpu.SemaphoreType.DMA((7,)), pltpu.SemaphoreType.DMA],
    )(x)


CHIP_FLIPS = (2, 4, 6)


def _row_tile(r, cdim):
    best = SUBLANE
    for t in range(SUBLANE, r + 1, SUBLANE):
        if r % t == 0 and t * cdim * 4 <= (3 << 20):
            best = t
    assert r % best == 0
    return best


def _gather_weights(ws):
    n = len(ws)

    def body(*refs):
        w_refs, out_refs = refs[:n], refs[n:2 * n]
        send_sems, recv_sems = refs[2 * n:]
        x, y, c = _me()
        sib = _flip(1)

        def slot(a, dev, half):
            return out_refs[a].at[2 * dev[0] + dev[1], half]

        first = []
        for a in range(n):
            for j, k in enumerate(CHIP_FLIPS):
                cp = pltpu.make_async_remote_copy(src_ref=w_refs[a].at[c], dst_ref=slot(a, (x, y), c), send_sem=send_sems.at[6 * a + j],
                                                  recv_sem=recv_sems.at[6 * a + j], device_id=_flip(k), device_id_type=MESH)
                cp.start()
                first.append(cp)
        passed = []
        for a in range(n):
            for j, k in enumerate(CHIP_FLIPS):
                src = _flip(k)
                landed = slot(a, src, c)
                pltpu.make_async_remote_copy(src_ref=landed, dst_ref=landed, send_sem=send_sems.at[6 * a + j],
                                             recv_sem=recv_sems.at[6 * a + j], device_id=src, device_id_type=MESH).wait_recv()
                cp = pltpu.make_async_remote_copy(src_ref=landed, dst_ref=landed, send_sem=send_sems.at[6 * a + 3 + j],
                                                  recv_sem=recv_sems.at[6 * a + 3 + j], device_id=sib, device_id_type=MESH)
                cp.start()
                passed.append(cp)
        for a in range(n):
            for j, k in enumerate(CHIP_FLIPS):
                got = slot(a, _flip(k), 1 - c)
                pltpu.make_async_remote_copy(src_ref=got, dst_ref=got, send_sem=send_sems.at[6 * a + 3 + j],
                                             recv_sem=recv_sems.at[6 * a + 3 + j], device_id=sib, device_id_type=MESH).wait_recv()
        for cp in first + passed:
            cp.wait_send()

    got = _pcall(
        body, name="gather_weights", out_shape=tuple(jax.ShapeDtypeStruct((N_CHIPS,) + w.shape, w.dtype) for w in ws),
        in_specs=[ANY] * n, out_specs=(ANY,) * n,
        scratch_shapes=[pltpu.SemaphoreType.DMA((6 * n,)), pltpu.SemaphoreType.DMA((6 * n,))],
    )(*ws)
    chip = 2 * lax.axis_index("x") + lax.axis_index("y")
    return [lax.dynamic_update_index_in_dim(g, w, chip, axis=0) for g, w in zip(got, ws)]


def _sibling_halves(gs):
    n = len(gs)

    def body(*refs):
        s_refs, out_refs = refs[:n], refs[n:2 * n]
        send_sems, recv_sems = refs[2 * n:]
        c = lax.axis_index("c")
        sib = _flip(1)
        copies = []
        for a in range(n):
            for j in range(N_CHIPS):
                cp = pltpu.make_async_remote_copy(src_ref=s_refs[a].at[j, 1 - c], dst_ref=out_refs[a].at[j], send_sem=send_sems.at[4 * a + j],
                                                  recv_sem=recv_sems.at[4 * a + j], device_id=sib, device_id_type=MESH)
                cp.start()
                copies.append(cp)
        for cp in copies:
            cp.wait()

    return _pcall(
        body, name="rs_sibling_halves", out_shape=tuple(jax.ShapeDtypeStruct((N_CHIPS,) + g.shape[2:], g.dtype) for g in gs),
        in_specs=[ANY] * n, out_specs=(ANY,) * n,
        scratch_shapes=[pltpu.SemaphoreType.DMA((4 * n,)), pltpu.SemaphoreType.DMA((4 * n,))],
    )(*gs)


def _scatter_to_chips(parts):
    n = len(parts)
    hbm = pltpu.MemorySpace.HBM
    p_refs = [jax.new_ref(p, memory_space=hbm) for p in parts]
    out_refs = [jax.empty_ref(jax.ShapeDtypeStruct((3,) + p.shape[1:], p.dtype), memory_space=hbm) for p in parts]

    @pl.kernel(mesh=plsc.ScalarSubcoreMesh(axis_name="sequencer", num_cores=1), name="scatter_to_chips",
               scratch_types=(pltpu.SemaphoreType.DMA((3 * n,)), pltpu.SemaphoreType.DMA((3 * n,))),
               compiler_params=pltpu.CompilerParams(collective_id=1))
    def launch(send_sems, recv_sems):
        barrier = pltpu.get_barrier_semaphore()
        for k in CHIP_FLIPS:
            pl.semaphore_signal(barrier, inc=1, device_id=_flip(k), device_id_type=MESH)
        pl.semaphore_wait(barrier, len(CHIP_FLIPS))
        copies = []
        for a in range(n):
            for j, k in enumerate(CHIP_FLIPS):
                to = _flip(k)
                cp = pltpu.make_async_remote_copy(src_ref=p_refs[a].at[2 * to[0] + to[1]], dst_ref=out_refs[a].at[j],
                                                  send_sem=send_sems.at[3 * a + j], recv_sem=recv_sems.at[3 * a + j],
                                                  device_id=to, device_id_type=MESH)
                cp.start()
                copies.append(cp)
        for cp in copies:
            cp.wait()

    launch()
    return [r[...] for r in out_refs]


def _sibling_result(halves):
    n = len(halves)

    def body(*refs):
        h_refs, out_refs = refs[:n], refs[n:2 * n]
        send_sems, recv_sems = refs[2 * n:]
        sib = _flip(1)
        copies = []
        for a in range(n):
            cp = pltpu.make_async_remote_copy(src_ref=h_refs[a], dst_ref=out_refs[a], send_sem=send_sems.at[a],
                                              recv_sem=recv_sems.at[a], device_id=sib, device_id_type=MESH)
            cp.start()
            copies.append(cp)
        for cp in copies:
            cp.wait()

    theirs = _pcall(
        body, name="rs_sibling_result", out_shape=tuple(jax.ShapeDtypeStruct(h.shape, h.dtype) for h in halves),
        in_specs=[ANY] * n, out_specs=(ANY,) * n,
        scratch_shapes=[pltpu.SemaphoreType.DMA((n,)), pltpu.SemaphoreType.DMA((n,))],
    )(*halves)
    c = lax.axis_index("c")
    return [jnp.where(c == 0, jnp.stack([m, t]), jnp.stack([t, m])) for m, t in zip(halves, theirs)]


def _add_halves(g, r1, name):
    n, _, r, cdim = g.shape
    tm = _row_tile(r, cdim)

    def body(c_ref, g_ref, r_ref, o_ref):
        o_ref[...] = g_ref[...] + r_ref[...]

    return _pcall(
        body, name=name, out_shape=jax.ShapeDtypeStruct((n, r, cdim), g.dtype),
        grid_spec=pltpu.PrefetchScalarGridSpec(
            num_scalar_prefetch=1, grid=(n, r // tm),
            in_specs=[pl.BlockSpec((None, None, tm, cdim), lambda j, i, c_ref: (j, c_ref[0], i, 0)),
                      pl.BlockSpec((None, tm, cdim), lambda j, i, c_ref: (j, i, 0))],
            out_specs=pl.BlockSpec((None, tm, cdim), lambda j, i, c_ref: (j, i, 0))),
        compiler_params=_params(("parallel", "parallel")),
    )(lax.axis_index("c").astype(jnp.int32).reshape(1), g, r1)


def _add_chips(part, got, name):
    _, r, cdim = part.shape
    tm = _row_tile(r, cdim)

    def body(chip_ref, p_ref, g_ref, o_ref):
        o_ref[...] = ((p_ref[...] + g_ref[0]) + g_ref[1]) + g_ref[2]

    chip = (2 * lax.axis_index("x") + lax.axis_index("y")).astype(jnp.int32).reshape(1)
    return _pcall(
        body, name=name, out_shape=jax.ShapeDtypeStruct((r, cdim), part.dtype),
        grid_spec=pltpu.PrefetchScalarGridSpec(
            num_scalar_prefetch=1, grid=(r // tm,),
            in_specs=[pl.BlockSpec((None, tm, cdim), lambda i, chip_ref: (chip_ref[0], i, 0)),
                      pl.BlockSpec((3, tm, cdim), lambda i, chip_ref: (0, i, 0))],
            out_specs=pl.BlockSpec((tm, cdim), lambda i, chip_ref: (i, 0))),
        compiler_params=_params(("parallel",)),
    )(chip, part, got)


def _reduce_scatter_start(gs, names):
    r1 = _sibling_halves(gs)
    parts = [_add_halves(g, r, "add_halves_" + nm) for g, r, nm in zip(gs, r1, names)]
    return parts, _scatter_to_chips(parts), names


def _reduce_scatter_finish(started):
    parts, got, names = started
    mine = [_add_chips(p, q, "add_chips_" + nm) for p, q, nm in zip(parts, got, names)]
    return _sibling_result(mine)


def _sum8(x):
    _, r, n = x.shape
    tm = 128 if r % 128 == 0 else r

    def body(x_ref, o_ref):
        acc = x_ref[0]
        for d in range(1, 8):
            acc = acc + x_ref[d]
        o_ref[...] = acc

    return _pcall(body, name="sum8", out_shape=jax.ShapeDtypeStruct((r, n), x.dtype), grid=(r // tm,),
                  in_specs=[pl.BlockSpec((8, tm, n), lambda i: (0, i, 0))], out_specs=pl.BlockSpec((tm, n), lambda i: (i, 0)),
                  compiler_params=_params(("parallel",)))(x)


SHARDED = ("w_in", "s5_w_glu", "mla_w_uq", "mla_w_ukv", "w_out", "ffn_w_up", "ffn_w_down")
COL_SHARDED = ("w_in", "mla_w_uq", "mla_w_ukv", "ffn_w_up")
REPLICATED = ("s5_lambda_re", "s5_lambda_im", "s5_log_dt", "s5_b_re", "s5_b_im", "s5_c_re", "s5_c_im", "s5_d",
              "mla_q_norm", "mla_kv_norm", "hg_lb_logits", "hg_out_norm", "mix_pre_norm", "mix_post_norm",
              "ffn_pre_norm", "ffn_post_norm", "ffn_conv_b")
WEIGHTS = ("w_in", "s5_lambda_re", "s5_lambda_im", "s5_log_dt", "s5_b_re", "s5_b_im", "s5_c_re", "s5_c_im", "s5_d",
           "s5_w_glu", "mla_q_norm", "mla_w_uq", "mla_kv_norm", "mla_w_ukv", "hg_lb_logits", "hg_out_norm", "w_out",
           "mix_pre_norm", "mix_post_norm", "ffn_pre_norm", "ffn_post_norm", "ffn_w_up", "ffn_conv_w", "ffn_conv_b",
           "ffn_w_down", "w_ada", "b_ada")


FF_PAD = 256


def _halves(t):
    return t.reshape(t.shape[:-2] + (2, t.shape[-2] // 2, t.shape[-1]))


def _unhalve(t):
    return t.reshape(t.shape[:-3] + (2 * t.shape[-2], t.shape[-1]))


def _cols_from_chips(t):
    return jnp.concatenate([t[j] for j in range(N_CHIPS)], axis=1)


def _swap_half(t):
    half = t.shape[-1] // 2
    return jnp.concatenate([-t[..., half:], t[..., :half]], axis=-1)


def _prep_win(w):
    s5, cq, ckv, kr, hq, hf, hi, hg = jnp.split(w, (512, 1024, 1280, 1344, 1856, 2368, 2880), axis=1)
    pad = jnp.zeros((w.shape[0], PROJ_W - 3456), w.dtype)
    return jnp.concatenate([s5, cq, hq, hf, hi, hg, ckv, kr, _swap_half(kr), pad], axis=1)


def _prep_wq(w):
    w3 = w.reshape(w.shape[0], MLA_HEADS, MLA_NOPE + MLA_ROPE)
    nope, rope = w3[..., :MLA_NOPE], w3[..., MLA_NOPE:]
    pair = jnp.concatenate([rope, _swap_half(rope)], axis=-1)
    return jnp.concatenate([nope.reshape(w.shape[0], -1), pair.reshape(w.shape[0], -1)], axis=1)


def _pad_ff_cols(w, cpad):
    r = w.shape[0]
    w3 = w.reshape(r, N_CHIPS, -1)
    return jnp.pad(w3, ((0, 0), (0, 0), (0, cpad - w3.shape[2]))).reshape(r, N_CHIPS * cpad)


def _pad_ff_rows(w, cpad):
    w3 = w.reshape(2, 2 * w.shape[1], w.shape[2])
    return jnp.pad(w3, ((0, 0), (0, cpad - w3.shape[1]), (0, 0))).reshape(2 * cpad, w.shape[2])


def _interleave(re, im, axis):
    re, im = jnp.moveaxis(re, axis, -1), jnp.moveaxis(im, axis, -1)
    lead = re.shape[:-1]
    both = jnp.stack([re.reshape(lead + (N_STATE // SCAN_W, SCAN_W)), im.reshape(lead + (N_STATE // SCAN_W, SCAN_W))], axis=-2)
    return jnp.moveaxis(both.reshape(lead + (2 * N_STATE,)), -1, axis)


def _s5_prep(lre, lim, logdt, bre, bim, cre, cim):
    dt = jnp.exp(logdt)[:, None]
    er = jnp.exp(lre * dt)
    ar, ai = er * jnp.cos(lim * dt), er * jnp.sin(lim * dt)
    nr, den = ar - 1.0, lre * lre + lim * lim
    cr, ci = (nr * lre + ai * lim) / den, (ai * lre - nr * lim) / den
    bbr = cr[..., None] * bre - ci[..., None] * bim
    bbi = cr[..., None] * bim + ci[..., None] * bre
    eye = jnp.eye(S5_GROUPS, dtype=F32)[:, None, :, None]

    def block_diag(t):
        return (t[:, :, None, :] * eye).reshape(S5_GROUPS * t.shape[1], S5_GROUPS * t.shape[2])

    tr = lambda t: jnp.transpose(t, (0, 2, 1))
    bd = _interleave(block_diag(tr(bbr)), block_diag(tr(bbi)), 1)
    cd = _interleave(block_diag(tr(cre)), block_diag(tr(-cim)), 0)
    a = _interleave(ar.reshape(1, N_STATE), ai.reshape(1, N_STATE), 1)
    return a, bd, cd


def _lower_bounds(logits):
    probs = jax.nn.softmax(logits, axis=0)
    return jnp.cumsum(probs, axis=0) - probs[0:1]


def _rope_table(positions):
    inv_freq = 1.0 / (ROPE_THETA ** (jnp.arange(0, MLA_ROPE, 2, dtype=F32) / MLA_ROPE))
    ang = positions.astype(F32)[:, None] * inv_freq
    cos, sin = jnp.cos(ang), jnp.sin(ang)
    return jnp.concatenate([cos, cos, sin, sin], axis=1)


def _split_mod(mod):
    return [mod[:, i * D_MODEL:(i + 1) * D_MODEL] for i in range(6)]


def _layer_fwd(x, wl, mod, cs):
    s = x.shape[0]
    ffp = wl["wdown_p"].shape[0]
    sh1, sc1, g1, sh2, sc2, g2 = _split_mod(mod)
    h1 = _normmod_fwd(x, wl["mix_pre_norm"], sc1, sh1, "mix_pre")
    proj = _mm(h1, wl["win_p"], mode="nn", name="in_proj", m=s, n=PROJ_W, k=D_MODEL)
    out_s5, s5_saved = _s5_fwd(proj, wl, s)
    o_mla, mla_saved = _mla_fwd(proj, wl, cs, s)
    y_hg, o_hg, states = _hg_fwd(proj, wl["hg_lb"], wl["hg_out_norm"])
    cat = jnp.concatenate([out_s5, o_mla.astype(BF16), y_hg], axis=1)
    mixed = _mm(cat, wl["w_out"], mode="nn", name="out_proj", m=s, n=D_MODEL, k=D_MODEL)
    x2 = _postnorm_fwd(x, mixed, wl["mix_post_norm"], g1, "mix_post")
    h2 = _normmod_fwd(x2, wl["ffn_pre_norm"], sc2, sh2, "ffn_pre")
    up = _mm(h2, wl["wup_cm"], mode="nn", name="ffn_up", m=s, n=2 * ffp, k=D_MODEL, cm=ffp // 2, tn=ffp // 4)
    act = _ffn_act_fwd(up, wl["conv_w_p"], wl["conv_b_p"], ffp)
    y = _mm(act, wl["wdown_p"], mode="nn", name="ffn_down", m=s, n=D_MODEL, k=ffp)
    x3 = _postnorm_fwd(x2, y, wl["ffn_post_norm"], g2, "ffn_post")
    return x3, (x, h1, proj, s5_saved, mla_saved, o_hg, states, cat, mixed, x2, h2, up, act, y)


def _layer_bwd(dx3, saved, wl, mod, cs):
    x, h1, proj, s5_saved, mla_saved, o_hg, states, cat, mixed, x2, h2, up, act, y = saved
    s = x.shape[0]
    ffp = wl["wdown_p"].shape[0]
    sh1, sc1, g1, sh2, sc2, g2 = _split_mod(mod)
    g = {}
    dy, g["ffn_post_norm"], dg2 = _postnorm_bwd(dx3, y, wl["ffn_post_norm"], g2, "ffn_post_bwd")
    da = _mm(dy, wl["wdown_p"], mode="nt", name="ffn_down_dx", m=s, n=ffp, k=D_MODEL)
    g["wdown_p"] = _mm(act, dy, mode="tn", name="ffn_down_dw", m=ffp, n=D_MODEL, k=s)
    dug, duv, dwg, dwv, dbg, dbv = _ffn_act_bwd(da, up, wl["conv_w_p"], wl["conv_b_p"], ffp)
    g["conv_w_p"] = jnp.concatenate([dwg, dwv], axis=1)
    g["conv_b_p"] = jnp.concatenate([dbg, dbv], axis=1)
    dup = jnp.concatenate([_conv_bwd_input(dug, wl["conv_w_p"], 0, "ffn_conv_bwd_gate"),
                           _conv_bwd_input(duv, wl["conv_w_p"], ffp, "ffn_conv_bwd_val")], axis=1)
    dh2 = _mm(dup, wl["wup_cm"], mode="nt", name="ffn_up_dx", m=s, n=D_MODEL, k=2 * ffp, cm=ffp // 2, tk=ffp // 4)
    g["wup_cm"] = _mm(h2, dup, mode="tn", name="ffn_up_dw", m=D_MODEL, n=2 * ffp, k=s, cm=ffp // 2, tn=ffp // 4)
    dx2, g["ffn_pre_norm"], dsc2, dsh2 = _normmod_bwd(dh2, x2, wl["ffn_pre_norm"], sc2, dx3, "ffn_pre_bwd")
    dmixed, g["mix_post_norm"], dg1 = _postnorm_bwd(dx2, mixed, wl["mix_post_norm"], g1, "mix_post_bwd")
    dcat = _mm(dmixed, wl["w_out"], mode="nt", name="out_proj_dx", m=s, n=D_MODEL, k=D_MODEL)
    g["w_out"] = _mm(cat, dmixed, mode="tn", name="out_proj_dw", m=D_MODEL, n=D_MODEL, k=s)
    du_s5, s5g = _s5_bwd(dcat, proj, wl, s5_saved, s)
    dcq, dckv, dkr, mlag = _mla_bwd(dcat, proj, wl, cs, mla_saved, s)
    dhq, dhf, dhi, dhg, g["hg_lb"], dgm = _hg_bwd(dcat, S5_WIDTH + MLA_WIDTH, proj, wl["hg_lb"], wl["hg_out_norm"], o_hg, states)
    g["hg_out_norm"] = jnp.sum(dgm, axis=0, keepdims=True)
    g.update(s5g)
    g.update(mlag)
    dproj = jnp.concatenate([du_s5, dcq, dhq, dhf, dhi, dhg, dckv, dkr, jnp.zeros((s, PROJ_W - 3456), F32)], axis=1).astype(BF16)
    dh1 = _mm(dproj, wl["win_p"], mode="nt", name="in_proj_dx", m=s, n=D_MODEL, k=PROJ_W)
    g["win_p"] = _mm(h1, dproj, mode="tn", name="in_proj_dw", m=D_MODEL, n=PROJ_W, k=s)
    dx, g["mix_pre_norm"], dsc1, dsh1 = _normmod_bwd(dh1, x, wl["mix_pre_norm"], sc1, dx2, "mix_pre_bwd")
    dmod = jnp.concatenate([dsh1, dsc1, dg1, dsh2, dsc2, dg2], axis=1)
    return dx, g, dmod


def _prepare_layer(gathered, conv_w, rep, cpad):
    def sharded_prep(w_in, s5_w_glu, mla_w_uq, mla_w_ukv, w_out, ffn_w_up, ffn_w_down, ffn_conv_w):
        merge = lambda t: t.reshape(N_CHIPS * t.shape[1], t.shape[2])
        return {"win_p": _prep_win(_cols_from_chips(w_in)), "s5_w_glu": merge(s5_w_glu), "mla_wq": _prep_wq(_cols_from_chips(mla_w_uq)),
                "mla_w_ukv": _cols_from_chips(mla_w_ukv), "w_out": merge(w_out), "wup_cm": ffn_w_up,
                "wdown_p": _pad_ff_rows(ffn_w_down, cpad), "conv_w_p": _pad_ff_cols(ffn_conv_w, cpad)}

    def rep_prep(lre, lim, logdt, bre, bim, cre, cim, conv_b):
        a, bd, cd = _s5_prep(lre, lim, logdt, bre, bim, cre, cim)
        return {"s5_a": a, "s5_bd": bd, "s5_cd": cd, "conv_b_p": _pad_ff_cols(conv_b, cpad)}

    sh_args = [gathered[k] for k in SHARDED] + [conv_w]
    rep_names = ("s5_lambda_re", "s5_lambda_im", "s5_log_dt", "s5_b_re", "s5_b_im", "s5_c_re", "s5_c_im", "ffn_conv_b")
    rep_args = [rep[k] for k in rep_names]
    wl = sharded_prep(*sh_args)
    rep_out, rep_vjp = jax.vjp(rep_prep, *rep_args)
    wl.update(rep_out)
    sh_t = jax.linear_transpose(sharded_prep, *[jax.ShapeDtypeStruct(a.shape, F32) for a in sh_args])

    def back(g):
        out = dict(zip(SHARDED + ("ffn_conv_w",), sh_t({k: g[k] for k in ("win_p", "s5_w_glu", "mla_wq", "mla_w_ukv", "w_out", "wup_cm", "wdown_p", "conv_w_p")})))
        out.update(zip(rep_names, rep_vjp({k: g[k] for k in ("s5_a", "s5_bd", "s5_cd", "conv_b_p")})))
        return out

    return wl, back


PER_LAYER_ROWS = ("s5_d", "mla_q_norm", "mla_kv_norm", "hg_out_norm", "mix_pre_norm", "mix_post_norm", "ffn_pre_norm", "ffn_post_norm")


def _flat_pad(parts, unit):
    flat = jnp.concatenate([p.reshape(-1) for p in parts])
    n = -(-flat.shape[0] // unit) * unit
    return jnp.pad(flat, (0, n - flat.shape[0])).reshape(-1, FLAT_COLS)


def _split_flat(flat, like):
    flat = flat.reshape(-1)
    out, pos = [], 0
    for t in like:
        out.append(flat[pos:pos + t.size].reshape(t.shape))
        pos += t.size
    return out


def kernel(x, c, positions, w_in, s5_lambda_re, s5_lambda_im, s5_log_dt, s5_b_re, s5_b_im, s5_c_re, s5_c_im, s5_d, s5_w_glu, mla_q_norm, mla_w_uq, mla_kv_norm, mla_w_ukv, hg_lb_logits, hg_out_norm, w_out, mix_pre_norm, mix_post_norm, ffn_pre_norm, ffn_post_norm, ffn_w_up, ffn_conv_w, ffn_conv_b, ffn_w_down, w_ada, b_ada, loss_target, m_w_in, m_s5_lambda_re, m_s5_lambda_im, m_s5_log_dt, m_s5_b_re, m_s5_b_im, m_s5_c_re, m_s5_c_im, m_s5_d, m_s5_w_glu, m_mla_q_norm, m_mla_w_uq, m_mla_kv_norm, m_mla_w_ukv, m_hg_lb_logits, m_hg_out_norm, m_w_out, m_mix_pre_norm, m_mix_post_norm, m_ffn_pre_norm, m_ffn_post_norm, m_ffn_w_up, m_ffn_conv_w, m_ffn_conv_b, m_ffn_w_down, m_w_ada, m_b_ada, v_w_in, v_s5_lambda_re, v_s5_lambda_im, v_s5_log_dt, v_s5_b_re, v_s5_b_im, v_s5_c_re, v_s5_c_im, v_s5_d, v_s5_w_glu, v_mla_q_norm, v_mla_w_uq, v_mla_kv_norm, v_mla_w_ukv, v_hg_lb_logits, v_hg_out_norm, v_w_out, v_mix_pre_norm, v_mix_post_norm, v_ffn_pre_norm, v_ffn_post_norm, v_ffn_w_up, v_ffn_conv_w, v_ffn_conv_b, v_ffn_w_down, v_w_ada, v_b_ada):
    p = dict(locals())
    n_layers = w_in.shape[0]
    c_up = ffn_w_up.shape[2]
    cpad = -(-c_up // FF_PAD) * FF_PAD
    xs, target = x[0], loss_target[0]
    me = 4 * lax.axis_index("x") + 2 * lax.axis_index("y") + lax.axis_index("c")
    chip = 2 * lax.axis_index("x") + lax.axis_index("y")
    cs = _rope_table(positions[0])

    cact = jax.nn.silu(_all_gather8(c, "gather_c")[:, 0, :])
    ada_cols = w_ada.shape[2]
    mod_part = jnp.stack([_mm(cact, w_ada[l], mode="nn", name="ada_mod", m=8, n=ada_cols, k=D_MODEL) for l in range(n_layers)])
    mod_all = _all_gather8(mod_part.reshape(1, -1), "gather_mod").reshape(N_CHIPS, 2, n_layers, 8, ada_cols)[:, 0]
    mod_mine = lax.dynamic_index_in_dim(mod_all, me, axis=2, keepdims=False)
    mods = mod_mine.transpose(1, 0, 2).reshape(n_layers, -1) + b_ada

    conv_w_all = _all_gather8(ffn_conv_w.reshape(1, -1), "gather_conv_w").reshape(N_CHIPS, 2, n_layers, 3, -1)[:, 0]
    conv_w_full = conv_w_all.transpose(1, 2, 0, 3).reshape(n_layers, 3, -1)

    lbs, lb_vjp = jax.vjp(_lower_bounds, hg_lb_logits)

    layers = []
    for l in range(n_layers):
        shards = {k: p[k][l] for k in SHARDED}
        shards["ffn_w_up"] = jnp.pad(shards["ffn_w_up"], ((0, 0), (0, cpad - c_up)))
        got = _gather_weights([_halves(shards[k].astype(BF16)) for k in SHARDED])
        gathered = {k: _unhalve(t) for k, t in zip(SHARDED, got)}
        rep = {k: p[k][l] for k in ("s5_lambda_re", "s5_lambda_im", "s5_log_dt", "s5_b_re", "s5_b_im", "s5_c_re", "s5_c_im")}
        rep["ffn_conv_b"] = ffn_conv_b[l][None, :]
        wl, back = _prepare_layer(gathered, conv_w_full[l], rep, cpad)
        for k in PER_LAYER_ROWS:
            wl[k] = p[k][l][None, :]
        wl["hg_lb"] = lbs[l][None, :]
        layers.append((wl, back))

    h = xs
    saved = []
    for l in range(n_layers):
        h, sv = _layer_fwd(h, layers[l][0], mods[l][None, :], cs)
        saved.append(sv)
    dh, loss_part = _loss_grad(h, target)
    loss = lax.psum(loss_part[0, 0], ("x", "y", "c"))

    grads = {k: [None] * n_layers for k in WEIGHTS}
    dmods, dlbs = [None] * n_layers, [None] * n_layers

    def take_in(layer, started):
        for k, t in zip(SHARDED, _reduce_scatter_finish(started)):
            grads[k][layer] = _unhalve(t)
        grads["ffn_w_up"][layer] = grads["ffn_w_up"][layer][:, :c_up]

    under_way = None
    for l in reversed(range(n_layers)):
        wl, back = layers[l]
        dh, g, dmods[l] = _layer_bwd(dh, saved[l], wl, mods[l][None, :], cs)
        if under_way is not None:
            take_in(*under_way)
        ref_g = back(g)
        under_way = (l, _reduce_scatter_start([_halves(ref_g[k]) for k in SHARDED], SHARDED))
        grads["ffn_conv_w"][l] = ref_g["ffn_conv_w"]
        for k in ("s5_lambda_re", "s5_lambda_im", "s5_log_dt", "s5_b_re", "s5_b_im", "s5_c_re", "s5_c_im"):
            grads[k][l] = ref_g[k]
        grads["ffn_conv_b"][l] = ref_g["ffn_conv_b"][0]
        for k in PER_LAYER_ROWS:
            grads[k][l] = g[k][0]
        dlbs[l] = g["hg_lb"][0]
    take_in(*under_way)
    grads = {k: jnp.stack(v) for k, v in grads.items() if v[0] is not None}
    grads["hg_lb_logits"] = lb_vjp(jnp.stack(dlbs))[0]

    summed = REPLICATED + ("ffn_conv_w",)
    rep_flat = _flat_pad([grads[k] for k in summed], 128 * FLAT_COLS)
    rep_sum = _sum8(_all_gather8(rep_flat, "gather_small_grads"))
    for k, t in zip(summed, _split_flat(rep_sum, [grads[k] for k in summed])):
        grads[k] = t
    conv_cols = grads["ffn_conv_w"].reshape(n_layers, 3, N_CHIPS, c_up)
    grads["ffn_conv_w"] = lax.dynamic_index_in_dim(conv_cols, chip, axis=2, keepdims=False)

    dmod_all = _all_gather8(jnp.stack(dmods).reshape(n_layers * 6 * D_MODEL // FLAT_COLS, FLAT_COLS), "gather_dmod")
    grads["b_ada"] = _sum8(dmod_all).reshape(n_layers, 6 * D_MODEL)
    dmod_cols = lax.dynamic_slice_in_dim(dmod_all.reshape(8, n_layers, N_CHIPS, ada_cols), chip, 1, axis=2)[:, :, 0]
    grads["w_ada"], delta_ada, new_m_ada, new_v_ada = _ada_grad_adamw(cact, dmod_cols.transpose(1, 0, 2), w_ada, m_w_ada, v_w_ada)

    delta, new_m, new_v = {"w_ada": delta_ada}, {"w_ada": new_m_ada}, {"w_ada": new_v_ada}
    for k in SHARDED:
        delta[k], new_m[k], new_v[k] = _adamw(p[k], grads[k], p["m_" + k], p["v_" + k], "adamw_" + k)
    small = REPLICATED + ("b_ada", "ffn_conv_w")
    flats = [_flat_pad([src[k] for k in small], 128 * FLAT_COLS)
             for src in (p, grads, {k: p["m_" + k] for k in small}, {k: p["v_" + k] for k in small})]
    for dst, flat in zip((delta, new_m, new_v), _adamw(*flats, "adamw_small")):
        for k, t in zip(small, _split_flat(flat, [p[k] for k in small])):
            dst[k] = t

    return (loss, dh[None], *[grads[k] for k in WEIGHTS], *[delta[k] for k in WEIGHTS],
            *[new_m[k] for k in WEIGHTS], *[new_v[k] for k in WEIGHTS])
```

```python
import functools
import math

import numpy as np
import jax
import jax.numpy as jnp
from jax import lax
from jax.experimental import pallas as pl
from jax.experimental.pallas import tpu as pltpu
from jax.experimental.pallas import tpu_sc as plsc

F32 = jnp.float32
BF16 = jnp.bfloat16
MESH = pl.DeviceIdType.MESH

D_MODEL = 2048
S5_WIDTH, S5_GROUP, S5_GROUPS, S5_STATE = 512, 16, 32, 64
MLA_HEADS, MLA_NOPE, MLA_ROPE, MLA_V = 8, 128, 64, 128
MLA_Q_RANK, MLA_KV_RANK = 512, 256
MLA_WIDTH = MLA_HEADS * MLA_V
ROPE_THETA = 10000.0
HG_HEADS, HG_DK, HG_DV = 4, 128, 128
HG_WIDTH = HG_HEADS * HG_DV
EPS = 1e-6
ADAM_LR, ADAM_B1, ADAM_B2, ADAM_EPS, ADAM_WD, ADAM_STEP = 0.001, 0.9, 0.999, 1e-08, 0.01, 10
GELU_K0 = math.sqrt(2.0 / math.pi)
GELU_K1 = 0.044715

LANE = 128
SUBLANE = 8
VMEM_LIMIT = 56 * 1024 * 1024

P_S5, P_CQ, P_HQ, P_HF, P_HI, P_HG, P_CKV, P_KR = 0, 512, 1024, 1536, 2048, 2560, 3072, 3328
PROJ_W = 3584
N_STATE = S5_GROUPS * S5_STATE
SCAN_W = 512
HG_CHUNK = 64
N_CHIPS = 4
FLAT_COLS = 1024
ADD_ROWS = 256


def _sigmoid(x):
    return 1.0 / (1.0 + jnp.exp(-x))


def _silu(x):
    return x * _sigmoid(x)


def _dsilu(x):
    s = _sigmoid(x)
    return s * (1.0 + x * (1.0 - s))


def _gelu(x):
    return 0.5 * x * (1.0 + jnp.tanh(GELU_K0 * (x + GELU_K1 * x * x * x)))


def _dgelu(x):
    t = jnp.tanh(GELU_K0 * (x + GELU_K1 * x * x * x))
    return 0.5 * (1.0 + t) + 0.5 * x * (1.0 - t * t) * GELU_K0 * (1.0 + 3.0 * GELU_K1 * x * x)


def _colsum(v):
    return jnp.sum(v, axis=0, keepdims=True)


def _rowmean(v):
    return jnp.mean(v, axis=-1, keepdims=True)


def _dot(a, b, dims):
    return lax.dot_general(a.astype(BF16), b.astype(BF16), (dims, ((), ())), preferred_element_type=F32)


NN = ((1,), (0,))
NT = ((1,), (1,))
TN = ((0,), (0,))


def _pcall(body, **kw):
    return pl.pallas_call(body, **kw)


def _params(sem):
    return pltpu.CompilerParams(dimension_semantics=sem, vmem_limit_bytes=VMEM_LIMIT)


def _tile(dim, pref):
    if dim <= pref:
        return dim
    t = (pref // LANE) * LANE
    while t > LANE and dim % t:
        t -= LANE
    assert dim % t == 0, (dim, pref)
    return t


def _mm(a, b, *, mode, name, m, n, k, a_off=0, b_off=0, tm=1024, tn=1024, tk=1024,
        out=((F32),), epi=None, extras=(), cm=None):
    if cm is not None:
        if mode == "nt":
            tk = _tile(cm, tk)
        else:
            tn = _tile(cm, tn)
    tm, tn, tk = _tile(m, tm), _tile(n, tn), _tile(k, tk)
    nk = k // tk
    dims = {"nn": NN, "nt": NT, "tn": TN}[mode]
    if mode == "tn":
        assert a_off % tm == 0 and b_off % tn == 0
        a_spec = pl.BlockSpec((tk, tm), lambda i, j, kk: (kk, i + a_off // tm))
        b_spec = pl.BlockSpec((tk, tn), lambda i, j, kk: (kk, j + b_off // tn))
    else:
        assert a_off % tk == 0 and b_off == 0
        a_spec = pl.BlockSpec((tm, tk), lambda i, j, kk: (i, kk + a_off // tk))
        if mode == "nn" and cm is not None:
            b_spec = pl.BlockSpec((None, tk, tn), lambda i, j, kk, per=cm // tn: (j // per, kk, j % per))
        elif mode == "nn":
            b_spec = pl.BlockSpec((tk, tn), lambda i, j, kk: (kk, j))
        elif cm is not None:
            b_spec = pl.BlockSpec((None, tn, tk), lambda i, j, kk, per=cm // tk: (kk // per, j, kk % per))
        else:
            b_spec = pl.BlockSpec((tn, tk), lambda i, j, kk: (j, kk))
    in_specs, ex_arrays = [a_spec, b_spec], []
    for e in extras:
        if e[0] == "tile":
            off = e[2] // tn
            assert e[2] % tn == 0
            in_specs.append(pl.BlockSpec((tm, tn), lambda i, j, kk, off=off: (i, j + off)))
        else:
            in_specs.append(pl.BlockSpec((e[1].shape[0], tn), lambda i, j, kk: (0, j)))
        ex_arrays.append(e[1])
    n_ex = len(ex_arrays)
    n_out = len(out)

    def body(*refs):
        a_ref, b_ref = refs[0], refs[1]
        ex_refs = refs[2:2 + n_ex]
        o_refs = refs[2 + n_ex:2 + n_ex + n_out]
        acc_ref = refs[-1]
        kk = pl.program_id(2)

        @pl.when(kk == 0)
        def _():
            acc_ref[...] = jnp.zeros_like(acc_ref)

        acc_ref[...] += _dot(a_ref[...], b_ref[...], dims)

        @pl.when(kk == nk - 1)
        def _():
            acc = acc_ref[...]
            if epi is None:
                o_refs[0][...] = acc.astype(o_refs[0].dtype)
            else:
                vals = epi(acc, *[r[...] for r in ex_refs])
                for r, v in zip(o_refs, vals):
                    r[...] = v.astype(r.dtype)

    if mode == "tn" and cm is not None:
        out_shape = tuple(jax.ShapeDtypeStruct((N_CHIPS, m, cm), d) for d in out)
        out_specs = tuple(pl.BlockSpec((None, tm, tn), lambda i, j, kk, per=cm // tn: (j // per, i, j % per)) for _ in out)
    else:
        out_shape = tuple(jax.ShapeDtypeStruct((m, n), d) for d in out)
        out_specs = tuple(pl.BlockSpec((tm, tn), lambda i, j, kk: (i, j)) for _ in out)
    res = _pcall(
        body, name=name,
        out_shape=out_shape,
        grid=(m // tm, n // tn, nk),
        in_specs=in_specs,
        out_specs=out_specs,
        scratch_shapes=[pltpu.VMEM((tm, tn), F32)],
        compiler_params=_params(("parallel", "parallel", "arbitrary")),
    )(a, b, *ex_arrays)
    return res[0] if n_out == 1 else res


def _rows(fn, *, name, s, tm, ins, outs, ncb=1):
    tm = min(tm, s)
    assert s % tm == 0 and tm % SUBLANE == 0
    ni = s // tm
    r8 = tm // SUBLANE
    in_specs, arrays = [], []
    for e in ins:
        kind, arr = e[0], e[1]
        if kind in ("row", "prev8", "next8", "vecb"):
            off, w = e[2] // e[3], e[3]
            assert e[2] % e[3] == 0
        if kind == "row":
            in_specs.append(pl.BlockSpec((tm, w), lambda j, i, off=off: (i, off + j)))
        elif kind == "prev8":
            in_specs.append(pl.BlockSpec((SUBLANE, w), lambda j, i, off=off: (jnp.maximum(i * r8 - 1, 0), off + j)))
        elif kind == "next8":
            last = s // SUBLANE - 1
            in_specs.append(pl.BlockSpec((SUBLANE, w), lambda j, i, off=off: (jnp.minimum((i + 1) * r8, last), off + j)))
        elif kind == "vec":
            in_specs.append(pl.BlockSpec(arr.shape, lambda j, i, nd=arr.ndim: (0,) * nd))
        else:
            in_specs.append(pl.BlockSpec((arr.shape[0], w), lambda j, i, off=off: (0, off + j)))
        arrays.append(arr)
    out_shape, out_specs = [], []
    for e in outs:
        if e[0] == "row":
            out_shape.append(jax.ShapeDtypeStruct((s, ncb * e[1]), e[2]))
            out_specs.append(pl.BlockSpec((tm, e[1]), lambda j, i: (i, j)))
        else:
            out_shape.append(jax.ShapeDtypeStruct((e[1], ncb * e[2]), F32))
            out_specs.append(pl.BlockSpec((e[1], e[2]), lambda j, i: (0, j)))

    def body(*refs):
        fn(pl.program_id(1), *refs)

    res = _pcall(
        body, name=name, out_shape=tuple(out_shape), grid=(ncb, ni),
        in_specs=in_specs, out_specs=tuple(out_specs),
        compiler_params=_params(("parallel", "arbitrary")),
    )(*arrays)
    return res[0] if len(outs) == 1 else res


def _acc(ref, i, val):
    @pl.when(i == 0)
    def _():
        ref[...] = val

    @pl.when(i > 0)
    def _():
        ref[...] += val


def _normmod_fwd(x, gain, sc, sh, name):
    s, d = x.shape

    def fn(i, x_ref, g_ref, sc_ref, sh_ref, h_ref):
        xv = x_ref[...]
        r = lax.rsqrt(_rowmean(xv * xv) + EPS)
        h_ref[...] = (((xv * r) * g_ref[...]) * (1.0 + sc_ref[...]) + sh_ref[...]).astype(h_ref.dtype)

    return _rows(fn, name=name, s=s, tm=256, ins=[("row", x, 0, d), ("vec", gain), ("vec", sc), ("vec", sh)],
                 outs=[("row", d, BF16)])


def _normmod_bwd(dh, x, gain, sc, dx_add, name):
    s, d = x.shape

    def fn(i, dh_ref, x_ref, g_ref, sc_ref, add_ref, dx_ref, dg_ref, dsc_ref, dsh_ref):
        xv, dhv = x_ref[...], dh_ref[...]
        r = lax.rsqrt(_rowmean(xv * xv) + EPS)
        xn = xv * r
        gain_v, one_sc = g_ref[...], 1.0 + sc_ref[...]
        ghat = dhv * gain_v * one_sc
        dx_ref[...] = r * (ghat - xn * _rowmean(ghat * xn)) + add_ref[...]
        _acc(dg_ref, i, _colsum(dhv * xn * one_sc))
        _acc(dsc_ref, i, _colsum(dhv * xn * gain_v))
        _acc(dsh_ref, i, _colsum(dhv))

    return _rows(fn, name=name, s=s, tm=256,
                 ins=[("row", dh, 0, d), ("row", x, 0, d), ("vec", gain), ("vec", sc), ("row", dx_add, 0, d)],
                 outs=[("row", d, F32), ("acc", 1, d), ("acc", 1, d), ("acc", 1, d)])


def _postnorm_fwd(x, m, gain, gate, name):
    s, d = x.shape

    def fn(i, x_ref, m_ref, g_ref, gate_ref, o_ref):
        mv = m_ref[...]
        r = lax.rsqrt(_rowmean(mv * mv) + EPS)
        o_ref[...] = x_ref[...] + gate_ref[...] * ((mv * r) * g_ref[...])

    return _rows(fn, name=name, s=s, tm=256, ins=[("row", x, 0, d), ("row", m, 0, d), ("vec", gain), ("vec", gate)],
                 outs=[("row", d, F32)])


def _postnorm_bwd(dxo, m, gain, gate, name):
    s, d = m.shape

    def fn(i, dx_ref, m_ref, g_ref, gate_ref, dm_ref, dg_ref, dgate_ref):
        mv, dxv = m_ref[...], dx_ref[...]
        r = lax.rsqrt(_rowmean(mv * mv) + EPS)
        mn = mv * r
        gain_v, gate_v = g_ref[...], gate_ref[...]
        ghat = dxv * gate_v * gain_v
        dm_ref[...] = (r * (ghat - mn * _rowmean(ghat * mn))).astype(dm_ref.dtype)
        _acc(dg_ref, i, _colsum(dxv * gate_v * mn))
        _acc(dgate_ref, i, _colsum(dxv * mn * gain_v))

    return _rows(fn, name=name, s=s, tm=256, ins=[("row", dxo, 0, d), ("row", m, 0, d), ("vec", gain), ("vec", gate)],
                 outs=[("row", d, BF16), ("acc", 1, d), ("acc", 1, d)])


def _rms_fwd(src, off, w, gain, name):
    s = src.shape[0]

    def fn(i, x_ref, g_ref, o_ref):
        xv = x_ref[...]
        r = lax.rsqrt(_rowmean(xv * xv) + EPS)
        o_ref[...] = ((xv * r) * g_ref[...]).astype(o_ref.dtype)

    return _rows(fn, name=name, s=s, tm=512, ins=[("row", src, off, w), ("vec", gain)], outs=[("row", w, BF16)])


def _rms_bwd(dy, src, off, w, gain, name):
    s = src.shape[0]

    def fn(i, dy_ref, x_ref, g_ref, dx_ref, dg_ref):
        xv, dyv = x_ref[...], dy_ref[...]
        r = lax.rsqrt(_rowmean(xv * xv) + EPS)
        xn = xv * r
        ghat = dyv * g_ref[...]
        dx_ref[...] = r * (ghat - xn * _rowmean(ghat * xn))
        _acc(dg_ref, i, _colsum(dyv * xn))

    return _rows(fn, name=name, s=s, tm=512, ins=[("row", dy, 0, w), ("row", src, off, w), ("vec", gain)],
                 outs=[("row", w, F32), ("acc", 1, w)])


def _loss_grad(x, target):
    s, d = x.shape

    def fn(i, x_ref, t_ref, dx_ref, l_ref):
        diff = x_ref[...] - t_ref[...]
        dx_ref[...] = diff * (1.0 / d)
        part = _colsum(jnp.sum(diff * diff, axis=1, keepdims=True)) * (0.5 / d)
        _acc(l_ref, i, jnp.broadcast_to(part, (1, LANE)))

    return _rows(fn, name="loss_grad", s=s, tm=256, ins=[("row", x, 0, d), ("row", target, 0, d)],
                 outs=[("row", d, F32), ("acc", 1, LANE)])


FFN_WC = 512


def _shift_rows(xv, h_ref, i, row, k):
    out = pltpu.roll(xv, k, 0)
    for r in range(k):
        hrow = jnp.where(i > 0, h_ref[SUBLANE - k + r:SUBLANE - k + r + 1, :], 0.0)
        out = jnp.where(row == r, hrow, out)
    return out


def _conv_rows(x_ref, h_ref, w_ref, b_ref, i, row):
    xv = x_ref[...]
    s1, s2 = _shift_rows(xv, h_ref, i, row, 1), _shift_rows(xv, h_ref, i, row, 2)
    u = ((b_ref[...] + s2 * w_ref[0:1, :]) + s1 * w_ref[1:2, :]) + xv * w_ref[2:3, :]
    return u, s1, s2, xv


def _ffn_act_fwd(up, conv_w, conv_b, ffp):
    s = up.shape[0]
    wc, ncb = FFN_WC, ffp // FFN_WC

    def fn(i, g_ref, gh_ref, v_ref, vh_ref, wg_ref, wv_ref, bg_ref, bv_ref, a_ref):
        row = lax.broadcasted_iota(jnp.int32, g_ref.shape, 0)
        ug = _conv_rows(g_ref, gh_ref, wg_ref, bg_ref, i, row)[0]
        uv = _conv_rows(v_ref, vh_ref, wv_ref, bv_ref, i, row)[0]
        a_ref[...] = (_gelu(ug) * uv).astype(a_ref.dtype)

    return _rows(fn, name="ffn_act_fwd", s=s, tm=512, ncb=ncb,
                 ins=[("row", up, 0, wc), ("prev8", up, 0, wc), ("row", up, ffp, wc), ("prev8", up, ffp, wc),
                      ("vecb", conv_w, 0, wc), ("vecb", conv_w, ffp, wc), ("vecb", conv_b, 0, wc), ("vecb", conv_b, ffp, wc)],
                 outs=[("row", wc, BF16)])


def _ffn_act_bwd(da, up, conv_w, conv_b, ffp):
    s = up.shape[0]
    wc, ncb = FFN_WC, ffp // FFN_WC

    def fn(i, da_ref, g_ref, gh_ref, v_ref, vh_ref, wg_ref, wv_ref, bg_ref, bv_ref,
           dug_ref, duv_ref, dwg_ref, dwv_ref, dbg_ref, dbv_ref):
        row = lax.broadcasted_iota(jnp.int32, g_ref.shape, 0)
        ug, g1, g2, g0 = _conv_rows(g_ref, gh_ref, wg_ref, bg_ref, i, row)
        uv, v1, v2, v0 = _conv_rows(v_ref, vh_ref, wv_ref, bv_ref, i, row)
        dav = da_ref[...]
        dug = dav * uv * _dgelu(ug)
        duv = dav * _gelu(ug)
        dug_ref[...] = dug
        duv_ref[...] = duv
        for r, (gt, vt) in enumerate(((g2, v2), (g1, v1), (g0, v0))):
            _acc(dwg_ref.at[r:r + 1, :], i, _colsum(dug * gt))
            _acc(dwv_ref.at[r:r + 1, :], i, _colsum(duv * vt))
        _acc(dbg_ref, i, _colsum(dug))
        _acc(dbv_ref, i, _colsum(duv))

    return _rows(fn, name="ffn_act_bwd", s=s, tm=512, ncb=ncb,
                 ins=[("row", da, 0, wc), ("row", up, 0, wc), ("prev8", up, 0, wc), ("row", up, ffp, wc), ("prev8", up, ffp, wc),
                      ("vecb", conv_w, 0, wc), ("vecb", conv_w, ffp, wc), ("vecb", conv_b, 0, wc), ("vecb", conv_b, ffp, wc)],
                 outs=[("row", wc, F32), ("row", wc, F32), ("acc", 3, wc), ("acc", 3, wc), ("acc", 1, wc), ("acc", 1, wc)])


def _conv_bwd_input(du, conv_w, w_off, name):
    s, ffp = du.shape
    wc, ncb = FFN_WC, ffp // FFN_WC
    ni = s // min(512, s)

    def fn(i, du_ref, nx_ref, w_ref, o_ref):
        dv = du_ref[...]
        tm = dv.shape[0]
        row = lax.broadcasted_iota(jnp.int32, dv.shape, 0)
        n0 = jnp.where(i < ni - 1, nx_ref[0:1, :], 0.0)
        n1 = jnp.where(i < ni - 1, nx_ref[1:2, :], 0.0)
        u1 = jnp.where(row == tm - 1, n0, pltpu.roll(dv, tm - 1, 0))
        u2 = jnp.where(row == tm - 1, n1, jnp.where(row == tm - 2, n0, pltpu.roll(dv, tm - 2, 0)))
        o_ref[...] = (dv * w_ref[2:3, :] + u1 * w_ref[1:2, :] + u2 * w_ref[0:1, :]).astype(o_ref.dtype)

    return _rows(fn, name=name, s=s, tm=512, ncb=ncb,
                 ins=[("row", du, 0, wc), ("next8", du, 0, wc), ("vecb", conv_w, w_off, wc)],
                 outs=[("row", wc, BF16)])


def _cmul(ar, ai, br, bi):
    return ar * br - ai * bi, ar * bi + ai * br


def _s5_scan(x, a, *, reverse, h=None, name):
    s = x.shape[0]
    w = SCAN_W
    t_rows = min(256, s)
    nt = s // t_rows
    ncol = N_STATE // w
    nbits = t_rows.bit_length()
    r8 = t_rows // SUBLANE

    def tblk(t):
        return nt - 1 - t if reverse else t

    def body(*refs):
        if reverse:
            x_ref, a_ref, h_ref, hh_ref, o_ref, da_ref, carry, ptab = refs
        else:
            x_ref, a_ref, o_ref, carry, ptab = refs
        t = pl.program_id(1)
        row = lax.broadcasted_iota(jnp.int32, (t_rows, w), 0)
        idx = (t_rows - 1 - row) if reverse else row
        ar = a_ref[:, :w]
        ai = -a_ref[:, w:] if reverse else a_ref[:, w:]
        pows = [(ar, ai)]
        for _ in range(nbits - 1):
            pows.append(_cmul(*pows[-1], *pows[-1]))

        @pl.when(t == 0)
        def _():
            carry[...] = jnp.zeros_like(carry)
            pr, pi = jnp.ones((t_rows, w), F32), jnp.zeros((t_rows, w), F32)
            for kbit in range(nbits):
                bit = ((idx + 1) >> kbit) & 1
                fr = jnp.where(bit == 1, pows[kbit][0], 1.0)
                fi = jnp.where(bit == 1, pows[kbit][1], 0.0)
                pr, pi = _cmul(pr, pi, fr, fi)
            ptab[:, :w] = pr
            ptab[:, w:] = pi

        xr, xi = x_ref[:, :w], x_ref[:, w:]
        step = 1
        kbit = 0
        while step < t_rows:
            shift = (t_rows - step) if reverse else step
            yr, yi = pltpu.roll(xr, shift, 0), pltpu.roll(xi, shift, 0)
            zr, zi = _cmul(pows[kbit][0], pows[kbit][1], yr, yi)
            keep = idx >= step
            xr = xr + jnp.where(keep, zr, 0.0)
            xi = xi + jnp.where(keep, zi, 0.0)
            step *= 2
            kbit += 1
        cr, ci = carry[0:1, :w], carry[0:1, w:]
        zr, zi = _cmul(ptab[:, :w], ptab[:, w:], cr, ci)
        xr, xi = xr + zr, xi + zi
        o_ref[:, :w] = xr
        o_ref[:, w:] = xi
        last = 0 if reverse else t_rows - 1
        carry[0:1, :] = o_ref[last:last + 1, :]
        if reverse:
            halo_r = jnp.where(t < nt - 1, hh_ref[SUBLANE - 1:SUBLANE, :w], 0.0)
            halo_i = jnp.where(t < nt - 1, hh_ref[SUBLANE - 1:SUBLANE, w:], 0.0)
            hr = jnp.where(row == 0, halo_r, pltpu.roll(h_ref[:, :w], 1, 0))
            hi = jnp.where(row == 0, halo_i, pltpu.roll(h_ref[:, w:], 1, 0))
            _acc(da_ref.at[:, :w], t, _colsum(xr * hr + xi * hi))
            _acc(da_ref.at[:, w:], t, _colsum(xi * hr - xr * hi))

    blk = pl.BlockSpec((t_rows, 2 * w), lambda j, t: (tblk(t), j))
    a_spec = pl.BlockSpec((1, 2 * w), lambda j, t: (0, j))
    in_specs, arrays = [blk, a_spec], [x, a]
    out_shape = [jax.ShapeDtypeStruct((s, 2 * N_STATE), F32)]
    out_specs = [blk]
    if reverse:
        in_specs += [blk, pl.BlockSpec((SUBLANE, 2 * w), lambda j, t: (jnp.maximum(tblk(t) * r8 - 1, 0), j))]
        arrays += [h, h]
        out_shape.append(jax.ShapeDtypeStruct((1, 2 * N_STATE), F32))
        out_specs.append(a_spec)
    res = _pcall(
        body, name=name, out_shape=tuple(out_shape), grid=(ncol, nt), in_specs=in_specs, out_specs=tuple(out_specs),
        scratch_shapes=[pltpu.VMEM((SUBLANE, 2 * w), F32), pltpu.VMEM((t_rows, 2 * w), F32)],
        compiler_params=_params(("parallel", "arbitrary")),
    )(*arrays)
    return res if reverse else res[0]


def _s5_glu_bwd_a(dout, dout_off, y, z):
    s = y.shape[0]
    w = S5_WIDTH

    def fn(i, do_ref, y_ref, z_ref, dz_ref, p_ref):
        dov = do_ref[...]
        sg = _sigmoid(z_ref[...])
        dz_ref[...] = (dov * _gelu(y_ref[...]) * sg * (1.0 - sg)).astype(dz_ref.dtype)
        p_ref[...] = dov * sg

    return _rows(fn, name="s5_glu_bwd", s=s, tm=512, ins=[("row", dout, dout_off, w), ("row", y, 0, w), ("row", z, 0, w)],
                 outs=[("row", w, BF16), ("row", w, F32)])


def _s5_dd(dy, proj):
    s = dy.shape[0]
    w = S5_WIDTH

    def fn(i, dy_ref, u_ref, dd_ref):
        _acc(dd_ref, i, _colsum(dy_ref[...] * u_ref[...]))

    return _rows(fn, name="s5_dd", s=s, tm=512, ins=[("row", dy, 0, w), ("row", proj, P_S5, w)], outs=[("acc", 1, w)])


def _s5_fwd(proj, wl, s):
    bu = _mm(proj, wl["s5_bd"], mode="nn", name="s5_bu", m=s, n=2 * N_STATE, k=S5_WIDTH, a_off=P_S5)
    h = _s5_scan(bu, wl["s5_a"], reverse=False, name="s5_scan_fwd")
    def y_epi(acc, u, d):
        yv = acc + d * u
        return yv, _gelu(yv)

    y, yg = _mm(h, wl["s5_cd"], mode="nn", name="s5_y", m=s, n=S5_WIDTH, k=2 * N_STATE, out=(F32, BF16),
                extras=[("tile", proj, P_S5), ("row", wl["s5_d"])], epi=y_epi)
    z, out = _mm(yg, wl["s5_w_glu"], mode="nn", name="s5_glu", m=s, n=S5_WIDTH, k=S5_WIDTH, out=(F32, BF16),
                 extras=[("tile", y, 0)], epi=lambda acc, yv: (acc, _gelu(yv) * _sigmoid(acc)))
    return out, (h, y, z, yg)


def _s5_bwd(dcat, proj, wl, saved, s):
    h, y, z, yg = saved
    dz, p1 = _s5_glu_bwd_a(dcat, 0, y, z)
    dy = _mm(dz, wl["s5_w_glu"], mode="nt", name="s5_dyg", m=s, n=S5_WIDTH, k=S5_WIDTH,
             extras=[("tile", p1, 0), ("tile", y, 0)], epi=lambda acc, p, yv: ((p + acc) * _dgelu(yv),))
    gh = _mm(dy, wl["s5_cd"], mode="nt", name="s5_gh", m=s, n=2 * N_STATE, k=S5_WIDTH)
    adj, da = _s5_scan(gh, wl["s5_a"], reverse=True, h=h, name="s5_scan_bwd")
    du = _mm(adj, wl["s5_bd"], mode="nt", name="s5_du", m=s, n=S5_WIDTH, k=2 * N_STATE,
             extras=[("tile", dy, 0), ("row", wl["s5_d"])], epi=lambda acc, dyv, d: (acc + dyv * d,))
    grads = {
        "s5_a": da,
        "s5_bd": _mm(proj, adj, mode="tn", name="s5_dbd", m=S5_WIDTH, n=2 * N_STATE, k=s, a_off=P_S5),
        "s5_cd": _mm(h, dy, mode="tn", name="s5_dcd", m=2 * N_STATE, n=S5_WIDTH, k=s),
        "s5_d": _s5_dd(dy, proj),
        "s5_w_glu": _mm(yg, dz, mode="tn", name="s5_dwglu", m=S5_WIDTH, n=S5_WIDTH, k=s),
    }
    return du, grads


def _mla_prep(qraw, kvraw, proj, cs):
    s = qraw.shape[0]
    hw = MLA_HEADS * LANE

    def fn(i, q_ref, kv_ref, kr_ref, cs_ref, qn_ref, qr_ref, kvb_ref, krb_ref):
        csv = cs_ref[...]
        qn_ref[...] = q_ref[:, :hw].astype(BF16)
        for hd in range(MLA_HEADS):
            p = q_ref[:, hw + hd * LANE:hw + (hd + 1) * LANE] * csv
            qr_ref[:, hd * LANE:(hd + 1) * LANE] = (p + pltpu.roll(p, LANE // 2, 1)).astype(BF16)
        kvb_ref[...] = kv_ref[...].astype(BF16)
        p = kr_ref[...] * csv
        lane = lax.broadcasted_iota(jnp.int32, p.shape, 1)
        krb_ref[...] = jnp.where(lane < LANE // 2, p + pltpu.roll(p, LANE // 2, 1), 0.0).astype(BF16)

    return _rows(fn, name="mla_prep", s=s, tm=256,
                 ins=[("row", qraw, 0, 2 * hw), ("row", kvraw, 0, 2 * hw), ("row", proj, P_KR, LANE), ("row", cs, 0, LANE)],
                 outs=[("row", hw, BF16), ("row", hw, BF16), ("row", 2 * hw, BF16), ("row", LANE, BF16)])


def _mla_rope_bwd(dqn, dqr2, dkr2h, cs):
    s = dqn.shape[0]
    hw = MLA_HEADS * LANE

    def fn(i, dqn_ref, dqr_ref, dkr_ref, cs_ref, dq_ref, dk_ref):
        csv = cs_ref[...]
        dq_ref[:, :hw] = dqn_ref[...].astype(BF16)
        ksum = jnp.zeros(csv.shape, F32)
        for hd in range(MLA_HEADS):
            g = dqr_ref[:, hd * LANE:(hd + 1) * LANE]
            dq_ref[:, hw + hd * LANE:hw + (hd + 1) * LANE] = ((g + pltpu.roll(g, LANE // 2, 1)) * csv).astype(BF16)
            ksum = ksum + dkr_ref[:, hd * LANE:(hd + 1) * LANE]
        dk_ref[...] = ksum * csv

    return _rows(fn, name="mla_rope_bwd", s=s, tm=256,
                 ins=[("row", dqn, 0, hw), ("row", dqr2, 0, hw), ("row", dkr2h, 0, hw), ("row", cs, 0, LANE)],
                 outs=[("row", 2 * hw, BF16), ("row", LANE, F32)])


def _lanes(a_ref, b_ref):
    return jnp.concatenate([a_ref[...], b_ref[...]], axis=1)


def _attn_scores(qn_ref, qr_ref, kn_ref, kr_ref, qi, ki, tq, tk):
    scale = (MLA_NOPE + MLA_ROPE) ** -0.5
    sc = _dot(_lanes(qn_ref, qr_ref), _lanes(kn_ref, kr_ref), NT) * scale
    assert tq == tk
    return sc, lax.broadcasted_iota(jnp.int32, (tq, tk), 1) <= lax.broadcasted_iota(jnp.int32, (tq, tk), 0)


def _attn_specs(tq, tk, q_of, k_of):
    qs = pl.BlockSpec((tq, LANE), lambda h, a, b: (q_of(a, b), h))
    return [qs, qs,
            pl.BlockSpec((tk, LANE), lambda h, a, b: (k_of(a, b), 2 * h)),
            pl.BlockSpec((tk, LANE), lambda h, a, b: (k_of(a, b), 2 * h + 1)),
            pl.BlockSpec((tk, LANE), lambda h, a, b: (k_of(a, b), 0))]


def _flash_fwd(qn, qr2, kv, kr2):
    s = qn.shape[0]
    tq = tk = min(512, s)
    nq = s // tq

    def body(qn_ref, qr_ref, kn_ref, v_ref, kr_ref, o_ref, lse_ref, m_sc, l_sc, acc_sc):
        qi, ki = pl.program_id(1), pl.program_id(2)

        @pl.when(ki == 0)
        def _():
            m_sc[...] = jnp.full(m_sc.shape, -jnp.inf, F32)
            l_sc[...] = jnp.zeros_like(l_sc)
            acc_sc[...] = jnp.zeros_like(acc_sc)

        def step(diagonal):
            sc, causal = _attn_scores(qn_ref, qr_ref, kn_ref, kr_ref, qi, ki, tq, tk)
            if diagonal:
                sc = jnp.where(causal, sc, -1e30)
            m_new = jnp.maximum(m_sc[...], jnp.max(sc, axis=1, keepdims=True))
            alpha = jnp.exp(m_sc[...] - m_new)
            p = jnp.exp(sc - m_new)
            l_sc[...] = alpha * l_sc[...] + jnp.sum(p, axis=1, keepdims=True)
            acc_sc[...] = alpha * acc_sc[...] + _dot(p, v_ref[...], NN)
            m_sc[...] = m_new

        @pl.when(ki < qi)
        def _():
            step(False)

        @pl.when(ki == qi)
        def _():
            step(True)
            o_ref[...] = acc_sc[...] / l_sc[...]
            lse_ref[0] = m_sc[...] + jnp.log(l_sc[...])

    return _pcall(
        body, name="mla_flash_fwd",
        out_shape=(jax.ShapeDtypeStruct((s, MLA_WIDTH), F32), jax.ShapeDtypeStruct((MLA_HEADS, s, 1), F32)),
        grid=(MLA_HEADS, nq, nq),
        in_specs=_attn_specs(tq, tk, lambda a, b: a, lambda a, b: jnp.minimum(a, b)),
        out_specs=(pl.BlockSpec((tq, LANE), lambda h, a, b: (a, h)), pl.BlockSpec((1, tq, 1), lambda h, a, b: (h, a, 0))),
        scratch_shapes=[pltpu.VMEM((tq, 1), F32), pltpu.VMEM((tq, 1), F32), pltpu.VMEM((tq, LANE), F32)],
        compiler_params=_params(("parallel", "parallel", "arbitrary")),
    )(qn, qr2, kv, kv, kr2)


def _flash_bwd_dq(qn, qr2, kv, kr2, do, do_off, o, lse):
    s = qn.shape[0]
    tq = tk = min(512, s)
    nq = s // tq
    scale = (MLA_NOPE + MLA_ROPE) ** -0.5
    ob = do_off // LANE

    def body(qn_ref, qr_ref, kn_ref, v_ref, kr_ref, do_ref, o_ref, lse_ref, dqn_ref, dqr_ref, dl_ref, dl_sc, aq_sc):
        qi, ki = pl.program_id(1), pl.program_id(2)

        @pl.when(ki == 0)
        def _():
            dl_sc[...] = jnp.sum(do_ref[...] * o_ref[...], axis=1, keepdims=True)
            aq_sc[...] = jnp.zeros_like(aq_sc)

        def step(diagonal):
            sc, causal = _attn_scores(qn_ref, qr_ref, kn_ref, kr_ref, qi, ki, tq, tk)
            p = jnp.exp(sc - lse_ref[0])
            if diagonal:
                p = jnp.where(causal, p, 0.0)
            dp = _dot(do_ref[...], v_ref[...], NT)
            ds = (p * (dp - dl_sc[...]) * scale).astype(BF16)
            aq_sc[...] += _dot(ds, _lanes(kn_ref, kr_ref), NN)

        @pl.when(ki < qi)
        def _():
            step(False)

        @pl.when(ki == qi)
        def _():
            step(True)
            dqn_ref[...] = aq_sc[:, :LANE]
            dqr_ref[...] = aq_sc[:, LANE:]
            dl_ref[0] = dl_sc[...]

    qblk = pl.BlockSpec((tq, LANE), lambda h, a, b: (a, h))
    vec = pl.BlockSpec((1, tq, 1), lambda h, a, b: (h, a, 0))
    return _pcall(
        body, name="mla_flash_dq",
        out_shape=(jax.ShapeDtypeStruct((s, MLA_WIDTH), F32), jax.ShapeDtypeStruct((s, MLA_WIDTH), F32),
                   jax.ShapeDtypeStruct((MLA_HEADS, s, 1), F32)),
        grid=(MLA_HEADS, nq, nq),
        in_specs=_attn_specs(tq, tk, lambda a, b: a, lambda a, b: jnp.minimum(a, b))
        + [pl.BlockSpec((tq, LANE), lambda h, a, b: (a, h + ob)), qblk, vec],
        out_specs=(qblk, qblk, vec),
        scratch_shapes=[pltpu.VMEM((tq, 1), F32), pltpu.VMEM((tq, 2 * LANE), F32)],
        compiler_params=_params(("parallel", "parallel", "arbitrary")),
    )(qn, qr2, kv, kv, kr2, do, o, lse)


def _flash_bwd_dkv(qn, qr2, kv, kr2, do, do_off, lse, delta):
    s = qn.shape[0]
    tq = tk = min(512, s)
    nq = s // tq
    scale = (MLA_NOPE + MLA_ROPE) ** -0.5
    ob = do_off // LANE

    def body(qn_ref, qr_ref, kn_ref, v_ref, kr_ref, do_ref, lse_ref, dl_ref, dkv_ref, dkr_ref, ak_sc, av_sc):
        ki, qi = pl.program_id(1), pl.program_id(2)

        @pl.when(qi == 0)
        def _():
            ak_sc[...] = jnp.zeros_like(ak_sc)
            av_sc[...] = jnp.zeros_like(av_sc)

        def step(diagonal):
            sc, causal = _attn_scores(qn_ref, qr_ref, kn_ref, kr_ref, qi, ki, tq, tk)
            p = jnp.exp(sc - lse_ref[0])
            if diagonal:
                p = jnp.where(causal, p, 0.0)
            dp = _dot(do_ref[...], v_ref[...], NT)
            ds = (p * (dp - dl_ref[0]) * scale).astype(BF16)
            av_sc[...] += _dot(p, do_ref[...], TN)
            ak_sc[...] += _dot(ds, _lanes(qn_ref, qr_ref), TN)

        @pl.when(qi > ki)
        def _():
            step(False)

        @pl.when(qi == ki)
        def _():
            step(True)

        @pl.when(qi == nq - 1)
        def _():
            dkv_ref[:, :LANE] = ak_sc[:, :LANE]
            dkv_ref[:, LANE:] = av_sc[...]
            dkr_ref[...] = ak_sc[:, LANE:]

    q_of = lambda a, b: jnp.maximum(a, b)
    k_of = lambda a, b: a
    vec = pl.BlockSpec((1, tq, 1), lambda h, a, b: (h, q_of(a, b), 0))
    return _pcall(
        body, name="mla_flash_dkv",
        out_shape=(jax.ShapeDtypeStruct((s, 2 * MLA_WIDTH), F32), jax.ShapeDtypeStruct((s, MLA_WIDTH), F32)),
        grid=(MLA_HEADS, nq, nq),
        in_specs=_attn_specs(tq, tk, q_of, k_of)
        + [pl.BlockSpec((tq, LANE), lambda h, a, b: (q_of(a, b), h + ob)), vec, vec],
        out_specs=(pl.BlockSpec((tk, 2 * LANE), lambda h, a, b: (a, h)), pl.BlockSpec((tk, LANE), lambda h, a, b: (a, h))),
        scratch_shapes=[pltpu.VMEM((tk, 2 * LANE), F32), pltpu.VMEM((tk, LANE), F32)],
        compiler_params=_params(("parallel", "parallel", "arbitrary")),
    )(qn, qr2, kv, kv, kr2, do, lse, delta)


def _mla_fwd(proj, wl, cs, s):
    cqn = _rms_fwd(proj, P_CQ, MLA_Q_RANK, wl["mla_q_norm"], "mla_q_rms")
    ckvn = _rms_fwd(proj, P_CKV, MLA_KV_RANK, wl["mla_kv_norm"], "mla_kv_rms")
    qraw = _mm(cqn, wl["mla_wq"], mode="nn", name="mla_q_proj", m=s, n=2 * MLA_WIDTH, k=MLA_Q_RANK)
    kvraw = _mm(ckvn, wl["mla_w_ukv"], mode="nn", name="mla_kv_proj", m=s, n=2 * MLA_WIDTH, k=MLA_KV_RANK)
    qn, qr2, kv, kr2 = _mla_prep(qraw, kvraw, proj, cs)
    o, lse = _flash_fwd(qn, qr2, kv, kr2)
    return o, (cqn, ckvn, qn, qr2, kv, kr2, o, lse)


def _mla_bwd(dcat, proj, wl, cs, saved, s):
    cqn, ckvn, qn, qr2, kv, kr2, o, lse = saved
    dqn, dqr2, delta = _flash_bwd_dq(qn, qr2, kv, kr2, dcat, S5_WIDTH, o, lse)
    dkv, dkr2h = _flash_bwd_dkv(qn, qr2, kv, kr2, dcat, S5_WIDTH, lse, delta)
    dqraw, dkr = _mla_rope_bwd(dqn, dqr2, dkr2h, cs)
    dcqn = _mm(dqraw, wl["mla_wq"], mode="nt", name="mla_dcqn", m=s, n=MLA_Q_RANK, k=2 * MLA_WIDTH)
    dckvn = _mm(dkv, wl["mla_w_ukv"], mode="nt", name="mla_dckvn", m=s, n=MLA_KV_RANK, k=2 * MLA_WIDTH)
    dcq, dqg = _rms_bwd(dcqn, proj, P_CQ, MLA_Q_RANK, wl["mla_q_norm"], "mla_q_rms_bwd")
    dckv, dkvg = _rms_bwd(dckvn, proj, P_CKV, MLA_KV_RANK, wl["mla_kv_norm"], "mla_kv_rms_bwd")
    grads = {
        "mla_wq": _mm(cqn, dqraw, mode="tn", name="mla_dwq", m=MLA_Q_RANK, n=2 * MLA_WIDTH, k=s),
        "mla_w_ukv": _mm(ckvn, dkv, mode="tn", name="mla_dwukv", m=MLA_KV_RANK, n=2 * MLA_WIDTH, k=s),
        "mla_q_norm": dqg,
        "mla_kv_norm": dkvg,
    }
    return dcq, dckv, dkr, grads


HG_ROWS = 256


def _cumsum_rows(x, row, reverse=False):
    n = x.shape[0]
    step = 1
    while step < n:
        if reverse:
            x = x + jnp.where(row < n - step, pltpu.roll(x, n - step, 0), 0.0)
        else:
            x = x + jnp.where(row >= step, pltpu.roll(x, step, 0), 0.0)
        step *= 2
    return x


def _hg_gates(hq, z, lb):
    sig = _sigmoid(z)
    sigm = _sigmoid(-z)
    f = lb + (1.0 - lb) * sig
    return _silu(hq), (1.0 - lb) * sigm, jnp.log(f), sig, sigm, f


HG_SUB = 16
HG_NSUB = HG_CHUNK // HG_SUB


def _hg_offdiag(q, k, b, row, b_buf):
    ops = []
    for j in range(HG_NSUB - 1):
        e = (j + 1) * HG_SUB
        be = b_buf[e - 1:e, :]
        fj = jnp.where(row >= e, jnp.exp(jnp.minimum(b - be, 0.0)), 0.0)
        gj = jnp.where((row >= e - HG_SUB) & (row < e), jnp.exp(jnp.minimum(be - b, 0.0)), 0.0)
        ops.append((q * fj, k * gj, fj, gj))
    return ops


def _hg_diag_weights(qb, bb, ks, bs, rr, srow):
    e = jnp.exp(jnp.minimum(bb - bs, 0.0))
    keep = rr >= srow
    w = jnp.where(keep, jnp.sum(qb * ks * e, axis=1, keepdims=True), 0.0)
    return e, keep, w


def _hg_specs(rows):
    def col(off):
        return pl.BlockSpec((rows, LANE), lambda h, n, off=off: (n, off // LANE + h))
    return col


def _hg_fwd(proj, lb, gamma):
    s = proj.shape[0]
    rows = min(HG_ROWS, s)
    c = HG_CHUNK
    npc = rows // c
    nb = s // rows

    def body(hq_ref, hf_ref, hi_ref, hg_ref, lb_ref, gm_ref, y_ref, o_ref, st_ref, st_sc, k_buf, b_buf):
        n = pl.program_id(1)

        @pl.when(n == 0)
        def _():
            st_sc[...] = jnp.zeros_like(st_sc)

        row = lax.broadcasted_iota(jnp.int32, (c, LANE), 0)
        rr = lax.broadcasted_iota(jnp.int32, (HG_SUB, 1), 0)
        lbv = lb_ref[...]
        for ch in range(npc):
            sl = slice(ch * c, (ch + 1) * c)
            q, k, logf, _, _, _ = _hg_gates(hq_ref[sl, :], hf_ref[sl, :], lbv)
            v = hi_ref[sl, :]
            b = _cumsum_rows(logf, row)
            bl = _colsum(logf)
            k_buf[...] = k
            b_buf[...] = b
            st = st_sc[...]
            st_ref[0, ch] = st
            a_off = sum(_dot(qf, kg, NT) for qf, kg, _, _ in _hg_offdiag(q, k, b, row, b_buf))
            o = _dot(q * jnp.exp(b), st, NT) + _dot(a_off, v, NN)
            st_sc[...] = st * jnp.exp(bl) + _dot(v, k * jnp.exp(bl - b), TN)
            diag = []
            for i in range(HG_NSUB):
                r0 = i * HG_SUB
                qb, bb = q[r0:r0 + HG_SUB], b[r0:r0 + HG_SUB]
                acc = jnp.zeros((HG_SUB, LANE), F32)
                for srow in range(HG_SUB):
                    t = r0 + srow
                    _, _, w = _hg_diag_weights(qb, bb, k_buf[t:t + 1, :], b_buf[t:t + 1, :], rr, srow)
                    acc = acc + w * hi_ref[ch * c + t:ch * c + t + 1, :]
                diag.append(acc)
            o = o + jnp.concatenate(diag, axis=0)
            o_ref[sl, :] = o
            r = lax.rsqrt(_rowmean(o * o) + EPS)
            y_ref[sl, :] = (((o * r) * gm_ref[...]) * _silu(hg_ref[sl, :])).astype(y_ref.dtype)

    col = _hg_specs(rows)
    out_blk = pl.BlockSpec((rows, LANE), lambda h, n: (n, h))
    return _pcall(
        body, name="hgrn_fwd",
        out_shape=(jax.ShapeDtypeStruct((s, HG_WIDTH), BF16), jax.ShapeDtypeStruct((s, HG_WIDTH), F32),
                   jax.ShapeDtypeStruct((HG_HEADS, s // c, HG_DV, HG_DK), F32)),
        grid=(HG_HEADS, nb),
        in_specs=[col(P_HQ), col(P_HF), col(P_HI), col(P_HG),
                  pl.BlockSpec((1, LANE), lambda h, n: (0, h)), pl.BlockSpec((1, LANE), lambda h, n: (0, 0))],
        out_specs=(out_blk, out_blk, pl.BlockSpec((1, npc, HG_DV, HG_DK), lambda h, n: (h, n, 0, 0))),
        scratch_shapes=[pltpu.VMEM((HG_DV, HG_DK), F32), pltpu.VMEM((c, LANE), F32), pltpu.VMEM((c, LANE), F32)],
        compiler_params=_params(("parallel", "arbitrary")),
    )(proj, proj, proj, proj, lb, gamma)


def _hg_bwd(dcat, dy_off, proj, lb, gamma, o_saved, states):
    s = proj.shape[0]
    rows = min(HG_ROWS, s)
    c = HG_CHUNK
    npc = rows // c
    nb = s // rows
    yb = dy_off // LANE

    def body(hq_ref, hf_ref, hi_ref, hg_ref, lb_ref, gm_ref, dy_ref, o_ref, st_ref,
             dq_ref, df_ref, di_ref, dg_ref, dlb_ref, dgm_ref, dst_sc, k_buf, b_buf, dk_buf, dv_buf):
        n = pl.program_id(1)

        @pl.when(n == 0)
        def _():
            dst_sc[...] = jnp.zeros_like(dst_sc)
            dlb_ref[...] = jnp.zeros_like(dlb_ref)
            dgm_ref[...] = jnp.zeros_like(dgm_ref)

        row = lax.broadcasted_iota(jnp.int32, (c, LANE), 0)
        rr = lax.broadcasted_iota(jnp.int32, (HG_SUB, 1), 0)
        lbv, gmv = lb_ref[...], gm_ref[...]
        for ch in reversed(range(npc)):
            sl = slice(ch * c, (ch + 1) * c)
            hq, z, v, g = hq_ref[sl, :], hf_ref[sl, :], hi_ref[sl, :], hg_ref[sl, :]
            q, k, logf, sig, sigm, f = _hg_gates(hq, z, lbv)
            b = _cumsum_rows(logf, row)
            bl = _colsum(logf)
            k_buf[...] = k
            b_buf[...] = b
            eb, ebl = jnp.exp(b), jnp.exp(bl)
            qe, kl = q * eb, k * jnp.exp(bl - b)
            st = st_ref[0, ch]
            dst = dst_sc[...]
            o, dyv = o_ref[sl, :], dy_ref[sl, :]
            r = lax.rsqrt(_rowmean(o * o) + EPS)
            on = o * r
            sg = _silu(g)
            dgm_ref[0] += _colsum(dyv * on * sg)
            dg_ref[sl, :] = dyv * on * gmv * _dsilu(g)
            go = dyv * gmv * sg
            do = r * (go - on * _rowmean(go * on))
            dq = _dot(do, st, NN) * eb
            dkl = _dot(v, dst, NN)
            dk = dkl * jnp.exp(bl - b)
            dv = _dot(kl, dst, NT)
            dbl = _colsum(dst * st) * ebl + _colsum(dkl * kl)
            dst_sc[...] = dst * ebl + _dot(do, qe, TN)
            ops = _hg_offdiag(q, k, b, row, b_buf)
            da = _dot(do, v, NT)
            a_off = sum(_dot(qf, kg, NT) for qf, kg, _, _ in ops)
            dv = dv + _dot(a_off, do, TN)
            for qf, kg, fj, gj in ops:
                dq = dq + _dot(da, kg, NN) * fj
                dk = dk + _dot(da, qf, TN) * gj
            dq_diag = []
            for i in range(HG_NSUB):
                r0 = i * HG_SUB
                qb, bb, dob = q[r0:r0 + HG_SUB], b[r0:r0 + HG_SUB], do[r0:r0 + HG_SUB]
                acc = jnp.zeros((HG_SUB, LANE), F32)
                for srow in range(HG_SUB):
                    t = r0 + srow
                    ks = k_buf[t:t + 1, :]
                    e, keep, w = _hg_diag_weights(qb, bb, ks, b_buf[t:t + 1, :], rr, srow)
                    dw = jnp.where(keep, jnp.sum(dob * hi_ref[ch * c + t:ch * c + t + 1, :], axis=1, keepdims=True), 0.0)
                    dv_buf[t:t + 1, :] = _colsum(w * dob)
                    dk_buf[t:t + 1, :] = _colsum(dw * qb * e)
                    acc = acc + dw * (ks * e)
                dq_diag.append(acc)
            dq = dq + jnp.concatenate(dq_diag, axis=0)
            dk = dk + dk_buf[...]
            di_ref[sl, :] = dv + dv_buf[...]
            db = q * dq - k * dk + jnp.where(row == c - 1, dbl, 0.0)
            dlogf = _cumsum_rows(db, row, reverse=True)
            dq_ref[sl, :] = dq * _dsilu(hq)
            s1 = sig * (1.0 - sig) * (1.0 - lbv)
            df_ref[sl, :] = dlogf * s1 / f - dk * s1
            dlb_ref[0] += _colsum(dlogf * sigm / f - dk * sigm)

    def col(off):
        return pl.BlockSpec((rows, LANE), lambda h, n, off=off: (nb - 1 - n, off // LANE + h))

    out_blk = pl.BlockSpec((rows, LANE), lambda h, n: (nb - 1 - n, h))
    acc_blk = pl.BlockSpec((1, 1, LANE), lambda h, n: (h, 0, 0))
    res = _pcall(
        body, name="hgrn_bwd",
        out_shape=tuple(jax.ShapeDtypeStruct((s, HG_WIDTH), F32) for _ in range(4))
        + (jax.ShapeDtypeStruct((HG_HEADS, 1, LANE), F32), jax.ShapeDtypeStruct((HG_HEADS, 1, LANE), F32)),
        grid=(HG_HEADS, nb),
        in_specs=[col(P_HQ), col(P_HF), col(P_HI), col(P_HG),
                  pl.BlockSpec((1, LANE), lambda h, n: (0, h)), pl.BlockSpec((1, LANE), lambda h, n: (0, 0)),
                  pl.BlockSpec((rows, LANE), lambda h, n: (nb - 1 - n, yb + h)), out_blk,
                  pl.BlockSpec((1, npc, HG_DV, HG_DK), lambda h, n: (h, nb - 1 - n, 0, 0))],
        out_specs=(out_blk, out_blk, out_blk, out_blk, acc_blk, acc_blk),
        scratch_shapes=[pltpu.VMEM((HG_DV, HG_DK), F32)] + [pltpu.VMEM((c, LANE), F32)] * 4,
        compiler_params=_params(("parallel", "arbitrary")),
    )(proj, proj, proj, proj, lb, gamma, dcat, o_saved, states)
    dq, df, di, dg, dlb, dgm = res
    return dq, df, di, dg, dlb.reshape(1, HG_WIDTH), dgm.reshape(HG_HEADS, LANE)


def _adamw_math(w, g, m, v):
    m = ADAM_B1 * m + (1.0 - ADAM_B1) * g
    v = ADAM_B2 * v + (1.0 - ADAM_B2) * (g * g)
    m_hat = m / (1.0 - ADAM_B1 ** ADAM_STEP)
    v_hat = v / (1.0 - ADAM_B2 ** ADAM_STEP)
    delta = -ADAM_LR * (m_hat / (jnp.sqrt(v_hat) + ADAM_EPS) + ADAM_WD * w)
    return delta, m, v


def _adamw(w, g, m, v, name):
    shape = w.shape
    width = shape[-1]
    rows = int(np.prod(shape[:-1]))
    tm = rows
    while tm * width * 4 > (1 << 20) and tm % 16 == 0:
        tm //= 2

    def fn(i, w_ref, g_ref, m_ref, v_ref, d_ref, mo_ref, vo_ref):
        d, mn, vn = _adamw_math(w_ref[...], g_ref[...], m_ref[...], v_ref[...])
        d_ref[...] = d
        mo_ref[...] = mn
        vo_ref[...] = vn

    v2 = lambda t: t.reshape(rows, width)
    res = _rows(fn, name=name, s=rows, tm=tm, ins=[("row", v2(t), 0, width) for t in (w, g, m, v)],
                outs=[("row", width, F32)] * 3)
    return tuple(r.reshape(shape) for r in res)


def _ada_grad_adamw(cact_all, dmod_cols, w, m, v):
    n_layers, kdim, n = w.shape
    tm, tn = _tile(kdim, 256), _tile(n, 1024)

    def body(c_ref, d_ref, w_ref, m_ref, v_ref, g_ref, dl_ref, mo_ref, vo_ref):
        g = _dot(c_ref[...], d_ref[...], TN)
        d, mn, vn = _adamw_math(w_ref[...], g, m_ref[...], v_ref[...])
        g_ref[...] = g
        dl_ref[...] = d
        mo_ref[...] = mn
        vo_ref[...] = vn

    blk = pl.BlockSpec((None, tm, tn), lambda l, i, j: (l, i, j))
    return _pcall(
        body, name="ada_grad_adamw", out_shape=tuple(jax.ShapeDtypeStruct(w.shape, F32) for _ in range(4)),
        grid=(n_layers, kdim // tm, n // tn),
        in_specs=[pl.BlockSpec((cact_all.shape[0], tm), lambda l, i, j: (0, i)),
                  pl.BlockSpec((None, dmod_cols.shape[1], tn), lambda l, i, j: (l, 0, j)), blk, blk, blk],
        out_specs=(blk, blk, blk, blk),
        compiler_params=_params(("parallel", "parallel", "parallel")),
    )(cact_all, dmod_cols, w, m, v)


def _me():
    return lax.axis_index("x"), lax.axis_index("y"), lax.axis_index("c")


def _flip(k):
    x, y, c = _me()
    return (x ^ ((k >> 2) & 1), y ^ ((k >> 1) & 1), c ^ (k & 1))


def _lin(dev):
    return 4 * dev[0] + 2 * dev[1] + dev[2]


ANY = pl.BlockSpec(memory_space=pl.ANY)


def _all_gather8(x, name):
    def body(x_ref, out_ref, send_sems, recv_sems, local_sem):
        me = _lin(_me())
        mine = pltpu.make_async_copy(x_ref, out_ref.at[me], local_sem)
        mine.start()
        copies = []
        for k in range(1, 8):
            cp = pltpu.make_async_remote_copy(src_ref=x_ref, dst_ref=out_ref.at[me], send_sem=send_sems.at[k - 1],
                                              recv_sem=recv_sems.at[k - 1], device_id=_flip(k), device_id_type=MESH)
            cp.start()
            copies.append(cp)
        for k in range(1, 8):
            pltpu.make_async_remote_copy(src_ref=x_ref, dst_ref=out_ref.at[_lin(_flip(k))], send_sem=send_sems.at[k - 1],
                                         recv_sem=recv_sems.at[k - 1], device_id=_flip(k), device_id_type=MESH).wait_recv()
        for cp in copies:
            cp.wait_send()
        mine.wait()

    return _pcall(
        body, name=name, out_shape=jax.ShapeDtypeStruct((8,) + x.shape, x.dtype),
        in_specs=[ANY], out_specs=ANY,
        scratch_shapes=[pltpu.SemaphoreType.DMA((7,)), pltpu.SemaphoreType.DMA((7,)), pltpu.SemaphoreType.DMA],
    )(x)


CHIP_FLIPS = (2, 4, 6)


def _row_tile(r, cdim):
    best = SUBLANE
    for t in range(SUBLANE, r + 1, SUBLANE):
        if r % t == 0 and t * cdim * 4 <= (3 << 20):
            best = t
    assert r % best == 0
    return best


def _gather_weights(ws):
    n = len(ws)
    hbm = pltpu.MemorySpace.HBM
    w_refs = [jax.new_ref(w, memory_space=hbm) for w in ws]
    out_refs = [jax.empty_ref(jax.ShapeDtypeStruct((N_CHIPS,) + w.shape, w.dtype), memory_space=hbm) for w in ws]

    @pl.kernel(mesh=plsc.ScalarSubcoreMesh(axis_name="sequencer", num_cores=1), name="gather_weights",
               scratch_types=(pltpu.SemaphoreType.DMA((6 * n,)), pltpu.SemaphoreType.DMA((6 * n,))),
               compiler_params=pltpu.CompilerParams(collective_id=2))
    def launch(send_sems, recv_sems):
        barrier = pltpu.get_barrier_semaphore()
        for k in CHIP_FLIPS + (1,):
            pl.semaphore_signal(barrier, inc=1, device_id=_flip(k), device_id_type=MESH)
        pl.semaphore_wait(barrier, len(CHIP_FLIPS) + 1)
        x, y, c = _me()
        sib = _flip(1)

        def slot(a, dev, half):
            return out_refs[a].at[2 * dev[0] + dev[1], half]

        first = []
        for a in range(n):
            for j, k in enumerate(CHIP_FLIPS):
                cp = pltpu.make_async_remote_copy(src_ref=w_refs[a].at[c], dst_ref=slot(a, (x, y), c), send_sem=send_sems.at[6 * a + j],
                                                  recv_sem=recv_sems.at[6 * a + j], device_id=_flip(k), device_id_type=MESH)
                cp.start()
                first.append(cp)
        passed = []
        for a in range(n):
            for j, k in enumerate(CHIP_FLIPS):
                src = _flip(k)
                landed = slot(a, src, c)
                pltpu.make_async_remote_copy(src_ref=landed, dst_ref=landed, send_sem=send_sems.at[6 * a + j],
                                             recv_sem=recv_sems.at[6 * a + j], device_id=src, device_id_type=MESH).wait_recv()
                cp = pltpu.make_async_remote_copy(src_ref=landed, dst_ref=landed, send_sem=send_sems.at[6 * a + 3 + j],
                                                  recv_sem=recv_sems.at[6 * a + 3 + j], device_id=sib, device_id_type=MESH)
                cp.start()
                passed.append(cp)
        for a in range(n):
            for j, k in enumerate(CHIP_FLIPS):
                got = slot(a, _flip(k), 1 - c)
                pltpu.make_async_remote_copy(src_ref=got, dst_ref=got, send_sem=send_sems.at[6 * a + 3 + j],
                                             recv_sem=recv_sems.at[6 * a + 3 + j], device_id=sib, device_id_type=MESH).wait_recv()
        for cp in first + passed:
            cp.wait_send()

    launch()
    chip = 2 * lax.axis_index("x") + lax.axis_index("y")
    return [lax.dynamic_update_index_in_dim(r[...], w, chip, axis=0) for r, w in zip(out_refs, ws)]


def _sibling_halves(gs):
    n = len(gs)

    def body(*refs):
        s_refs, out_refs = refs[:n], refs[n:2 * n]
        send_sems, recv_sems = refs[2 * n:]
        c = lax.axis_index("c")
        sib = _flip(1)
        copies = []
        for a in range(n):
            for j in range(N_CHIPS):
                cp = pltpu.make_async_remote_copy(src_ref=s_refs[a].at[j, 1 - c], dst_ref=out_refs[a].at[j], send_sem=send_sems.at[4 * a + j],
                                                  recv_sem=recv_sems.at[4 * a + j], device_id=sib, device_id_type=MESH)
                cp.start()
                copies.append(cp)
        for cp in copies:
            cp.wait()

    return _pcall(
        body, name="rs_sibling_halves", out_shape=tuple(jax.ShapeDtypeStruct((N_CHIPS,) + g.shape[2:], g.dtype) for g in gs),
        in_specs=[ANY] * n, out_specs=(ANY,) * n,
        scratch_shapes=[pltpu.SemaphoreType.DMA((4 * n,)), pltpu.SemaphoreType.DMA((4 * n,))],
    )(*gs)


def _scatter_to_chips(parts):
    n = len(parts)
    hbm = pltpu.MemorySpace.HBM
    p_refs = [jax.new_ref(p, memory_space=hbm) for p in parts]
    out_refs = [jax.empty_ref(jax.ShapeDtypeStruct((3,) + p.shape[1:], p.dtype), memory_space=hbm) for p in parts]

    @pl.kernel(mesh=plsc.ScalarSubcoreMesh(axis_name="sequencer", num_cores=1), name="scatter_to_chips",
               scratch_types=(pltpu.SemaphoreType.DMA((3 * n,)), pltpu.SemaphoreType.DMA((3 * n,))),
               compiler_params=pltpu.CompilerParams(collective_id=1))
    def launch(send_sems, recv_sems):
        barrier = pltpu.get_barrier_semaphore()
        for k in CHIP_FLIPS:
            pl.semaphore_signal(barrier, inc=1, device_id=_flip(k), device_id_type=MESH)
        pl.semaphore_wait(barrier, len(CHIP_FLIPS))
        copies = []
        for a in range(n):
            for j, k in enumerate(CHIP_FLIPS):
                to = _flip(k)
                cp = pltpu.make_async_remote_copy(src_ref=p_refs[a].at[2 * to[0] + to[1]], dst_ref=out_refs[a].at[j],
                                                  send_sem=send_sems.at[3 * a + j], recv_sem=recv_sems.at[3 * a + j],
                                                  device_id=to, device_id_type=MESH)
                cp.start()
                copies.append(cp)
        for cp in copies:
            cp.wait()

    launch()
    return [r[...] for r in out_refs]


def _sibling_result(halves):
    n = len(halves)

    def body(*refs):
        h_refs, out_refs = refs[:n], refs[n:2 * n]
        send_sems, recv_sems = refs[2 * n:]
        sib = _flip(1)
        copies = []
        for a in range(n):
            cp = pltpu.make_async_remote_copy(src_ref=h_refs[a], dst_ref=out_refs[a], send_sem=send_sems.at[a],
                                              recv_sem=recv_sems.at[a], device_id=sib, device_id_type=MESH)
            cp.start()
            copies.append(cp)
        for cp in copies:
            cp.wait()

    theirs = _pcall(
        body, name="rs_sibling_result", out_shape=tuple(jax.ShapeDtypeStruct(h.shape, h.dtype) for h in halves),
        in_specs=[ANY] * n, out_specs=(ANY,) * n,
        scratch_shapes=[pltpu.SemaphoreType.DMA((n,)), pltpu.SemaphoreType.DMA((n,))],
    )(*halves)
    c = lax.axis_index("c")
    return [jnp.where(c == 0, jnp.stack([m, t]), jnp.stack([t, m])) for m, t in zip(halves, theirs)]


def _add_halves(g, r1, name):
    n, _, r, cdim = g.shape
    tm = _row_tile(r, cdim)

    def body(c_ref, g_ref, r_ref, o_ref):
        o_ref[...] = g_ref[...] + r_ref[...]

    return _pcall(
        body, name=name, out_shape=jax.ShapeDtypeStruct((n, r, cdim), g.dtype),
        grid_spec=pltpu.PrefetchScalarGridSpec(
            num_scalar_prefetch=1, grid=(n, r // tm),
            in_specs=[pl.BlockSpec((None, None, tm, cdim), lambda j, i, c_ref: (j, c_ref[0], i, 0)),
                      pl.BlockSpec((None, tm, cdim), lambda j, i, c_ref: (j, i, 0))],
            out_specs=pl.BlockSpec((None, tm, cdim), lambda j, i, c_ref: (j, i, 0))),
        compiler_params=_params(("parallel", "parallel")),
    )(lax.axis_index("c").astype(jnp.int32).reshape(1), g, r1)


def _add_chips(part, got, name):
    _, r, cdim = part.shape
    tm = _row_tile(r, cdim)

    def body(chip_ref, p_ref, g_ref, o_ref):
        o_ref[...] = ((p_ref[...] + g_ref[0]) + g_ref[1]) + g_ref[2]

    chip = (2 * lax.axis_index("x") + lax.axis_index("y")).astype(jnp.int32).reshape(1)
    return _pcall(
        body, name=name, out_shape=jax.ShapeDtypeStruct((r, cdim), part.dtype),
        grid_spec=pltpu.PrefetchScalarGridSpec(
            num_scalar_prefetch=1, grid=(r // tm,),
            in_specs=[pl.BlockSpec((None, tm, cdim), lambda i, chip_ref: (chip_ref[0], i, 0)),
                      pl.BlockSpec((3, tm, cdim), lambda i, chip_ref: (0, i, 0))],
            out_specs=pl.BlockSpec((tm, cdim), lambda i, chip_ref: (i, 0))),
        compiler_params=_params(("parallel",)),
    )(chip, part, got)


def _reduce_scatter_start(gs, names):
    r1 = _sibling_halves(gs)
    parts = [_add_halves(g, r, "add_halves_" + nm) for g, r, nm in zip(gs, r1, names)]
    return parts, _scatter_to_chips(parts), names


def _reduce_scatter_finish(started):
    parts, got, names = started
    mine = [_add_chips(p, q, "add_chips_" + nm) for p, q, nm in zip(parts, got, names)]
    return _sibling_result(mine)


def _sum8(x):
    _, r, n = x.shape
    tm = 128 if r % 128 == 0 else r

    def body(x_ref, o_ref):
        acc = x_ref[0]
        for d in range(1, 8):
            acc = acc + x_ref[d]
        o_ref[...] = acc

    return _pcall(body, name="sum8", out_shape=jax.ShapeDtypeStruct((r, n), x.dtype), grid=(r // tm,),
                  in_specs=[pl.BlockSpec((8, tm, n), lambda i: (0, i, 0))], out_specs=pl.BlockSpec((tm, n), lambda i: (i, 0)),
                  compiler_params=_params(("parallel",)))(x)


SHARDED = ("w_in", "s5_w_glu", "mla_w_uq", "mla_w_ukv", "w_out", "ffn_w_up", "ffn_w_down")
COL_SHARDED = ("w_in", "mla_w_uq", "mla_w_ukv", "ffn_w_up")
REPLICATED = ("s5_lambda_re", "s5_lambda_im", "s5_log_dt", "s5_b_re", "s5_b_im", "s5_c_re", "s5_c_im", "s5_d",
              "mla_q_norm", "mla_kv_norm", "hg_lb_logits", "hg_out_norm", "mix_pre_norm", "mix_post_norm",
              "ffn_pre_norm", "ffn_post_norm", "ffn_conv_b")
WEIGHTS = ("w_in", "s5_lambda_re", "s5_lambda_im", "s5_log_dt", "s5_b_re", "s5_b_im", "s5_c_re", "s5_c_im", "s5_d",
           "s5_w_glu", "mla_q_norm", "mla_w_uq", "mla_kv_norm", "mla_w_ukv", "hg_lb_logits", "hg_out_norm", "w_out",
           "mix_pre_norm", "mix_post_norm", "ffn_pre_norm", "ffn_post_norm", "ffn_w_up", "ffn_conv_w", "ffn_conv_b",
           "ffn_w_down", "w_ada", "b_ada")


FF_PAD = 256


def _halves(t):
    return t.reshape(t.shape[:-2] + (2, t.shape[-2] // 2, t.shape[-1]))


def _unhalve(t):
    return t.reshape(t.shape[:-3] + (2 * t.shape[-2], t.shape[-1]))


def _cols_from_chips(t):
    return jnp.concatenate([t[j] for j in range(N_CHIPS)], axis=1)


def _swap_half(t):
    half = t.shape[-1] // 2
    return jnp.concatenate([-t[..., half:], t[..., :half]], axis=-1)


def _prep_win(w):
    s5, cq, ckv, kr, hq, hf, hi, hg = jnp.split(w, (512, 1024, 1280, 1344, 1856, 2368, 2880), axis=1)
    pad = jnp.zeros((w.shape[0], PROJ_W - 3456), w.dtype)
    return jnp.concatenate([s5, cq, hq, hf, hi, hg, ckv, kr, _swap_half(kr), pad], axis=1)


def _prep_wq(w):
    w3 = w.reshape(w.shape[0], MLA_HEADS, MLA_NOPE + MLA_ROPE)
    nope, rope = w3[..., :MLA_NOPE], w3[..., MLA_NOPE:]
    pair = jnp.concatenate([rope, _swap_half(rope)], axis=-1)
    return jnp.concatenate([nope.reshape(w.shape[0], -1), pair.reshape(w.shape[0], -1)], axis=1)


def _pad_ff_cols(w, cpad):
    r = w.shape[0]
    w3 = w.reshape(r, N_CHIPS, -1)
    return jnp.pad(w3, ((0, 0), (0, 0), (0, cpad - w3.shape[2]))).reshape(r, N_CHIPS * cpad)


def _pad_ff_rows(w, cpad):
    w3 = w.reshape(2, 2 * w.shape[1], w.shape[2])
    return jnp.pad(w3, ((0, 0), (0, cpad - w3.shape[1]), (0, 0))).reshape(2 * cpad, w.shape[2])


def _interleave(re, im, axis):
    re, im = jnp.moveaxis(re, axis, -1), jnp.moveaxis(im, axis, -1)
    lead = re.shape[:-1]
    both = jnp.stack([re.reshape(lead + (N_STATE // SCAN_W, SCAN_W)), im.reshape(lead + (N_STATE // SCAN_W, SCAN_W))], axis=-2)
    return jnp.moveaxis(both.reshape(lead + (2 * N_STATE,)), -1, axis)


def _s5_prep(lre, lim, logdt, bre, bim, cre, cim):
    dt = jnp.exp(logdt)[:, None]
    er = jnp.exp(lre * dt)
    ar, ai = er * jnp.cos(lim * dt), er * jnp.sin(lim * dt)
    nr, den = ar - 1.0, lre * lre + lim * lim
    cr, ci = (nr * lre + ai * lim) / den, (ai * lre - nr * lim) / den
    bbr = cr[..., None] * bre - ci[..., None] * bim
    bbi = cr[..., None] * bim + ci[..., None] * bre
    eye = jnp.eye(S5_GROUPS, dtype=F32)[:, None, :, None]

    def block_diag(t):
        return (t[:, :, None, :] * eye).reshape(S5_GROUPS * t.shape[1], S5_GROUPS * t.shape[2])

    tr = lambda t: jnp.transpose(t, (0, 2, 1))
    bd = _interleave(block_diag(tr(bbr)), block_diag(tr(bbi)), 1)
    cd = _interleave(block_diag(tr(cre)), block_diag(tr(-cim)), 0)
    a = _interleave(ar.reshape(1, N_STATE), ai.reshape(1, N_STATE), 1)
    return a, bd, cd


def _lower_bounds(logits):
    probs = jax.nn.softmax(logits, axis=0)
    return jnp.cumsum(probs, axis=0) - probs[0:1]


def _rope_table(positions):
    inv_freq = 1.0 / (ROPE_THETA ** (jnp.arange(0, MLA_ROPE, 2, dtype=F32) / MLA_ROPE))
    ang = positions.astype(F32)[:, None] * inv_freq
    cos, sin = jnp.cos(ang), jnp.sin(ang)
    return jnp.concatenate([cos, cos, sin, sin], axis=1)


def _split_mod(mod):
    return [mod[:, i * D_MODEL:(i + 1) * D_MODEL] for i in range(6)]


def _layer_fwd(x, wl, mod, cs):
    s = x.shape[0]
    ffp = wl["wdown_p"].shape[0]
    sh1, sc1, g1, sh2, sc2, g2 = _split_mod(mod)
    h1 = _normmod_fwd(x, wl["mix_pre_norm"], sc1, sh1, "mix_pre")
    proj = _mm(h1, wl["win_p"], mode="nn", name="in_proj", m=s, n=PROJ_W, k=D_MODEL)
    out_s5, s5_saved = _s5_fwd(proj, wl, s)
    o_mla, mla_saved = _mla_fwd(proj, wl, cs, s)
    y_hg, o_hg, states = _hg_fwd(proj, wl["hg_lb"], wl["hg_out_norm"])
    cat = jnp.concatenate([out_s5, o_mla.astype(BF16), y_hg], axis=1)
    mixed = _mm(cat, wl["w_out"], mode="nn", name="out_proj", m=s, n=D_MODEL, k=D_MODEL)
    x2 = _postnorm_fwd(x, mixed, wl["mix_post_norm"], g1, "mix_post")
    h2 = _normmod_fwd(x2, wl["ffn_pre_norm"], sc2, sh2, "ffn_pre")
    up = _mm(h2, wl["wup_cm"], mode="nn", name="ffn_up", m=s, n=2 * ffp, k=D_MODEL, cm=ffp // 2, tn=ffp // 4)
    act = _ffn_act_fwd(up, wl["conv_w_p"], wl["conv_b_p"], ffp)
    y = _mm(act, wl["wdown_p"], mode="nn", name="ffn_down", m=s, n=D_MODEL, k=ffp)
    x3 = _postnorm_fwd(x2, y, wl["ffn_post_norm"], g2, "ffn_post")
    return x3, (x, h1, proj, s5_saved, mla_saved, o_hg, states, cat, mixed, x2, h2, up, act, y)


def _layer_bwd(dx3, saved, wl, mod, cs):
    x, h1, proj, s5_saved, mla_saved, o_hg, states, cat, mixed, x2, h2, up, act, y = saved
    s = x.shape[0]
    ffp = wl["wdown_p"].shape[0]
    sh1, sc1, g1, sh2, sc2, g2 = _split_mod(mod)
    g = {}
    dy, g["ffn_post_norm"], dg2 = _postnorm_bwd(dx3, y, wl["ffn_post_norm"], g2, "ffn_post_bwd")
    da = _mm(dy, wl["wdown_p"], mode="nt", name="ffn_down_dx", m=s, n=ffp, k=D_MODEL)
    g["wdown_p"] = _mm(act, dy, mode="tn", name="ffn_down_dw", m=ffp, n=D_MODEL, k=s)
    dug, duv, dwg, dwv, dbg, dbv = _ffn_act_bwd(da, up, wl["conv_w_p"], wl["conv_b_p"], ffp)
    g["conv_w_p"] = jnp.concatenate([dwg, dwv], axis=1)
    g["conv_b_p"] = jnp.concatenate([dbg, dbv], axis=1)
    dup = jnp.concatenate([_conv_bwd_input(dug, wl["conv_w_p"], 0, "ffn_conv_bwd_gate"),
                           _conv_bwd_input(duv, wl["conv_w_p"], ffp, "ffn_conv_bwd_val")], axis=1)
    dh2 = _mm(dup, wl["wup_cm"], mode="nt", name="ffn_up_dx", m=s, n=D_MODEL, k=2 * ffp, cm=ffp // 2, tk=ffp // 4)
    g["wup_cm"] = _mm(h2, dup, mode="tn", name="ffn_up_dw", m=D_MODEL, n=2 * ffp, k=s, cm=ffp // 2, tn=ffp // 4)
    dx2, g["ffn_pre_norm"], dsc2, dsh2 = _normmod_bwd(dh2, x2, wl["ffn_pre_norm"], sc2, dx3, "ffn_pre_bwd")
    dmixed, g["mix_post_norm"], dg1 = _postnorm_bwd(dx2, mixed, wl["mix_post_norm"], g1, "mix_post_bwd")
    dcat = _mm(dmixed, wl["w_out"], mode="nt", name="out_proj_dx", m=s, n=D_MODEL, k=D_MODEL)
    g["w_out"] = _mm(cat, dmixed, mode="tn", name="out_proj_dw", m=D_MODEL, n=D_MODEL, k=s)
    du_s5, s5g = _s5_bwd(dcat, proj, wl, s5_saved, s)
    dcq, dckv, dkr, mlag = _mla_bwd(dcat, proj, wl, cs, mla_saved, s)
    dhq, dhf, dhi, dhg, g["hg_lb"], dgm = _hg_bwd(dcat, S5_WIDTH + MLA_WIDTH, proj, wl["hg_lb"], wl["hg_out_norm"], o_hg, states)
    g["hg_out_norm"] = jnp.sum(dgm, axis=0, keepdims=True)
    g.update(s5g)
    g.update(mlag)
    dproj = jnp.concatenate([du_s5, dcq, dhq, dhf, dhi, dhg, dckv, dkr, jnp.zeros((s, PROJ_W - 3456), F32)], axis=1).astype(BF16)
    dh1 = _mm(dproj, wl["win_p"], mode="nt", name="in_proj_dx", m=s, n=D_MODEL, k=PROJ_W)
    g["win_p"] = _mm(h1, dproj, mode="tn", name="in_proj_dw", m=D_MODEL, n=PROJ_W, k=s)
    dx, g["mix_pre_norm"], dsc1, dsh1 = _normmod_bwd(dh1, x, wl["mix_pre_norm"], sc1, dx2, "mix_pre_bwd")
    dmod = jnp.concatenate([dsh1, dsc1, dg1, dsh2, dsc2, dg2], axis=1)
    return dx, g, dmod


def _prepare_layer(gathered, conv_w, rep, cpad):
    def sharded_prep(w_in, s5_w_glu, mla_w_uq, mla_w_ukv, w_out, ffn_w_up, ffn_w_down, ffn_conv_w):
        merge = lambda t: t.reshape(N_CHIPS * t.shape[1], t.shape[2])
        return {"win_p": _prep_win(_cols_from_chips(w_in)), "s5_w_glu": merge(s5_w_glu), "mla_wq": _prep_wq(_cols_from_chips(mla_w_uq)),
                "mla_w_ukv": _cols_from_chips(mla_w_ukv), "w_out": merge(w_out), "wup_cm": ffn_w_up,
                "wdown_p": _pad_ff_rows(ffn_w_down, cpad), "conv_w_p": _pad_ff_cols(ffn_conv_w, cpad)}

    def rep_prep(lre, lim, logdt, bre, bim, cre, cim, conv_b):
        a, bd, cd = _s5_prep(lre, lim, logdt, bre, bim, cre, cim)
        return {"s5_a": a, "s5_bd": bd, "s5_cd": cd, "conv_b_p": _pad_ff_cols(conv_b, cpad)}

    sh_args = [gathered[k] for k in SHARDED] + [conv_w]
    rep_names = ("s5_lambda_re", "s5_lambda_im", "s5_log_dt", "s5_b_re", "s5_b_im", "s5_c_re", "s5_c_im", "ffn_conv_b")
    rep_args = [rep[k] for k in rep_names]
    wl = sharded_prep(*sh_args)
    rep_out, rep_vjp = jax.vjp(rep_prep, *rep_args)
    wl.update(rep_out)
    sh_t = jax.linear_transpose(sharded_prep, *[jax.ShapeDtypeStruct(a.shape, F32) for a in sh_args])

    def back(g):
        out = dict(zip(SHARDED + ("ffn_conv_w",), sh_t({k: g[k] for k in ("win_p", "s5_w_glu", "mla_wq", "mla_w_ukv", "w_out", "wup_cm", "wdown_p", "conv_w_p")})))
        out.update(zip(rep_names, rep_vjp({k: g[k] for k in ("s5_a", "s5_bd", "s5_cd", "conv_b_p")})))
        return out

    return wl, back


PER_LAYER_ROWS = ("s5_d", "mla_q_norm", "mla_kv_norm", "hg_out_norm", "mix_pre_norm", "mix_post_norm", "ffn_pre_norm", "ffn_post_norm")


def _flat_pad(parts, unit):
    flat = jnp.concatenate([p.reshape(-1) for p in parts])
    n = -(-flat.shape[0] // unit) * unit
    return jnp.pad(flat, (0, n - flat.shape[0])).reshape(-1, FLAT_COLS)


def _split_flat(flat, like):
    flat = flat.reshape(-1)
    out, pos = [], 0
    for t in like:
        out.append(flat[pos:pos + t.size].reshape(t.shape))
        pos += t.size
    return out


def kernel(x, c, positions, w_in, s5_lambda_re, s5_lambda_im, s5_log_dt, s5_b_re, s5_b_im, s5_c_re, s5_c_im, s5_d, s5_w_glu, mla_q_norm, mla_w_uq, mla_kv_norm, mla_w_ukv, hg_lb_logits, hg_out_norm, w_out, mix_pre_norm, mix_post_norm, ffn_pre_norm, ffn_post_norm, ffn_w_up, ffn_conv_w, ffn_conv_b, ffn_w_down, w_ada, b_ada, loss_target, m_w_in, m_s5_lambda_re, m_s5_lambda_im, m_s5_log_dt, m_s5_b_re, m_s5_b_im, m_s5_c_re, m_s5_c_im, m_s5_d, m_s5_w_glu, m_mla_q_norm, m_mla_w_uq, m_mla_kv_norm, m_mla_w_ukv, m_hg_lb_logits, m_hg_out_norm, m_w_out, m_mix_pre_norm, m_mix_post_norm, m_ffn_pre_norm, m_ffn_post_norm, m_ffn_w_up, m_ffn_conv_w, m_ffn_conv_b, m_ffn_w_down, m_w_ada, m_b_ada, v_w_in, v_s5_lambda_re, v_s5_lambda_im, v_s5_log_dt, v_s5_b_re, v_s5_b_im, v_s5_c_re, v_s5_c_im, v_s5_d, v_s5_w_glu, v_mla_q_norm, v_mla_w_uq, v_mla_kv_norm, v_mla_w_ukv, v_hg_lb_logits, v_hg_out_norm, v_w_out, v_mix_pre_norm, v_mix_post_norm, v_ffn_pre_norm, v_ffn_post_norm, v_ffn_w_up, v_ffn_conv_w, v_ffn_conv_b, v_ffn_w_down, v_w_ada, v_b_ada):
    p = dict(locals())
    n_layers = w_in.shape[0]
    c_up = ffn_w_up.shape[2]
    cpad = -(-c_up // FF_PAD) * FF_PAD
    xs, target = x[0], loss_target[0]
    me = 4 * lax.axis_index("x") + 2 * lax.axis_index("y") + lax.axis_index("c")
    chip = 2 * lax.axis_index("x") + lax.axis_index("y")
    cs = _rope_table(positions[0])

    cact = jax.nn.silu(_all_gather8(c, "gather_c")[:, 0, :])
    ada_cols = w_ada.shape[2]
    mod_part = jnp.stack([_mm(cact, w_ada[l], mode="nn", name="ada_mod", m=8, n=ada_cols, k=D_MODEL) for l in range(n_layers)])
    mod_all = _all_gather8(mod_part.reshape(1, -1), "gather_mod").reshape(N_CHIPS, 2, n_layers, 8, ada_cols)[:, 0]
    mod_mine = lax.dynamic_index_in_dim(mod_all, me, axis=2, keepdims=False)
    mods = mod_mine.transpose(1, 0, 2).reshape(n_layers, -1) + b_ada

    conv_w_all = _all_gather8(ffn_conv_w.reshape(1, -1), "gather_conv_w").reshape(N_CHIPS, 2, n_layers, 3, -1)[:, 0]
    conv_w_full = conv_w_all.transpose(1, 2, 0, 3).reshape(n_layers, 3, -1)

    lbs, lb_vjp = jax.vjp(_lower_bounds, hg_lb_logits)

    layers = []
    for l in range(n_layers):
        shards = {k: p[k][l] for k in SHARDED}
        shards["ffn_w_up"] = jnp.pad(shards["ffn_w_up"], ((0, 0), (0, cpad - c_up)))
        got = _gather_weights([_halves(shards[k].astype(BF16)) for k in SHARDED])
        gathered = {k: _unhalve(t) for k, t in zip(SHARDED, got)}
        rep = {k: p[k][l] for k in ("s5_lambda_re", "s5_lambda_im", "s5_log_dt", "s5_b_re", "s5_b_im", "s5_c_re", "s5_c_im")}
        rep["ffn_conv_b"] = ffn_conv_b[l][None, :]
        wl, back = _prepare_layer(gathered, conv_w_full[l], rep, cpad)
        for k in PER_LAYER_ROWS:
            wl[k] = p[k][l][None, :]
        wl["hg_lb"] = lbs[l][None, :]
        layers.append((wl, back))

    h = xs
    saved = []
    for l in range(n_layers):
        h, sv = _layer_fwd(h, layers[l][0], mods[l][None, :], cs)
        saved.append(sv)
    dh, loss_part = _loss_grad(h, target)
    loss = lax.psum(loss_part[0, 0], ("x", "y", "c"))

    grads = {k: [None] * n_layers for k in WEIGHTS}
    dmods, dlbs = [None] * n_layers, [None] * n_layers

    def take_in(layer, started):
        for k, t in zip(SHARDED, _reduce_scatter_finish(started)):
            grads[k][layer] = _unhalve(t)
        grads["ffn_w_up"][layer] = grads["ffn_w_up"][layer][:, :c_up]

    under_way = None
    for l in reversed(range(n_layers)):
        wl, back = layers[l]
        dh, g, dmods[l] = _layer_bwd(dh, saved[l], wl, mods[l][None, :], cs)
        if under_way is not None:
            take_in(*under_way)
        ref_g = back(g)
        under_way = (l, _reduce_scatter_start([_halves(ref_g[k]) for k in SHARDED], SHARDED))
        grads["ffn_conv_w"][l] = ref_g["ffn_conv_w"]
        for k in ("s5_lambda_re", "s5_lambda_im", "s5_log_dt", "s5_b_re", "s5_b_im", "s5_c_re", "s5_c_im"):
            grads[k][l] = ref_g[k]
        grads["ffn_conv_b"][l] = ref_g["ffn_conv_b"][0]
        for k in PER_LAYER_ROWS:
            grads[k][l] = g[k][0]
        dlbs[l] = g["hg_lb"][0]
    take_in(*under_way)
    grads = {k: jnp.stack(v) for k, v in grads.items() if v[0] is not None}
    grads["hg_lb_logits"] = lb_vjp(jnp.stack(dlbs))[0]

    summed = REPLICATED + ("ffn_conv_w",)
    rep_flat = _flat_pad([grads[k] for k in summed], 128 * FLAT_COLS)
    rep_sum = _sum8(_all_gather8(rep_flat, "gather_small_grads"))
    for k, t in zip(summed, _split_flat(rep_sum, [grads[k] for k in summed])):
        grads[k] = t
    conv_cols = grads["ffn_conv_w"].reshape(n_layers, 3, N_CHIPS, c_up)
    grads["ffn_conv_w"] = lax.dynamic_index_in_dim(conv_cols, chip, axis=2, keepdims=False)

    dmod_all = _all_gather8(jnp.stack(dmods).reshape(n_layers * 6 * D_MODEL // FLAT_COLS, FLAT_COLS), "gather_dmod")
    grads["b_ada"] = _sum8(dmod_all).reshape(n_layers, 6 * D_MODEL)
    dmod_cols = lax.dynamic_slice_in_dim(dmod_all.reshape(8, n_layers, N_CHIPS, ada_cols), chip, 1, axis=2)[:, :, 0]
    grads["w_ada"], delta_ada, new_m_ada, new_v_ada = _ada_grad_adamw(cact, dmod_cols.transpose(1, 0, 2), w_ada, m_w_ada, v_w_ada)

    delta, new_m, new_v = {"w_ada": delta_ada}, {"w_ada": new_m_ada}, {"w_ada": new_v_ada}
    for k in SHARDED:
        delta[k], new_m[k], new_v[k] = _adamw(p[k], grads[k], p["m_" + k], p["v_" + k], "adamw_" + k)
    small = REPLICATED + ("b_ada", "ffn_conv_w")
    flats = [_flat_pad([src[k] for k in small], 128 * FLAT_COLS)
             for src in (p, grads, {k: p["m_" + k] for k in small}, {k: p["v_" + k] for k in small})]
    for dst, flat in zip((delta, new_m, new_v), _adamw(*flats, "adamw_small")):
        for k, t in zip(small, _split_flat(flat, [p[k] for k in small])):
            dst[k] = t

    return (loss, dh[None], *[grads[k] for k in WEIGHTS], *[delta[k] for k in WEIGHTS],
            *[new_m[k] for k in WEIGHTS], *[new_v[k] for k in WEIGHTS])
```

```python
import functools
import math

import numpy as np
import jax
import jax.numpy as jnp
from jax import lax
from jax.experimental import pallas as pl
from jax.experimental.pallas import tpu as pltpu
from jax.experimental.pallas import tpu_sc as plsc

F32 = jnp.float32
BF16 = jnp.bfloat16
MESH = pl.DeviceIdType.MESH

D_MODEL = 2048
S5_WIDTH, S5_GROUP, S5_GROUPS, S5_STATE = 512, 16, 32, 64
MLA_HEADS, MLA_NOPE, MLA_ROPE, MLA_V = 8, 128, 64, 128
MLA_Q_RANK, MLA_KV_RANK = 512, 256
MLA_WIDTH = MLA_HEADS * MLA_V
ROPE_THETA = 10000.0
HG_HEADS, HG_DK, HG_DV = 4, 128, 128
HG_WIDTH = HG_HEADS * HG_DV
EPS = 1e-6
ADAM_LR, ADAM_B1, ADAM_B2, ADAM_EPS, ADAM_WD, ADAM_STEP = 0.001, 0.9, 0.999, 1e-08, 0.01, 10
GELU_K0 = math.sqrt(2.0 / math.pi)
GELU_K1 = 0.044715

LANE = 128
SUBLANE = 8
VMEM_LIMIT = 56 * 1024 * 1024

P_S5, P_CQ, P_HQ, P_HF, P_HI, P_HG, P_CKV, P_KR = 0, 512, 1024, 1536, 2048, 2560, 3072, 3328
PROJ_W = 3584
N_STATE = S5_GROUPS * S5_STATE
SCAN_W = 512
HG_CHUNK = 64
N_CHIPS = 4
FLAT_COLS = 1024
ADD_ROWS = 256


def _sigmoid(x):
    return 1.0 / (1.0 + jnp.exp(-x))


def _silu(x):
    return x * _sigmoid(x)


def _dsilu(x):
    s = _sigmoid(x)
    return s * (1.0 + x * (1.0 - s))


def _gelu(x):
    return 0.5 * x * (1.0 + jnp.tanh(GELU_K0 * (x + GELU_K1 * x * x * x)))


def _dgelu(x):
    t = jnp.tanh(GELU_K0 * (x + GELU_K1 * x * x * x))
    return 0.5 * (1.0 + t) + 0.5 * x * (1.0 - t * t) * GELU_K0 * (1.0 + 3.0 * GELU_K1 * x * x)


def _colsum(v):
    return jnp.sum(v, axis=0, keepdims=True)


def _rowmean(v):
    return jnp.mean(v, axis=-1, keepdims=True)


def _dot(a, b, dims):
    return lax.dot_general(a.astype(BF16), b.astype(BF16), (dims, ((), ())), preferred_element_type=F32)


NN = ((1,), (0,))
NT = ((1,), (1,))
TN = ((0,), (0,))


def _pcall(body, **kw):
    return pl.pallas_call(body, **kw)


def _params(sem):
    return pltpu.CompilerParams(dimension_semantics=sem, vmem_limit_bytes=VMEM_LIMIT)


def _tile(dim, pref):
    if dim <= pref:
        return dim
    t = (pref // LANE) * LANE
    while t > LANE and dim % t:
        t -= LANE
    assert dim % t == 0, (dim, pref)
    return t


def _mm(a, b, *, mode, name, m, n, k, a_off=0, b_off=0, tm=1024, tn=1024, tk=1024,
        out=((F32),), epi=None, extras=(), cm=None):
    if cm is not None:
        if mode == "nt":
            tk = _tile(cm, tk)
        else:
            tn = _tile(cm, tn)
    tm, tn, tk = _tile(m, tm), _tile(n, tn), _tile(k, tk)
    nk = k // tk
    dims = {"nn": NN, "nt": NT, "tn": TN}[mode]
    if mode == "tn":
        assert a_off % tm == 0 and b_off % tn == 0
        a_spec = pl.BlockSpec((tk, tm), lambda i, j, kk: (kk, i + a_off // tm))
        b_spec = pl.BlockSpec((tk, tn), lambda i, j, kk: (kk, j + b_off // tn))
    else:
        assert a_off % tk == 0 and b_off == 0
        a_spec = pl.BlockSpec((tm, tk), lambda i, j, kk: (i, kk + a_off // tk))
        if mode == "nn" and cm is not None:
            b_spec = pl.BlockSpec((None, tk, tn), lambda i, j, kk, per=cm // tn: (j // per, kk, j % per))
        elif mode == "nn":
            b_spec = pl.BlockSpec((tk, tn), lambda i, j, kk: (kk, j))
        elif cm is not None:
            b_spec = pl.BlockSpec((None, tn, tk), lambda i, j, kk, per=cm // tk: (kk // per, j, kk % per))
        else:
            b_spec = pl.BlockSpec((tn, tk), lambda i, j, kk: (j, kk))
    in_specs, ex_arrays = [a_spec, b_spec], []
    for e in extras:
        if e[0] == "tile":
            off = e[2] // tn
            assert e[2] % tn == 0
            in_specs.append(pl.BlockSpec((tm, tn), lambda i, j, kk, off=off: (i, j + off)))
        else:
            in_specs.append(pl.BlockSpec((e[1].shape[0], tn), lambda i, j, kk: (0, j)))
        ex_arrays.append(e[1])
    n_ex = len(ex_arrays)
    n_out = len(out)

    def body(*refs):
        a_ref, b_ref = refs[0], refs[1]
        ex_refs = refs[2:2 + n_ex]
        o_refs = refs[2 + n_ex:2 + n_ex + n_out]
        acc_ref = refs[-1]
        kk = pl.program_id(2)

        @pl.when(kk == 0)
        def _():
            acc_ref[...] = jnp.zeros_like(acc_ref)

        acc_ref[...] += _dot(a_ref[...], b_ref[...], dims)

        @pl.when(kk == nk - 1)
        def _():
            acc = acc_ref[...]
            if epi is None:
                o_refs[0][...] = acc.astype(o_refs[0].dtype)
            else:
                vals = epi(acc, *[r[...] for r in ex_refs])
                for r, v in zip(o_refs, vals):
                    r[...] = v.astype(r.dtype)

    if mode == "tn" and cm is not None:
        out_shape = tuple(jax.ShapeDtypeStruct((N_CHIPS, m, cm), d) for d in out)
        out_specs = tuple(pl.BlockSpec((None, tm, tn), lambda i, j, kk, per=cm // tn: (j // per, i, j % per)) for _ in out)
    else:
        out_shape = tuple(jax.ShapeDtypeStruct((m, n), d) for d in out)
        out_specs = tuple(pl.BlockSpec((tm, tn), lambda i, j, kk: (i, j)) for _ in out)
    res = _pcall(
        body, name=name,
        out_shape=out_shape,
        grid=(m // tm, n // tn, nk),
        in_specs=in_specs,
        out_specs=out_specs,
        scratch_shapes=[pltpu.VMEM((tm, tn), F32)],
        compiler_params=_params(("parallel", "parallel", "arbitrary")),
    )(a, b, *ex_arrays)
    return res[0] if n_out == 1 else res


def _rows(fn, *, name, s, tm, ins, outs, ncb=1):
    tm = min(tm, s)
    assert s % tm == 0 and tm % SUBLANE == 0
    ni = s // tm
    r8 = tm // SUBLANE
    in_specs, arrays = [], []
    for e in ins:
        kind, arr = e[0], e[1]
        if kind in ("row", "prev8", "next8", "vecb"):
            off, w = e[2] // e[3], e[3]
            assert e[2] % e[3] == 0
        if kind == "row":
            in_specs.append(pl.BlockSpec((tm, w), lambda j, i, off=off: (i, off + j)))
        elif kind == "prev8":
            in_specs.append(pl.BlockSpec((SUBLANE, w), lambda j, i, off=off: (jnp.maximum(i * r8 - 1, 0), off + j)))
        elif kind == "next8":
            last = s // SUBLANE - 1
            in_specs.append(pl.BlockSpec((SUBLANE, w), lambda j, i, off=off: (jnp.minimum((i + 1) * r8, last), off + j)))
        elif kind == "vec":
            in_specs.append(pl.BlockSpec(arr.shape, lambda j, i, nd=arr.ndim: (0,) * nd))
        else:
            in_specs.append(pl.BlockSpec((arr.shape[0], w), lambda j, i, off=off: (0, off + j)))
        arrays.append(arr)
    out_shape, out_specs = [], []
    for e in outs:
        if e[0] == "row":
            out_shape.append(jax.ShapeDtypeStruct((s, ncb * e[1]), e[2]))
            out_specs.append(pl.BlockSpec((tm, e[1]), lambda j, i: (i, j)))
        else:
            out_shape.append(jax.ShapeDtypeStruct((e[1], ncb * e[2]), F32))
            out_specs.append(pl.BlockSpec((e[1], e[2]), lambda j, i: (0, j)))

    def body(*refs):
        fn(pl.program_id(1), *refs)

    res = _pcall(
        body, name=name, out_shape=tuple(out_shape), grid=(ncb, ni),
        in_specs=in_specs, out_specs=tuple(out_specs),
        compiler_params=_params(("parallel", "arbitrary")),
    )(*arrays)
    return res[0] if len(outs) == 1 else res


def _acc(ref, i, val):
    @pl.when(i == 0)
    def _():
        ref[...] = val

    @pl.when(i > 0)
    def _():
        ref[...] += val


def _normmod_fwd(x, gain, sc, sh, name):
    s, d = x.shape

    def fn(i, x_ref, g_ref, sc_ref, sh_ref, h_ref):
        xv = x_ref[...]
        r = lax.rsqrt(_rowmean(xv * xv) + EPS)
        h_ref[...] = (((xv * r) * g_ref[...]) * (1.0 + sc_ref[...]) + sh_ref[...]).astype(h_ref.dtype)

    return _rows(fn, name=name, s=s, tm=256, ins=[("row", x, 0, d), ("vec", gain), ("vec", sc), ("vec", sh)],
                 outs=[("row", d, BF16)])


def _normmod_bwd(dh, x, gain, sc, dx_add, name):
    s, d = x.shape

    def fn(i, dh_ref, x_ref, g_ref, sc_ref, add_ref, dx_ref, dg_ref, dsc_ref, dsh_ref):
        xv, dhv = x_ref[...], dh_ref[...]
        r = lax.rsqrt(_rowmean(xv * xv) + EPS)
        xn = xv * r
        gain_v, one_sc = g_ref[...], 1.0 + sc_ref[...]
        ghat = dhv * gain_v * one_sc
        dx_ref[...] = r * (ghat - xn * _rowmean(ghat * xn)) + add_ref[...]
        _acc(dg_ref, i, _colsum(dhv * xn * one_sc))
        _acc(dsc_ref, i, _colsum(dhv * xn * gain_v))
        _acc(dsh_ref, i, _colsum(dhv))

    return _rows(fn, name=name, s=s, tm=256,
                 ins=[("row", dh, 0, d), ("row", x, 0, d), ("vec", gain), ("vec", sc), ("row", dx_add, 0, d)],
                 outs=[("row", d, F32), ("acc", 1, d), ("acc", 1, d), ("acc", 1, d)])


def _postnorm_fwd(x, m, gain, gate, name):
    s, d = x.shape

    def fn(i, x_ref, m_ref, g_ref, gate_ref, o_ref):
        mv = m_ref[...]
        r = lax.rsqrt(_rowmean(mv * mv) + EPS)
        o_ref[...] = x_ref[...] + gate_ref[...] * ((mv * r) * g_ref[...])

    return _rows(fn, name=name, s=s, tm=256, ins=[("row", x, 0, d), ("row", m, 0, d), ("vec", gain), ("vec", gate)],
                 outs=[("row", d, F32)])


def _postnorm_bwd(dxo, m, gain, gate, name):
    s, d = m.shape

    def fn(i, dx_ref, m_ref, g_ref, gate_ref, dm_ref, dg_ref, dgate_ref):
        mv, dxv = m_ref[...], dx_ref[...]
        r = lax.rsqrt(_rowmean(mv * mv) + EPS)
        mn = mv * r
        gain_v, gate_v = g_ref[...], gate_ref[...]
        ghat = dxv * gate_v * gain_v
        dm_ref[...] = (r * (ghat - mn * _rowmean(ghat * mn))).astype(dm_ref.dtype)
        _acc(dg_ref, i, _colsum(dxv * gate_v * mn))
        _acc(dgate_ref, i, _colsum(dxv * mn * gain_v))

    return _rows(fn, name=name, s=s, tm=256, ins=[("row", dxo, 0, d), ("row", m, 0, d), ("vec", gain), ("vec", gate)],
                 outs=[("row", d, BF16), ("acc", 1, d), ("acc", 1, d)])


def _rms_fwd(src, off, w, gain, name):
    s = src.shape[0]

    def fn(i, x_ref, g_ref, o_ref):
        xv = x_ref[...]
        r = lax.rsqrt(_rowmean(xv * xv) + EPS)
        o_ref[...] = ((xv * r) * g_ref[...]).astype(o_ref.dtype)

    return _rows(fn, name=name, s=s, tm=512, ins=[("row", src, off, w), ("vec", gain)], outs=[("row", w, BF16)])


def _rms_bwd(dy, src, off, w, gain, name):
    s = src.shape[0]

    def fn(i, dy_ref, x_ref, g_ref, dx_ref, dg_ref):
        xv, dyv = x_ref[...], dy_ref[...]
        r = lax.rsqrt(_rowmean(xv * xv) + EPS)
        xn = xv * r
        ghat = dyv * g_ref[...]
        dx_ref[...] = r * (ghat - xn * _rowmean(ghat * xn))
        _acc(dg_ref, i, _colsum(dyv * xn))

    return _rows(fn, name=name, s=s, tm=512, ins=[("row", dy, 0, w), ("row", src, off, w), ("vec", gain)],
                 outs=[("row", w, F32), ("acc", 1, w)])


def _loss_grad(x, target):
    s, d = x.shape

    def fn(i, x_ref, t_ref, dx_ref, l_ref):
        diff = x_ref[...] - t_ref[...]
        dx_ref[...] = diff * (1.0 / d)
        part = _colsum(jnp.sum(diff * diff, axis=1, keepdims=True)) * (0.5 / d)
        _acc(l_ref, i, jnp.broadcast_to(part, (1, LANE)))

    return _rows(fn, name="loss_grad", s=s, tm=256, ins=[("row", x, 0, d), ("row", target, 0, d)],
                 outs=[("row", d, F32), ("acc", 1, LANE)])


FFN_WC = 512


def _shift_rows(xv, h_ref, i, row, k):
    out = pltpu.roll(xv, k, 0)
    for r in range(k):
        hrow = jnp.where(i > 0, h_ref[SUBLANE - k + r:SUBLANE - k + r + 1, :], 0.0)
        out = jnp.where(row == r, hrow, out)
    return out


def _conv_rows(x_ref, h_ref, w_ref, b_ref, i, row):
    xv = x_ref[...]
    s1, s2 = _shift_rows(xv, h_ref, i, row, 1), _shift_rows(xv, h_ref, i, row, 2)
    u = ((b_ref[...] + s2 * w_ref[0:1, :]) + s1 * w_ref[1:2, :]) + xv * w_ref[2:3, :]
    return u, s1, s2, xv


def _ffn_act_fwd(up, conv_w, conv_b, ffp):
    s = up.shape[0]
    wc, ncb = FFN_WC, ffp // FFN_WC

    def fn(i, g_ref, gh_ref, v_ref, vh_ref, wg_ref, wv_ref, bg_ref, bv_ref, a_ref):
        row = lax.broadcasted_iota(jnp.int32, g_ref.shape, 0)
        ug = _conv_rows(g_ref, gh_ref, wg_ref, bg_ref, i, row)[0]
        uv = _conv_rows(v_ref, vh_ref, wv_ref, bv_ref, i, row)[0]
        a_ref[...] = (_gelu(ug) * uv).astype(a_ref.dtype)

    return _rows(fn, name="ffn_act_fwd", s=s, tm=512, ncb=ncb,
                 ins=[("row", up, 0, wc), ("prev8", up, 0, wc), ("row", up, ffp, wc), ("prev8", up, ffp, wc),
                      ("vecb", conv_w, 0, wc), ("vecb", conv_w, ffp, wc), ("vecb", conv_b, 0, wc), ("vecb", conv_b, ffp, wc)],
                 outs=[("row", wc, BF16)])


def _ffn_act_bwd(da, up, conv_w, conv_b, ffp):
    s = up.shape[0]
    wc, ncb = FFN_WC, ffp // FFN_WC

    def fn(i, da_ref, g_ref, gh_ref, v_ref, vh_ref, wg_ref, wv_ref, bg_ref, bv_ref,
           dug_ref, duv_ref, dwg_ref, dwv_ref, dbg_ref, dbv_ref):
        row = lax.broadcasted_iota(jnp.int32, g_ref.shape, 0)
        ug, g1, g2, g0 = _conv_rows(g_ref, gh_ref, wg_ref, bg_ref, i, row)
        uv, v1, v2, v0 = _conv_rows(v_ref, vh_ref, wv_ref, bv_ref, i, row)
        dav = da_ref[...]
        dug = dav * uv * _dgelu(ug)
        duv = dav * _gelu(ug)
        dug_ref[...] = dug
        duv_ref[...] = duv
        for r, (gt, vt) in enumerate(((g2, v2), (g1, v1), (g0, v0))):
            _acc(dwg_ref.at[r:r + 1, :], i, _colsum(dug * gt))
            _acc(dwv_ref.at[r:r + 1, :], i, _colsum(duv * vt))
        _acc(dbg_ref, i, _colsum(dug))
        _acc(dbv_ref, i, _colsum(duv))

    return _rows(fn, name="ffn_act_bwd", s=s, tm=512, ncb=ncb,
                 ins=[("row", da, 0, wc), ("row", up, 0, wc), ("prev8", up, 0, wc), ("row", up, ffp, wc), ("prev8", up, ffp, wc),
                      ("vecb", conv_w, 0, wc), ("vecb", conv_w, ffp, wc), ("vecb", conv_b, 0, wc), ("vecb", conv_b, ffp, wc)],
                 outs=[("row", wc, F32), ("row", wc, F32), ("acc", 3, wc), ("acc", 3, wc), ("acc", 1, wc), ("acc", 1, wc)])


def _conv_bwd_input(du, conv_w, w_off, name):
    s, ffp = du.shape
    wc, ncb = FFN_WC, ffp // FFN_WC
    ni = s // min(512, s)

    def fn(i, du_ref, nx_ref, w_ref, o_ref):
        dv = du_ref[...]
        tm = dv.shape[0]
        row = lax.broadcasted_iota(jnp.int32, dv.shape, 0)
        n0 = jnp.where(i < ni - 1, nx_ref[0:1, :], 0.0)
        n1 = jnp.where(i < ni - 1, nx_ref[1:2, :], 0.0)
        u1 = jnp.where(row == tm - 1, n0, pltpu.roll(dv, tm - 1, 0))
        u2 = jnp.where(row == tm - 1, n1, jnp.where(row == tm - 2, n0, pltpu.roll(dv, tm - 2, 0)))
        o_ref[...] = (dv * w_ref[2:3, :] + u1 * w_ref[1:2, :] + u2 * w_ref[0:1, :]).astype(o_ref.dtype)

    return _rows(fn, name=name, s=s, tm=512, ncb=ncb,
                 ins=[("row", du, 0, wc), ("next8", du, 0, wc), ("vecb", conv_w, w_off, wc)],
                 outs=[("row", wc, BF16)])


def _cmul(ar, ai, br, bi):
    return ar * br - ai * bi, ar * bi + ai * br


def _s5_scan(x, a, *, reverse, h=None, name):
    s = x.shape[0]
    w = SCAN_W
    t_rows = min(256, s)
    nt = s // t_rows
    ncol = N_STATE // w
    nbits = t_rows.bit_length()
    r8 = t_rows // SUBLANE

    def tblk(t):
        return nt - 1 - t if reverse else t

    def body(*refs):
        if reverse:
            x_ref, a_ref, h_ref, hh_ref, o_ref, da_ref, carry, ptab = refs
        else:
            x_ref, a_ref, o_ref, carry, ptab = refs
        t = pl.program_id(1)
        row = lax.broadcasted_iota(jnp.int32, (t_rows, w), 0)
        idx = (t_rows - 1 - row) if reverse else row
        ar = a_ref[:, :w]
        ai = -a_ref[:, w:] if reverse else a_ref[:, w:]
        pows = [(ar, ai)]
        for _ in range(nbits - 1):
            pows.append(_cmul(*pows[-1], *pows[-1]))

        @pl.when(t == 0)
        def _():
            carry[...] = jnp.zeros_like(carry)
            pr, pi = jnp.ones((t_rows, w), F32), jnp.zeros((t_rows, w), F32)
            for kbit in range(nbits):
                bit = ((idx + 1) >> kbit) & 1
                fr = jnp.where(bit == 1, pows[kbit][0], 1.0)
                fi = jnp.where(bit == 1, pows[kbit][1], 0.0)
                pr, pi = _cmul(pr, pi, fr, fi)
            ptab[:, :w] = pr
            ptab[:, w:] = pi

        xr, xi = x_ref[:, :w], x_ref[:, w:]
        step = 1
        kbit = 0
        while step < t_rows:
            shift = (t_rows - step) if reverse else step
            yr, yi = pltpu.roll(xr, shift, 0), pltpu.roll(xi, shift, 0)
            zr, zi = _cmul(pows[kbit][0], pows[kbit][1], yr, yi)
            keep = idx >= step
            xr = xr + jnp.where(keep, zr, 0.0)
            xi = xi + jnp.where(keep, zi, 0.0)
            step *= 2
            kbit += 1
        cr, ci = carry[0:1, :w], carry[0:1, w:]
        zr, zi = _cmul(ptab[:, :w], ptab[:, w:], cr, ci)
        xr, xi = xr + zr, xi + zi
        o_ref[:, :w] = xr
        o_ref[:, w:] = xi
        last = 0 if reverse else t_rows - 1
        carry[0:1, :] = o_ref[last:last + 1, :]
        if reverse:
            halo_r = jnp.where(t < nt - 1, hh_ref[SUBLANE - 1:SUBLANE, :w], 0.0)
            halo_i = jnp.where(t < nt - 1, hh_ref[SUBLANE - 1:SUBLANE, w:], 0.0)
            hr = jnp.where(row == 0, halo_r, pltpu.roll(h_ref[:, :w], 1, 0))
            hi = jnp.where(row == 0, halo_i, pltpu.roll(h_ref[:, w:], 1, 0))
            _acc(da_ref.at[:, :w], t, _colsum(xr * hr + xi * hi))
            _acc(da_ref.at[:, w:], t, _colsum(xi * hr - xr * hi))

    blk = pl.BlockSpec((t_rows, 2 * w), lambda j, t: (tblk(t), j))
    a_spec = pl.BlockSpec((1, 2 * w), lambda j, t: (0, j))
    in_specs, arrays = [blk, a_spec], [x, a]
    out_shape = [jax.ShapeDtypeStruct((s, 2 * N_STATE), F32)]
    out_specs = [blk]
    if reverse:
        in_specs += [blk, pl.BlockSpec((SUBLANE, 2 * w), lambda j, t: (jnp.maximum(tblk(t) * r8 - 1, 0), j))]
        arrays += [h, h]
        out_shape.append(jax.ShapeDtypeStruct((1, 2 * N_STATE), F32))
        out_specs.append(a_spec)
    res = _pcall(
        body, name=name, out_shape=tuple(out_shape), grid=(ncol, nt), in_specs=in_specs, out_specs=tuple(out_specs),
        scratch_shapes=[pltpu.VMEM((SUBLANE, 2 * w), F32), pltpu.VMEM((t_rows, 2 * w), F32)],
        compiler_params=_params(("parallel", "arbitrary")),
    )(*arrays)
    return res if reverse else res[0]


def _s5_glu_bwd_a(dout, dout_off, y, z):
    s = y.shape[0]
    w = S5_WIDTH

    def fn(i, do_ref, y_ref, z_ref, dz_ref, p_ref):
        dov = do_ref[...]
        sg = _sigmoid(z_ref[...])
        dz_ref[...] = (dov * _gelu(y_ref[...]) * sg * (1.0 - sg)).astype(dz_ref.dtype)
        p_ref[...] = dov * sg

    return _rows(fn, name="s5_glu_bwd", s=s, tm=512, ins=[("row", dout, dout_off, w), ("row", y, 0, w), ("row", z, 0, w)],
                 outs=[("row", w, BF16), ("row", w, F32)])


def _s5_dd(dy, proj):
    s = dy.shape[0]
    w = S5_WIDTH

    def fn(i, dy_ref, u_ref, dd_ref):
        _acc(dd_ref, i, _colsum(dy_ref[...] * u_ref[...]))

    return _rows(fn, name="s5_dd", s=s, tm=512, ins=[("row", dy, 0, w), ("row", proj, P_S5, w)], outs=[("acc", 1, w)])


def _s5_fwd(proj, wl, s):
    bu = _mm(proj, wl["s5_bd"], mode="nn", name="s5_bu", m=s, n=2 * N_STATE, k=S5_WIDTH, a_off=P_S5)
    h = _s5_scan(bu, wl["s5_a"], reverse=False, name="s5_scan_fwd")
    def y_epi(acc, u, d):
        yv = acc + d * u
        return yv, _gelu(yv)

    y, yg = _mm(h, wl["s5_cd"], mode="nn", name="s5_y", m=s, n=S5_WIDTH, k=2 * N_STATE, out=(F32, BF16),
                extras=[("tile", proj, P_S5), ("row", wl["s5_d"])], epi=y_epi)
    z, out = _mm(yg, wl["s5_w_glu"], mode="nn", name="s5_glu", m=s, n=S5_WIDTH, k=S5_WIDTH, out=(F32, BF16),
                 extras=[("tile", y, 0)], epi=lambda acc, yv: (acc, _gelu(yv) * _sigmoid(acc)))
    return out, (h, y, z, yg)


def _s5_bwd(dcat, proj, wl, saved, s):
    h, y, z, yg = saved
    dz, p1 = _s5_glu_bwd_a(dcat, 0, y, z)
    dy = _mm(dz, wl["s5_w_glu"], mode="nt", name="s5_dyg", m=s, n=S5_WIDTH, k=S5_WIDTH,
             extras=[("tile", p1, 0), ("tile", y, 0)], epi=lambda acc, p, yv: ((p + acc) * _dgelu(yv),))
    gh = _mm(dy, wl["s5_cd"], mode="nt", name="s5_gh", m=s, n=2 * N_STATE, k=S5_WIDTH)
    adj, da = _s5_scan(gh, wl["s5_a"], reverse=True, h=h, name="s5_scan_bwd")
    du = _mm(adj, wl["s5_bd"], mode="nt", name="s5_du", m=s, n=S5_WIDTH, k=2 * N_STATE,
             extras=[("tile", dy, 0), ("row", wl["s5_d"])], epi=lambda acc, dyv, d: (acc + dyv * d,))
    grads = {
        "s5_a": da,
        "s5_bd": _mm(proj, adj, mode="tn", name="s5_dbd", m=S5_WIDTH, n=2 * N_STATE, k=s, a_off=P_S5),
        "s5_cd": _mm(h, dy, mode="tn", name="s5_dcd", m=2 * N_STATE, n=S5_WIDTH, k=s),
        "s5_d": _s5_dd(dy, proj),
        "s5_w_glu": _mm(yg, dz, mode="tn", name="s5_dwglu", m=S5_WIDTH, n=S5_WIDTH, k=s),
    }
    return du, grads


def _mla_prep(qraw, kvraw, proj, cs):
    s = qraw.shape[0]
    hw = MLA_HEADS * LANE

    def fn(i, q_ref, kv_ref, kr_ref, cs_ref, qn_ref, qr_ref, kvb_ref, krb_ref):
        csv = cs_ref[...]
        qn_ref[...] = q_ref[:, :hw].astype(BF16)
        for hd in range(MLA_HEADS):
            p = q_ref[:, hw + hd * LANE:hw + (hd + 1) * LANE] * csv
            qr_ref[:, hd * LANE:(hd + 1) * LANE] = (p + pltpu.roll(p, LANE // 2, 1)).astype(BF16)
        kvb_ref[...] = kv_ref[...].astype(BF16)
        p = kr_ref[...] * csv
        lane = lax.broadcasted_iota(jnp.int32, p.shape, 1)
        krb_ref[...] = jnp.where(lane < LANE // 2, p + pltpu.roll(p, LANE // 2, 1), 0.0).astype(BF16)

    return _rows(fn, name="mla_prep", s=s, tm=256,
                 ins=[("row", qraw, 0, 2 * hw), ("row", kvraw, 0, 2 * hw), ("row", proj, P_KR, LANE), ("row", cs, 0, LANE)],
                 outs=[("row", hw, BF16), ("row", hw, BF16), ("row", 2 * hw, BF16), ("row", LANE, BF16)])


def _mla_rope_bwd(dqn, dqr2, dkr2h, cs):
    s = dqn.shape[0]
    hw = MLA_HEADS * LANE

    def fn(i, dqn_ref, dqr_ref, dkr_ref, cs_ref, dq_ref, dk_ref):
        csv = cs_ref[...]
        dq_ref[:, :hw] = dqn_ref[...].astype(BF16)
        ksum = jnp.zeros(csv.shape, F32)
        for hd in range(MLA_HEADS):
            g = dqr_ref[:, hd * LANE:(hd + 1) * LANE]
            dq_ref[:, hw + hd * LANE:hw + (hd + 1) * LANE] = ((g + pltpu.roll(g, LANE // 2, 1)) * csv).astype(BF16)
            ksum = ksum + dkr_ref[:, hd * LANE:(hd + 1) * LANE]
        dk_ref[...] = ksum * csv

    return _rows(fn, name="mla_rope_bwd", s=s, tm=256,
                 ins=[("row", dqn, 0, hw), ("row", dqr2, 0, hw), ("row", dkr2h, 0, hw), ("row", cs, 0, LANE)],
                 outs=[("row", 2 * hw, BF16), ("row", LANE, F32)])


def _lanes(a_ref, b_ref):
    return jnp.concatenate([a_ref[...], b_ref[...]], axis=1)


def _attn_scores(qn_ref, qr_ref, kn_ref, kr_ref, qi, ki, tq, tk):
    scale = (MLA_NOPE + MLA_ROPE) ** -0.5
    sc = _dot(_lanes(qn_ref, qr_ref), _lanes(kn_ref, kr_ref), NT) * scale
    assert tq == tk
    return sc, lax.broadcasted_iota(jnp.int32, (tq, tk), 1) <= lax.broadcasted_iota(jnp.int32, (tq, tk), 0)


def _attn_specs(tq, tk, q_of, k_of):
    qs = pl.BlockSpec((tq, LANE), lambda h, a, b: (q_of(a, b), h))
    return [qs, qs,
            pl.BlockSpec((tk, LANE), lambda h, a, b: (k_of(a, b), 2 * h)),
            pl.BlockSpec((tk, LANE), lambda h, a, b: (k_of(a, b), 2 * h + 1)),
            pl.BlockSpec((tk, LANE), lambda h, a, b: (k_of(a, b), 0))]


def _flash_fwd(qn, qr2, kv, kr2):
    s = qn.shape[0]
    tq = tk = min(512, s)
    nq = s // tq

    def body(qn_ref, qr_ref, kn_ref, v_ref, kr_ref, o_ref, lse_ref, m_sc, l_sc, acc_sc):
        qi, ki = pl.program_id(1), pl.program_id(2)

        @pl.when(ki == 0)
        def _():
            m_sc[...] = jnp.full(m_sc.shape, -jnp.inf, F32)
            l_sc[...] = jnp.zeros_like(l_sc)
            acc_sc[...] = jnp.zeros_like(acc_sc)

        def step(diagonal):
            sc, causal = _attn_scores(qn_ref, qr_ref, kn_ref, kr_ref, qi, ki, tq, tk)
            if diagonal:
                sc = jnp.where(causal, sc, -1e30)
            m_new = jnp.maximum(m_sc[...], jnp.max(sc, axis=1, keepdims=True))
            alpha = jnp.exp(m_sc[...] - m_new)
            p = jnp.exp(sc - m_new)
            l_sc[...] = alpha * l_sc[...] + jnp.sum(p, axis=1, keepdims=True)
            acc_sc[...] = alpha * acc_sc[...] + _dot(p, v_ref[...], NN)
            m_sc[...] = m_new

        @pl.when(ki < qi)
        def _():
            step(False)

        @pl.when(ki == qi)
        def _():
            step(True)
            o_ref[...] = acc_sc[...] / l_sc[...]
            lse_ref[0] = m_sc[...] + jnp.log(l_sc[...])

    return _pcall(
        body, name="mla_flash_fwd",
        out_shape=(jax.ShapeDtypeStruct((s, MLA_WIDTH), F32), jax.ShapeDtypeStruct((MLA_HEADS, s, 1), F32)),
        grid=(MLA_HEADS, nq, nq),
        in_specs=_attn_specs(tq, tk, lambda a, b: a, lambda a, b: jnp.minimum(a, b)),
        out_specs=(pl.BlockSpec((tq, LANE), lambda h, a, b: (a, h)), pl.BlockSpec((1, tq, 1), lambda h, a, b: (h, a, 0))),
        scratch_shapes=[pltpu.VMEM((tq, 1), F32), pltpu.VMEM((tq, 1), F32), pltpu.VMEM((tq, LANE), F32)],
        compiler_params=_params(("parallel", "parallel", "arbitrary")),
    )(qn, qr2, kv, kv, kr2)


def _flash_bwd_dq(qn, qr2, kv, kr2, do, do_off, o, lse):
    s = qn.shape[0]
    tq = tk = min(512, s)
    nq = s // tq
    scale = (MLA_NOPE + MLA_ROPE) ** -0.5
    ob = do_off // LANE

    def body(qn_ref, qr_ref, kn_ref, v_ref, kr_ref, do_ref, o_ref, lse_ref, dqn_ref, dqr_ref, dl_ref, dl_sc, aq_sc):
        qi, ki = pl.program_id(1), pl.program_id(2)

        @pl.when(ki == 0)
        def _():
            dl_sc[...] = jnp.sum(do_ref[...] * o_ref[...], axis=1, keepdims=True)
            aq_sc[...] = jnp.zeros_like(aq_sc)

        def step(diagonal):
            sc, causal = _attn_scores(qn_ref, qr_ref, kn_ref, kr_ref, qi, ki, tq, tk)
            p = jnp.exp(sc - lse_ref[0])
            if diagonal:
                p = jnp.where(causal, p, 0.0)
            dp = _dot(do_ref[...], v_ref[...], NT)
            ds = (p * (dp - dl_sc[...]) * scale).astype(BF16)
            aq_sc[...] += _dot(ds, _lanes(kn_ref, kr_ref), NN)

        @pl.when(ki < qi)
        def _():
            step(False)

        @pl.when(ki == qi)
        def _():
            step(True)
            dqn_ref[...] = aq_sc[:, :LANE]
            dqr_ref[...] = aq_sc[:, LANE:]
            dl_ref[0] = dl_sc[...]

    qblk = pl.BlockSpec((tq, LANE), lambda h, a, b: (a, h))
    vec = pl.BlockSpec((1, tq, 1), lambda h, a, b: (h, a, 0))
    return _pcall(
        body, name="mla_flash_dq",
        out_shape=(jax.ShapeDtypeStruct((s, MLA_WIDTH), F32), jax.ShapeDtypeStruct((s, MLA_WIDTH), F32),
                   jax.ShapeDtypeStruct((MLA_HEADS, s, 1), F32)),
        grid=(MLA_HEADS, nq, nq),
        in_specs=_attn_specs(tq, tk, lambda a, b: a, lambda a, b: jnp.minimum(a, b))
        + [pl.BlockSpec((tq, LANE), lambda h, a, b: (a, h + ob)), qblk, vec],
        out_specs=(qblk, qblk, vec),
        scratch_shapes=[pltpu.VMEM((tq, 1), F32), pltpu.VMEM((tq, 2 * LANE), F32)],
        compiler_params=_params(("parallel", "parallel", "arbitrary")),
    )(qn, qr2, kv, kv, kr2, do, o, lse)


def _flash_bwd_dkv(qn, qr2, kv, kr2, do, do_off, lse, delta):
    s = qn.shape[0]
    tq = tk = min(512, s)
    nq = s // tq
    scale = (MLA_NOPE + MLA_ROPE) ** -0.5
    ob = do_off // LANE

    def body(qn_ref, qr_ref, kn_ref, v_ref, kr_ref, do_ref, lse_ref, dl_ref, dkv_ref, dkr_ref, ak_sc, av_sc):
        ki, qi = pl.program_id(1), pl.program_id(2)

        @pl.when(qi == 0)
        def _():
            ak_sc[...] = jnp.zeros_like(ak_sc)
            av_sc[...] = jnp.zeros_like(av_sc)

        def step(diagonal):
            sc, causal = _attn_scores(qn_ref, qr_ref, kn_ref, kr_ref, qi, ki, tq, tk)
            p = jnp.exp(sc - lse_ref[0])
            if diagonal:
                p = jnp.where(causal, p, 0.0)
            dp = _dot(do_ref[...], v_ref[...], NT)
            ds = (p * (dp - dl_ref[0]) * scale).astype(BF16)
            av_sc[...] += _dot(p, do_ref[...], TN)
            ak_sc[...] += _dot(ds, _lanes(qn_ref, qr_ref), TN)

        @pl.when(qi > ki)
        def _():
            step(False)

        @pl.when(qi == ki)
        def _():
            step(True)

        @pl.when(qi == nq - 1)
        def _():
            dkv_ref[:, :LANE] = ak_sc[:, :LANE]
            dkv_ref[:, LANE:] = av_sc[...]
            dkr_ref[...] = ak_sc[:, LANE:]

    q_of = lambda a, b: jnp.maximum(a, b)
    k_of = lambda a, b: a
    vec = pl.BlockSpec((1, tq, 1), lambda h, a, b: (h, q_of(a, b), 0))
    return _pcall(
        body, name="mla_flash_dkv",
        out_shape=(jax.ShapeDtypeStruct((s, 2 * MLA_WIDTH), F32), jax.ShapeDtypeStruct((s, MLA_WIDTH), F32)),
        grid=(MLA_HEADS, nq, nq),
        in_specs=_attn_specs(tq, tk, q_of, k_of)
        + [pl.BlockSpec((tq, LANE), lambda h, a, b: (q_of(a, b), h + ob)), vec, vec],
        out_specs=(pl.BlockSpec((tk, 2 * LANE), lambda h, a, b: (a, h)), pl.BlockSpec((tk, LANE), lambda h, a, b: (a, h))),
        scratch_shapes=[pltpu.VMEM((tk, 2 * LANE), F32), pltpu.VMEM((tk, LANE), F32)],
        compiler_params=_params(("parallel", "parallel", "arbitrary")),
    )(qn, qr2, kv, kv, kr2, do, lse, delta)


def _mla_fwd(proj, wl, cs, s):
    cqn = _rms_fwd(proj, P_CQ, MLA_Q_RANK, wl["mla_q_norm"], "mla_q_rms")
    ckvn = _rms_fwd(proj, P_CKV, MLA_KV_RANK, wl["mla_kv_norm"], "mla_kv_rms")
    qraw = _mm(cqn, wl["mla_wq"], mode="nn", name="mla_q_proj", m=s, n=2 * MLA_WIDTH, k=MLA_Q_RANK)
    kvraw = _mm(ckvn, wl["mla_w_ukv"], mode="nn", name="mla_kv_proj", m=s, n=2 * MLA_WIDTH, k=MLA_KV_RANK)
    qn, qr2, kv, kr2 = _mla_prep(qraw, kvraw, proj, cs)
    o, lse = _flash_fwd(qn, qr2, kv, kr2)
    return o, (cqn, ckvn, qn, qr2, kv, kr2, o, lse)


def _mla_bwd(dcat, proj, wl, cs, saved, s):
    cqn, ckvn, qn, qr2, kv, kr2, o, lse = saved
    dqn, dqr2, delta = _flash_bwd_dq(qn, qr2, kv, kr2, dcat, S5_WIDTH, o, lse)
    dkv, dkr2h = _flash_bwd_dkv(qn, qr2, kv, kr2, dcat, S5_WIDTH, lse, delta)
    dqraw, dkr = _mla_rope_bwd(dqn, dqr2, dkr2h, cs)
    dcqn = _mm(dqraw, wl["mla_wq"], mode="nt", name="mla_dcqn", m=s, n=MLA_Q_RANK, k=2 * MLA_WIDTH)
    dckvn = _mm(dkv, wl["mla_w_ukv"], mode="nt", name="mla_dckvn", m=s, n=MLA_KV_RANK, k=2 * MLA_WIDTH)
    dcq, dqg = _rms_bwd(dcqn, proj, P_CQ, MLA_Q_RANK, wl["mla_q_norm"], "mla_q_rms_bwd")
    dckv, dkvg = _rms_bwd(dckvn, proj, P_CKV, MLA_KV_RANK, wl["mla_kv_norm"], "mla_kv_rms_bwd")
    grads = {
        "mla_wq": _mm(cqn, dqraw, mode="tn", name="mla_dwq", m=MLA_Q_RANK, n=2 * MLA_WIDTH, k=s),
        "mla_w_ukv": _mm(ckvn, dkv, mode="tn", name="mla_dwukv", m=MLA_KV_RANK, n=2 * MLA_WIDTH, k=s),
        "mla_q_norm": dqg,
        "mla_kv_norm": dkvg,
    }
    return dcq, dckv, dkr, grads


HG_ROWS = 256


def _cumsum_rows(x, row, reverse=False):
    n = x.shape[0]
    step = 1
    while step < n:
        if reverse:
            x = x + jnp.where(row < n - step, pltpu.roll(x, n - step, 0), 0.0)
        else:
            x = x + jnp.where(row >= step, pltpu.roll(x, step, 0), 0.0)
        step *= 2
    return x


def _hg_gates(hq, z, lb):
    sig = _sigmoid(z)
    sigm = _sigmoid(-z)
    f = lb + (1.0 - lb) * sig
    return _silu(hq), (1.0 - lb) * sigm, jnp.log(f), sig, sigm, f


HG_SUB = 16
HG_NSUB = HG_CHUNK // HG_SUB


def _hg_offdiag(q, k, b, row, b_buf):
    ops = []
    for j in range(HG_NSUB - 1):
        e = (j + 1) * HG_SUB
        be = b_buf[e - 1:e, :]
        fj = jnp.where(row >= e, jnp.exp(jnp.minimum(b - be, 0.0)), 0.0)
        gj = jnp.where((row >= e - HG_SUB) & (row < e), jnp.exp(jnp.minimum(be - b, 0.0)), 0.0)
        ops.append((q * fj, k * gj, fj, gj))
    return ops


def _hg_diag_weights(qb, bb, ks, bs, rr, srow):
    e = jnp.exp(jnp.minimum(bb - bs, 0.0))
    keep = rr >= srow
    w = jnp.where(keep, jnp.sum(qb * ks * e, axis=1, keepdims=True), 0.0)
    return e, keep, w


def _hg_specs(rows):
    def col(off):
        return pl.BlockSpec((rows, LANE), lambda h, n, off=off: (n, off // LANE + h))
    return col


def _hg_fwd(proj, lb, gamma):
    s = proj.shape[0]
    rows = min(HG_ROWS, s)
    c = HG_CHUNK
    npc = rows // c
    nb = s // rows

    def body(hq_ref, hf_ref, hi_ref, hg_ref, lb_ref, gm_ref, y_ref, o_ref, st_ref, st_sc, k_buf, b_buf):
        n = pl.program_id(1)

        @pl.when(n == 0)
        def _():
            st_sc[...] = jnp.zeros_like(st_sc)

        row = lax.broadcasted_iota(jnp.int32, (c, LANE), 0)
        rr = lax.broadcasted_iota(jnp.int32, (HG_SUB, 1), 0)
        lbv = lb_ref[...]
        for ch in range(npc):
            sl = slice(ch * c, (ch + 1) * c)
            q, k, logf, _, _, _ = _hg_gates(hq_ref[sl, :], hf_ref[sl, :], lbv)
            v = hi_ref[sl, :]
            b = _cumsum_rows(logf, row)
            bl = _colsum(logf)
            k_buf[...] = k
            b_buf[...] = b
            st = st_sc[...]
            st_ref[0, ch] = st
            a_off = sum(_dot(qf, kg, NT) for qf, kg, _, _ in _hg_offdiag(q, k, b, row, b_buf))
            o = _dot(q * jnp.exp(b), st, NT) + _dot(a_off, v, NN)
            st_sc[...] = st * jnp.exp(bl) + _dot(v, k * jnp.exp(bl - b), TN)
            diag = []
            for i in range(HG_NSUB):
                r0 = i * HG_SUB
                qb, bb = q[r0:r0 + HG_SUB], b[r0:r0 + HG_SUB]
                acc = jnp.zeros((HG_SUB, LANE), F32)
                for srow in range(HG_SUB):
                    t = r0 + srow
                    _, _, w = _hg_diag_weights(qb, bb, k_buf[t:t + 1, :], b_buf[t:t + 1, :], rr, srow)
                    acc = acc + w * hi_ref[ch * c + t:ch * c + t + 1, :]
                diag.append(acc)
            o = o + jnp.concatenate(diag, axis=0)
            o_ref[sl, :] = o
            r = lax.rsqrt(_rowmean(o * o) + EPS)
            y_ref[sl, :] = (((o * r) * gm_ref[...]) * _silu(hg_ref[sl, :])).astype(y_ref.dtype)

    col = _hg_specs(rows)
    out_blk = pl.BlockSpec((rows, LANE), lambda h, n: (n, h))
    return _pcall(
        body, name="hgrn_fwd",
        out_shape=(jax.ShapeDtypeStruct((s, HG_WIDTH), BF16), jax.ShapeDtypeStruct((s, HG_WIDTH), F32),
                   jax.ShapeDtypeStruct((HG_HEADS, s // c, HG_DV, HG_DK), F32)),
        grid=(HG_HEADS, nb),
        in_specs=[col(P_HQ), col(P_HF), col(P_HI), col(P_HG),
                  pl.BlockSpec((1, LANE), lambda h, n: (0, h)), pl.BlockSpec((1, LANE), lambda h, n: (0, 0))],
        out_specs=(out_blk, out_blk, pl.BlockSpec((1, npc, HG_DV, HG_DK), lambda h, n: (h, n, 0, 0))),
        scratch_shapes=[pltpu.VMEM((HG_DV, HG_DK), F32), pltpu.VMEM((c, LANE), F32), pltpu.VMEM((c, LANE), F32)],
        compiler_params=_params(("parallel", "arbitrary")),
    )(proj, proj, proj, proj, lb, gamma)


def _hg_bwd(dcat, dy_off, proj, lb, gamma, o_saved, states):
    s = proj.shape[0]
    rows = min(HG_ROWS, s)
    c = HG_CHUNK
    npc = rows // c
    nb = s // rows
    yb = dy_off // LANE

    def body(hq_ref, hf_ref, hi_ref, hg_ref, lb_ref, gm_ref, dy_ref, o_ref, st_ref,
             dq_ref, df_ref, di_ref, dg_ref, dlb_ref, dgm_ref, dst_sc, k_buf, b_buf, dk_buf, dv_buf):
        n = pl.program_id(1)

        @pl.when(n == 0)
        def _():
            dst_sc[...] = jnp.zeros_like(dst_sc)
            dlb_ref[...] = jnp.zeros_like(dlb_ref)
            dgm_ref[...] = jnp.zeros_like(dgm_ref)

        row = lax.broadcasted_iota(jnp.int32, (c, LANE), 0)
        rr = lax.broadcasted_iota(jnp.int32, (HG_SUB, 1), 0)
        lbv, gmv = lb_ref[...], gm_ref[...]
        for ch in reversed(range(npc)):
            sl = slice(ch * c, (ch + 1) * c)
            hq, z, v, g = hq_ref[sl, :], hf_ref[sl, :], hi_ref[sl, :], hg_ref[sl, :]
            q, k, logf, sig, sigm, f = _hg_gates(hq, z, lbv)
            b = _cumsum_rows(logf, row)
            bl = _colsum(logf)
            k_buf[...] = k
            b_buf[...] = b
            eb, ebl = jnp.exp(b), jnp.exp(bl)
            qe, kl = q * eb, k * jnp.exp(bl - b)
            st = st_ref[0, ch]
            dst = dst_sc[...]
            o, dyv = o_ref[sl, :], dy_ref[sl, :]
            r = lax.rsqrt(_rowmean(o * o) + EPS)
            on = o * r
            sg = _silu(g)
            dgm_ref[0] += _colsum(dyv * on * sg)
            dg_ref[sl, :] = dyv * on * gmv * _dsilu(g)
            go = dyv * gmv * sg
            do = r * (go - on * _rowmean(go * on))
            dq = _dot(do, st, NN) * eb
            dkl = _dot(v, dst, NN)
            dk = dkl * jnp.exp(bl - b)
            dv = _dot(kl, dst, NT)
            dbl = _colsum(dst * st) * ebl + _colsum(dkl * kl)
            dst_sc[...] = dst * ebl + _dot(do, qe, TN)
            ops = _hg_offdiag(q, k, b, row, b_buf)
            da = _dot(do, v, NT)
            a_off = sum(_dot(qf, kg, NT) for qf, kg, _, _ in ops)
            dv = dv + _dot(a_off, do, TN)
            for qf, kg, fj, gj in ops:
                dq = dq + _dot(da, kg, NN) * fj
                dk = dk + _dot(da, qf, TN) * gj
            dq_diag = []
            for i in range(HG_NSUB):
                r0 = i * HG_SUB
                qb, bb, dob = q[r0:r0 + HG_SUB], b[r0:r0 + HG_SUB], do[r0:r0 + HG_SUB]
                acc = jnp.zeros((HG_SUB, LANE), F32)
                for srow in range(HG_SUB):
                    t = r0 + srow
                    ks = k_buf[t:t + 1, :]
                    e, keep, w = _hg_diag_weights(qb, bb, ks, b_buf[t:t + 1, :], rr, srow)
                    dw = jnp.where(keep, jnp.sum(dob * hi_ref[ch * c + t:ch * c + t + 1, :], axis=1, keepdims=True), 0.0)
                    dv_buf[t:t + 1, :] = _colsum(w * dob)
                    dk_buf[t:t + 1, :] = _colsum(dw * qb * e)
                    acc = acc + dw * (ks * e)
                dq_diag.append(acc)
            dq = dq + jnp.concatenate(dq_diag, axis=0)
            dk = dk + dk_buf[...]
            di_ref[sl, :] = dv + dv_buf[...]
            db = q * dq - k * dk + jnp.where(row == c - 1, dbl, 0.0)
            dlogf = _cumsum_rows(db, row, reverse=True)
            dq_ref[sl, :] = dq * _dsilu(hq)
            s1 = sig * (1.0 - sig) * (1.0 - lbv)
            df_ref[sl, :] = dlogf * s1 / f - dk * s1
            dlb_ref[0] += _colsum(dlogf * sigm / f - dk * sigm)

    def col(off):
        return pl.BlockSpec((rows, LANE), lambda h, n, off=off: (nb - 1 - n, off // LANE + h))

    out_blk = pl.BlockSpec((rows, LANE), lambda h, n: (nb - 1 - n, h))
    acc_blk = pl.BlockSpec((1, 1, LANE), lambda h, n: (h, 0, 0))
    res = _pcall(
        body, name="hgrn_bwd",
        out_shape=tuple(jax.ShapeDtypeStruct((s, HG_WIDTH), F32) for _ in range(4))
        + (jax.ShapeDtypeStruct((HG_HEADS, 1, LANE), F32), jax.ShapeDtypeStruct((HG_HEADS, 1, LANE), F32)),
        grid=(HG_HEADS, nb),
        in_specs=[col(P_HQ), col(P_HF), col(P_HI), col(P_HG),
                  pl.BlockSpec((1, LANE), lambda h, n: (0, h)), pl.BlockSpec((1, LANE), lambda h, n: (0, 0)),
                  pl.BlockSpec((rows, LANE), lambda h, n: (nb - 1 - n, yb + h)), out_blk,
                  pl.BlockSpec((1, npc, HG_DV, HG_DK), lambda h, n: (h, nb - 1 - n, 0, 0))],
        out_specs=(out_blk, out_blk, out_blk, out_blk, acc_blk, acc_blk),
        scratch_shapes=[pltpu.VMEM((HG_DV, HG_DK), F32)] + [pltpu.VMEM((c, LANE), F32)] * 4,
        compiler_params=_params(("parallel", "arbitrary")),
    )(proj, proj, proj, proj, lb, gamma, dcat, o_saved, states)
    dq, df, di, dg, dlb, dgm = res
    return dq, df, di, dg, dlb.reshape(1, HG_WIDTH), dgm.reshape(HG_HEADS, LANE)


def _adamw_math(w, g, m, v):
    m = ADAM_B1 * m + (1.0 - ADAM_B1) * g
    v = ADAM_B2 * v + (1.0 - ADAM_B2) * (g * g)
    m_hat = m / (1.0 - ADAM_B1 ** ADAM_STEP)
    v_hat = v / (1.0 - ADAM_B2 ** ADAM_STEP)
    delta = -ADAM_LR * (m_hat / (jnp.sqrt(v_hat) + ADAM_EPS) + ADAM_WD * w)
    return delta, m, v


def _adamw(w, g, m, v, name):
    shape = w.shape
    width = shape[-1]
    rows = int(np.prod(shape[:-1]))
    tm = rows
    while tm * width * 4 > (1 << 20) and tm % 16 == 0:
        tm //= 2

    def fn(i, w_ref, g_ref, m_ref, v_ref, d_ref, mo_ref, vo_ref):
        d, mn, vn = _adamw_math(w_ref[...], g_ref[...], m_ref[...], v_ref[...])
        d_ref[...] = d
        mo_ref[...] = mn
        vo_ref[...] = vn

    v2 = lambda t: t.reshape(rows, width)
    res = _rows(fn, name=name, s=rows, tm=tm, ins=[("row", v2(t), 0, width) for t in (w, g, m, v)],
                outs=[("row", width, F32)] * 3)
    return tuple(r.reshape(shape) for r in res)


def _ada_grad_adamw(cact_all, dmod_cols, w, m, v):
    n_layers, kdim, n = w.shape
    tm, tn = _tile(kdim, 256), _tile(n, 1024)

    def body(c_ref, d_ref, w_ref, m_ref, v_ref, g_ref, dl_ref, mo_ref, vo_ref):
        g = _dot(c_ref[...], d_ref[...], TN)
        d, mn, vn = _adamw_math(w_ref[...], g, m_ref[...], v_ref[...])
        g_ref[...] = g
        dl_ref[...] = d
        mo_ref[...] = mn
        vo_ref[...] = vn

    blk = pl.BlockSpec((None, tm, tn), lambda l, i, j: (l, i, j))
    return _pcall(
        body, name="ada_grad_adamw", out_shape=tuple(jax.ShapeDtypeStruct(w.shape, F32) for _ in range(4)),
        grid=(n_layers, kdim // tm, n // tn),
        in_specs=[pl.BlockSpec((cact_all.shape[0], tm), lambda l, i, j: (0, i)),
                  pl.BlockSpec((None, dmod_cols.shape[1], tn), lambda l, i, j: (l, 0, j)), blk, blk, blk],
        out_specs=(blk, blk, blk, blk),
        compiler_params=_params(("parallel", "parallel", "parallel")),
    )(cact_all, dmod_cols, w, m, v)


def _me():
    return lax.axis_index("x"), lax.axis_index("y"), lax.axis_index("c")


def _flip(k):
    x, y, c = _me()
    return (x ^ ((k >> 2) & 1), y ^ ((k >> 1) & 1), c ^ (k & 1))


def _lin(dev):
    return 4 * dev[0] + 2 * dev[1] + dev[2]


ANY = pl.BlockSpec(memory_space=pl.ANY)


def _all_gather8(x, name):
    def body(x_ref, out_ref, send_sems, recv_sems, local_sem):
        me = _lin(_me())
        mine = pltpu.make_async_copy(x_ref, out_ref.at[me], local_sem)
        mine.start()
        copies = []
        for k in range(1, 8):
            cp = pltpu.make_async_remote_copy(src_ref=x_ref, dst_ref=out_ref.at[me], send_sem=send_sems.at[k - 1],
                                              recv_sem=recv_sems.at[k - 1], device_id=_flip(k), device_id_type=MESH)
            cp.start()
            copies.append(cp)
        for k in range(1, 8):
            pltpu.make_async_remote_copy(src_ref=x_ref, dst_ref=out_ref.at[_lin(_flip(k))], send_sem=send_sems.at[k - 1],
                                         recv_sem=recv_sems.at[k - 1], device_id=_flip(k), device_id_type=MESH).wait_recv()
        for cp in copies:
            cp.wait_send()
        mine.wait()

    return _pcall(
        body, name=name, out_shape=jax.ShapeDtypeStruct((8,) + x.shape, x.dtype),
        in_specs=[ANY], out_specs=ANY,
        scratch_shapes=[pltpu.SemaphoreType.DMA((7,)), pltpu.SemaphoreType.DMA((7,)), pltpu.SemaphoreType.DMA],
    )(x)


CHIP_FLIPS = (2, 4, 6)


def _row_tile(r, cdim):
    best = SUBLANE
    for t in range(SUBLANE, r + 1, SUBLANE):
        if r % t == 0 and t * cdim * 4 <= (3 << 20):
            best = t
    assert r % best == 0
    return best


def _gather_weights(ws):
    n = len(ws)
    hbm = pltpu.MemorySpace.HBM
    w_refs = [jax.new_ref(w, memory_space=hbm) for w in ws]
    out_refs = [jax.empty_ref(jax.ShapeDtypeStruct((N_CHIPS,) + w.shape, w.dtype), memory_space=hbm) for w in ws]

    @pl.kernel(mesh=plsc.ScalarSubcoreMesh(axis_name="sequencer", num_cores=1), name="gather_weights",
               scratch_types=(pltpu.SemaphoreType.DMA((6 * n,)), pltpu.SemaphoreType.DMA((6 * n,))),
               compiler_params=pltpu.CompilerParams(collective_id=2))
    def launch(send_sems, recv_sems):
        barrier = pltpu.get_barrier_semaphore()
        for k in CHIP_FLIPS + (1,):
            pl.semaphore_signal(barrier, inc=1, device_id=_flip(k), device_id_type=MESH)
        pl.semaphore_wait(barrier, len(CHIP_FLIPS) + 1)
        x, y, c = _me()
        sib = _flip(1)

        def slot(a, dev, half):
            return out_refs[a].at[2 * dev[0] + dev[1], half]

        first = []
        for a in range(n):
            for j, k in enumerate(CHIP_FLIPS):
                cp = pltpu.make_async_remote_copy(src_ref=w_refs[a].at[c], dst_ref=slot(a, (x, y), c), send_sem=send_sems.at[6 * a + j],
                                                  recv_sem=recv_sems.at[6 * a + j], device_id=_flip(k), device_id_type=MESH)
                cp.start()
                first.append(cp)
        passed = []
        for a in range(n):
            for j, k in enumerate(CHIP_FLIPS):
                src = _flip(k)
                landed = slot(a, src, c)
                pltpu.make_async_remote_copy(src_ref=landed, dst_ref=landed, send_sem=send_sems.at[6 * a + j],
                                             recv_sem=recv_sems.at[6 * a + j], device_id=src, device_id_type=MESH).wait_recv()
                cp = pltpu.make_async_remote_copy(src_ref=landed, dst_ref=landed, send_sem=send_sems.at[6 * a + 3 + j],
                                                  recv_sem=recv_sems.at[6 * a + 3 + j], device_id=sib, device_id_type=MESH)
                cp.start()
                passed.append(cp)
        for a in range(n):
            for j, k in enumerate(CHIP_FLIPS):
                got = slot(a, _flip(k), 1 - c)
                pltpu.make_async_remote_copy(src_ref=got, dst_ref=got, send_sem=send_sems.at[6 * a + 3 + j],
                                             recv_sem=recv_sems.at[6 * a + 3 + j], device_id=sib, device_id_type=MESH).wait_recv()
        for cp in first + passed:
            cp.wait_send()

    launch()
    chip = 2 * lax.axis_index("x") + lax.axis_index("y")
    return [lax.dynamic_update_index_in_dim(r[...], w, chip, axis=0) for r, w in zip(out_refs, ws)]


def _sibling_halves(gs):
    n = len(gs)

    def body(*refs):
        s_refs, out_refs = refs[:n], refs[n:2 * n]
        send_sems, recv_sems = refs[2 * n:]
        c = lax.axis_index("c")
        sib = _flip(1)
        copies = []
        for a in range(n):
            for j in range(N_CHIPS):
                cp = pltpu.make_async_remote_copy(src_ref=s_refs[a].at[j, 1 - c], dst_ref=out_refs[a].at[j], send_sem=send_sems.at[4 * a + j],
                                                  recv_sem=recv_sems.at[4 * a + j], device_id=sib, device_id_type=MESH)
                cp.start()
                copies.append(cp)
        for cp in copies:
            cp.wait()

    return _pcall(
        body, name="rs_sibling_halves", out_shape=tuple(jax.ShapeDtypeStruct((N_CHIPS,) + g.shape[2:], g.dtype) for g in gs),
        in_specs=[ANY] * n, out_specs=(ANY,) * n,
        scratch_shapes=[pltpu.SemaphoreType.DMA((4 * n,)), pltpu.SemaphoreType.DMA((4 * n,))],
    )(*gs)


def _scatter_to_chips(parts):
    n = len(parts)
    hbm = pltpu.MemorySpace.HBM
    p_refs = [jax.new_ref(p, memory_space=hbm) for p in parts]
    out_refs = [jax.empty_ref(jax.ShapeDtypeStruct((3,) + p.shape[1:], p.dtype), memory_space=hbm) for p in parts]

    @pl.kernel(mesh=plsc.ScalarSubcoreMesh(axis_name="sequencer", num_cores=1), name="scatter_to_chips",
               scratch_types=(pltpu.SemaphoreType.DMA((3 * n,)), pltpu.SemaphoreType.DMA((3 * n,))),
               compiler_params=pltpu.CompilerParams(collective_id=1))
    def launch(send_sems, recv_sems):
        barrier = pltpu.get_barrier_semaphore()
        for k in CHIP_FLIPS:
            pl.semaphore_signal(barrier, inc=1, device_id=_flip(k), device_id_type=MESH)
        pl.semaphore_wait(barrier, len(CHIP_FLIPS))
        copies = []
        for a in range(n):
            for j, k in enumerate(CHIP_FLIPS):
                to = _flip(k)
                cp = pltpu.make_async_remote_copy(src_ref=p_refs[a].at[2 * to[0] + to[1]], dst_ref=out_refs[a].at[j],
                                                  send_sem=send_sems.at[3 * a + j], recv_sem=recv_sems.at[3 * a + j],
                                                  device_id=to, device_id_type=MESH)
                cp.start()
                copies.append(cp)
        for cp in copies:
            cp.wait()

    launch()
    return [r[...] for r in out_refs]


def _sibling_result(halves):
    n = len(halves)

    def body(*refs):
        h_refs, out_refs = refs[:n], refs[n:2 * n]
        send_sems, recv_sems = refs[2 * n:]
        sib = _flip(1)
        copies = []
        for a in range(n):
            cp = pltpu.make_async_remote_copy(src_ref=h_refs[a], dst_ref=out_refs[a], send_sem=send_sems.at[a],
                                              recv_sem=recv_sems.at[a], device_id=sib, device_id_type=MESH)
            cp.start()
            copies.append(cp)
        for cp in copies:
            cp.wait()

    theirs = _pcall(
        body, name="rs_sibling_result", out_shape=tuple(jax.ShapeDtypeStruct(h.shape, h.dtype) for h in halves),
        in_specs=[ANY] * n, out_specs=(ANY,) * n,
        scratch_shapes=[pltpu.SemaphoreType.DMA((n,)), pltpu.SemaphoreType.DMA((n,))],
    )(*halves)
    c = lax.axis_index("c")
    return [jnp.where(c == 0, jnp.stack([m, t]), jnp.stack([t, m])) for m, t in zip(halves, theirs)]


def _add_halves(g, r1, name):
    n, _, r, cdim = g.shape
    tm = _row_tile(r, cdim)

    def body(c_ref, g_ref, r_ref, o_ref):
        o_ref[...] = g_ref[...] + r_ref[...]

    return _pcall(
        body, name=name, out_shape=jax.ShapeDtypeStruct((n, r, cdim), g.dtype),
        grid_spec=pltpu.PrefetchScalarGridSpec(
            num_scalar_prefetch=1, grid=(n, r // tm),
            in_specs=[pl.BlockSpec((None, None, tm, cdim), lambda j, i, c_ref: (j, c_ref[0], i, 0)),
                      pl.BlockSpec((None, tm, cdim), lambda j, i, c_ref: (j, i, 0))],
            out_specs=pl.BlockSpec((None, tm, cdim), lambda j, i, c_ref: (j, i, 0))),
        compiler_params=_params(("parallel", "parallel")),
    )(lax.axis_index("c").astype(jnp.int32).reshape(1), g, r1)


def _add_chips(part, got, name):
    _, r, cdim = part.shape
    tm = _row_tile(r, cdim)

    def body(chip_ref, p_ref, g_ref, o_ref):
        o_ref[...] = ((p_ref[...] + g_ref[0]) + g_ref[1]) + g_ref[2]

    chip = (2 * lax.axis_index("x") + lax.axis_index("y")).astype(jnp.int32).reshape(1)
    return _pcall(
        body, name=name, out_shape=jax.ShapeDtypeStruct((r, cdim), part.dtype),
        grid_spec=pltpu.PrefetchScalarGridSpec(
            num_scalar_prefetch=1, grid=(r // tm,),
            in_specs=[pl.BlockSpec((None, tm, cdim), lambda i, chip_ref: (chip_ref[0], i, 0)),
                      pl.BlockSpec((3, tm, cdim), lambda i, chip_ref: (0, i, 0))],
            out_specs=pl.BlockSpec((tm, cdim), lambda i, chip_ref: (i, 0))),
        compiler_params=_params(("parallel",)),
    )(chip, part, got)


def _reduce_scatter_start(gs, names):
    r1 = _sibling_halves(gs)
    parts = [_add_halves(g, r, "add_halves_" + nm) for g, r, nm in zip(gs, r1, names)]
    return parts, _scatter_to_chips(parts), names


def _reduce_scatter_finish(started):
    parts, got, names = started
    mine = [_add_chips(p, q, "add_chips_" + nm) for p, q, nm in zip(parts, got, names)]
    return _sibling_result(mine)


def _sum8(x):
    _, r, n = x.shape
    tm = 128 if r % 128 == 0 else r

    def body(x_ref, o_ref):
        acc = x_ref[0]
        for d in range(1, 8):
            acc = acc + x_ref[d]
        o_ref[...] = acc

    return _pcall(body, name="sum8", out_shape=jax.ShapeDtypeStruct((r, n), x.dtype), grid=(r // tm,),
                  in_specs=[pl.BlockSpec((8, tm, n), lambda i: (0, i, 0))], out_specs=pl.BlockSpec((tm, n), lambda i: (i, 0)),
                  compiler_params=_params(("parallel",)))(x)


SHARDED = ("w_in", "s5_w_glu", "mla_w_uq", "mla_w_ukv", "w_out", "ffn_w_up", "ffn_w_down")
COL_SHARDED = ("w_in", "mla_w_uq", "mla_w_ukv", "ffn_w_up")
REPLICATED = ("s5_lambda_re", "s5_lambda_im", "s5_log_dt", "s5_b_re", "s5_b_im", "s5_c_re", "s5_c_im", "s5_d",
              "mla_q_norm", "mla_kv_norm", "hg_lb_logits", "hg_out_norm", "mix_pre_norm", "mix_post_norm",
              "ffn_pre_norm", "ffn_post_norm", "ffn_conv_b")
WEIGHTS = ("w_in", "s5_lambda_re", "s5_lambda_im", "s5_log_dt", "s5_b_re", "s5_b_im", "s5_c_re", "s5_c_im", "s5_d",
           "s5_w_glu", "mla_q_norm", "mla_w_uq", "mla_kv_norm", "mla_w_ukv", "hg_lb_logits", "hg_out_norm", "w_out",
           "mix_pre_norm", "mix_post_norm", "ffn_pre_norm", "ffn_post_norm", "ffn_w_up", "ffn_conv_w", "ffn_conv_b",
           "ffn_w_down", "w_ada", "b_ada")


FF_PAD = 256


def _halves(t):
    return t.reshape(t.shape[:-2] + (2, t.shape[-2] // 2, t.shape[-1]))


def _unhalve(t):
    return t.reshape(t.shape[:-3] + (2 * t.shape[-2], t.shape[-1]))


def _cols_from_chips(t):
    return jnp.concatenate([t[j] for j in range(N_CHIPS)], axis=1)


def _swap_half(t):
    half = t.shape[-1] // 2
    return jnp.concatenate([-t[..., half:], t[..., :half]], axis=-1)


def _prep_win(w):
    s5, cq, ckv, kr, hq, hf, hi, hg = jnp.split(w, (512, 1024, 1280, 1344, 1856, 2368, 2880), axis=1)
    pad = jnp.zeros((w.shape[0], PROJ_W - 3456), w.dtype)
    return jnp.concatenate([s5, cq, hq, hf, hi, hg, ckv, kr, _swap_half(kr), pad], axis=1)


def _prep_wq(w):
    w3 = w.reshape(w.shape[0], MLA_HEADS, MLA_NOPE + MLA_ROPE)
    nope, rope = w3[..., :MLA_NOPE], w3[..., MLA_NOPE:]
    pair = jnp.concatenate([rope, _swap_half(rope)], axis=-1)
    return jnp.concatenate([nope.reshape(w.shape[0], -1), pair.reshape(w.shape[0], -1)], axis=1)


def _pad_ff_cols(w, cpad):
    r = w.shape[0]
    w3 = w.reshape(r, N_CHIPS, -1)
    return jnp.pad(w3, ((0, 0), (0, 0), (0, cpad - w3.shape[2]))).reshape(r, N_CHIPS * cpad)


def _pad_ff_rows(w, cpad):
    w3 = w.reshape(2, 2 * w.shape[1], w.shape[2])
    return jnp.pad(w3, ((0, 0), (0, cpad - w3.shape[1]), (0, 0))).reshape(2 * cpad, w.shape[2])


def _interleave(re, im, axis):
    re, im = jnp.moveaxis(re, axis, -1), jnp.moveaxis(im, axis, -1)
    lead = re.shape[:-1]
    both = jnp.stack([re.reshape(lead + (N_STATE // SCAN_W, SCAN_W)), im.reshape(lead + (N_STATE // SCAN_W, SCAN_W))], axis=-2)
    return jnp.moveaxis(both.reshape(lead + (2 * N_STATE,)), -1, axis)


def _s5_prep(lre, lim, logdt, bre, bim, cre, cim):
    dt = jnp.exp(logdt)[:, None]
    er = jnp.exp(lre * dt)
    ar, ai = er * jnp.cos(lim * dt), er * jnp.sin(lim * dt)
    nr, den = ar - 1.0, lre * lre + lim * lim
    cr, ci = (nr * lre + ai * lim) / den, (ai * lre - nr * lim) / den
    bbr = cr[..., None] * bre - ci[..., None] * bim
    bbi = cr[..., None] * bim + ci[..., None] * bre
    eye = jnp.eye(S5_GROUPS, dtype=F32)[:, None, :, None]

    def block_diag(t):
        return (t[:, :, None, :] * eye).reshape(S5_GROUPS * t.shape[1], S5_GROUPS * t.shape[2])

    tr = lambda t: jnp.transpose(t, (0, 2, 1))
    bd = _interleave(block_diag(tr(bbr)), block_diag(tr(bbi)), 1)
    cd = _interleave(block_diag(tr(cre)), block_diag(tr(-cim)), 0)
    a = _interleave(ar.reshape(1, N_STATE), ai.reshape(1, N_STATE), 1)
    return a, bd, cd


def _lower_bounds(logits):
    probs = jax.nn.softmax(logits, axis=0)
    return jnp.cumsum(probs, axis=0) - probs[0:1]


def _rope_table(positions):
    inv_freq = 1.0 / (ROPE_THETA ** (jnp.arange(0, MLA_ROPE, 2, dtype=F32) / MLA_ROPE))
    ang = positions.astype(F32)[:, None] * inv_freq
    cos, sin = jnp.cos(ang), jnp.sin(ang)
    return jnp.concatenate([cos, cos, sin, sin], axis=1)


def _split_mod(mod):
    return [mod[:, i * D_MODEL:(i + 1) * D_MODEL] for i in range(6)]


def _layer_fwd(x, wl, mod, cs):
    s = x.shape[0]
    ffp = wl["wdown_p"].shape[0]
    sh1, sc1, g1, sh2, sc2, g2 = _split_mod(mod)
    h1 = _normmod_fwd(x, wl["mix_pre_norm"], sc1, sh1, "mix_pre")
    proj = _mm(h1, wl["win_p"], mode="nn", name="in_proj", m=s, n=PROJ_W, k=D_MODEL)
    out_s5, s5_saved = _s5_fwd(proj, wl, s)
    o_mla, mla_saved = _mla_fwd(proj, wl, cs, s)
    y_hg, o_hg, states = _hg_fwd(proj, wl["hg_lb"], wl["hg_out_norm"])
    cat = jnp.concatenate([out_s5, o_mla.astype(BF16), y_hg], axis=1)
    mixed = _mm(cat, wl["w_out"], mode="nn", name="out_proj", m=s, n=D_MODEL, k=D_MODEL)
    x2 = _postnorm_fwd(x, mixed, wl["mix_post_norm"], g1, "mix_post")
    h2 = _normmod_fwd(x2, wl["ffn_pre_norm"], sc2, sh2, "ffn_pre")
    up = _mm(h2, wl["wup_cm"], mode="nn", name="ffn_up", m=s, n=2 * ffp, k=D_MODEL, cm=ffp // 2, tn=ffp // 4)
    act = _ffn_act_fwd(up, wl["conv_w_p"], wl["conv_b_p"], ffp)
    y = _mm(act, wl["wdown_p"], mode="nn", name="ffn_down", m=s, n=D_MODEL, k=ffp)
    x3 = _postnorm_fwd(x2, y, wl["ffn_post_norm"], g2, "ffn_post")
    return x3, (x, h1, proj, s5_saved, mla_saved, o_hg, states, cat, mixed, x2, h2, up, act, y)


def _layer_bwd(dx3, saved, wl, mod, cs):
    x, h1, proj, s5_saved, mla_saved, o_hg, states, cat, mixed, x2, h2, up, act, y = saved
    s = x.shape[0]
    ffp = wl["wdown_p"].shape[0]
    sh1, sc1, g1, sh2, sc2, g2 = _split_mod(mod)
    g = {}
    dy, g["ffn_post_norm"], dg2 = _postnorm_bwd(dx3, y, wl["ffn_post_norm"], g2, "ffn_post_bwd")
    da = _mm(dy, wl["wdown_p"], mode="nt", name="ffn_down_dx", m=s, n=ffp, k=D_MODEL)
    g["wdown_p"] = _mm(act, dy, mode="tn", name="ffn_down_dw", m=ffp, n=D_MODEL, k=s)
    dug, duv, dwg, dwv, dbg, dbv = _ffn_act_bwd(da, up, wl["conv_w_p"], wl["conv_b_p"], ffp)
    g["conv_w_p"] = jnp.concatenate([dwg, dwv], axis=1)
    g["conv_b_p"] = jnp.concatenate([dbg, dbv], axis=1)
    dup = jnp.concatenate([_conv_bwd_input(dug, wl["conv_w_p"], 0, "ffn_conv_bwd_gate"),
                           _conv_bwd_input(duv, wl["conv_w_p"], ffp, "ffn_conv_bwd_val")], axis=1)
    dh2 = _mm(dup, wl["wup_cm"], mode="nt", name="ffn_up_dx", m=s, n=D_MODEL, k=2 * ffp, cm=ffp // 2, tk=ffp // 4)
    g["wup_cm"] = _mm(h2, dup, mode="tn", name="ffn_up_dw", m=D_MODEL, n=2 * ffp, k=s, cm=ffp // 2, tn=ffp // 4)
    dx2, g["ffn_pre_norm"], dsc2, dsh2 = _normmod_bwd(dh2, x2, wl["ffn_pre_norm"], sc2, dx3, "ffn_pre_bwd")
    dmixed, g["mix_post_norm"], dg1 = _postnorm_bwd(dx2, mixed, wl["mix_post_norm"], g1, "mix_post_bwd")
    dcat = _mm(dmixed, wl["w_out"], mode="nt", name="out_proj_dx", m=s, n=D_MODEL, k=D_MODEL)
    g["w_out"] = _mm(cat, dmixed, mode="tn", name="out_proj_dw", m=D_MODEL, n=D_MODEL, k=s)
    du_s5, s5g = _s5_bwd(dcat, proj, wl, s5_saved, s)
    dcq, dckv, dkr, mlag = _mla_bwd(dcat, proj, wl, cs, mla_saved, s)
    dhq, dhf, dhi, dhg, g["hg_lb"], dgm = _hg_bwd(dcat, S5_WIDTH + MLA_WIDTH, proj, wl["hg_lb"], wl["hg_out_norm"], o_hg, states)
    g["hg_out_norm"] = jnp.sum(dgm, axis=0, keepdims=True)
    g.update(s5g)
    g.update(mlag)
    dproj = jnp.concatenate([du_s5, dcq, dhq, dhf, dhi, dhg, dckv, dkr, jnp.zeros((s, PROJ_W - 3456), F32)], axis=1).astype(BF16)
    dh1 = _mm(dproj, wl["win_p"], mode="nt", name="in_proj_dx", m=s, n=D_MODEL, k=PROJ_W)
    g["win_p"] = _mm(h1, dproj, mode="tn", name="in_proj_dw", m=D_MODEL, n=PROJ_W, k=s)
    dx, g["mix_pre_norm"], dsc1, dsh1 = _normmod_bwd(dh1, x, wl["mix_pre_norm"], sc1, dx2, "mix_pre_bwd")
    dmod = jnp.concatenate([dsh1, dsc1, dg1, dsh2, dsc2, dg2], axis=1)
    return dx, g, dmod


def _prepare_layer(gathered, conv_w, rep, cpad):
    def sharded_prep(w_in, s5_w_glu, mla_w_uq, mla_w_ukv, w_out, ffn_w_up, ffn_w_down, ffn_conv_w):
        merge = lambda t: t.reshape(N_CHIPS * t.shape[1], t.shape[2])
        return {"win_p": _prep_win(_cols_from_chips(w_in)), "s5_w_glu": merge(s5_w_glu), "mla_wq": _prep_wq(_cols_from_chips(mla_w_uq)),
                "mla_w_ukv": _cols_from_chips(mla_w_ukv), "w_out": merge(w_out), "wup_cm": ffn_w_up,
                "wdown_p": _pad_ff_rows(ffn_w_down, cpad), "conv_w_p": _pad_ff_cols(ffn_conv_w, cpad)}

    def rep_prep(lre, lim, logdt, bre, bim, cre, cim, conv_b):
        a, bd, cd = _s5_prep(lre, lim, logdt, bre, bim, cre, cim)
        return {"s5_a": a, "s5_bd": bd, "s5_cd": cd, "conv_b_p": _pad_ff_cols(conv_b, cpad)}

    sh_args = [gathered[k] for k in SHARDED] + [conv_w]
    rep_names = ("s5_lambda_re", "s5_lambda_im", "s5_log_dt", "s5_b_re", "s5_b_im", "s5_c_re", "s5_c_im", "ffn_conv_b")
    rep_args = [rep[k] for k in rep_names]
    wl = sharded_prep(*sh_args)
    rep_out, rep_vjp = jax.vjp(rep_prep, *rep_args)
    wl.update(rep_out)
    sh_t = jax.linear_transpose(sharded_prep, *[jax.ShapeDtypeStruct(a.shape, F32) for a in sh_args])

    def back(g):
        out = dict(zip(SHARDED + ("ffn_conv_w",), sh_t({k: g[k] for k in ("win_p", "s5_w_glu", "mla_wq", "mla_w_ukv", "w_out", "wup_cm", "wdown_p", "conv_w_p")})))
        out.update(zip(rep_names, rep_vjp({k: g[k] for k in ("s5_a", "s5_bd", "s5_cd", "conv_b_p")})))
        return out

    return wl, back


PER_LAYER_ROWS = ("s5_d", "mla_q_norm", "mla_kv_norm", "hg_out_norm", "mix_pre_norm", "mix_post_norm", "ffn_pre_norm", "ffn_post_norm")


def _flat_pad(parts, unit):
    flat = jnp.concatenate([p.reshape(-1) for p in parts])
    n = -(-flat.shape[0] // unit) * unit
    return jnp.pad(flat, (0, n - flat.shape[0])).reshape(-1, FLAT_COLS)


def _split_flat(flat, like):
    flat = flat.reshape(-1)
    out, pos = [], 0
    for t in like:
        out.append(flat[pos:pos + t.size].reshape(t.shape))
        pos += t.size
    return out


def kernel(x, c, positions, w_in, s5_lambda_re, s5_lambda_im, s5_log_dt, s5_b_re, s5_b_im, s5_c_re, s5_c_im, s5_d, s5_w_glu, mla_q_norm, mla_w_uq, mla_kv_norm, mla_w_ukv, hg_lb_logits, hg_out_norm, w_out, mix_pre_norm, mix_post_norm, ffn_pre_norm, ffn_post_norm, ffn_w_up, ffn_conv_w, ffn_conv_b, ffn_w_down, w_ada, b_ada, loss_target, m_w_in, m_s5_lambda_re, m_s5_lambda_im, m_s5_log_dt, m_s5_b_re, m_s5_b_im, m_s5_c_re, m_s5_c_im, m_s5_d, m_s5_w_glu, m_mla_q_norm, m_mla_w_uq, m_mla_kv_norm, m_mla_w_ukv, m_hg_lb_logits, m_hg_out_norm, m_w_out, m_mix_pre_norm, m_mix_post_norm, m_ffn_pre_norm, m_ffn_post_norm, m_ffn_w_up, m_ffn_conv_w, m_ffn_conv_b, m_ffn_w_down, m_w_ada, m_b_ada, v_w_in, v_s5_lambda_re, v_s5_lambda_im, v_s5_log_dt, v_s5_b_re, v_s5_b_im, v_s5_c_re, v_s5_c_im, v_s5_d, v_s5_w_glu, v_mla_q_norm, v_mla_w_uq, v_mla_kv_norm, v_mla_w_ukv, v_hg_lb_logits, v_hg_out_norm, v_w_out, v_mix_pre_norm, v_mix_post_norm, v_ffn_pre_norm, v_ffn_post_norm, v_ffn_w_up, v_ffn_conv_w, v_ffn_conv_b, v_ffn_w_down, v_w_ada, v_b_ada):
    p = dict(locals())
    n_layers = w_in.shape[0]
    c_up = ffn_w_up.shape[2]
    cpad = -(-c_up // FF_PAD) * FF_PAD
    xs, target = x[0], loss_target[0]
    me = 4 * lax.axis_index("x") + 2 * lax.axis_index("y") + lax.axis_index("c")
    chip = 2 * lax.axis_index("x") + lax.axis_index("y")
    cs = _rope_table(positions[0])

    cact = jax.nn.silu(_all_gather8(c, "gather_c")[:, 0, :])
    ada_cols = w_ada.shape[2]
    mod_part = jnp.stack([_mm(cact, w_ada[l], mode="nn", name="ada_mod", m=8, n=ada_cols, k=D_MODEL) for l in range(n_layers)])
    mod_all = _all_gather8(mod_part.reshape(1, -1), "gather_mod").reshape(N_CHIPS, 2, n_layers, 8, ada_cols)[:, 0]
    mod_mine = lax.dynamic_index_in_dim(mod_all, me, axis=2, keepdims=False)
    mods = mod_mine.transpose(1, 0, 2).reshape(n_layers, -1) + b_ada

    conv_w_all = _all_gather8(ffn_conv_w.reshape(1, -1), "gather_conv_w").reshape(N_CHIPS, 2, n_layers, 3, -1)[:, 0]
    conv_w_full = conv_w_all.transpose(1, 2, 0, 3).reshape(n_layers, 3, -1)

    lbs, lb_vjp = jax.vjp(_lower_bounds, hg_lb_logits)

    gathers = []
    for l in range(n_layers):
        shards = {k: p[k][l] for k in SHARDED}
        shards["ffn_w_up"] = jnp.pad(shards["ffn_w_up"], ((0, 0), (0, cpad - c_up)))
        gathers.append(_gather_weights([_halves(shards[k].astype(BF16)) for k in SHARDED]))

    h = xs
    saved, layers = [], []
    for l in range(n_layers):
        got = gathers[l]
        if l > 0:
            got, h = lax.optimization_barrier((got, h))
        gathered = {k: _unhalve(t) for k, t in zip(SHARDED, got)}
        rep = {k: p[k][l] for k in ("s5_lambda_re", "s5_lambda_im", "s5_log_dt", "s5_b_re", "s5_b_im", "s5_c_re", "s5_c_im")}
        rep["ffn_conv_b"] = ffn_conv_b[l][None, :]
        wl, back = _prepare_layer(gathered, conv_w_full[l], rep, cpad)
        for k in PER_LAYER_ROWS:
            wl[k] = p[k][l][None, :]
        wl["hg_lb"] = lbs[l][None, :]
        layers.append((wl, back))
        h, sv = _layer_fwd(h, wl, mods[l][None, :], cs)
        saved.append(sv)
    dh, loss_part = _loss_grad(h, target)
    loss = lax.psum(loss_part[0, 0], ("x", "y", "c"))

    grads = {k: [None] * n_layers for k in WEIGHTS}
    dmods, dlbs = [None] * n_layers, [None] * n_layers

    def take_in(layer, started):
        for k, t in zip(SHARDED, _reduce_scatter_finish(started)):
            grads[k][layer] = _unhalve(t)
        grads["ffn_w_up"][layer] = grads["ffn_w_up"][layer][:, :c_up]

    under_way = None
    for l in reversed(range(n_layers)):
        wl, back = layers[l]
        dh, g, dmods[l] = _layer_bwd(dh, saved[l], wl, mods[l][None, :], cs)
        if under_way is not None:
            take_in(*under_way)
        ref_g = back(g)
        under_way = (l, _reduce_scatter_start([_halves(ref_g[k]) for k in SHARDED], SHARDED))
        grads["ffn_conv_w"][l] = ref_g["ffn_conv_w"]
        for k in ("s5_lambda_re", "s5_lambda_im", "s5_log_dt", "s5_b_re", "s5_b_im", "s5_c_re", "s5_c_im"):
            grads[k][l] = ref_g[k]
        grads["ffn_conv_b"][l] = ref_g["ffn_conv_b"][0]
        for k in PER_LAYER_ROWS:
            grads[k][l] = g[k][0]
        dlbs[l] = g["hg_lb"][0]
    take_in(*under_way)
    grads = {k: jnp.stack(v) for k, v in grads.items() if v[0] is not None}
    grads["hg_lb_logits"] = lb_vjp(jnp.stack(dlbs))[0]

    summed = REPLICATED + ("ffn_conv_w",)
    rep_flat = _flat_pad([grads[k] for k in summed], 128 * FLAT_COLS)
    rep_sum = _sum8(_all_gather8(rep_flat, "gather_small_grads"))
    for k, t in zip(summed, _split_flat(rep_sum, [grads[k] for k in summed])):
        grads[k] = t
    conv_cols = grads["ffn_conv_w"].reshape(n_layers, 3, N_CHIPS, c_up)
    grads["ffn_conv_w"] = lax.dynamic_index_in_dim(conv_cols, chip, axis=2, keepdims=False)

    dmod_all = _all_gather8(jnp.stack(dmods).reshape(n_layers * 6 * D_MODEL // FLAT_COLS, FLAT_COLS), "gather_dmod")
    grads["b_ada"] = _sum8(dmod_all).reshape(n_layers, 6 * D_MODEL)
    dmod_cols = lax.dynamic_slice_in_dim(dmod_all.reshape(8, n_layers, N_CHIPS, ada_cols), chip, 1, axis=2)[:, :, 0]
    grads["w_ada"], delta_ada, new_m_ada, new_v_ada = _ada_grad_adamw(cact, dmod_cols.transpose(1, 0, 2), w_ada, m_w_ada, v_w_ada)

    delta, new_m, new_v = {"w_ada": delta_ada}, {"w_ada": new_m_ada}, {"w_ada": new_v_ada}
    for k in SHARDED:
        delta[k], new_m[k], new_v[k] = _adamw(p[k], grads[k], p["m_" + k], p["v_" + k], "adamw_" + k)
    small = REPLICATED + ("b_ada", "ffn_conv_w")
    flats = [_flat_pad([src[k] for k in small], 128 * FLAT_COLS)
             for src in (p, grads, {k: p["m_" + k] for k in small}, {k: p["v_" + k] for k in small})]
    for dst, flat in zip((delta, new_m, new_v), _adamw(*flats, "adamw_small")):
        for k, t in zip(small, _split_flat(flat, [p[k] for k in small])):
            dst[k] = t

    return (loss, dh[None], *[grads[k] for k in WEIGHTS], *[delta[k] for k in WEIGHTS],
            *[new_m[k] for k in WEIGHTS], *[new_v[k] for k in WEIGHTS])
```

```python
import functools
import math

import numpy as np
import jax
import jax.numpy as jnp
from jax import lax
from jax.experimental import pallas as pl
from jax.experimental.pallas import tpu as pltpu
from jax.experimental.pallas import tpu_sc as plsc

F32 = jnp.float32
BF16 = jnp.bfloat16
MESH = pl.DeviceIdType.MESH

D_MODEL = 2048
S5_WIDTH, S5_GROUP, S5_GROUPS, S5_STATE = 512, 16, 32, 64
MLA_HEADS, MLA_NOPE, MLA_ROPE, MLA_V = 8, 128, 64, 128
MLA_Q_RANK, MLA_KV_RANK = 512, 256
MLA_WIDTH = MLA_HEADS * MLA_V
ROPE_THETA = 10000.0
HG_HEADS, HG_DK, HG_DV = 4, 128, 128
HG_WIDTH = HG_HEADS * HG_DV
EPS = 1e-6
ADAM_LR, ADAM_B1, ADAM_B2, ADAM_EPS, ADAM_WD, ADAM_STEP = 0.001, 0.9, 0.999, 1e-08, 0.01, 10
GELU_K0 = math.sqrt(2.0 / math.pi)
GELU_K1 = 0.044715

LANE = 128
SUBLANE = 8
VMEM_LIMIT = 56 * 1024 * 1024

P_S5, P_CQ, P_HQ, P_HF, P_HI, P_HG, P_CKV, P_KR = 0, 512, 1024, 1536, 2048, 2560, 3072, 3328
PROJ_W = 3584
N_STATE = S5_GROUPS * S5_STATE
SCAN_W = 512
HG_CHUNK = 64
N_CHIPS = 4
FLAT_COLS = 1024
ADD_ROWS = 256


def _sigmoid(x):
    return 1.0 / (1.0 + jnp.exp(-x))


def _silu(x):
    return x * _sigmoid(x)


def _dsilu(x):
    s = _sigmoid(x)
    return s * (1.0 + x * (1.0 - s))


def _gelu(x):
    return 0.5 * x * (1.0 + jnp.tanh(GELU_K0 * (x + GELU_K1 * x * x * x)))


def _dgelu(x):
    t = jnp.tanh(GELU_K0 * (x + GELU_K1 * x * x * x))
    return 0.5 * (1.0 + t) + 0.5 * x * (1.0 - t * t) * GELU_K0 * (1.0 + 3.0 * GELU_K1 * x * x)


def _colsum(v):
    return jnp.sum(v, axis=0, keepdims=True)


def _rowmean(v):
    return jnp.mean(v, axis=-1, keepdims=True)


def _dot(a, b, dims):
    return lax.dot_general(a.astype(BF16), b.astype(BF16), (dims, ((), ())), preferred_element_type=F32)


NN = ((1,), (0,))
NT = ((1,), (1,))
TN = ((0,), (0,))


def _pcall(body, **kw):
    return pl.pallas_call(body, **kw)


def _params(sem):
    return pltpu.CompilerParams(dimension_semantics=sem, vmem_limit_bytes=VMEM_LIMIT)


def _tile(dim, pref):
    if dim <= pref:
        return dim
    t = (pref // LANE) * LANE
    while t > LANE and dim % t:
        t -= LANE
    assert dim % t == 0, (dim, pref)
    return t


def _mm(a, b, *, mode, name, m, n, k, a_off=0, b_off=0, tm=1024, tn=1024, tk=1024,
        out=((F32),), epi=None, extras=(), cm=None):
    if cm is not None:
        if mode == "nt":
            tk = _tile(cm, tk)
        else:
            tn = _tile(cm, tn)
    tm, tn, tk = _tile(m, tm), _tile(n, tn), _tile(k, tk)
    nk = k // tk
    dims = {"nn": NN, "nt": NT, "tn": TN}[mode]
    if mode == "tn":
        assert a_off % tm == 0 and b_off % tn == 0
        a_spec = pl.BlockSpec((tk, tm), lambda i, j, kk: (kk, i + a_off // tm))
        b_spec = pl.BlockSpec((tk, tn), lambda i, j, kk: (kk, j + b_off // tn))
    else:
        assert a_off % tk == 0 and b_off == 0
        a_spec = pl.BlockSpec((tm, tk), lambda i, j, kk: (i, kk + a_off // tk))
        if mode == "nn" and cm is not None:
            b_spec = pl.BlockSpec((None, tk, tn), lambda i, j, kk, per=cm // tn: (j // per, kk, j % per))
        elif mode == "nn":
            b_spec = pl.BlockSpec((tk, tn), lambda i, j, kk: (kk, j))
        elif cm is not None:
            b_spec = pl.BlockSpec((None, tn, tk), lambda i, j, kk, per=cm // tk: (kk // per, j, kk % per))
        else:
            b_spec = pl.BlockSpec((tn, tk), lambda i, j, kk: (j, kk))
    in_specs, ex_arrays = [a_spec, b_spec], []
    for e in extras:
        if e[0] == "tile":
            off = e[2] // tn
            assert e[2] % tn == 0
            in_specs.append(pl.BlockSpec((tm, tn), lambda i, j, kk, off=off: (i, j + off)))
        else:
            in_specs.append(pl.BlockSpec((e[1].shape[0], tn), lambda i, j, kk: (0, j)))
        ex_arrays.append(e[1])
    n_ex = len(ex_arrays)
    n_out = len(out)

    def body(*refs):
        a_ref, b_ref = refs[0], refs[1]
        ex_refs = refs[2:2 + n_ex]
        o_refs = refs[2 + n_ex:2 + n_ex + n_out]
        acc_ref = refs[-1]
        kk = pl.program_id(2)

        @pl.when(kk == 0)
        def _():
            acc_ref[...] = jnp.zeros_like(acc_ref)

        acc_ref[...] += _dot(a_ref[...], b_ref[...], dims)

        @pl.when(kk == nk - 1)
        def _():
            acc = acc_ref[...]
            if epi is None:
                o_refs[0][...] = acc.astype(o_refs[0].dtype)
            else:
                vals = epi(acc, *[r[...] for r in ex_refs])
                for r, v in zip(o_refs, vals):
                    r[...] = v.astype(r.dtype)

    if mode == "tn" and cm is not None:
        out_shape = tuple(jax.ShapeDtypeStruct((N_CHIPS, m, cm), d) for d in out)
        out_specs = tuple(pl.BlockSpec((None, tm, tn), lambda i, j, kk, per=cm // tn: (j // per, i, j % per)) for _ in out)
    else:
        out_shape = tuple(jax.ShapeDtypeStruct((m, n), d) for d in out)
        out_specs = tuple(pl.BlockSpec((tm, tn), lambda i, j, kk: (i, j)) for _ in out)
    res = _pcall(
        body, name=name,
        out_shape=out_shape,
        grid=(m // tm, n // tn, nk),
        in_specs=in_specs,
        out_specs=out_specs,
        scratch_shapes=[pltpu.VMEM((tm, tn), F32)],
        compiler_params=_params(("parallel", "parallel", "arbitrary")),
    )(a, b, *ex_arrays)
    return res[0] if n_out == 1 else res


def _rows(fn, *, name, s, tm, ins, outs, ncb=1):
    tm = min(tm, s)
    assert s % tm == 0 and tm % SUBLANE == 0
    ni = s // tm
    r8 = tm // SUBLANE
    in_specs, arrays = [], []
    for e in ins:
        kind, arr = e[0], e[1]
        if kind in ("row", "prev8", "next8", "vecb"):
            off, w = e[2] // e[3], e[3]
            assert e[2] % e[3] == 0
        if kind == "row":
            in_specs.append(pl.BlockSpec((tm, w), lambda j, i, off=off: (i, off + j)))
        elif kind == "prev8":
            in_specs.append(pl.BlockSpec((SUBLANE, w), lambda j, i, off=off: (jnp.maximum(i * r8 - 1, 0), off + j)))
        elif kind == "next8":
            last = s // SUBLANE - 1
            in_specs.append(pl.BlockSpec((SUBLANE, w), lambda j, i, off=off: (jnp.minimum((i + 1) * r8, last), off + j)))
        elif kind == "vec":
            in_specs.append(pl.BlockSpec(arr.shape, lambda j, i, nd=arr.ndim: (0,) * nd))
        else:
            in_specs.append(pl.BlockSpec((arr.shape[0], w), lambda j, i, off=off: (0, off + j)))
        arrays.append(arr)
    out_shape, out_specs = [], []
    for e in outs:
        if e[0] == "row":
            out_shape.append(jax.ShapeDtypeStruct((s, ncb * e[1]), e[2]))
            out_specs.append(pl.BlockSpec((tm, e[1]), lambda j, i: (i, j)))
        else:
            out_shape.append(jax.ShapeDtypeStruct((e[1], ncb * e[2]), F32))
            out_specs.append(pl.BlockSpec((e[1], e[2]), lambda j, i: (0, j)))

    def body(*refs):
        fn(pl.program_id(1), *refs)

    res = _pcall(
        body, name=name, out_shape=tuple(out_shape), grid=(ncb, ni),
        in_specs=in_specs, out_specs=tuple(out_specs),
        compiler_params=_params(("parallel", "arbitrary")),
    )(*arrays)
    return res[0] if len(outs) == 1 else res


def _acc(ref, i, val):
    @pl.when(i == 0)
    def _():
        ref[...] = val

    @pl.when(i > 0)
    def _():
        ref[...] += val


def _normmod_fwd(x, gain, sc, sh, name):
    s, d = x.shape

    def fn(i, x_ref, g_ref, sc_ref, sh_ref, h_ref):
        xv = x_ref[...]
        r = lax.rsqrt(_rowmean(xv * xv) + EPS)
        h_ref[...] = (((xv * r) * g_ref[...]) * (1.0 + sc_ref[...]) + sh_ref[...]).astype(h_ref.dtype)

    return _rows(fn, name=name, s=s, tm=256, ins=[("row", x, 0, d), ("vec", gain), ("vec", sc), ("vec", sh)],
                 outs=[("row", d, BF16)])


def _normmod_bwd(dh, x, gain, sc, dx_add, name):
    s, d = x.shape

    def fn(i, dh_ref, x_ref, g_ref, sc_ref, add_ref, dx_ref, dg_ref, dsc_ref, dsh_ref):
        xv, dhv = x_ref[...], dh_ref[...]
        r = lax.rsqrt(_rowmean(xv * xv) + EPS)
        xn = xv * r
        gain_v, one_sc = g_ref[...], 1.0 + sc_ref[...]
        ghat = dhv * gain_v * one_sc
        dx_ref[...] = r * (ghat - xn * _rowmean(ghat * xn)) + add_ref[...]
        _acc(dg_ref, i, _colsum(dhv * xn * one_sc))
        _acc(dsc_ref, i, _colsum(dhv * xn * gain_v))
        _acc(dsh_ref, i, _colsum(dhv))

    return _rows(fn, name=name, s=s, tm=256,
                 ins=[("row", dh, 0, d), ("row", x, 0, d), ("vec", gain), ("vec", sc), ("row", dx_add, 0, d)],
                 outs=[("row", d, F32), ("acc", 1, d), ("acc", 1, d), ("acc", 1, d)])


def _postnorm_fwd(x, m, gain, gate, name):
    s, d = x.shape

    def fn(i, x_ref, m_ref, g_ref, gate_ref, o_ref):
        mv = m_ref[...]
        r = lax.rsqrt(_rowmean(mv * mv) + EPS)
        o_ref[...] = x_ref[...] + gate_ref[...] * ((mv * r) * g_ref[...])

    return _rows(fn, name=name, s=s, tm=256, ins=[("row", x, 0, d), ("row", m, 0, d), ("vec", gain), ("vec", gate)],
                 outs=[("row", d, F32)])


def _postnorm_bwd(dxo, m, gain, gate, name):
    s, d = m.shape

    def fn(i, dx_ref, m_ref, g_ref, gate_ref, dm_ref, dg_ref, dgate_ref):
        mv, dxv = m_ref[...], dx_ref[...]
        r = lax.rsqrt(_rowmean(mv * mv) + EPS)
        mn = mv * r
        gain_v, gate_v = g_ref[...], gate_ref[...]
        ghat = dxv * gate_v * gain_v
        dm_ref[...] = (r * (ghat - mn * _rowmean(ghat * mn))).astype(dm_ref.dtype)
        _acc(dg_ref, i, _colsum(dxv * gate_v * mn))
        _acc(dgate_ref, i, _colsum(dxv * mn * gain_v))

    return _rows(fn, name=name, s=s, tm=256, ins=[("row", dxo, 0, d), ("row", m, 0, d), ("vec", gain), ("vec", gate)],
                 outs=[("row", d, BF16), ("acc", 1, d), ("acc", 1, d)])


def _rms_fwd(src, off, w, gain, name):
    s = src.shape[0]

    def fn(i, x_ref, g_ref, o_ref):
        xv = x_ref[...]
        r = lax.rsqrt(_rowmean(xv * xv) + EPS)
        o_ref[...] = ((xv * r) * g_ref[...]).astype(o_ref.dtype)

    return _rows(fn, name=name, s=s, tm=512, ins=[("row", src, off, w), ("vec", gain)], outs=[("row", w, BF16)])


def _rms_bwd(dy, src, off, w, gain, name):
    s = src.shape[0]

    def fn(i, dy_ref, x_ref, g_ref, dx_ref, dg_ref):
        xv, dyv = x_ref[...], dy_ref[...]
        r = lax.rsqrt(_rowmean(xv * xv) + EPS)
        xn = xv * r
        ghat = dyv * g_ref[...]
        dx_ref[...] = r * (ghat - xn * _rowmean(ghat * xn))
        _acc(dg_ref, i, _colsum(dyv * xn))

    return _rows(fn, name=name, s=s, tm=512, ins=[("row", dy, 0, w), ("row", src, off, w), ("vec", gain)],
                 outs=[("row", w, F32), ("acc", 1, w)])


def _loss_grad(x, target):
    s, d = x.shape

    def fn(i, x_ref, t_ref, dx_ref, l_ref):
        diff = x_ref[...] - t_ref[...]
        dx_ref[...] = diff * (1.0 / d)
        part = _colsum(jnp.sum(diff * diff, axis=1, keepdims=True)) * (0.5 / d)
        _acc(l_ref, i, jnp.broadcast_to(part, (1, LANE)))

    return _rows(fn, name="loss_grad", s=s, tm=256, ins=[("row", x, 0, d), ("row", target, 0, d)],
                 outs=[("row", d, F32), ("acc", 1, LANE)])


FFN_WC = 512


def _shift_rows(xv, h_ref, i, row, k):
    out = pltpu.roll(xv, k, 0)
    for r in range(k):
        hrow = jnp.where(i > 0, h_ref[SUBLANE - k + r:SUBLANE - k + r + 1, :], 0.0)
        out = jnp.where(row == r, hrow, out)
    return out


def _conv_rows(x_ref, h_ref, w_ref, b_ref, i, row):
    xv = x_ref[...]
    s1, s2 = _shift_rows(xv, h_ref, i, row, 1), _shift_rows(xv, h_ref, i, row, 2)
    u = ((b_ref[...] + s2 * w_ref[0:1, :]) + s1 * w_ref[1:2, :]) + xv * w_ref[2:3, :]
    return u, s1, s2, xv


def _ffn_act_fwd(up, conv_w, conv_b, ffp):
    s = up.shape[0]
    wc, ncb = FFN_WC, ffp // FFN_WC

    def fn(i, g_ref, gh_ref, v_ref, vh_ref, wg_ref, wv_ref, bg_ref, bv_ref, a_ref):
        row = lax.broadcasted_iota(jnp.int32, g_ref.shape, 0)
        ug = _conv_rows(g_ref, gh_ref, wg_ref, bg_ref, i, row)[0]
        uv = _conv_rows(v_ref, vh_ref, wv_ref, bv_ref, i, row)[0]
        a_ref[...] = (_gelu(ug) * uv).astype(a_ref.dtype)

    return _rows(fn, name="ffn_act_fwd", s=s, tm=512, ncb=ncb,
                 ins=[("row", up, 0, wc), ("prev8", up, 0, wc), ("row", up, ffp, wc), ("prev8", up, ffp, wc),
                      ("vecb", conv_w, 0, wc), ("vecb", conv_w, ffp, wc), ("vecb", conv_b, 0, wc), ("vecb", conv_b, ffp, wc)],
                 outs=[("row", wc, BF16)])


def _ffn_act_bwd(da, up, conv_w, conv_b, ffp):
    s = up.shape[0]
    wc, ncb = FFN_WC, ffp // FFN_WC

    def fn(i, da_ref, g_ref, gh_ref, v_ref, vh_ref, wg_ref, wv_ref, bg_ref, bv_ref,
           dug_ref, duv_ref, dwg_ref, dwv_ref, dbg_ref, dbv_ref):
        row = lax.broadcasted_iota(jnp.int32, g_ref.shape, 0)
        ug, g1, g2, g0 = _conv_rows(g_ref, gh_ref, wg_ref, bg_ref, i, row)
        uv, v1, v2, v0 = _conv_rows(v_ref, vh_ref, wv_ref, bv_ref, i, row)
        dav = da_ref[...]
        dug = dav * uv * _dgelu(ug)
        duv = dav * _gelu(ug)
        dug_ref[...] = dug
        duv_ref[...] = duv
        for r, (gt, vt) in enumerate(((g2, v2), (g1, v1), (g0, v0))):
            _acc(dwg_ref.at[r:r + 1, :], i, _colsum(dug * gt))
            _acc(dwv_ref.at[r:r + 1, :], i, _colsum(duv * vt))
        _acc(dbg_ref, i, _colsum(dug))
        _acc(dbv_ref, i, _colsum(duv))

    return _rows(fn, name="ffn_act_bwd", s=s, tm=512, ncb=ncb,
                 ins=[("row", da, 0, wc), ("row", up, 0, wc), ("prev8", up, 0, wc), ("row", up, ffp, wc), ("prev8", up, ffp, wc),
                      ("vecb", conv_w, 0, wc), ("vecb", conv_w, ffp, wc), ("vecb", conv_b, 0, wc), ("vecb", conv_b, ffp, wc)],
                 outs=[("row", wc, F32), ("row", wc, F32), ("acc", 3, wc), ("acc", 3, wc), ("acc", 1, wc), ("acc", 1, wc)])


def _conv_bwd_input(du, conv_w, w_off, name):
    s, ffp = du.shape
    wc, ncb = FFN_WC, ffp // FFN_WC
    ni = s // min(512, s)

    def fn(i, du_ref, nx_ref, w_ref, o_ref):
        dv = du_ref[...]
        tm = dv.shape[0]
        row = lax.broadcasted_iota(jnp.int32, dv.shape, 0)
        n0 = jnp.where(i < ni - 1, nx_ref[0:1, :], 0.0)
        n1 = jnp.where(i < ni - 1, nx_ref[1:2, :], 0.0)
        u1 = jnp.where(row == tm - 1, n0, pltpu.roll(dv, tm - 1, 0))
        u2 = jnp.where(row == tm - 1, n1, jnp.where(row == tm - 2, n0, pltpu.roll(dv, tm - 2, 0)))
        o_ref[...] = (dv * w_ref[2:3, :] + u1 * w_ref[1:2, :] + u2 * w_ref[0:1, :]).astype(o_ref.dtype)

    return _rows(fn, name=name, s=s, tm=512, ncb=ncb,
                 ins=[("row", du, 0, wc), ("next8", du, 0, wc), ("vecb", conv_w, w_off, wc)],
                 outs=[("row", wc, BF16)])


def _cmul(ar, ai, br, bi):
    return ar * br - ai * bi, ar * bi + ai * br


def _s5_scan(x, a, *, reverse, h=None, name):
    s = x.shape[0]
    w = SCAN_W
    t_rows = min(256, s)
    nt = s // t_rows
    ncol = N_STATE // w
    nbits = t_rows.bit_length()
    r8 = t_rows // SUBLANE

    def tblk(t):
        return nt - 1 - t if reverse else t

    def body(*refs):
        if reverse:
            x_ref, a_ref, h_ref, hh_ref, o_ref, da_ref, carry, ptab = refs
        else:
            x_ref, a_ref, o_ref, carry, ptab = refs
        t = pl.program_id(1)
        row = lax.broadcasted_iota(jnp.int32, (t_rows, w), 0)
        idx = (t_rows - 1 - row) if reverse else row
        ar = a_ref[:, :w]
        ai = -a_ref[:, w:] if reverse else a_ref[:, w:]
        pows = [(ar, ai)]
        for _ in range(nbits - 1):
            pows.append(_cmul(*pows[-1], *pows[-1]))

        @pl.when(t == 0)
        def _():
            carry[...] = jnp.zeros_like(carry)
            pr, pi = jnp.ones((t_rows, w), F32), jnp.zeros((t_rows, w), F32)
            for kbit in range(nbits):
                bit = ((idx + 1) >> kbit) & 1
                fr = jnp.where(bit == 1, pows[kbit][0], 1.0)
                fi = jnp.where(bit == 1, pows[kbit][1], 0.0)
                pr, pi = _cmul(pr, pi, fr, fi)
            ptab[:, :w] = pr
            ptab[:, w:] = pi

        xr, xi = x_ref[:, :w], x_ref[:, w:]
        step = 1
        kbit = 0
        while step < t_rows:
            shift = (t_rows - step) if reverse else step
            yr, yi = pltpu.roll(xr, shift, 0), pltpu.roll(xi, shift, 0)
            zr, zi = _cmul(pows[kbit][0], pows[kbit][1], yr, yi)
            keep = idx >= step
            xr = xr + jnp.where(keep, zr, 0.0)
            xi = xi + jnp.where(keep, zi, 0.0)
            step *= 2
            kbit += 1
        cr, ci = carry[0:1, :w], carry[0:1, w:]
        zr, zi = _cmul(ptab[:, :w], ptab[:, w:], cr, ci)
        xr, xi = xr + zr, xi + zi
        o_ref[:, :w] = xr
        o_ref[:, w:] = xi
        last = 0 if reverse else t_rows - 1
        carry[0:1, :] = o_ref[last:last + 1, :]
        if reverse:
            halo_r = jnp.where(t < nt - 1, hh_ref[SUBLANE - 1:SUBLANE, :w], 0.0)
            halo_i = jnp.where(t < nt - 1, hh_ref[SUBLANE - 1:SUBLANE, w:], 0.0)
            hr = jnp.where(row == 0, halo_r, pltpu.roll(h_ref[:, :w], 1, 0))
            hi = jnp.where(row == 0, halo_i, pltpu.roll(h_ref[:, w:], 1, 0))
            _acc(da_ref.at[:, :w], t, _colsum(xr * hr + xi * hi))
            _acc(da_ref.at[:, w:], t, _colsum(xi * hr - xr * hi))

    blk = pl.BlockSpec((t_rows, 2 * w), lambda j, t: (tblk(t), j))
    a_spec = pl.BlockSpec((1, 2 * w), lambda j, t: (0, j))
    in_specs, arrays = [blk, a_spec], [x, a]
    out_shape = [jax.ShapeDtypeStruct((s, 2 * N_STATE), F32)]
    out_specs = [blk]
    if reverse:
        in_specs += [blk, pl.BlockSpec((SUBLANE, 2 * w), lambda j, t: (jnp.maximum(tblk(t) * r8 - 1, 0), j))]
        arrays += [h, h]
        out_shape.append(jax.ShapeDtypeStruct((1, 2 * N_STATE), F32))
        out_specs.append(a_spec)
    res = _pcall(
        body, name=name, out_shape=tuple(out_shape), grid=(ncol, nt), in_specs=in_specs, out_specs=tuple(out_specs),
        scratch_shapes=[pltpu.VMEM((SUBLANE, 2 * w), F32), pltpu.VMEM((t_rows, 2 * w), F32)],
        compiler_params=_params(("parallel", "arbitrary")),
    )(*arrays)
    return res if reverse else res[0]


def _s5_glu_bwd_a(dout, dout_off, y, z):
    s = y.shape[0]
    w = S5_WIDTH

    def fn(i, do_ref, y_ref, z_ref, dz_ref, p_ref):
        dov = do_ref[...]
        sg = _sigmoid(z_ref[...])
        dz_ref[...] = (dov * _gelu(y_ref[...]) * sg * (1.0 - sg)).astype(dz_ref.dtype)
        p_ref[...] = dov * sg

    return _rows(fn, name="s5_glu_bwd", s=s, tm=512, ins=[("row", dout, dout_off, w), ("row", y, 0, w), ("row", z, 0, w)],
                 outs=[("row", w, BF16), ("row", w, F32)])


def _s5_dd(dy, proj):
    s = dy.shape[0]
    w = S5_WIDTH

    def fn(i, dy_ref, u_ref, dd_ref):
        _acc(dd_ref, i, _colsum(dy_ref[...] * u_ref[...]))

    return _rows(fn, name="s5_dd", s=s, tm=512, ins=[("row", dy, 0, w), ("row", proj, P_S5, w)], outs=[("acc", 1, w)])


def _s5_fwd(proj, wl, s):
    bu = _mm(proj, wl["s5_bd"], mode="nn", name="s5_bu", m=s, n=2 * N_STATE, k=S5_WIDTH, a_off=P_S5)
    h = _s5_scan(bu, wl["s5_a"], reverse=False, name="s5_scan_fwd")
    def y_epi(acc, u, d):
        yv = acc + d * u
        return yv, _gelu(yv)

    y, yg = _mm(h, wl["s5_cd"], mode="nn", name="s5_y", m=s, n=S5_WIDTH, k=2 * N_STATE, out=(F32, BF16),
                extras=[("tile", proj, P_S5), ("row", wl["s5_d"])], epi=y_epi)
    z, out = _mm(yg, wl["s5_w_glu"], mode="nn", name="s5_glu", m=s, n=S5_WIDTH, k=S5_WIDTH, out=(F32, BF16),
                 extras=[("tile", y, 0)], epi=lambda acc, yv: (acc, _gelu(yv) * _sigmoid(acc)))
    return out, (h, y, z, yg)


def _s5_bwd(dcat, proj, wl, saved, s):
    h, y, z, yg = saved
    dz, p1 = _s5_glu_bwd_a(dcat, 0, y, z)
    dy = _mm(dz, wl["s5_w_glu"], mode="nt", name="s5_dyg", m=s, n=S5_WIDTH, k=S5_WIDTH,
             extras=[("tile", p1, 0), ("tile", y, 0)], epi=lambda acc, p, yv: ((p + acc) * _dgelu(yv),))
    gh = _mm(dy, wl["s5_cd"], mode="nt", name="s5_gh", m=s, n=2 * N_STATE, k=S5_WIDTH)
    adj, da = _s5_scan(gh, wl["s5_a"], reverse=True, h=h, name="s5_scan_bwd")
    du = _mm(adj, wl["s5_bd"], mode="nt", name="s5_du", m=s, n=S5_WIDTH, k=2 * N_STATE,
             extras=[("tile", dy, 0), ("row", wl["s5_d"])], epi=lambda acc, dyv, d: (acc + dyv * d,))
    grads = {
        "s5_a": da,
        "s5_bd": _mm(proj, adj, mode="tn", name="s5_dbd", m=S5_WIDTH, n=2 * N_STATE, k=s, a_off=P_S5),
        "s5_cd": _mm(h, dy, mode="tn", name="s5_dcd", m=2 * N_STATE, n=S5_WIDTH, k=s),
        "s5_d": _s5_dd(dy, proj),
        "s5_w_glu": _mm(yg, dz, mode="tn", name="s5_dwglu", m=S5_WIDTH, n=S5_WIDTH, k=s),
    }
    return du, grads


def _mla_prep(qraw, kvraw, proj, cs):
    s = qraw.shape[0]
    hw = MLA_HEADS * LANE

    def fn(i, q_ref, kv_ref, kr_ref, cs_ref, qn_ref, qr_ref, kvb_ref, krb_ref):
        csv = cs_ref[...]
        qn_ref[...] = q_ref[:, :hw].astype(BF16)
        for hd in range(MLA_HEADS):
            p = q_ref[:, hw + hd * LANE:hw + (hd + 1) * LANE] * csv
            qr_ref[:, hd * LANE:(hd + 1) * LANE] = (p + pltpu.roll(p, LANE // 2, 1)).astype(BF16)
        kvb_ref[...] = kv_ref[...].astype(BF16)
        p = kr_ref[...] * csv
        lane = lax.broadcasted_iota(jnp.int32, p.shape, 1)
        krb_ref[...] = jnp.where(lane < LANE // 2, p + pltpu.roll(p, LANE // 2, 1), 0.0).astype(BF16)

    return _rows(fn, name="mla_prep", s=s, tm=256,
                 ins=[("row", qraw, 0, 2 * hw), ("row", kvraw, 0, 2 * hw), ("row", proj, P_KR, LANE), ("row", cs, 0, LANE)],
                 outs=[("row", hw, BF16), ("row", hw, BF16), ("row", 2 * hw, BF16), ("row", LANE, BF16)])


def _mla_rope_bwd(dqn, dqr2, dkr2h, cs):
    s = dqn.shape[0]
    hw = MLA_HEADS * LANE

    def fn(i, dqn_ref, dqr_ref, dkr_ref, cs_ref, dq_ref, dk_ref):
        csv = cs_ref[...]
        dq_ref[:, :hw] = dqn_ref[...].astype(BF16)
        ksum = jnp.zeros(csv.shape, F32)
        for hd in range(MLA_HEADS):
            g = dqr_ref[:, hd * LANE:(hd + 1) * LANE]
            dq_ref[:, hw + hd * LANE:hw + (hd + 1) * LANE] = ((g + pltpu.roll(g, LANE // 2, 1)) * csv).astype(BF16)
            ksum = ksum + dkr_ref[:, hd * LANE:(hd + 1) * LANE]
        dk_ref[...] = ksum * csv

    return _rows(fn, name="mla_rope_bwd", s=s, tm=256,
                 ins=[("row", dqn, 0, hw), ("row", dqr2, 0, hw), ("row", dkr2h, 0, hw), ("row", cs, 0, LANE)],
                 outs=[("row", 2 * hw, BF16), ("row", LANE, F32)])


def _lanes(a_ref, b_ref):
    return jnp.concatenate([a_ref[...], b_ref[...]], axis=1)


def _attn_scores(qn_ref, qr_ref, kn_ref, kr_ref, qi, ki, tq, tk):
    scale = (MLA_NOPE + MLA_ROPE) ** -0.5
    sc = _dot(_lanes(qn_ref, qr_ref), _lanes(kn_ref, kr_ref), NT) * scale
    assert tq == tk
    return sc, lax.broadcasted_iota(jnp.int32, (tq, tk), 1) <= lax.broadcasted_iota(jnp.int32, (tq, tk), 0)


def _attn_specs(tq, tk, q_of, k_of):
    qs = pl.BlockSpec((tq, LANE), lambda h, a, b: (q_of(a, b), h))
    return [qs, qs,
            pl.BlockSpec((tk, LANE), lambda h, a, b: (k_of(a, b), 2 * h)),
            pl.BlockSpec((tk, LANE), lambda h, a, b: (k_of(a, b), 2 * h + 1)),
            pl.BlockSpec((tk, LANE), lambda h, a, b: (k_of(a, b), 0))]


def _flash_fwd(qn, qr2, kv, kr2):
    s = qn.shape[0]
    tq = tk = min(512, s)
    nq = s // tq

    scale = (MLA_NOPE + MLA_ROPE) ** -0.5
    pair = 2

    def body(qn_ref, qr_ref, kv_ref, kr_ref, o_ref, lse_ref, m_sc, l_sc, acc_sc):
        qi, ki = pl.program_id(1), pl.program_id(2)

        @pl.when(ki == 0)
        def _():
            m_sc[...] = jnp.full(m_sc.shape, -jnp.inf, F32)
            l_sc[...] = jnp.zeros_like(l_sc)
            acc_sc[...] = jnp.zeros_like(acc_sc)

        def step(diagonal):
            causal = lax.broadcasted_iota(jnp.int32, (tq, tk), 1) <= lax.broadcasted_iota(jnp.int32, (tq, tk), 0)
            for hd in range(pair):
                cols = slice(hd * LANE, (hd + 1) * LANE)
                q = jnp.concatenate([qn_ref[:, cols], qr_ref[:, cols]], axis=1)
                k = jnp.concatenate([kv_ref[:, 2 * hd * LANE:(2 * hd + 1) * LANE], kr_ref[...]], axis=1)
                sc = _dot(q, k, NT) * scale
                if diagonal:
                    sc = jnp.where(causal, sc, -1e30)
                m_old = m_sc[hd]
                m_new = jnp.maximum(m_old, jnp.max(sc, axis=1, keepdims=True))
                alpha = jnp.exp(m_old - m_new)
                p = jnp.exp(sc - m_new)
                l_sc[hd] = alpha * l_sc[hd] + jnp.sum(p, axis=1, keepdims=True)
                acc_sc[:, cols] = alpha * acc_sc[:, cols] + _dot(p, kv_ref[:, (2 * hd + 1) * LANE:(2 * hd + 2) * LANE], NN)
                m_sc[hd] = m_new

        @pl.when(ki < qi)
        def _():
            step(False)

        @pl.when(ki == qi)
        def _():
            step(True)
            for hd in range(pair):
                cols = slice(hd * LANE, (hd + 1) * LANE)
                o_ref[:, cols] = acc_sc[:, cols] / l_sc[hd]
                lse_ref[hd] = m_sc[hd] + jnp.log(l_sc[hd])

    kof = lambda a, b: jnp.minimum(a, b)
    return _pcall(
        body, name="mla_flash_fwd",
        out_shape=(jax.ShapeDtypeStruct((s, MLA_WIDTH), F32), jax.ShapeDtypeStruct((MLA_HEADS, s, 1), F32)),
        grid=(MLA_HEADS // pair, nq, nq),
        in_specs=[pl.BlockSpec((tq, pair * LANE), lambda h, a, b: (a, h)), pl.BlockSpec((tq, pair * LANE), lambda h, a, b: (a, h)),
                  pl.BlockSpec((tk, 2 * pair * LANE), lambda h, a, b: (kof(a, b), h)),
                  pl.BlockSpec((tk, LANE), lambda h, a, b: (kof(a, b), 0))],
        out_specs=(pl.BlockSpec((tq, pair * LANE), lambda h, a, b: (a, h)), pl.BlockSpec((pair, tq, 1), lambda h, a, b: (h, a, 0))),
        scratch_shapes=[pltpu.VMEM((pair, tq, 1), F32), pltpu.VMEM((pair, tq, 1), F32), pltpu.VMEM((tq, pair * LANE), F32)],
        compiler_params=_params(("parallel", "parallel", "arbitrary")),
    )(qn, qr2, kv, kr2)


def _flash_bwd_dq(qn, qr2, kv, kr2, do, do_off, o, lse):
    s = qn.shape[0]
    tq = tk = min(512, s)
    nq = s // tq
    scale = (MLA_NOPE + MLA_ROPE) ** -0.5
    ob = do_off // LANE

    def body(qn_ref, qr_ref, kn_ref, v_ref, kr_ref, do_ref, o_ref, lse_ref, dqn_ref, dqr_ref, dl_ref, dl_sc, aq_sc):
        qi, ki = pl.program_id(1), pl.program_id(2)

        @pl.when(ki == 0)
        def _():
            dl_sc[...] = jnp.sum(do_ref[...] * o_ref[...], axis=1, keepdims=True)
            aq_sc[...] = jnp.zeros_like(aq_sc)

        def step(diagonal):
            sc, causal = _attn_scores(qn_ref, qr_ref, kn_ref, kr_ref, qi, ki, tq, tk)
            p = jnp.exp(sc - lse_ref[0])
            if diagonal:
                p = jnp.where(causal, p, 0.0)
            dp = _dot(do_ref[...], v_ref[...], NT)
            ds = (p * (dp - dl_sc[...]) * scale).astype(BF16)
            aq_sc[...] += _dot(ds, _lanes(kn_ref, kr_ref), NN)

        @pl.when(ki < qi)
        def _():
            step(False)

        @pl.when(ki == qi)
        def _():
            step(True)
            dqn_ref[...] = aq_sc[:, :LANE]
            dqr_ref[...] = aq_sc[:, LANE:]
            dl_ref[0] = dl_sc[...]

    qblk = pl.BlockSpec((tq, LANE), lambda h, a, b: (a, h))
    vec = pl.BlockSpec((1, tq, 1), lambda h, a, b: (h, a, 0))
    return _pcall(
        body, name="mla_flash_dq",
        out_shape=(jax.ShapeDtypeStruct((s, MLA_WIDTH), F32), jax.ShapeDtypeStruct((s, MLA_WIDTH), F32),
                   jax.ShapeDtypeStruct((MLA_HEADS, s, 1), F32)),
        grid=(MLA_HEADS, nq, nq),
        in_specs=_attn_specs(tq, tk, lambda a, b: a, lambda a, b: jnp.minimum(a, b))
        + [pl.BlockSpec((tq, LANE), lambda h, a, b: (a, h + ob)), qblk, vec],
        out_specs=(qblk, qblk, vec),
        scratch_shapes=[pltpu.VMEM((tq, 1), F32), pltpu.VMEM((tq, 2 * LANE), F32)],
        compiler_params=_params(("parallel", "parallel", "arbitrary")),
    )(qn, qr2, kv, kv, kr2, do, o, lse)


def _flash_bwd_dkv(qn, qr2, kv, kr2, do, do_off, lse, delta):
    s = qn.shape[0]
    tq = tk = min(512, s)
    nq = s // tq
    scale = (MLA_NOPE + MLA_ROPE) ** -0.5
    ob = do_off // LANE

    def body(qn_ref, qr_ref, kn_ref, v_ref, kr_ref, do_ref, lse_ref, dl_ref, dkv_ref, dkr_ref, ak_sc, av_sc):
        ki, qi = pl.program_id(1), pl.program_id(2)

        @pl.when(qi == 0)
        def _():
            ak_sc[...] = jnp.zeros_like(ak_sc)
            av_sc[...] = jnp.zeros_like(av_sc)

        def step(diagonal):
            sc, causal = _attn_scores(qn_ref, qr_ref, kn_ref, kr_ref, qi, ki, tq, tk)
            p = jnp.exp(sc - lse_ref[0])
            if diagonal:
                p = jnp.where(causal, p, 0.0)
            dp = _dot(do_ref[...], v_ref[...], NT)
            ds = (p * (dp - dl_ref[0]) * scale).astype(BF16)
            av_sc[...] += _dot(p, do_ref[...], TN)
            ak_sc[...] += _dot(ds, _lanes(qn_ref, qr_ref), TN)

        @pl.when(qi > ki)
        def _():
            step(False)

        @pl.when(qi == ki)
        def _():
            step(True)

        @pl.when(qi == nq - 1)
        def _():
            dkv_ref[:, :LANE] = ak_sc[:, :LANE]
            dkv_ref[:, LANE:] = av_sc[...]
            dkr_ref[...] = ak_sc[:, LANE:]

    q_of = lambda a, b: jnp.maximum(a, b)
    k_of = lambda a, b: a
    vec = pl.BlockSpec((1, tq, 1), lambda h, a, b: (h, q_of(a, b), 0))
    return _pcall(
        body, name="mla_flash_dkv",
        out_shape=(jax.ShapeDtypeStruct((s, 2 * MLA_WIDTH), F32), jax.ShapeDtypeStruct((s, MLA_WIDTH), F32)),
        grid=(MLA_HEADS, nq, nq),
        in_specs=_attn_specs(tq, tk, q_of, k_of)
        + [pl.BlockSpec((tq, LANE), lambda h, a, b: (q_of(a, b), h + ob)), vec, vec],
        out_specs=(pl.BlockSpec((tk, 2 * LANE), lambda h, a, b: (a, h)), pl.BlockSpec((tk, LANE), lambda h, a, b: (a, h))),
        scratch_shapes=[pltpu.VMEM((tk, 2 * LANE), F32), pltpu.VMEM((tk, LANE), F32)],
        compiler_params=_params(("parallel", "parallel", "arbitrary")),
    )(qn, qr2, kv, kv, kr2, do, lse, delta)


def _mla_fwd(proj, wl, cs, s):
    cqn = _rms_fwd(proj, P_CQ, MLA_Q_RANK, wl["mla_q_norm"], "mla_q_rms")
    ckvn = _rms_fwd(proj, P_CKV, MLA_KV_RANK, wl["mla_kv_norm"], "mla_kv_rms")
    qraw = _mm(cqn, wl["mla_wq"], mode="nn", name="mla_q_proj", m=s, n=2 * MLA_WIDTH, k=MLA_Q_RANK)
    kvraw = _mm(ckvn, wl["mla_w_ukv"], mode="nn", name="mla_kv_proj", m=s, n=2 * MLA_WIDTH, k=MLA_KV_RANK)
    qn, qr2, kv, kr2 = _mla_prep(qraw, kvraw, proj, cs)
    o, lse = _flash_fwd(qn, qr2, kv, kr2)
    return o, (cqn, ckvn, qn, qr2, kv, kr2, o, lse)


def _mla_bwd(dcat, proj, wl, cs, saved, s):
    cqn, ckvn, qn, qr2, kv, kr2, o, lse = saved
    dqn, dqr2, delta = _flash_bwd_dq(qn, qr2, kv, kr2, dcat, S5_WIDTH, o, lse)
    dkv, dkr2h = _flash_bwd_dkv(qn, qr2, kv, kr2, dcat, S5_WIDTH, lse, delta)
    dqraw, dkr = _mla_rope_bwd(dqn, dqr2, dkr2h, cs)
    dcqn = _mm(dqraw, wl["mla_wq"], mode="nt", name="mla_dcqn", m=s, n=MLA_Q_RANK, k=2 * MLA_WIDTH)
    dckvn = _mm(dkv, wl["mla_w_ukv"], mode="nt", name="mla_dckvn", m=s, n=MLA_KV_RANK, k=2 * MLA_WIDTH)
    dcq, dqg = _rms_bwd(dcqn, proj, P_CQ, MLA_Q_RANK, wl["mla_q_norm"], "mla_q_rms_bwd")
    dckv, dkvg = _rms_bwd(dckvn, proj, P_CKV, MLA_KV_RANK, wl["mla_kv_norm"], "mla_kv_rms_bwd")
    grads = {
        "mla_wq": _mm(cqn, dqraw, mode="tn", name="mla_dwq", m=MLA_Q_RANK, n=2 * MLA_WIDTH, k=s),
        "mla_w_ukv": _mm(ckvn, dkv, mode="tn", name="mla_dwukv", m=MLA_KV_RANK, n=2 * MLA_WIDTH, k=s),
        "mla_q_norm": dqg,
        "mla_kv_norm": dkvg,
    }
    return dcq, dckv, dkr, grads


HG_ROWS = 256


def _cumsum_rows(x, row, reverse=False):
    n = x.shape[0]
    step = 1
    while step < n:
        if reverse:
            x = x + jnp.where(row < n - step, pltpu.roll(x, n - step, 0), 0.0)
        else:
            x = x + jnp.where(row >= step, pltpu.roll(x, step, 0), 0.0)
        step *= 2
    return x


def _hg_gates(hq, z, lb):
    sig = _sigmoid(z)
    sigm = _sigmoid(-z)
    f = lb + (1.0 - lb) * sig
    return _silu(hq), (1.0 - lb) * sigm, jnp.log(f), sig, sigm, f


HG_SUB = 16
HG_NSUB = HG_CHUNK // HG_SUB


def _hg_offdiag(q, k, b, row, b_buf):
    ops = []
    for j in range(HG_NSUB - 1):
        e = (j + 1) * HG_SUB
        be = b_buf[e - 1:e, :]
        fj = jnp.where(row >= e, jnp.exp(jnp.minimum(b - be, 0.0)), 0.0)
        gj = jnp.where((row >= e - HG_SUB) & (row < e), jnp.exp(jnp.minimum(be - b, 0.0)), 0.0)
        ops.append((q * fj, k * gj, fj, gj))
    return ops


def _hg_diag_weights(qb, bb, ks, bs, rr, srow):
    e = jnp.exp(jnp.minimum(bb - bs, 0.0))
    keep = rr >= srow
    w = jnp.where(keep, jnp.sum(qb * ks * e, axis=1, keepdims=True), 0.0)
    return e, keep, w


def _hg_specs(rows):
    def col(off):
        return pl.BlockSpec((rows, LANE), lambda h, n, off=off: (n, off // LANE + h))
    return col


def _hg_fwd(proj, lb, gamma):
    s = proj.shape[0]
    rows = min(HG_ROWS, s)
    c = HG_CHUNK
    npc = rows // c
    nb = s // rows

    def body(hq_ref, hf_ref, hi_ref, hg_ref, lb_ref, gm_ref, y_ref, o_ref, st_ref, st_sc, k_buf, b_buf):
        n = pl.program_id(1)

        @pl.when(n == 0)
        def _():
            st_sc[...] = jnp.zeros_like(st_sc)

        row = lax.broadcasted_iota(jnp.int32, (c, LANE), 0)
        rr = lax.broadcasted_iota(jnp.int32, (HG_SUB, 1), 0)
        lbv = lb_ref[...]
        for ch in range(npc):
            sl = slice(ch * c, (ch + 1) * c)
            q, k, logf, _, _, _ = _hg_gates(hq_ref[sl, :], hf_ref[sl, :], lbv)
            v = hi_ref[sl, :]
            b = _cumsum_rows(logf, row)
            bl = _colsum(logf)
            k_buf[...] = k
            b_buf[...] = b
            st = st_sc[...]
            st_ref[0, ch] = st
            a_off = sum(_dot(qf, kg, NT) for qf, kg, _, _ in _hg_offdiag(q, k, b, row, b_buf))
            o = _dot(q * jnp.exp(b), st, NT) + _dot(a_off, v, NN)
            st_sc[...] = st * jnp.exp(bl) + _dot(v, k * jnp.exp(bl - b), TN)
            diag = []
            for i in range(HG_NSUB):
                r0 = i * HG_SUB
                qb, bb = q[r0:r0 + HG_SUB], b[r0:r0 + HG_SUB]
                acc = jnp.zeros((HG_SUB, LANE), F32)
                for srow in range(HG_SUB):
                    t = r0 + srow
                    _, _, w = _hg_diag_weights(qb, bb, k_buf[t:t + 1, :], b_buf[t:t + 1, :], rr, srow)
                    acc = acc + w * hi_ref[ch * c + t:ch * c + t + 1, :]
                diag.append(acc)
            o = o + jnp.concatenate(diag, axis=0)
            o_ref[sl, :] = o
            r = lax.rsqrt(_rowmean(o * o) + EPS)
            y_ref[sl, :] = (((o * r) * gm_ref[...]) * _silu(hg_ref[sl, :])).astype(y_ref.dtype)

    col = _hg_specs(rows)
    out_blk = pl.BlockSpec((rows, LANE), lambda h, n: (n, h))
    return _pcall(
        body, name="hgrn_fwd",
        out_shape=(jax.ShapeDtypeStruct((s, HG_WIDTH), BF16), jax.ShapeDtypeStruct((s, HG_WIDTH), F32),
                   jax.ShapeDtypeStruct((HG_HEADS, s // c, HG_DV, HG_DK), F32)),
        grid=(HG_HEADS, nb),
        in_specs=[col(P_HQ), col(P_HF), col(P_HI), col(P_HG),
                  pl.BlockSpec((1, LANE), lambda h, n: (0, h)), pl.BlockSpec((1, LANE), lambda h, n: (0, 0))],
        out_specs=(out_blk, out_blk, pl.BlockSpec((1, npc, HG_DV, HG_DK), lambda h, n: (h, n, 0, 0))),
        scratch_shapes=[pltpu.VMEM((HG_DV, HG_DK), F32), pltpu.VMEM((c, LANE), F32), pltpu.VMEM((c, LANE), F32)],
        compiler_params=_params(("parallel", "arbitrary")),
    )(proj, proj, proj, proj, lb, gamma)


def _hg_bwd(dcat, dy_off, proj, lb, gamma, o_saved, states):
    s = proj.shape[0]
    rows = min(HG_ROWS, s)
    c = HG_CHUNK
    npc = rows // c
    nb = s // rows
    yb = dy_off // LANE

    def body(hq_ref, hf_ref, hi_ref, hg_ref, lb_ref, gm_ref, dy_ref, o_ref, st_ref,
             dq_ref, df_ref, di_ref, dg_ref, dlb_ref, dgm_ref, dst_sc, k_buf, b_buf, dk_buf, dv_buf):
        n = pl.program_id(1)

        @pl.when(n == 0)
        def _():
            dst_sc[...] = jnp.zeros_like(dst_sc)
            dlb_ref[...] = jnp.zeros_like(dlb_ref)
            dgm_ref[...] = jnp.zeros_like(dgm_ref)

        row = lax.broadcasted_iota(jnp.int32, (c, LANE), 0)
        rr = lax.broadcasted_iota(jnp.int32, (HG_SUB, 1), 0)
        lbv, gmv = lb_ref[...], gm_ref[...]
        for ch in reversed(range(npc)):
            sl = slice(ch * c, (ch + 1) * c)
            hq, z, v, g = hq_ref[sl, :], hf_ref[sl, :], hi_ref[sl, :], hg_ref[sl, :]
            q, k, logf, sig, sigm, f = _hg_gates(hq, z, lbv)
            b = _cumsum_rows(logf, row)
            bl = _colsum(logf)
            k_buf[...] = k
            b_buf[...] = b
            eb, ebl = jnp.exp(b), jnp.exp(bl)
            qe, kl = q * eb, k * jnp.exp(bl - b)
            st = st_ref[0, ch]
            dst = dst_sc[...]
            o, dyv = o_ref[sl, :], dy_ref[sl, :]
            r = lax.rsqrt(_rowmean(o * o) + EPS)
            on = o * r
            sg = _silu(g)
            dgm_ref[0] += _colsum(dyv * on * sg)
            dg_ref[sl, :] = dyv * on * gmv * _dsilu(g)
            go = dyv * gmv * sg
            do = r * (go - on * _rowmean(go * on))
            dq = _dot(do, st, NN) * eb
            dkl = _dot(v, dst, NN)
            dk = dkl * jnp.exp(bl - b)
            dv = _dot(kl, dst, NT)
            dbl = _colsum(dst * st) * ebl + _colsum(dkl * kl)
            dst_sc[...] = dst * ebl + _dot(do, qe, TN)
            ops = _hg_offdiag(q, k, b, row, b_buf)
            da = _dot(do, v, NT)
            a_off = sum(_dot(qf, kg, NT) for qf, kg, _, _ in ops)
            dv = dv + _dot(a_off, do, TN)
            for qf, kg, fj, gj in ops:
                dq = dq + _dot(da, kg, NN) * fj
                dk = dk + _dot(da, qf, TN) * gj
            dq_diag = []
            for i in range(HG_NSUB):
                r0 = i * HG_SUB
                qb, bb, dob = q[r0:r0 + HG_SUB], b[r0:r0 + HG_SUB], do[r0:r0 + HG_SUB]
                acc = jnp.zeros((HG_SUB, LANE), F32)
                for srow in range(HG_SUB):
                    t = r0 + srow
                    ks = k_buf[t:t + 1, :]
                    e, keep, w = _hg_diag_weights(qb, bb, ks, b_buf[t:t + 1, :], rr, srow)
                    dw = jnp.where(keep, jnp.sum(dob * hi_ref[ch * c + t:ch * c + t + 1, :], axis=1, keepdims=True), 0.0)
                    dv_buf[t:t + 1, :] = _colsum(w * dob)
                    dk_buf[t:t + 1, :] = _colsum(dw * qb * e)
                    acc = acc + dw * (ks * e)
                dq_diag.append(acc)
            dq = dq + jnp.concatenate(dq_diag, axis=0)
            dk = dk + dk_buf[...]
            di_ref[sl, :] = dv + dv_buf[...]
            db = q * dq - k * dk + jnp.where(row == c - 1, dbl, 0.0)
            dlogf = _cumsum_rows(db, row, reverse=True)
            dq_ref[sl, :] = dq * _dsilu(hq)
            s1 = sig * (1.0 - sig) * (1.0 - lbv)
            df_ref[sl, :] = dlogf * s1 / f - dk * s1
            dlb_ref[0] += _colsum(dlogf * sigm / f - dk * sigm)

    def col(off):
        return pl.BlockSpec((rows, LANE), lambda h, n, off=off: (nb - 1 - n, off // LANE + h))

    out_blk = pl.BlockSpec((rows, LANE), lambda h, n: (nb - 1 - n, h))
    acc_blk = pl.BlockSpec((1, 1, LANE), lambda h, n: (h, 0, 0))
    res = _pcall(
        body, name="hgrn_bwd",
        out_shape=tuple(jax.ShapeDtypeStruct((s, HG_WIDTH), F32) for _ in range(4))
        + (jax.ShapeDtypeStruct((HG_HEADS, 1, LANE), F32), jax.ShapeDtypeStruct((HG_HEADS, 1, LANE), F32)),
        grid=(HG_HEADS, nb),
        in_specs=[col(P_HQ), col(P_HF), col(P_HI), col(P_HG),
                  pl.BlockSpec((1, LANE), lambda h, n: (0, h)), pl.BlockSpec((1, LANE), lambda h, n: (0, 0)),
                  pl.BlockSpec((rows, LANE), lambda h, n: (nb - 1 - n, yb + h)), out_blk,
                  pl.BlockSpec((1, npc, HG_DV, HG_DK), lambda h, n: (h, nb - 1 - n, 0, 0))],
        out_specs=(out_blk, out_blk, out_blk, out_blk, acc_blk, acc_blk),
        scratch_shapes=[pltpu.VMEM((HG_DV, HG_DK), F32)] + [pltpu.VMEM((c, LANE), F32)] * 4,
        compiler_params=_params(("parallel", "arbitrary")),
    )(proj, proj, proj, proj, lb, gamma, dcat, o_saved, states)
    dq, df, di, dg, dlb, dgm = res
    return dq, df, di, dg, dlb.reshape(1, HG_WIDTH), dgm.reshape(HG_HEADS, LANE)


def _adamw_math(w, g, m, v):
    m = ADAM_B1 * m + (1.0 - ADAM_B1) * g
    v = ADAM_B2 * v + (1.0 - ADAM_B2) * (g * g)
    m_hat = m / (1.0 - ADAM_B1 ** ADAM_STEP)
    v_hat = v / (1.0 - ADAM_B2 ** ADAM_STEP)
    delta = -ADAM_LR * (m_hat / (jnp.sqrt(v_hat) + ADAM_EPS) + ADAM_WD * w)
    return delta, m, v


def _adamw(w, g, m, v, name):
    shape = w.shape
    width = shape[-1]
    rows = int(np.prod(shape[:-1]))
    tm = rows
    while tm * width * 4 > (1 << 20) and tm % 16 == 0:
        tm //= 2

    def fn(i, w_ref, g_ref, m_ref, v_ref, d_ref, mo_ref, vo_ref):
        d, mn, vn = _adamw_math(w_ref[...], g_ref[...], m_ref[...], v_ref[...])
        d_ref[...] = d
        mo_ref[...] = mn
        vo_ref[...] = vn

    v2 = lambda t: t.reshape(rows, width)
    res = _rows(fn, name=name, s=rows, tm=tm, ins=[("row", v2(t), 0, width) for t in (w, g, m, v)],
                outs=[("row", width, F32)] * 3)
    return tuple(r.reshape(shape) for r in res)


def _ada_grad_adamw(cact_all, dmod_cols, w, m, v):
    n_layers, kdim, n = w.shape
    tm, tn = _tile(kdim, 256), _tile(n, 1024)

    def body(c_ref, d_ref, w_ref, m_ref, v_ref, g_ref, dl_ref, mo_ref, vo_ref):
        g = _dot(c_ref[...], d_ref[...], TN)
        d, mn, vn = _adamw_math(w_ref[...], g, m_ref[...], v_ref[...])
        g_ref[...] = g
        dl_ref[...] = d
        mo_ref[...] = mn
        vo_ref[...] = vn

    blk = pl.BlockSpec((None, tm, tn), lambda l, i, j: (l, i, j))
    return _pcall(
        body, name="ada_grad_adamw", out_shape=tuple(jax.ShapeDtypeStruct(w.shape, F32) for _ in range(4)),
        grid=(n_layers, kdim // tm, n // tn),
        in_specs=[pl.BlockSpec((cact_all.shape[0], tm), lambda l, i, j: (0, i)),
                  pl.BlockSpec((None, dmod_cols.shape[1], tn), lambda l, i, j: (l, 0, j)), blk, blk, blk],
        out_specs=(blk, blk, blk, blk),
        compiler_params=_params(("parallel", "parallel", "parallel")),
    )(cact_all, dmod_cols, w, m, v)


def _me():
    return lax.axis_index("x"), lax.axis_index("y"), lax.axis_index("c")


def _flip(k):
    x, y, c = _me()
    return (x ^ ((k >> 2) & 1), y ^ ((k >> 1) & 1), c ^ (k & 1))


def _lin(dev):
    return 4 * dev[0] + 2 * dev[1] + dev[2]


ANY = pl.BlockSpec(memory_space=pl.ANY)


def _all_gather8(x, name):
    def body(x_ref, out_ref, send_sems, recv_sems, local_sem):
        me = _lin(_me())
        mine = pltpu.make_async_copy(x_ref, out_ref.at[me], local_sem)
        mine.start()
        copies = []
        for k in range(1, 8):
            cp = pltpu.make_async_remote_copy(src_ref=x_ref, dst_ref=out_ref.at[me], send_sem=send_sems.at[k - 1],
                                              recv_sem=recv_sems.at[k - 1], device_id=_flip(k), device_id_type=MESH)
            cp.start()
            copies.append(cp)
        for k in range(1, 8):
            pltpu.make_async_remote_copy(src_ref=x_ref, dst_ref=out_ref.at[_lin(_flip(k))], send_sem=send_sems.at[k - 1],
                                         recv_sem=recv_sems.at[k - 1], device_id=_flip(k), device_id_type=MESH).wait_recv()
        for cp in copies:
            cp.wait_send()
        mine.wait()

    return _pcall(
        body, name=name, out_shape=jax.ShapeDtypeStruct((8,) + x.shape, x.dtype),
        in_specs=[ANY], out_specs=ANY,
        scratch_shapes=[pltpu.SemaphoreType.DMA((7,)), pltpu.SemaphoreType.DMA((7,)), pltpu.SemaphoreType.DMA],
    )(x)


CHIP_FLIPS = (2, 4, 6)


def _row_tile(r, cdim):
    best = SUBLANE
    for t in range(SUBLANE, r + 1, SUBLANE):
        if r % t == 0 and t * cdim * 4 <= (3 << 20):
            best = t
    assert r % best == 0
    return best


def _gather_weights(ws):
    n = len(ws)
    hbm = pltpu.MemorySpace.HBM
    w_refs = [jax.new_ref(w, memory_space=hbm) for w in ws]
    out_refs = [jax.empty_ref(jax.ShapeDtypeStruct((N_CHIPS,) + w.shape, w.dtype), memory_space=hbm) for w in ws]

    @pl.kernel(mesh=plsc.ScalarSubcoreMesh(axis_name="sequencer", num_cores=1), name="gather_weights",
               scratch_types=(pltpu.SemaphoreType.DMA((6 * n,)), pltpu.SemaphoreType.DMA((6 * n,))),
               compiler_params=pltpu.CompilerParams(collective_id=2))
    def launch(send_sems, recv_sems):
        barrier = pltpu.get_barrier_semaphore()
        for k in CHIP_FLIPS + (1,):
            pl.semaphore_signal(barrier, inc=1, device_id=_flip(k), device_id_type=MESH)
        pl.semaphore_wait(barrier, len(CHIP_FLIPS) + 1)
        x, y, c = _me()
        sib = _flip(1)

        def slot(a, dev, half):
            return out_refs[a].at[2 * dev[0] + dev[1], half]

        first = []
        for a in range(n):
            for j, k in enumerate(CHIP_FLIPS):
                cp = pltpu.make_async_remote_copy(src_ref=w_refs[a].at[c], dst_ref=slot(a, (x, y), c), send_sem=send_sems.at[6 * a + j],
                                                  recv_sem=recv_sems.at[6 * a + j], device_id=_flip(k), device_id_type=MESH)
                cp.start()
                first.append(cp)
        passed = []
        for a in range(n):
            for j, k in enumerate(CHIP_FLIPS):
                src = _flip(k)
                landed = slot(a, src, c)
                pltpu.make_async_remote_copy(src_ref=landed, dst_ref=landed, send_sem=send_sems.at[6 * a + j],
                                             recv_sem=recv_sems.at[6 * a + j], device_id=src, device_id_type=MESH).wait_recv()
                cp = pltpu.make_async_remote_copy(src_ref=landed, dst_ref=landed, send_sem=send_sems.at[6 * a + 3 + j],
                                                  recv_sem=recv_sems.at[6 * a + 3 + j], device_id=sib, device_id_type=MESH)
                cp.start()
                passed.append(cp)
        for a in range(n):
            for j, k in enumerate(CHIP_FLIPS):
                got = slot(a, _flip(k), 1 - c)
                pltpu.make_async_remote_copy(src_ref=got, dst_ref=got, send_sem=send_sems.at[6 * a + 3 + j],
                                             recv_sem=recv_sems.at[6 * a + 3 + j], device_id=sib, device_id_type=MESH).wait_recv()
        for cp in first + passed:
            cp.wait_send()

    launch()
    chip = 2 * lax.axis_index("x") + lax.axis_index("y")
    return [lax.dynamic_update_index_in_dim(r[...], w, chip, axis=0) for r, w in zip(out_refs, ws)]


def _sibling_halves(gs):
    n = len(gs)

    def body(*refs):
        s_refs, out_refs = refs[:n], refs[n:2 * n]
        send_sems, recv_sems = refs[2 * n:]
        c = lax.axis_index("c")
        sib = _flip(1)
        copies = []
        for a in range(n):
            for j in range(N_CHIPS):
                cp = pltpu.make_async_remote_copy(src_ref=s_refs[a].at[j, 1 - c], dst_ref=out_refs[a].at[j], send_sem=send_sems.at[4 * a + j],
                                                  recv_sem=recv_sems.at[4 * a + j], device_id=sib, device_id_type=MESH)
                cp.start()
                copies.append(cp)
        for cp in copies:
            cp.wait()

    return _pcall(
        body, name="rs_sibling_halves", out_shape=tuple(jax.ShapeDtypeStruct((N_CHIPS,) + g.shape[2:], g.dtype) for g in gs),
        in_specs=[ANY] * n, out_specs=(ANY,) * n,
        scratch_shapes=[pltpu.SemaphoreType.DMA((4 * n,)), pltpu.SemaphoreType.DMA((4 * n,))],
    )(*gs)


def _scatter_to_chips(parts):
    n = len(parts)
    hbm = pltpu.MemorySpace.HBM
    p_refs = [jax.new_ref(p, memory_space=hbm) for p in parts]
    out_refs = [jax.empty_ref(jax.ShapeDtypeStruct((3,) + p.shape[1:], p.dtype), memory_space=hbm) for p in parts]

    @pl.kernel(mesh=plsc.ScalarSubcoreMesh(axis_name="sequencer", num_cores=1), name="scatter_to_chips",
               scratch_types=(pltpu.SemaphoreType.DMA((3 * n,)), pltpu.SemaphoreType.DMA((3 * n,))),
               compiler_params=pltpu.CompilerParams(collective_id=1))
    def launch(send_sems, recv_sems):
        barrier = pltpu.get_barrier_semaphore()
        for k in CHIP_FLIPS:
            pl.semaphore_signal(barrier, inc=1, device_id=_flip(k), device_id_type=MESH)
        pl.semaphore_wait(barrier, len(CHIP_FLIPS))
        copies = []
        for a in range(n):
            for j, k in enumerate(CHIP_FLIPS):
                to = _flip(k)
                cp = pltpu.make_async_remote_copy(src_ref=p_refs[a].at[2 * to[0] + to[1]], dst_ref=out_refs[a].at[j],
                                                  send_sem=send_sems.at[3 * a + j], recv_sem=recv_sems.at[3 * a + j],
                                                  device_id=to, device_id_type=MESH)
                cp.start()
                copies.append(cp)
        for cp in copies:
            cp.wait()

    launch()
    return [r[...] for r in out_refs]


def _sibling_result(halves):
    n = len(halves)

    def body(*refs):
        h_refs, out_refs = refs[:n], refs[n:2 * n]
        send_sems, recv_sems = refs[2 * n:]
        sib = _flip(1)
        copies = []
        for a in range(n):
            cp = pltpu.make_async_remote_copy(src_ref=h_refs[a], dst_ref=out_refs[a], send_sem=send_sems.at[a],
                                              recv_sem=recv_sems.at[a], device_id=sib, device_id_type=MESH)
            cp.start()
            copies.append(cp)
        for cp in copies:
            cp.wait()

    theirs = _pcall(
        body, name="rs_sibling_result", out_shape=tuple(jax.ShapeDtypeStruct(h.shape, h.dtype) for h in halves),
        in_specs=[ANY] * n, out_specs=(ANY,) * n,
        scratch_shapes=[pltpu.SemaphoreType.DMA((n,)), pltpu.SemaphoreType.DMA((n,))],
    )(*halves)
    c = lax.axis_index("c")
    return [jnp.where(c == 0, jnp.stack([m, t]), jnp.stack([t, m])) for m, t in zip(halves, theirs)]


def _add_halves(g, r1, name):
    n, _, r, cdim = g.shape
    tm = _row_tile(r, cdim)

    def body(c_ref, g_ref, r_ref, o_ref):
        o_ref[...] = g_ref[...] + r_ref[...]

    return _pcall(
        body, name=name, out_shape=jax.ShapeDtypeStruct((n, r, cdim), g.dtype),
        grid_spec=pltpu.PrefetchScalarGridSpec(
            num_scalar_prefetch=1, grid=(n, r // tm),
            in_specs=[pl.BlockSpec((None, None, tm, cdim), lambda j, i, c_ref: (j, c_ref[0], i, 0)),
                      pl.BlockSpec((None, tm, cdim), lambda j, i, c_ref: (j, i, 0))],
            out_specs=pl.BlockSpec((None, tm, cdim), lambda j, i, c_ref: (j, i, 0))),
        compiler_params=_params(("parallel", "parallel")),
    )(lax.axis_index("c").astype(jnp.int32).reshape(1), g, r1)


def _add_chips(part, got, name):
    _, r, cdim = part.shape
    tm = _row_tile(r, cdim)

    def body(chip_ref, p_ref, g_ref, o_ref):
        o_ref[...] = ((p_ref[...] + g_ref[0]) + g_ref[1]) + g_ref[2]

    chip = (2 * lax.axis_index("x") + lax.axis_index("y")).astype(jnp.int32).reshape(1)
    return _pcall(
        body, name=name, out_shape=jax.ShapeDtypeStruct((r, cdim), part.dtype),
        grid_spec=pltpu.PrefetchScalarGridSpec(
            num_scalar_prefetch=1, grid=(r // tm,),
            in_specs=[pl.BlockSpec((None, tm, cdim), lambda i, chip_ref: (chip_ref[0], i, 0)),
                      pl.BlockSpec((3, tm, cdim), lambda i, chip_ref: (0, i, 0))],
            out_specs=pl.BlockSpec((tm, cdim), lambda i, chip_ref: (i, 0))),
        compiler_params=_params(("parallel",)),
    )(chip, part, got)


def _reduce_scatter_start(gs, names):
    r1 = _sibling_halves(gs)
    parts = [_add_halves(g, r, "add_halves_" + nm) for g, r, nm in zip(gs, r1, names)]
    return parts, _scatter_to_chips(parts), names


def _reduce_scatter_finish(started):
    parts, got, names = started
    mine = [_add_chips(p, q, "add_chips_" + nm) for p, q, nm in zip(parts, got, names)]
    return _sibling_result(mine)


def _sum8(x):
    _, r, n = x.shape
    tm = 128 if r % 128 == 0 else r

    def body(x_ref, o_ref):
        acc = x_ref[0]
        for d in range(1, 8):
            acc = acc + x_ref[d]
        o_ref[...] = acc

    return _pcall(body, name="sum8", out_shape=jax.ShapeDtypeStruct((r, n), x.dtype), grid=(r // tm,),
                  in_specs=[pl.BlockSpec((8, tm, n), lambda i: (0, i, 0))], out_specs=pl.BlockSpec((tm, n), lambda i: (i, 0)),
                  compiler_params=_params(("parallel",)))(x)


SHARDED = ("w_in", "s5_w_glu", "mla_w_uq", "mla_w_ukv", "w_out", "ffn_w_up", "ffn_w_down")
COL_SHARDED = ("w_in", "mla_w_uq", "mla_w_ukv", "ffn_w_up")
REPLICATED = ("s5_lambda_re", "s5_lambda_im", "s5_log_dt", "s5_b_re", "s5_b_im", "s5_c_re", "s5_c_im", "s5_d",
              "mla_q_norm", "mla_kv_norm", "hg_lb_logits", "hg_out_norm", "mix_pre_norm", "mix_post_norm",
              "ffn_pre_norm", "ffn_post_norm", "ffn_conv_b")
WEIGHTS = ("w_in", "s5_lambda_re", "s5_lambda_im", "s5_log_dt", "s5_b_re", "s5_b_im", "s5_c_re", "s5_c_im", "s5_d",
           "s5_w_glu", "mla_q_norm", "mla_w_uq", "mla_kv_norm", "mla_w_ukv", "hg_lb_logits", "hg_out_norm", "w_out",
           "mix_pre_norm", "mix_post_norm", "ffn_pre_norm", "ffn_post_norm", "ffn_w_up", "ffn_conv_w", "ffn_conv_b",
           "ffn_w_down", "w_ada", "b_ada")


FF_PAD = 256


def _halves(t):
    return t.reshape(t.shape[:-2] + (2, t.shape[-2] // 2, t.shape[-1]))


def _unhalve(t):
    return t.reshape(t.shape[:-3] + (2 * t.shape[-2], t.shape[-1]))


def _cols_from_chips(t):
    return jnp.concatenate([t[j] for j in range(N_CHIPS)], axis=1)


def _swap_half(t):
    half = t.shape[-1] // 2
    return jnp.concatenate([-t[..., half:], t[..., :half]], axis=-1)


def _prep_win(w):
    s5, cq, ckv, kr, hq, hf, hi, hg = jnp.split(w, (512, 1024, 1280, 1344, 1856, 2368, 2880), axis=1)
    pad = jnp.zeros((w.shape[0], PROJ_W - 3456), w.dtype)
    return jnp.concatenate([s5, cq, hq, hf, hi, hg, ckv, kr, _swap_half(kr), pad], axis=1)


def _prep_wq(w):
    w3 = w.reshape(w.shape[0], MLA_HEADS, MLA_NOPE + MLA_ROPE)
    nope, rope = w3[..., :MLA_NOPE], w3[..., MLA_NOPE:]
    pair = jnp.concatenate([rope, _swap_half(rope)], axis=-1)
    return jnp.concatenate([nope.reshape(w.shape[0], -1), pair.reshape(w.shape[0], -1)], axis=1)


def _pad_ff_cols(w, cpad):
    r = w.shape[0]
    w3 = w.reshape(r, N_CHIPS, -1)
    return jnp.pad(w3, ((0, 0), (0, 0), (0, cpad - w3.shape[2]))).reshape(r, N_CHIPS * cpad)


def _pad_ff_rows(w, cpad):
    w3 = w.reshape(2, 2 * w.shape[1], w.shape[2])
    return jnp.pad(w3, ((0, 0), (0, cpad - w3.shape[1]), (0, 0))).reshape(2 * cpad, w.shape[2])


def _interleave(re, im, axis):
    re, im = jnp.moveaxis(re, axis, -1), jnp.moveaxis(im, axis, -1)
    lead = re.shape[:-1]
    both = jnp.stack([re.reshape(lead + (N_STATE // SCAN_W, SCAN_W)), im.reshape(lead + (N_STATE // SCAN_W, SCAN_W))], axis=-2)
    return jnp.moveaxis(both.reshape(lead + (2 * N_STATE,)), -1, axis)


def _s5_prep(lre, lim, logdt, bre, bim, cre, cim):
    dt = jnp.exp(logdt)[:, None]
    er = jnp.exp(lre * dt)
    ar, ai = er * jnp.cos(lim * dt), er * jnp.sin(lim * dt)
    nr, den = ar - 1.0, lre * lre + lim * lim
    cr, ci = (nr * lre + ai * lim) / den, (ai * lre - nr * lim) / den
    bbr = cr[..., None] * bre - ci[..., None] * bim
    bbi = cr[..., None] * bim + ci[..., None] * bre
    eye = jnp.eye(S5_GROUPS, dtype=F32)[:, None, :, None]

    def block_diag(t):
        return (t[:, :, None, :] * eye).reshape(S5_GROUPS * t.shape[1], S5_GROUPS * t.shape[2])

    tr = lambda t: jnp.transpose(t, (0, 2, 1))
    bd = _interleave(block_diag(tr(bbr)), block_diag(tr(bbi)), 1)
    cd = _interleave(block_diag(tr(cre)), block_diag(tr(-cim)), 0)
    a = _interleave(ar.reshape(1, N_STATE), ai.reshape(1, N_STATE), 1)
    return a, bd, cd


def _lower_bounds(logits):
    probs = jax.nn.softmax(logits, axis=0)
    return jnp.cumsum(probs, axis=0) - probs[0:1]


def _rope_table(positions):
    inv_freq = 1.0 / (ROPE_THETA ** (jnp.arange(0, MLA_ROPE, 2, dtype=F32) / MLA_ROPE))
    ang = positions.astype(F32)[:, None] * inv_freq
    cos, sin = jnp.cos(ang), jnp.sin(ang)
    return jnp.concatenate([cos, cos, sin, sin], axis=1)


def _split_mod(mod):
    return [mod[:, i * D_MODEL:(i + 1) * D_MODEL] for i in range(6)]


def _layer_fwd(x, wl, mod, cs):
    s = x.shape[0]
    ffp = wl["wdown_p"].shape[0]
    sh1, sc1, g1, sh2, sc2, g2 = _split_mod(mod)
    h1 = _normmod_fwd(x, wl["mix_pre_norm"], sc1, sh1, "mix_pre")
    proj = _mm(h1, wl["win_p"], mode="nn", name="in_proj", m=s, n=PROJ_W, k=D_MODEL)
    out_s5, s5_saved = _s5_fwd(proj, wl, s)
    o_mla, mla_saved = _mla_fwd(proj, wl, cs, s)
    y_hg, o_hg, states = _hg_fwd(proj, wl["hg_lb"], wl["hg_out_norm"])
    cat = jnp.concatenate([out_s5, o_mla.astype(BF16), y_hg], axis=1)
    mixed = _mm(cat, wl["w_out"], mode="nn", name="out_proj", m=s, n=D_MODEL, k=D_MODEL)
    x2 = _postnorm_fwd(x, mixed, wl["mix_post_norm"], g1, "mix_post")
    h2 = _normmod_fwd(x2, wl["ffn_pre_norm"], sc2, sh2, "ffn_pre")
    up = _mm(h2, wl["wup_cm"], mode="nn", name="ffn_up", m=s, n=2 * ffp, k=D_MODEL, cm=ffp // 2, tn=ffp // 4)
    act = _ffn_act_fwd(up, wl["conv_w_p"], wl["conv_b_p"], ffp)
    y = _mm(act, wl["wdown_p"], mode="nn", name="ffn_down", m=s, n=D_MODEL, k=ffp)
    x3 = _postnorm_fwd(x2, y, wl["ffn_post_norm"], g2, "ffn_post")
    return x3, (x, h1, proj, s5_saved, mla_saved, o_hg, states, cat, mixed, x2, h2, up, act, y)


def _layer_bwd(dx3, saved, wl, mod, cs):
    x, h1, proj, s5_saved, mla_saved, o_hg, states, cat, mixed, x2, h2, up, act, y = saved
    s = x.shape[0]
    ffp = wl["wdown_p"].shape[0]
    sh1, sc1, g1, sh2, sc2, g2 = _split_mod(mod)
    g = {}
    dy, g["ffn_post_norm"], dg2 = _postnorm_bwd(dx3, y, wl["ffn_post_norm"], g2, "ffn_post_bwd")
    da = _mm(dy, wl["wdown_p"], mode="nt", name="ffn_down_dx", m=s, n=ffp, k=D_MODEL)
    g["wdown_p"] = _mm(act, dy, mode="tn", name="ffn_down_dw", m=ffp, n=D_MODEL, k=s)
    dug, duv, dwg, dwv, dbg, dbv = _ffn_act_bwd(da, up, wl["conv_w_p"], wl["conv_b_p"], ffp)
    g["conv_w_p"] = jnp.concatenate([dwg, dwv], axis=1)
    g["conv_b_p"] = jnp.concatenate([dbg, dbv], axis=1)
    dup = jnp.concatenate([_conv_bwd_input(dug, wl["conv_w_p"], 0, "ffn_conv_bwd_gate"),
                           _conv_bwd_input(duv, wl["conv_w_p"], ffp, "ffn_conv_bwd_val")], axis=1)
    dh2 = _mm(dup, wl["wup_cm"], mode="nt", name="ffn_up_dx", m=s, n=D_MODEL, k=2 * ffp, cm=ffp // 2, tk=ffp // 4)
    g["wup_cm"] = _mm(h2, dup, mode="tn", name="ffn_up_dw", m=D_MODEL, n=2 * ffp, k=s, cm=ffp // 2, tn=ffp // 4)
    dx2, g["ffn_pre_norm"], dsc2, dsh2 = _normmod_bwd(dh2, x2, wl["ffn_pre_norm"], sc2, dx3, "ffn_pre_bwd")
    dmixed, g["mix_post_norm"], dg1 = _postnorm_bwd(dx2, mixed, wl["mix_post_norm"], g1, "mix_post_bwd")
    dcat = _mm(dmixed, wl["w_out"], mode="nt", name="out_proj_dx", m=s, n=D_MODEL, k=D_MODEL)
    g["w_out"] = _mm(cat, dmixed, mode="tn", name="out_proj_dw", m=D_MODEL, n=D_MODEL, k=s)
    du_s5, s5g = _s5_bwd(dcat, proj, wl, s5_saved, s)
    dcq, dckv, dkr, mlag = _mla_bwd(dcat, proj, wl, cs, mla_saved, s)
    dhq, dhf, dhi, dhg, g["hg_lb"], dgm = _hg_bwd(dcat, S5_WIDTH + MLA_WIDTH, proj, wl["hg_lb"], wl["hg_out_norm"], o_hg, states)
    g["hg_out_norm"] = jnp.sum(dgm, axis=0, keepdims=True)
    g.update(s5g)
    g.update(mlag)
    dproj = jnp.concatenate([du_s5, dcq, dhq, dhf, dhi, dhg, dckv, dkr, jnp.zeros((s, PROJ_W - 3456), F32)], axis=1).astype(BF16)
    dh1 = _mm(dproj, wl["win_p"], mode="nt", name="in_proj_dx", m=s, n=D_MODEL, k=PROJ_W)
    g["win_p"] = _mm(h1, dproj, mode="tn", name="in_proj_dw", m=D_MODEL, n=PROJ_W, k=s)
    dx, g["mix_pre_norm"], dsc1, dsh1 = _normmod_bwd(dh1, x, wl["mix_pre_norm"], sc1, dx2, "mix_pre_bwd")
    dmod = jnp.concatenate([dsh1, dsc1, dg1, dsh2, dsc2, dg2], axis=1)
    return dx, g, dmod


def _prepare_layer(gathered, conv_w, rep, cpad):
    def sharded_prep(w_in, s5_w_glu, mla_w_uq, mla_w_ukv, w_out, ffn_w_up, ffn_w_down, ffn_conv_w):
        merge = lambda t: t.reshape(N_CHIPS * t.shape[1], t.shape[2])
        return {"win_p": _prep_win(_cols_from_chips(w_in)), "s5_w_glu": merge(s5_w_glu), "mla_wq": _prep_wq(_cols_from_chips(mla_w_uq)),
                "mla_w_ukv": _cols_from_chips(mla_w_ukv), "w_out": merge(w_out), "wup_cm": ffn_w_up,
                "wdown_p": _pad_ff_rows(ffn_w_down, cpad), "conv_w_p": _pad_ff_cols(ffn_conv_w, cpad)}

    def rep_prep(lre, lim, logdt, bre, bim, cre, cim, conv_b):
        a, bd, cd = _s5_prep(lre, lim, logdt, bre, bim, cre, cim)
        return {"s5_a": a, "s5_bd": bd, "s5_cd": cd, "conv_b_p": _pad_ff_cols(conv_b, cpad)}

    sh_args = [gathered[k] for k in SHARDED] + [conv_w]
    rep_names = ("s5_lambda_re", "s5_lambda_im", "s5_log_dt", "s5_b_re", "s5_b_im", "s5_c_re", "s5_c_im", "ffn_conv_b")
    rep_args = [rep[k] for k in rep_names]
    wl = sharded_prep(*sh_args)
    rep_out, rep_vjp = jax.vjp(rep_prep, *rep_args)
    wl.update(rep_out)
    sh_t = jax.linear_transpose(sharded_prep, *[jax.ShapeDtypeStruct(a.shape, F32) for a in sh_args])

    def back(g):
        out = dict(zip(SHARDED + ("ffn_conv_w",), sh_t({k: g[k] for k in ("win_p", "s5_w_glu", "mla_wq", "mla_w_ukv", "w_out", "wup_cm", "wdown_p", "conv_w_p")})))
        out.update(zip(rep_names, rep_vjp({k: g[k] for k in ("s5_a", "s5_bd", "s5_cd", "conv_b_p")})))
        return out

    return wl, back


PER_LAYER_ROWS = ("s5_d", "mla_q_norm", "mla_kv_norm", "hg_out_norm", "mix_pre_norm", "mix_post_norm", "ffn_pre_norm", "ffn_post_norm")


def _flat_pad(parts, unit):
    flat = jnp.concatenate([p.reshape(-1) for p in parts])
    n = -(-flat.shape[0] // unit) * unit
    return jnp.pad(flat, (0, n - flat.shape[0])).reshape(-1, FLAT_COLS)


def _split_flat(flat, like):
    flat = flat.reshape(-1)
    out, pos = [], 0
    for t in like:
        out.append(flat[pos:pos + t.size].reshape(t.shape))
        pos += t.size
    return out


def kernel(x, c, positions, w_in, s5_lambda_re, s5_lambda_im, s5_log_dt, s5_b_re, s5_b_im, s5_c_re, s5_c_im, s5_d, s5_w_glu, mla_q_norm, mla_w_uq, mla_kv_norm, mla_w_ukv, hg_lb_logits, hg_out_norm, w_out, mix_pre_norm, mix_post_norm, ffn_pre_norm, ffn_post_norm, ffn_w_up, ffn_conv_w, ffn_conv_b, ffn_w_down, w_ada, b_ada, loss_target, m_w_in, m_s5_lambda_re, m_s5_lambda_im, m_s5_log_dt, m_s5_b_re, m_s5_b_im, m_s5_c_re, m_s5_c_im, m_s5_d, m_s5_w_glu, m_mla_q_norm, m_mla_w_uq, m_mla_kv_norm, m_mla_w_ukv, m_hg_lb_logits, m_hg_out_norm, m_w_out, m_mix_pre_norm, m_mix_post_norm, m_ffn_pre_norm, m_ffn_post_norm, m_ffn_w_up, m_ffn_conv_w, m_ffn_conv_b, m_ffn_w_down, m_w_ada, m_b_ada, v_w_in, v_s5_lambda_re, v_s5_lambda_im, v_s5_log_dt, v_s5_b_re, v_s5_b_im, v_s5_c_re, v_s5_c_im, v_s5_d, v_s5_w_glu, v_mla_q_norm, v_mla_w_uq, v_mla_kv_norm, v_mla_w_ukv, v_hg_lb_logits, v_hg_out_norm, v_w_out, v_mix_pre_norm, v_mix_post_norm, v_ffn_pre_norm, v_ffn_post_norm, v_ffn_w_up, v_ffn_conv_w, v_ffn_conv_b, v_ffn_w_down, v_w_ada, v_b_ada):
    p = dict(locals())
    n_layers = w_in.shape[0]
    c_up = ffn_w_up.shape[2]
    cpad = -(-c_up // FF_PAD) * FF_PAD
    xs, target = x[0], loss_target[0]
    me = 4 * lax.axis_index("x") + 2 * lax.axis_index("y") + lax.axis_index("c")
    chip = 2 * lax.axis_index("x") + lax.axis_index("y")
    cs = _rope_table(positions[0])

    cact = jax.nn.silu(_all_gather8(c, "gather_c")[:, 0, :])
    ada_cols = w_ada.shape[2]
    mod_part = jnp.stack([_mm(cact, w_ada[l], mode="nn", name="ada_mod", m=8, n=ada_cols, k=D_MODEL) for l in range(n_layers)])
    mod_all = _all_gather8(mod_part.reshape(1, -1), "gather_mod").reshape(N_CHIPS, 2, n_layers, 8, ada_cols)[:, 0]
    mod_mine = lax.dynamic_index_in_dim(mod_all, me, axis=2, keepdims=False)
    mods = mod_mine.transpose(1, 0, 2).reshape(n_layers, -1) + b_ada

    conv_w_all = _all_gather8(ffn_conv_w.reshape(1, -1), "gather_conv_w").reshape(N_CHIPS, 2, n_layers, 3, -1)[:, 0]
    conv_w_full = conv_w_all.transpose(1, 2, 0, 3).reshape(n_layers, 3, -1)

    lbs, lb_vjp = jax.vjp(_lower_bounds, hg_lb_logits)

    gathers = []
    for l in range(n_layers):
        shards = {k: p[k][l] for k in SHARDED}
        shards["ffn_w_up"] = jnp.pad(shards["ffn_w_up"], ((0, 0), (0, cpad - c_up)))
        gathers.append(_gather_weights([_halves(shards[k].astype(BF16)) for k in SHARDED]))

    h = xs
    saved, layers = [], []
    for l in range(n_layers):
        got = gathers[l]
        if l > 0:
            got, h = lax.optimization_barrier((got, h))
        gathered = {k: _unhalve(t) for k, t in zip(SHARDED, got)}
        rep = {k: p[k][l] for k in ("s5_lambda_re", "s5_lambda_im", "s5_log_dt", "s5_b_re", "s5_b_im", "s5_c_re", "s5_c_im")}
        rep["ffn_conv_b"] = ffn_conv_b[l][None, :]
        wl, back = _prepare_layer(gathered, conv_w_full[l], rep, cpad)
        for k in PER_LAYER_ROWS:
            wl[k] = p[k][l][None, :]
        wl["hg_lb"] = lbs[l][None, :]
        layers.append((wl, back))
        h, sv = _layer_fwd(h, wl, mods[l][None, :], cs)
        saved.append(sv)
    dh, loss_part = _loss_grad(h, target)
    loss = lax.psum(loss_part[0, 0], ("x", "y", "c"))

    grads = {k: [None] * n_layers for k in WEIGHTS}
    dmods, dlbs = [None] * n_layers, [None] * n_layers

    def take_in(layer, started):
        for k, t in zip(SHARDED, _reduce_scatter_finish(started)):
            grads[k][layer] = _unhalve(t)
        grads["ffn_w_up"][layer] = grads["ffn_w_up"][layer][:, :c_up]

    under_way = None
    for l in reversed(range(n_layers)):
        wl, back = layers[l]
        dh, g, dmods[l] = _layer_bwd(dh, saved[l], wl, mods[l][None, :], cs)
        if under_way is not None:
            take_in(*under_way)
        ref_g = back(g)
        under_way = (l, _reduce_scatter_start([_halves(ref_g[k]) for k in SHARDED], SHARDED))
        grads["ffn_conv_w"][l] = ref_g["ffn_conv_w"]
        for k in ("s5_lambda_re", "s5_lambda_im", "s5_log_dt", "s5_b_re", "s5_b_im", "s5_c_re", "s5_c_im"):
            grads[k][l] = ref_g[k]
        grads["ffn_conv_b"][l] = ref_g["ffn_conv_b"][0]
        for k in PER_LAYER_ROWS:
            grads[k][l] = g[k][0]
        dlbs[l] = g["hg_lb"][0]
    take_in(*under_way)
    grads = {k: jnp.stack(v) for k, v in grads.items() if v[0] is not None}
    grads["hg_lb_logits"] = lb_vjp(jnp.stack(dlbs))[0]

    summed = REPLICATED + ("ffn_conv_w",)
    rep_flat = _flat_pad([grads[k] for k in summed], 128 * FLAT_COLS)
    rep_sum = _sum8(_all_gather8(rep_flat, "gather_small_grads"))
    for k, t in zip(summed, _split_flat(rep_sum, [grads[k] for k in summed])):
        grads[k] = t
    conv_cols = grads["ffn_conv_w"].reshape(n_layers, 3, N_CHIPS, c_up)
    grads["ffn_conv_w"] = lax.dynamic_index_in_dim(conv_cols, chip, axis=2, keepdims=False)

    dmod_all = _all_gather8(jnp.stack(dmods).reshape(n_layers * 6 * D_MODEL // FLAT_COLS, FLAT_COLS), "gather_dmod")
    grads["b_ada"] = _sum8(dmod_all).reshape(n_layers, 6 * D_MODEL)
    dmod_cols = lax.dynamic_slice_in_dim(dmod_all.reshape(8, n_layers, N_CHIPS, ada_cols), chip, 1, axis=2)[:, :, 0]
    grads["w_ada"], delta_ada, new_m_ada, new_v_ada = _ada_grad_adamw(cact, dmod_cols.transpose(1, 0, 2), w_ada, m_w_ada, v_w_ada)

    delta, new_m, new_v = {"w_ada": delta_ada}, {"w_ada": new_m_ada}, {"w_ada": new_v_ada}
    for k in SHARDED:
        delta[k], new_m[k], new_v[k] = _adamw(p[k], grads[k], p["m_" + k], p["v_" + k], "adamw_" + k)
    small = REPLICATED + ("b_ada", "ffn_conv_w")
    flats = [_flat_pad([src[k] for k in small], 128 * FLAT_COLS)
             for src in (p, grads, {k: p["m_" + k] for k in small}, {k: p["v_" + k] for k in small})]
    for dst, flat in zip((delta, new_m, new_v), _adamw(*flats, "adamw_small")):
        for k, t in zip(small, _split_flat(flat, [p[k] for k in small])):
            dst[k] = t

    return (loss, dh[None], *[grads[k] for k in WEIGHTS], *[delta[k] for k in WEIGHTS],
            *[new_m[k] for k in WEIGHTS], *[new_v[k] for k in WEIGHTS])
```
